```python
import math
import jax, jax.numpy as jnp
from jax import lax
import numpy as np

D_MODEL = 1024
BATCH = 8
SEQ = 4096
DEPTH = 1

HEAD_DIM = 64
DILATIONS = (1, 4, 16)
KEYS_PER_QUERY = 128
WINDOWS = tuple(KEYS_PER_QUERY * d for d in DILATIONS)
N_GROUPS = len(DILATIONS)
HEADS_PER_GROUP = 4
ATT_HEADS = N_GROUPS * HEADS_PER_GROUP
ATT_WIDTH = ATT_HEADS * HEAD_DIM
ATT_OUT_WIDTH = HEADS_PER_GROUP * HEAD_DIM
QUERY_BLOCK = 64
N_BUCKETS = 32
MAX_DISTANCE = WINDOWS[-1]
RWKV_HEADS = D_MODEL // HEAD_DIM
RWKV_WIDTH = RWKV_HEADS * HEAD_DIM
DECAY_LORA = 64
ICLR_LORA = 64
GATE_LORA = 128
RWKV_COLS = 3 * RWKV_WIDTH + DECAY_LORA + ICLR_LORA + GATE_LORA
N_BRANCHES = 2
IN_COLS = 3 * ATT_WIDTH + RWKV_COLS + N_BRANCHES * D_MODEL
D_FF = 4 * D_MODEL
NORM_EPS = 1e-6
LN_X_EPS = 64e-5
L2_EPS = 1e-12

kernel_name = "hybrid_dilated_attn_rwkv7_gated_block"


def rmsnorm(x, g):
    xf = x.astype(jnp.float32)
    y = xf * lax.rsqrt(jnp.mean(xf * xf, axis=-1, keepdims=True) + NORM_EPS)
    return (y * g.astype(jnp.float32)).astype(x.dtype)


def t5_bucket(dist):
    max_exact = N_BUCKETS // 2
    d_f = jnp.maximum(dist, 1).astype(jnp.float32)
    large = max_exact + (jnp.log(d_f / max_exact) / math.log(MAX_DISTANCE / max_exact)
                         * (N_BUCKETS - max_exact)).astype(jnp.int32)
    large = jnp.minimum(large, N_BUCKETS - 1)
    return jnp.where(dist < max_exact, dist, large)


def dilated_window_attention(q, k, v, rel_bias):
    b, s = q.shape[0], q.shape[1]
    dil = jnp.array(DILATIONS, jnp.int32)
    dist = dil[:, None] * jnp.arange(KEYS_PER_QUERY + 1, dtype=jnp.int32)[None, :]
    bucket = t5_bucket(dist)
    bias = rel_bias.reshape(N_BUCKETS, N_GROUPS, HEADS_PER_GROUP)[bucket, jnp.arange(N_GROUPS)[:, None]]
    bias = jnp.transpose(bias, (0, 2, 1)).astype(jnp.float32)
    qg = jnp.moveaxis(q, 2, 0)
    kg = jnp.moveaxis(k, 2, 0)
    vg = jnp.moveaxis(v, 2, 0)
    scale = HEAD_DIM ** -0.5

    def one_block(blk):
        start = blk * QUERY_BLOCK
        t = start + jnp.arange(QUERY_BLOCK, dtype=jnp.int32)
        pos = t[None, :, None] - dist[:, None, :]
        valid = pos >= 0
        pos = jnp.maximum(pos, 0)
        qb = lax.dynamic_slice_in_dim(qg, start, QUERY_BLOCK, axis=2)
        kb = jax.vmap(lambda a, p: a[:, p])(kg, pos)
        vb = jax.vmap(lambda a, p: a[:, p])(vg, pos)
        logits = jnp.einsum('gbqhd,gbqjhd->gbhqj', qb, kb,
                            preferred_element_type=jnp.float32) * scale
        logits = logits + bias[:, None, :, None, :]
        logits = jnp.where(valid[:, None, None, :, :], logits, -jnp.inf)
        lse = jax.nn.logsumexp(logits, axis=-1)
        p = jnp.exp(logits - lse[..., None])
        o = jnp.einsum('gbhqj,gbqjhd->gbqhd', p, vb.astype(jnp.float32))
        w_mix = jax.nn.softmax(lse, axis=0)
        o = jnp.einsum('gbhq,gbqhd->bqhd', w_mix, o)
        return o.astype(v.dtype)

    out = lax.map(one_block, jnp.arange(s // QUERY_BLOCK, dtype=jnp.int32))
    return jnp.moveaxis(out, 0, 1).reshape(b, s, ATT_OUT_WIDTH)


def token_shift(f, mu):
    f_prev = jnp.pad(f, ((0, 0), (1, 0), (0, 0)))[:, :-1]
    return f + (f_prev - f) * mu


def rwkv7_time_mix(f, w0, w_w2, a0, w_a2, w_g2, k_k, k_a, r_k, ln_w, ln_b):
    b, s, _ = f.shape
    splits = [RWKV_WIDTH, 2 * RWKV_WIDTH, 3 * RWKV_WIDTH,
              3 * RWKV_WIDTH + DECAY_LORA, 3 * RWKV_WIDTH + DECAY_LORA + ICLR_LORA]
    r, k, v, fw, fa, fg = jnp.split(f, splits, axis=-1)
    w = -jax.nn.softplus(-(w0 + jnp.tanh(fw) @ w_w2)) - 0.5
    decay = jnp.exp(-jnp.exp(w.astype(jnp.float32)))
    a = jax.nn.sigmoid(a0 + fa @ w_a2)
    g = jax.nn.sigmoid(fg) @ w_g2
    kk = (k * k_k).reshape(b, s, RWKV_HEADS, HEAD_DIM).astype(jnp.float32)
    kk = kk / jnp.maximum(jnp.linalg.norm(kk, axis=-1, keepdims=True), L2_EPS)
    k = k * (1 + (a - 1) * k_a)
    heads = lambda t: t.reshape(b, s, RWKV_HEADS, HEAD_DIM).astype(jnp.float32)
    r, k, v, decay, a = heads(r), heads(k), heads(v), heads(decay), heads(a)

    def step(state, inp):
        r_t, w_t, k_t, v_t, kk_t, a_t = inp
        sa = jnp.einsum('bhij,bhj->bhi', state, -kk_t)
        state = (state * w_t[:, :, None, :]
                 + sa[..., None] * (kk_t * a_t)[:, :, None, :]
                 + v_t[..., None] * k_t[:, :, None, :])
        y_t = jnp.einsum('bhij,bhj->bhi', state, r_t)
        return state, y_t

    xs = tuple(jnp.moveaxis(t, 1, 0) for t in (r, decay, k, v, kk, a))
    state0 = jnp.zeros((b, RWKV_HEADS, HEAD_DIM, HEAD_DIM), jnp.float32)
    _, y = lax.scan(step, state0, xs)
    y = jnp.moveaxis(y, 0, 1)
    mu = jnp.mean(y, axis=-1, keepdims=True)
    var = jnp.mean(jnp.square(y - mu), axis=-1, keepdims=True)
    y = ((y - mu) * lax.rsqrt(var + LN_X_EPS)).reshape(b, s, RWKV_WIDTH)
    y = y * ln_w.astype(jnp.float32) + ln_b.astype(jnp.float32)
    bonus = jnp.sum(r * k * r_k.astype(jnp.float32), axis=-1, keepdims=True) * v
    y = (y + bonus.reshape(b, s, RWKV_WIDTH)) * g.astype(jnp.float32)
    return y.astype(f.dtype)


def setup_inputs(seed: int = 0) -> dict:
    key = jax.random.key(seed)
    ks = jax.random.split(key, 32)
    L = DEPTH
    nrm = lambda k, shape, scale: jax.random.normal(k, shape, jnp.float32) * scale
    gain = lambda k, shape: 1.0 + 0.05 * jax.random.normal(k, shape, jnp.float32)
    return {
        "x": nrm(ks[0], (BATCH, SEQ, D_MODEL), 1.0),
        "rel_bias": nrm(ks[1], (N_BUCKETS, ATT_HEADS), 0.2),
        "norm_mix_pre": gain(ks[2], (L, D_MODEL)),
        "norm_mix_post": gain(ks[3], (L, D_MODEL)),
        "norm_ffn_pre": gain(ks[4], (L, D_MODEL)),
        "norm_ffn_post": gain(ks[5], (L, D_MODEL)),
        "w_in": nrm(ks[6], (L, D_MODEL, IN_COLS), D_MODEL ** -0.5),
        "b_gate": nrm(ks[7], (L, N_BRANCHES * D_MODEL), 0.02),
        "shift_mu": jax.random.uniform(ks[8], (L, RWKV_COLS), jnp.float32),
        "w0": jax.random.uniform(ks[9], (L, RWKV_WIDTH), jnp.float32, -6.0, 1.0),
        "w_w2": nrm(ks[10], (L, DECAY_LORA, RWKV_WIDTH), 0.1 * DECAY_LORA ** -0.5),
        "a0": nrm(ks[11], (L, RWKV_WIDTH), 0.1),
        "w_a2": nrm(ks[12], (L, ICLR_LORA, RWKV_WIDTH), 0.5 * ICLR_LORA ** -0.5),
        "w_g2": nrm(ks[13], (L, GATE_LORA, RWKV_WIDTH), GATE_LORA ** -0.5),
        "k_k": 0.85 + 0.05 * jax.random.normal(ks[14], (L, RWKV_WIDTH), jnp.float32),
        "k_a": gain(ks[15], (L, RWKV_WIDTH)),
        "r_k": nrm(ks[16], (L, RWKV_HEADS, HEAD_DIM), 0.1),
        "ln_x_w": gain(ks[17], (L, RWKV_WIDTH)),
        "ln_x_b": nrm(ks[18], (L, RWKV_WIDTH), 0.02),
        "w_att_branch": nrm(ks[19], (L, ATT_OUT_WIDTH, D_MODEL), ATT_OUT_WIDTH ** -0.5),
        "w_rwkv_branch": nrm(ks[20], (L, RWKV_WIDTH, D_MODEL), RWKV_WIDTH ** -0.5),
        "w_out": nrm(ks[21], (L, D_MODEL, D_MODEL), D_MODEL ** -0.5),
        "w_ffn1": nrm(ks[22], (L, D_MODEL, D_FF), D_MODEL ** -0.5),
        "w_ffn2": nrm(ks[23], (L, D_FF, D_MODEL), D_FF ** -0.5),
    }


def reference(x, rel_bias, norm_mix_pre, norm_mix_post, norm_ffn_pre, norm_ffn_post,
              w_in, b_gate, shift_mu, w0, w_w2, a0, w_a2, w_g2, k_k, k_a, r_k,
              ln_x_w, ln_x_b, w_att_branch, w_rwkv_branch, w_out, w_ffn1, w_ffn2):
    b, s, _ = x.shape
    for l in range(DEPTH):
        h = rmsnorm(x, norm_mix_pre[l])
        proj = h @ w_in[l]
        f_att, f_rwkv, f_gate = jnp.split(
            proj, [3 * ATT_WIDTH, 3 * ATT_WIDTH + RWKV_COLS], axis=-1)
        qkv = f_att.reshape(b, s, 3, N_GROUPS, HEADS_PER_GROUP, HEAD_DIM)
        o_att = dilated_window_attention(qkv[:, :, 0], qkv[:, :, 1], qkv[:, :, 2], rel_bias)
        o_rwkv = rwkv7_time_mix(token_shift(f_rwkv, shift_mu[l]), w0[l], w_w2[l], a0[l],
                                w_a2[l], w_g2[l], k_k[l], k_a[l], r_k[l],
                                ln_x_w[l], ln_x_b[l])
        gates = jax.nn.sigmoid(f_gate + b_gate[l]).reshape(b, s, N_BRANCHES, D_MODEL)
        merged = (gates[:, :, 0] * (o_att @ w_att_branch[l])
                  + gates[:, :, 1] * (o_rwkv @ w_rwkv_branch[l]))
        x = x + rmsnorm(merged @ w_out[l], norm_mix_post[l])
        h = rmsnorm(x, norm_ffn_pre[l])
        ff = jnp.square(jax.nn.relu(h @ w_ffn1[l])) @ w_ffn2[l]
        x = x + rmsnorm(ff, norm_ffn_post[l])
    return x
```

```python
import functools
import math

import jax
import jax.numpy as jnp
from jax import lax
from jax.experimental import pallas as pl
from jax.experimental.pallas import tpu as pltpu

D_MODEL = 1024
HEAD_DIM = 64
DILATIONS = (1, 4, 16)
KEYS_PER_QUERY = 128
N_GROUPS = len(DILATIONS)
HEADS_PER_GROUP = 4
ATT_HEADS = N_GROUPS * HEADS_PER_GROUP
ATT_WIDTH = ATT_HEADS * HEAD_DIM
ATT_OUT_WIDTH = HEADS_PER_GROUP * HEAD_DIM
N_BUCKETS = 32
MAX_DISTANCE = KEYS_PER_QUERY * DILATIONS[-1]
RWKV_WIDTH = D_MODEL
DECAY_LORA = 64
ICLR_LORA = 64
GATE_LORA = 128
RWKV_COLS = 3 * RWKV_WIDTH + DECAY_LORA + ICLR_LORA + GATE_LORA
N_BRANCHES = 2
IN_COLS = 3 * ATT_WIDTH + RWKV_COLS + N_BRANCHES * D_MODEL
D_FF = 4 * D_MODEL
NORM_EPS = 1e-6
LN_X_EPS = 64e-5
L2_EPS = 1e-12

LANES = 128
VMEM_LIMIT = 56 * 1024 * 1024
NEG = -1e30

GATE_COLS = N_BRANCHES * D_MODEL
COL_ROT = IN_COLS - GATE_COLS
N_SLABS = IN_COLS // LANES
GATE_SLAB0 = 0
Q_SLAB0 = GATE_COLS // LANES
K_SLAB0 = Q_SLAB0 + ATT_WIDTH // LANES
V_SLAB0 = K_SLAB0 + ATT_WIDTH // LANES
R_SLAB0 = V_SLAB0 + ATT_WIDTH // LANES
RK_SLAB0 = R_SLAB0 + RWKV_WIDTH // LANES
RV_SLAB0 = RK_SLAB0 + RWKV_WIDTH // LANES
LORA_SLAB = RV_SLAB0 + RWKV_WIDTH // LANES
FG_SLAB = LORA_SLAB + 1
N_HEAD_PAIRS = RWKV_WIDTH // LANES

PROJ_TN = 512
ATT_CHUNK = KEYS_PER_QUERY * DILATIONS[-1]
QB = KEYS_PER_QUERY
RWKV_CHUNK = 64
RWKV_T = 512


def _bf16(x):
    return x.astype(jnp.bfloat16)


def _dot(a, b):
    return jnp.dot(a, b, preferred_element_type=jnp.float32)


def _dot_nt(a, b):
    return lax.dot_general(a, b, (((1,), (1,)), ((), ())), preferred_element_type=jnp.float32)


def _dot_tn(a, b):
    return lax.dot_general(a, b, (((0,), (0,)), ((), ())), preferred_element_type=jnp.float32)


def _proj_body(x_ref, g_ref, w_ref, o_ref, h_scr):
    @pl.when(pl.program_id(1) == 0)
    def _():
        x = x_ref[...]
        ms = jnp.mean(x * x, axis=-1, keepdims=True)
        h_scr[...] = _bf16(x * lax.rsqrt(ms + NORM_EPS) * g_ref[...])

    acc = _dot(h_scr[...], w_ref[...])
    for s in range(PROJ_TN // LANES):
        o_ref[s] = acc[:, s * LANES:(s + 1) * LANES]


def _proj(x2, g, w_bf, tm):
    m = x2.shape[0]
    n_col_blocks = IN_COLS // PROJ_TN
    rot_blocks = COL_ROT // PROJ_TN
    return pl.pallas_call(
        _proj_body,
        out_shape=jax.ShapeDtypeStruct((N_SLABS, m, LANES), jnp.float32),
        grid=(m // tm, n_col_blocks),
        in_specs=[
            pl.BlockSpec((tm, D_MODEL), lambda i, j: (i, 0)),
            pl.BlockSpec((1, D_MODEL), lambda i, j: (0, 0)),
            pl.BlockSpec((D_MODEL, PROJ_TN), lambda i, j: (0, (j + rot_blocks) % n_col_blocks)),
        ],
        out_specs=pl.BlockSpec((PROJ_TN // LANES, tm, LANES), lambda i, j: (j, i, 0)),
        scratch_shapes=[pltpu.VMEM((tm, D_MODEL), jnp.bfloat16)],
        compiler_params=pltpu.CompilerParams(
            dimension_semantics=("parallel", "arbitrary"), vmem_limit_bytes=VMEM_LIMIT),
        name="proj",
    )(x2, g, w_bf)


def _attn_unit(d, q, k, v, bias0, bias1, lo):
    scale = HEAD_DIM ** -0.5
    qs = q * scale
    kb = _bf16(k)
    zero = jnp.zeros_like(qs)
    s0 = _dot_nt(_bf16(jnp.where(lo, qs, zero)), kb) + bias0
    s1 = _dot_nt(_bf16(jnp.where(lo, zero, qs)), kb) + bias1
    m0 = jnp.max(s0, axis=-1, keepdims=True)
    m1 = jnp.max(s1, axis=-1, keepdims=True)
    p0 = _bf16(jnp.exp(s0 - m0))
    p1 = _bf16(jnp.exp(s1 - m1))
    zv = jnp.zeros_like(v)
    ones = jnp.ones_like(v)
    rhs0 = _bf16(jnp.concatenate([jnp.where(lo, v, zv), jnp.where(lo, ones, zv)], axis=1))
    rhs1 = _bf16(jnp.concatenate([jnp.where(lo, zv, v), jnp.where(lo, zv, ones)], axis=1))
    ol = _dot(p0, rhs0) + _dot(p1, rhs1)
    o = ol[:, :LANES]
    l = ol[:, LANES:]
    m = jnp.where(lo, m0, m1)
    return o / l, m + jnp.log(l)


def _attn_body(q_ref, kc_ref, kp_ref, vc_ref, vp_ref, bias_ref, o_ref, o_scr, l_scr):
    c = pl.program_id(1)
    g = pl.program_id(2)
    lo = lax.broadcasted_iota(jnp.int32, (1, LANES), 1) < HEAD_DIM
    first = jnp.where(c == 0, 1, 0)

    def rows(start, n, d):
        return pl.ds(start, n) if d == 1 else pl.ds(start, n, stride=d)

    def group(gi):
        d = DILATIONS[gi]
        n_qb = ATT_CHUNK // (QB * d)

        def unit(p, start_q, k, v, variant):
            h0 = gi * HEADS_PER_GROUP + 2 * p
            q = q_ref[p, rows(start_q, QB, d), :]
            o, lse = _attn_unit(d, q, k, v, bias_ref[variant, h0], bias_ref[variant, h0 + 1], lo)
            o_scr[gi, p, rows(start_q, QB, d), :] = o
            l_scr[gi, p, rows(start_q, QB, d), :] = lse

        def first_blocks(r, carry):
            prev_start = r + QB * d * (n_qb - 1)
            for p in range(2):
                k = jnp.concatenate([kp_ref[p, rows(prev_start, QB, d), :], kc_ref[p, rows(r, QB, d), :]], axis=0)
                v = jnp.concatenate([vp_ref[p, rows(prev_start, QB, d), :], vc_ref[p, rows(r, QB, d), :]], axis=0)
                unit(p, r, k, v, first)
            return carry

        lax.fori_loop(0, d, first_blocks, 0)

        if n_qb > 1:
            def later_blocks(u, carry):
                r = u % d
                qb = 1 + u // d
                start_q = r + QB * d * qb
                for p in range(2):
                    k = kc_ref[p, rows(start_q - QB * d, 2 * QB, d), :]
                    v = vc_ref[p, rows(start_q - QB * d, 2 * QB, d), :]
                    unit(p, start_q, k, v, 0)
                return carry

            lax.fori_loop(0, d * (n_qb - 1), later_blocks, 0)

    for gi in range(N_GROUPS):
        pl.when(g == gi)(functools.partial(group, gi))

    @pl.when(g == N_GROUPS - 1)
    def _():
        tile = 256

        def comb(i, carry):
            rs = pl.ds(pl.multiple_of(i * tile, tile), tile)
            for p in range(2):
                ls = [l_scr[gi, p, rs, :] for gi in range(N_GROUPS)]
                mx = jnp.maximum(jnp.maximum(ls[0], ls[1]), ls[2])
                ws = [jnp.exp(l - mx) for l in ls]
                num = ws[0] * o_scr[0, p, rs, :] + ws[1] * o_scr[1, p, rs, :] + ws[2] * o_scr[2, p, rs, :]
                o_ref[rs, p * LANES:(p + 1) * LANES] = num / (ws[0] + ws[1] + ws[2])
            return carry

        lax.fori_loop(0, ATT_CHUNK // tile, comb, 0)


def _attn(slabs, bias_tiles, batch, seq):
    m = slabs.shape[1]
    n_chunks = seq // ATT_CHUNK
    blk = (2, ATT_CHUNK, LANES)

    def cur(slab0):
        return pl.BlockSpec(blk, lambda b, c, g: (slab0 // 2 + g, b * n_chunks + c, 0))

    def prev(slab0):
        return pl.BlockSpec(blk, lambda b, c, g: (slab0 // 2 + g, b * n_chunks + jnp.maximum(c - 1, 0), 0))

    return pl.pallas_call(
        _attn_body,
        out_shape=jax.ShapeDtypeStruct((m, ATT_OUT_WIDTH), jnp.float32),
        grid=(batch, n_chunks, N_GROUPS),
        in_specs=[cur(Q_SLAB0), cur(K_SLAB0), prev(K_SLAB0), cur(V_SLAB0), prev(V_SLAB0),
                  pl.BlockSpec(bias_tiles.shape, lambda b, c, g: (0, 0, 0, 0))],
        out_specs=pl.BlockSpec((ATT_CHUNK, ATT_OUT_WIDTH), lambda b, c, g: (b * n_chunks + c, 0)),
        scratch_shapes=[pltpu.VMEM((N_GROUPS, 2, ATT_CHUNK, LANES), jnp.float32),
                        pltpu.VMEM((N_GROUPS, 2, ATT_CHUNK, LANES), jnp.float32)],
        compiler_params=pltpu.CompilerParams(
            dimension_semantics=("parallel", "parallel", "arbitrary"), vmem_limit_bytes=VMEM_LIMIT),
        name="attn",
    )(slabs, slabs, slabs, slabs, slabs, bias_tiles)


def _t5_bucket(dist):
    max_exact = N_BUCKETS // 2
    d_f = jnp.maximum(dist, 1).astype(jnp.float32)
    large = max_exact + (jnp.log(d_f / max_exact) / math.log(MAX_DISTANCE / max_exact)
                         * (N_BUCKETS - max_exact)).astype(jnp.int32)
    large = jnp.minimum(large, N_BUCKETS - 1)
    return jnp.where(dist < max_exact, dist, large)


def _bias_tiles(rel_bias):
    dil = jnp.array(DILATIONS, jnp.int32)
    dist = dil[:, None] * jnp.arange(KEYS_PER_QUERY + 1, dtype=jnp.int32)[None, :]
    bucket = _t5_bucket(dist)
    bias = rel_bias.reshape(N_BUCKETS, N_GROUPS, HEADS_PER_GROUP)[bucket, jnp.arange(N_GROUPS)[:, None]]
    bias = jnp.transpose(bias, (0, 2, 1)).astype(jnp.float32).reshape(ATT_HEADS, KEYS_PER_QUERY + 1)
    i = jnp.arange(QB)[:, None]
    col = jnp.arange(2 * QB)[None, :]
    j = QB + i - col
    valid = (j >= 0) & (j <= KEYS_PER_QUERY)
    vals = bias[:, jnp.clip(j, 0, KEYS_PER_QUERY)]
    t0 = jnp.where(valid[None], vals, NEG)
    t1 = jnp.where((valid & (col >= QB))[None], vals, NEG)
    return jnp.stack([t0, t1])


def _rwkv_body(r_ref, k_ref, v_ref, lora_ref, fg_ref, pr_ref, sh_ref, wl_ref, o_ref,
               carry, st, u_s, bp_s, z_s, rt_s, gam_s, y_s):
    t_blk = r_ref.shape[1]
    n_chunks = t_blk // RWKV_CHUNK
    cc = RWKV_CHUNK

    @pl.when(pl.program_id(2) == 0)
    def _():
        carry[...] = jnp.zeros_like(carry)
        st[...] = jnp.zeros_like(st)

    row = lax.broadcasted_iota(jnp.int32, (t_blk, LANES), 0)
    pr = pr_ref[0]
    sh = sh_ref[...]

    def shifted(ref, idx, mu):
        x = ref[0]
        prev = pltpu.roll(x, 1, 0)
        prev = jnp.where(row == 0, carry[idx, 7:8, :], prev)
        carry[idx] = x[t_blk - 8:, :]
        return x + (prev - x) * mu

    r = shifted(r_ref, 0, pr[0:1])
    k = shifted(k_ref, 1, pr[1:2])
    v = shifted(v_ref, 2, pr[2:3])
    f_lora = shifted(lora_ref, 3, sh[0:1])
    f_g = shifted(fg_ref, 4, sh[1:2])
    w0, a0, k_k, k_a, r_k, ln_w, ln_b = (pr[i:i + 1] for i in range(3, 10))

    idx = lax.broadcasted_iota(jnp.int32, (LANES, LANES), 0)
    jdx = lax.broadcasted_iota(jnp.int32, (LANES, LANES), 1)
    same_head = (idx // HEAD_DIM) == (jdx // HEAD_DIM)
    head_ones = _bf16(jnp.where(same_head, 1.0, 0.0))

    def head_sum(x):
        hi = _bf16(x)
        lo_ = _bf16(x - hi.astype(jnp.float32))
        return _dot(hi, head_ones) + _dot(lo_, head_ones)

    wd = w0 + _dot(_bf16(jnp.tanh(f_lora)), wl_ref[0, 0])
    nwd = -wd
    softplus = jnp.maximum(nwd, 0.0) + jnp.log(1.0 + jnp.exp(-jnp.abs(nwd)))
    logw = -jnp.exp(-softplus - 0.5)
    a = jax.nn.sigmoid(a0 + _dot(_bf16(f_lora), wl_ref[0, 1]))
    gate = _dot(_bf16(jax.nn.sigmoid(f_g)), wl_ref[0, 2])
    kk = k * k_k
    kk = kk / jnp.maximum(jnp.sqrt(head_sum(kk * kk)), L2_EPS)
    k2 = k * (1.0 + (a - 1.0) * k_a)
    bb = kk * a

    lane_lo = lax.broadcasted_iota(jnp.int32, (1, LANES), 1) < HEAD_DIM

    def stack(x):
        z = jnp.zeros_like(x)
        return jnp.concatenate([jnp.where(lane_lo, x, z), jnp.where(lane_lo, z, x)], axis=0)

    ti = idx % cc
    tj = jdx % cc
    strict_lower = same_head & (ti > tj)
    lower = same_head & (ti >= tj)
    eye = idx == jdx
    ci = lax.broadcasted_iota(jnp.int32, (cc, cc), 0)
    cj = lax.broadcasted_iota(jnp.int32, (cc, cc), 1)
    tri = _bf16(jnp.where(ci >= cj, 1.0, 0.0))
    zero_m = jnp.zeros((LANES, LANES), jnp.float32)

    def cumsum(x):
        hi = _bf16(x)
        r1 = x - hi.astype(jnp.float32)
        mid = _bf16(r1)
        lo_ = _bf16(r1 - mid.astype(jnp.float32))
        return _dot(tri, hi) + _dot(tri, mid) + _dot(tri, lo_)

    def unit_lower_inverse(n):
        x = jnp.where(eye, 1.0, 0.0) - jnp.where((idx // 2) == (jdx // 2), n, zero_m)
        blk = 4
        while blk <= cc:
            half = blk // 2
            sel = ((idx // blk) == (jdx // blk)) & ((idx % blk) >= half) & ((jdx % blk) < half)
            xb = _bf16(x)
            x = x - _dot(_bf16(_dot(xb, _bf16(jnp.where(sel, n, zero_m)))), xb)
            blk *= 2
        return x

    for c in range(n_chunks):
        sl = slice(c * cc, (c + 1) * cc)
        lw = logw[sl]
        big_l = cumsum(lw)
        l_end = big_l[cc - 1:cc]
        e_l = jnp.exp(big_l)
        e_lm = jnp.exp(big_l - lw)
        e_nl = jnp.exp(-big_l)
        e_c = jnp.exp(l_end - big_l)
        a_kk = _bf16(stack(kk[sl] * e_lm))
        a_r_f = stack(r[sl] * e_l)
        a_r = _bf16(a_r_f)
        kd = stack(k2[sl] * e_nl)
        bd = stack(bb[sl] * e_nl)
        kc = stack(k2[sl] * e_c)
        bc = stack(bb[sl] * e_c)
        vs = stack(v[sl])
        kb = _bf16(jnp.concatenate([kd, bd], axis=0))
        s1 = _dot_nt(a_kk, kb)
        s2 = _dot_nt(a_r, kb)
        mk = jnp.where(strict_lower, s1[:, :LANES], zero_m)
        mb = jnp.where(strict_lower, s1[:, LANES:], zero_m)
        nk = jnp.where(lower, s2[:, :LANES], zero_m)
        nb = jnp.where(lower, s2[:, LANES:], zero_m)
        tinv = _bf16(unit_lower_inverse(mb))
        pm = _dot(tinv, a_kk)
        q = _dot(tinv, _bf16(_dot(_bf16(mk), _bf16(vs))))
        pm_b = _bf16(pm)
        vq = _bf16(jnp.concatenate([vs, q], axis=0))
        bp_s[c] = _bf16(_dot_tn(_bf16(bc), pm_b))
        rt_s[c] = _dot_tn(vq, _bf16(jnp.concatenate([kc, -bc], axis=0)))
        u_s[c] = _bf16(a_r_f - _dot(_bf16(nb), pm_b))
        z_s[c] = _dot(_bf16(jnp.concatenate([nk, -nb], axis=1)), vq)
        gam_s[c] = jnp.broadcast_to(jnp.exp(l_end), (8, LANES))

    for c in range(n_chunks):
        s_f = st[...]
        s_b = _bf16(s_f)
        y_st = _dot_nt(u_s[c], s_b) + z_s[c]
        y_s[c * cc:(c + 1) * cc, :] = y_st[:cc] + y_st[cc:]
        st[...] = s_f * gam_s[c, 0:1, :] - _dot_nt(s_b, bp_s[c]) + rt_s[c]

    y = y_s[...]
    mu = head_sum(y) * (1.0 / HEAD_DIM)
    yc = y - mu
    var = head_sum(yc * yc) * (1.0 / HEAD_DIM)
    yn = yc * lax.rsqrt(var + LN_X_EPS) * ln_w + ln_b
    bonus = head_sum(r * k2 * r_k) * v
    o_ref[...] = (yn + bonus) * gate


def _rwkv(slabs, pr, sh, wl, batch, seq):
    m = slabs.shape[1]
    t_blk = RWKV_T
    n_t = seq // t_blk
    n_c = t_blk // RWKV_CHUNK

    def tok(slab0, per_pair):
        if per_pair:
            return pl.BlockSpec((1, t_blk, LANES), lambda b, hp, t: (slab0 + hp, b * n_t + t, 0))
        return pl.BlockSpec((1, t_blk, LANES), lambda b, hp, t: (slab0, b * n_t + t, 0))

    return pl.pallas_call(
        _rwkv_body,
        out_shape=jax.ShapeDtypeStruct((m, RWKV_WIDTH), jnp.float32),
        grid=(batch, N_HEAD_PAIRS, n_t),
        in_specs=[tok(R_SLAB0, True), tok(RK_SLAB0, True), tok(RV_SLAB0, True),
                  tok(LORA_SLAB, False), tok(FG_SLAB, False),
                  pl.BlockSpec((1,) + pr.shape[1:], lambda b, hp, t: (hp, 0, 0)),
                  pl.BlockSpec(sh.shape, lambda b, hp, t: (0, 0)),
                  pl.BlockSpec((1,) + wl.shape[1:], lambda b, hp, t: (hp, 0, 0, 0))],
        out_specs=pl.BlockSpec((t_blk, LANES), lambda b, hp, t: (b * n_t + t, hp)),
        scratch_shapes=[
            pltpu.VMEM((5, 8, LANES), jnp.float32),
            pltpu.VMEM((LANES, LANES), jnp.float32),
            pltpu.VMEM((n_c, LANES, LANES), jnp.bfloat16),
            pltpu.VMEM((n_c, LANES, LANES), jnp.bfloat16),
            pltpu.VMEM((n_c, LANES, LANES), jnp.float32),
            pltpu.VMEM((n_c, LANES, LANES), jnp.float32),
            pltpu.VMEM((n_c, 8, LANES), jnp.float32),
            pltpu.VMEM((t_blk, LANES), jnp.float32),
        ],
        compiler_params=pltpu.CompilerParams(
            dimension_semantics=("parallel", "parallel", "arbitrary"), vmem_limit_bytes=VMEM_LIMIT),
        name="rwkv",
    )(slabs, slabs, slabs, slabs, slabs, pr, sh, wl)


def _rwkv_params(shift_mu, w0, a0, k_k, k_a, r_k, ln_w, ln_b, w_w2, w_a2, w_g2):
    def pairs(vec):
        return vec.reshape(N_HEAD_PAIRS, LANES)

    mu_r, mu_k, mu_v = (pairs(shift_mu[i * RWKV_WIDTH:(i + 1) * RWKV_WIDTH]) for i in range(3))
    rows = [mu_r, mu_k, mu_v, pairs(w0), pairs(a0), pairs(k_k), pairs(k_a), pairs(r_k.reshape(-1)),
            pairs(ln_w), pairs(ln_b)]
    pr = jnp.stack(rows, axis=1)
    pr = jnp.pad(pr, ((0, 0), (0, 16 - pr.shape[1]), (0, 0)))
    mu_rest = shift_mu[3 * RWKV_WIDTH:]
    sh = jnp.pad(mu_rest.reshape(2, LANES), ((0, 6), (0, 0)))
    zeros = jnp.zeros((DECAY_LORA, RWKV_WIDTH), jnp.float32)
    wd = jnp.concatenate([w_w2, zeros], axis=0)
    wa = jnp.concatenate([zeros, w_a2], axis=0)
    wl = jnp.stack([wd, wa, w_g2], axis=0)
    wl = wl.reshape(3, LANES, N_HEAD_PAIRS, LANES).transpose(2, 0, 1, 3)
    return pr, sh, _bf16(wl)


def _rms(x, g):
    ms = jnp.mean(x * x, axis=-1, keepdims=True)
    return x * lax.rsqrt(ms + NORM_EPS) * g


def _merge_body(x_ref, att_ref, rw_ref, gate_ref, bg_ref, wa_ref, wr_ref, wo_ref, gn_ref, o_ref):
    n_gs = D_MODEL // LANES
    f0 = jnp.concatenate([gate_ref[s] for s in range(n_gs)], axis=1)
    f1 = jnp.concatenate([gate_ref[n_gs + s] for s in range(n_gs)], axis=1)
    g0 = jax.nn.sigmoid(f0 + bg_ref[:, :D_MODEL])
    g1 = jax.nn.sigmoid(f1 + bg_ref[:, D_MODEL:])
    merged = g0 * _dot(_bf16(att_ref[...]), wa_ref[...]) + g1 * _dot(_bf16(rw_ref[...]), wr_ref[...])
    z = _dot(_bf16(merged), wo_ref[...])
    o_ref[...] = x_ref[...] + _rms(z, gn_ref[...])


def _merge(x2, o_att, o_rwkv, slabs, b_gate, wa, wr, wo, g_post, tm):
    m = x2.shape[0]
    n_gate_slabs = GATE_COLS // LANES
    const = lambda shape: pl.BlockSpec(shape, lambda i: (0, 0))
    return pl.pallas_call(
        _merge_body,
        out_shape=jax.ShapeDtypeStruct((m, D_MODEL), jnp.float32),
        grid=(m // tm,),
        in_specs=[
            pl.BlockSpec((tm, D_MODEL), lambda i: (i, 0)),
            pl.BlockSpec((tm, ATT_OUT_WIDTH), lambda i: (i, 0)),
            pl.BlockSpec((tm, RWKV_WIDTH), lambda i: (i, 0)),
            pl.BlockSpec((n_gate_slabs, tm, LANES), lambda i: (GATE_SLAB0 // n_gate_slabs, i, 0)),
            const((1, GATE_COLS)), const(wa.shape), const(wr.shape), const(wo.shape), const((1, D_MODEL)),
        ],
        out_specs=pl.BlockSpec((tm, D_MODEL), lambda i: (i, 0)),
        compiler_params=pltpu.CompilerParams(
            dimension_semantics=("parallel",), vmem_limit_bytes=VMEM_LIMIT),
        name="merge",
    )(x2, o_att, o_rwkv, slabs, b_gate, wa, wr, wo, g_post)


def _ffn_body(x_ref, gpre_ref, w1_ref, w2_ref, gpost_ref, o_ref, *, tf):
    x = x_ref[...]
    h = _bf16(_rms(x, gpre_ref[...]))
    acc = jnp.zeros(x.shape, jnp.float32)
    for c in range(D_FF // tf):
        u = jnp.maximum(_dot(h, w1_ref[:, c * tf:(c + 1) * tf]), 0.0)
        acc = acc + _dot(_bf16(u * u), w2_ref[c * tf:(c + 1) * tf, :])
    o_ref[...] = x + _rms(acc, gpost_ref[...])


def _ffn(x2, g_pre, w1, w2, g_post, tm, tf=1024):
    m = x2.shape[0]
    const = lambda shape: pl.BlockSpec(shape, lambda i: (0, 0))
    return pl.pallas_call(
        functools.partial(_ffn_body, tf=tf),
        out_shape=jax.ShapeDtypeStruct((m, D_MODEL), jnp.float32),
        grid=(m // tm,),
        in_specs=[pl.BlockSpec((tm, D_MODEL), lambda i: (i, 0)), const((1, D_MODEL)),
                  const(w1.shape), const(w2.shape), const((1, D_MODEL))],
        out_specs=pl.BlockSpec((tm, D_MODEL), lambda i: (i, 0)),
        compiler_params=pltpu.CompilerParams(
            dimension_semantics=("parallel",), vmem_limit_bytes=VMEM_LIMIT),
        name="ffn",
    )(x2, g_pre, w1, w2, g_post)


def kernel(x, rel_bias, norm_mix_pre, norm_mix_post, norm_ffn_pre, norm_ffn_post, w_in, b_gate, shift_mu, w0, w_w2, a0, w_a2, w_g2, k_k, k_a, r_k, ln_x_w, ln_x_b, w_att_branch, w_rwkv_branch, w_out, w_ffn1, w_ffn2):
    batch, seq, d_model = x.shape
    assert d_model == D_MODEL and seq % ATT_CHUNK == 0 and seq % RWKV_T == 0
    m = batch * seq
    tm = 512
    assert m % tm == 0
    bias_tiles = _bias_tiles(rel_bias)
    row = lambda vec: vec.reshape(1, -1)
    x2 = x.reshape(m, D_MODEL)
    for l in range(w_in.shape[0]):
        slabs = _proj(x2, row(norm_mix_pre[l]), _bf16(w_in[l]), tm)
        o_att = _attn(slabs, bias_tiles, batch, seq)
        pr, sh, wl = _rwkv_params(shift_mu[l], w0[l], a0[l], k_k[l], k_a[l], r_k[l], ln_x_w[l], ln_x_b[l],
                                  w_w2[l], w_a2[l], w_g2[l])
        o_rwkv = _rwkv(slabs, pr, sh, wl, batch, seq)
        x2 = _merge(x2, o_att, o_rwkv, slabs, row(b_gate[l]), _bf16(w_att_branch[l]), _bf16(w_rwkv_branch[l]),
                    _bf16(w_out[l]), row(norm_mix_post[l]), tm)
        x2 = _ffn(x2, row(norm_ffn_pre[l]), _bf16(w_ffn1[l]), _bf16(w_ffn2[l]), row(norm_ffn_post[l]), tm)
    return x2.reshape(batch, seq, D_MODEL)
```

```python
import functools
import math

import jax
import jax.numpy as jnp
from jax import lax
from jax.experimental import pallas as pl
from jax.experimental.pallas import tpu as pltpu

D_MODEL = 1024
HEAD_DIM = 64
DILATIONS = (1, 4, 16)
KEYS_PER_QUERY = 128
N_GROUPS = len(DILATIONS)
HEADS_PER_GROUP = 4
ATT_HEADS = N_GROUPS * HEADS_PER_GROUP
ATT_WIDTH = ATT_HEADS * HEAD_DIM
ATT_OUT_WIDTH = HEADS_PER_GROUP * HEAD_DIM
N_BUCKETS = 32
MAX_DISTANCE = KEYS_PER_QUERY * DILATIONS[-1]
RWKV_WIDTH = D_MODEL
DECAY_LORA = 64
ICLR_LORA = 64
GATE_LORA = 128
RWKV_COLS = 3 * RWKV_WIDTH + DECAY_LORA + ICLR_LORA + GATE_LORA
N_BRANCHES = 2
IN_COLS = 3 * ATT_WIDTH + RWKV_COLS + N_BRANCHES * D_MODEL
D_FF = 4 * D_MODEL
NORM_EPS = 1e-6
LN_X_EPS = 64e-5
L2_EPS = 1e-12

LANES = 128
VMEM_LIMIT = 56 * 1024 * 1024
NEG = -1e30

GATE_COLS = N_BRANCHES * D_MODEL
COL_ROT = IN_COLS - GATE_COLS
N_SLABS = IN_COLS // LANES
GATE_SLAB0 = 0
Q_SLAB0 = GATE_COLS // LANES
K_SLAB0 = Q_SLAB0 + ATT_WIDTH // LANES
V_SLAB0 = K_SLAB0 + ATT_WIDTH // LANES
R_SLAB0 = V_SLAB0 + ATT_WIDTH // LANES
RK_SLAB0 = R_SLAB0 + RWKV_WIDTH // LANES
RV_SLAB0 = RK_SLAB0 + RWKV_WIDTH // LANES
LORA_SLAB = RV_SLAB0 + RWKV_WIDTH // LANES
FG_SLAB = LORA_SLAB + 1
N_HEAD_PAIRS = RWKV_WIDTH // LANES

PROJ_TN = 512
ATT_CHUNK = KEYS_PER_QUERY * DILATIONS[-1]
QB = KEYS_PER_QUERY
RWKV_CHUNK = 64
RWKV_T = 512


def _bf16(x):
    return x.astype(jnp.bfloat16)


def _dot(a, b):
    return jnp.dot(a, b, preferred_element_type=jnp.float32)


def _dot_nt(a, b):
    return lax.dot_general(a, b, (((1,), (1,)), ((), ())), preferred_element_type=jnp.float32)


def _dot_tn(a, b):
    return lax.dot_general(a, b, (((0,), (0,)), ((), ())), preferred_element_type=jnp.float32)


def _proj_body(x_ref, g_ref, w_ref, o_ref, h_scr):
    @pl.when(pl.program_id(1) == 0)
    def _():
        x = x_ref[...]
        ms = jnp.mean(x * x, axis=-1, keepdims=True)
        h_scr[...] = _bf16(x * lax.rsqrt(ms + NORM_EPS) * g_ref[...])

    acc = _dot(h_scr[...], w_ref[...])
    for s in range(PROJ_TN // LANES):
        o_ref[s] = acc[:, s * LANES:(s + 1) * LANES]


def _proj(x2, g, w_bf, tm):
    m = x2.shape[0]
    n_col_blocks = IN_COLS // PROJ_TN
    rot_blocks = COL_ROT // PROJ_TN
    return pl.pallas_call(
        _proj_body,
        out_shape=jax.ShapeDtypeStruct((N_SLABS, m, LANES), jnp.float32),
        grid=(m // tm, n_col_blocks),
        in_specs=[
            pl.BlockSpec((tm, D_MODEL), lambda i, j: (i, 0)),
            pl.BlockSpec((1, D_MODEL), lambda i, j: (0, 0)),
            pl.BlockSpec((D_MODEL, PROJ_TN), lambda i, j: (0, (j + rot_blocks) % n_col_blocks)),
        ],
        out_specs=pl.BlockSpec((PROJ_TN // LANES, tm, LANES), lambda i, j: (j, i, 0)),
        scratch_shapes=[pltpu.VMEM((tm, D_MODEL), jnp.bfloat16)],
        compiler_params=pltpu.CompilerParams(
            dimension_semantics=("parallel", "arbitrary"), vmem_limit_bytes=VMEM_LIMIT),
        name="proj",
    )(x2, g, w_bf)


def _attn_unit(d, q, k, v, bias0, bias1, lo):
    scale = HEAD_DIM ** -0.5
    qs = q * scale
    kb = _bf16(k)
    zero = jnp.zeros_like(qs)
    s0 = _dot_nt(_bf16(jnp.where(lo, qs, zero)), kb) + bias0
    s1 = _dot_nt(_bf16(jnp.where(lo, zero, qs)), kb) + bias1
    m0 = jnp.max(s0, axis=-1, keepdims=True)
    m1 = jnp.max(s1, axis=-1, keepdims=True)
    p0 = _bf16(jnp.exp(s0 - m0))
    p1 = _bf16(jnp.exp(s1 - m1))
    zv = jnp.zeros_like(v)
    ones = jnp.ones_like(v)
    rhs0 = _bf16(jnp.concatenate([jnp.where(lo, v, zv), jnp.where(lo, ones, zv)], axis=1))
    rhs1 = _bf16(jnp.concatenate([jnp.where(lo, zv, v), jnp.where(lo, zv, ones)], axis=1))
    ol = _dot(p0, rhs0) + _dot(p1, rhs1)
    o = ol[:, :LANES]
    l = ol[:, LANES:]
    m = jnp.where(lo, m0, m1)
    return o / l, m + jnp.log(l)


def _attn_body(q_ref, kc_ref, kp_ref, vc_ref, vp_ref, bias_ref, o_ref, o_scr, l_scr):
    c = pl.program_id(1)
    g = pl.program_id(2)
    lo = lax.broadcasted_iota(jnp.int32, (1, LANES), 1) < HEAD_DIM
    first = jnp.where(c == 0, 1, 0)

    def rows(start, n, d):
        return pl.ds(start, n) if d == 1 else pl.ds(start, n, stride=d)

    def group(gi):
        d = DILATIONS[gi]
        n_qb = ATT_CHUNK // (QB * d)

        def unit(p, start_q, k, v, variant):
            h0 = gi * HEADS_PER_GROUP + 2 * p
            q = q_ref[p, rows(start_q, QB, d), :]
            o, lse = _attn_unit(d, q, k, v, bias_ref[variant, h0], bias_ref[variant, h0 + 1], lo)
            o_scr[gi, p, rows(start_q, QB, d), :] = o
            l_scr[gi, p, rows(start_q, QB, d), :] = lse

        def first_blocks(r, carry):
            prev_start = r + QB * d * (n_qb - 1)
            for p in range(2):
                k = jnp.concatenate([kp_ref[p, rows(prev_start, QB, d), :], kc_ref[p, rows(r, QB, d), :]], axis=0)
                v = jnp.concatenate([vp_ref[p, rows(prev_start, QB, d), :], vc_ref[p, rows(r, QB, d), :]], axis=0)
                unit(p, r, k, v, first)
            return carry

        lax.fori_loop(0, d, first_blocks, 0)

        if n_qb > 1:
            def later_blocks(u, carry):
                r = u % d
                qb = 1 + u // d
                start_q = r + QB * d * qb
                for p in range(2):
                    k = kc_ref[p, rows(start_q - QB * d, 2 * QB, d), :]
                    v = vc_ref[p, rows(start_q - QB * d, 2 * QB, d), :]
                    unit(p, start_q, k, v, 0)
                return carry

            lax.fori_loop(0, d * (n_qb - 1), later_blocks, 0)

    for gi in range(N_GROUPS):
        pl.when(g == gi)(functools.partial(group, gi))

    @pl.when(g == N_GROUPS - 1)
    def _():
        tile = 256

        def comb(i, carry):
            rs = pl.ds(pl.multiple_of(i * tile, tile), tile)
            for p in range(2):
                ls = [l_scr[gi, p, rs, :] for gi in range(N_GROUPS)]
                mx = jnp.maximum(jnp.maximum(ls[0], ls[1]), ls[2])
                ws = [jnp.exp(l - mx) for l in ls]
                num = ws[0] * o_scr[0, p, rs, :] + ws[1] * o_scr[1, p, rs, :] + ws[2] * o_scr[2, p, rs, :]
                o_ref[rs, p * LANES:(p + 1) * LANES] = num / (ws[0] + ws[1] + ws[2])
            return carry

        lax.fori_loop(0, ATT_CHUNK // tile, comb, 0)


def _attn(slabs, bias_tiles, batch, seq):
    m = slabs.shape[1]
    n_chunks = seq // ATT_CHUNK
    blk = (2, ATT_CHUNK, LANES)

    def cur(slab0):
        return pl.BlockSpec(blk, lambda b, c, g: (slab0 // 2 + g, b * n_chunks + c, 0))

    def prev(slab0):
        return pl.BlockSpec(blk, lambda b, c, g: (slab0 // 2 + g, b * n_chunks + jnp.maximum(c - 1, 0), 0))

    return pl.pallas_call(
        _attn_body,
        out_shape=jax.ShapeDtypeStruct((m, ATT_OUT_WIDTH), jnp.float32),
        grid=(batch, n_chunks, N_GROUPS),
        in_specs=[cur(Q_SLAB0), cur(K_SLAB0), prev(K_SLAB0), cur(V_SLAB0), prev(V_SLAB0),
                  pl.BlockSpec(bias_tiles.shape, lambda b, c, g: (0, 0, 0, 0))],
        out_specs=pl.BlockSpec((ATT_CHUNK, ATT_OUT_WIDTH), lambda b, c, g: (b * n_chunks + c, 0)),
        scratch_shapes=[pltpu.VMEM((N_GROUPS, 2, ATT_CHUNK, LANES), jnp.float32),
                        pltpu.VMEM((N_GROUPS, 2, ATT_CHUNK, LANES), jnp.float32)],
        compiler_params=pltpu.CompilerParams(
            dimension_semantics=("parallel", "parallel", "arbitrary"), vmem_limit_bytes=VMEM_LIMIT),
        name="attn",
    )(slabs, slabs, slabs, slabs, slabs, bias_tiles)


def _t5_bucket(dist):
    max_exact = N_BUCKETS // 2
    d_f = jnp.maximum(dist, 1).astype(jnp.float32)
    large = max_exact + (jnp.log(d_f / max_exact) / math.log(MAX_DISTANCE / max_exact)
                         * (N_BUCKETS - max_exact)).astype(jnp.int32)
    large = jnp.minimum(large, N_BUCKETS - 1)
    return jnp.where(dist < max_exact, dist, large)


def _bias_tiles(rel_bias):
    dil = jnp.array(DILATIONS, jnp.int32)
    dist = dil[:, None] * jnp.arange(KEYS_PER_QUERY + 1, dtype=jnp.int32)[None, :]
    bucket = _t5_bucket(dist)
    bias = rel_bias.reshape(N_BUCKETS, N_GROUPS, HEADS_PER_GROUP)[bucket, jnp.arange(N_GROUPS)[:, None]]
    bias = jnp.transpose(bias, (0, 2, 1)).astype(jnp.float32).reshape(ATT_HEADS, KEYS_PER_QUERY + 1)
    i = jnp.arange(QB)[:, None]
    col = jnp.arange(2 * QB)[None, :]
    j = QB + i - col
    valid = (j >= 0) & (j <= KEYS_PER_QUERY)
    vals = bias[:, jnp.clip(j, 0, KEYS_PER_QUERY)]
    t0 = jnp.where(valid[None], vals, NEG)
    t1 = jnp.where((valid & (col >= QB))[None], vals, NEG)
    return jnp.stack([t0, t1])


def _rwkv_body(r_ref, k_ref, v_ref, lora_ref, fg_ref, pr_ref, sh_ref, wl_ref, o_ref,
               carry, st, u_s, bp_s, z_s, rt_s, gam_s, y_s):
    t_blk = r_ref.shape[1]
    n_chunks = t_blk // RWKV_CHUNK
    cc = RWKV_CHUNK

    @pl.when(pl.program_id(2) == 0)
    def _():
        carry[...] = jnp.zeros_like(carry)
        st[...] = jnp.zeros_like(st)

    row = lax.broadcasted_iota(jnp.int32, (t_blk, LANES), 0)
    pr = pr_ref[0]
    sh = sh_ref[...]

    def shifted(ref, idx, mu):
        x = ref[0]
        prev = pltpu.roll(x, 1, 0)
        prev = jnp.where(row == 0, carry[idx, 7:8, :], prev)
        carry[idx] = x[t_blk - 8:, :]
        return x + (prev - x) * mu

    r = shifted(r_ref, 0, pr[0:1])
    k = shifted(k_ref, 1, pr[1:2])
    v = shifted(v_ref, 2, pr[2:3])
    f_lora = shifted(lora_ref, 3, sh[0:1])
    f_g = shifted(fg_ref, 4, sh[1:2])
    w0, a0, k_k, k_a, r_k, ln_w, ln_b = (pr[i:i + 1] for i in range(3, 10))

    idx = lax.broadcasted_iota(jnp.int32, (LANES, LANES), 0)
    jdx = lax.broadcasted_iota(jnp.int32, (LANES, LANES), 1)
    same_head = (idx // HEAD_DIM) == (jdx // HEAD_DIM)
    head_ones = _bf16(jnp.where(same_head, 1.0, 0.0))

    def head_sum(x):
        hi = _bf16(x)
        lo_ = _bf16(x - hi.astype(jnp.float32))
        return _dot(hi, head_ones) + _dot(lo_, head_ones)

    wd = w0 + _dot(_bf16(jnp.tanh(f_lora)), wl_ref[0, 0])
    nwd = -wd
    softplus = jnp.maximum(nwd, 0.0) + jnp.log(1.0 + jnp.exp(-jnp.abs(nwd)))
    logw = -jnp.exp(-softplus - 0.5)
    a = jax.nn.sigmoid(a0 + _dot(_bf16(f_lora), wl_ref[0, 1]))
    gate = _dot(_bf16(jax.nn.sigmoid(f_g)), wl_ref[0, 2])
    kk = k * k_k
    kk = kk / jnp.maximum(jnp.sqrt(head_sum(kk * kk)), L2_EPS)
    k2 = k * (1.0 + (a - 1.0) * k_a)
    bb = kk * a

    lane_lo = lax.broadcasted_iota(jnp.int32, (1, LANES), 1) < HEAD_DIM

    def stack(x):
        z = jnp.zeros_like(x)
        return jnp.concatenate([jnp.where(lane_lo, x, z), jnp.where(lane_lo, z, x)], axis=0)

    ti = idx % cc
    tj = jdx % cc
    strict_lower = same_head & (ti > tj)
    lower = same_head & (ti >= tj)
    eye = idx == jdx
    ci = lax.broadcasted_iota(jnp.int32, (cc, cc), 0)
    cj = lax.broadcasted_iota(jnp.int32, (cc, cc), 1)
    tri = _bf16(jnp.where(ci >= cj, 1.0, 0.0))
    zero_m = jnp.zeros((LANES, LANES), jnp.float32)

    cs = range(n_chunks)

    def each(fn, *lists):
        return [fn(*args) for args in zip(*lists)]

    def split3(x):
        hi = _bf16(x)
        r1 = x - hi.astype(jnp.float32)
        mid = _bf16(r1)
        return hi, mid, _bf16(r1 - mid.astype(jnp.float32))

    lw = [logw[c * cc:(c + 1) * cc] for c in cs]
    parts = each(split3, lw)
    big_l = each(lambda p: _dot(tri, p[0]) + _dot(tri, p[1]) + _dot(tri, p[2]), parts)
    l_end = each(lambda l: l[cc - 1:cc], big_l)
    e_l = each(jnp.exp, big_l)
    e_lm = each(lambda l, w: jnp.exp(l - w), big_l, lw)
    e_nl = each(lambda l: jnp.exp(-l), big_l)
    e_c = each(lambda le, l: jnp.exp(le - l), l_end, big_l)
    rows_of = lambda x: [x[c * cc:(c + 1) * cc] for c in cs]
    a_kk = each(lambda x, e: _bf16(stack(x * e)), rows_of(kk), e_lm)
    a_r_f = each(lambda x, e: stack(x * e), rows_of(r), e_l)
    a_r = each(_bf16, a_r_f)
    kb = each(lambda xk, xb, e: _bf16(jnp.concatenate([stack(xk * e), stack(xb * e)], axis=0)),
              rows_of(k2), rows_of(bb), e_nl)
    kcb = each(lambda xk, xb, e: _bf16(jnp.concatenate([stack(xk * e), stack(-xb * e)], axis=0)),
               rows_of(k2), rows_of(bb), e_c)
    bc = each(lambda xb, e: _bf16(stack(xb * e)), rows_of(bb), e_c)
    vs = each(lambda x: _bf16(stack(x)), rows_of(v))
    s1 = each(_dot_nt, a_kk, kb)
    s2 = each(_dot_nt, a_r, kb)
    mk = each(lambda s: _bf16(jnp.where(strict_lower, s[:, :LANES], zero_m)), s1)
    mb = each(lambda s: jnp.where(strict_lower, s[:, LANES:], zero_m), s1)
    nkb = each(lambda s: _bf16(jnp.concatenate([jnp.where(lower, s[:, :LANES], zero_m),
                                                 jnp.where(lower, -s[:, LANES:], zero_m)], axis=1)), s2)

    eye_f = jnp.where(eye, 1.0, 0.0)
    leaf = (idx // 2) == (jdx // 2)
    x = each(lambda n: eye_f - jnp.where(leaf, n, zero_m), mb)
    blk = 4
    while blk <= cc:
        half = blk // 2
        sel = ((idx // blk) == (jdx // blk)) & ((idx % blk) >= half) & ((jdx % blk) < half)
        xb = each(_bf16, x)
        t1 = each(lambda xc, n: _bf16(_dot(xc, _bf16(jnp.where(sel, n, zero_m)))), xb, mb)
        x = each(lambda xf, t, xc: xf - _dot(t, xc), x, t1, xb)
        blk *= 2
    tinv = each(_bf16, x)

    pm = each(lambda t, ak: _bf16(_dot(t, ak)), tinv, a_kk)
    mkv = each(lambda m_, v_: _bf16(_dot(m_, v_)), mk, vs)
    q = each(lambda t, m_: _bf16(_dot(t, m_)), tinv, mkv)
    vq = each(lambda v_, q_: jnp.concatenate([v_, q_], axis=0), vs, q)
    for c in cs:
        bp_s[c] = _bf16(_dot_tn(bc[c], pm[c]))
    for c in cs:
        rt_s[c] = _dot_tn(vq[c], kcb[c])
    for c in cs:
        u_s[c] = _bf16(a_r_f[c] + _dot(nkb[c][:, LANES:], pm[c]))
    for c in cs:
        z_s[c] = _dot(nkb[c], vq[c])
    for c in cs:
        gam_s[c] = jnp.broadcast_to(jnp.exp(l_end[c]), (8, LANES))

    for c in range(n_chunks):
        s_f = st[...]
        s_b = _bf16(s_f)
        y_st = _dot_nt(u_s[c], s_b) + z_s[c]
        y_s[c * cc:(c + 1) * cc, :] = y_st[:cc] + y_st[cc:]
        st[...] = s_f * gam_s[c, 0:1, :] - _dot_nt(s_b, bp_s[c]) + rt_s[c]

    y = y_s[...]
    mu = head_sum(y) * (1.0 / HEAD_DIM)
    yc = y - mu
    var = head_sum(yc * yc) * (1.0 / HEAD_DIM)
    yn = yc * lax.rsqrt(var + LN_X_EPS) * ln_w + ln_b
    bonus = head_sum(r * k2 * r_k) * v
    o_ref[...] = (yn + bonus) * gate


def _rwkv(slabs, pr, sh, wl, batch, seq):
    m = slabs.shape[1]
    t_blk = RWKV_T
    n_t = seq // t_blk
    n_c = t_blk // RWKV_CHUNK

    def tok(slab0, per_pair):
        if per_pair:
            return pl.BlockSpec((1, t_blk, LANES), lambda b, hp, t: (slab0 + hp, b * n_t + t, 0))
        return pl.BlockSpec((1, t_blk, LANES), lambda b, hp, t: (slab0, b * n_t + t, 0))

    return pl.pallas_call(
        _rwkv_body,
        out_shape=jax.ShapeDtypeStruct((m, RWKV_WIDTH), jnp.float32),
        grid=(batch, N_HEAD_PAIRS, n_t),
        in_specs=[tok(R_SLAB0, True), tok(RK_SLAB0, True), tok(RV_SLAB0, True),
                  tok(LORA_SLAB, False), tok(FG_SLAB, False),
                  pl.BlockSpec((1,) + pr.shape[1:], lambda b, hp, t: (hp, 0, 0)),
                  pl.BlockSpec(sh.shape, lambda b, hp, t: (0, 0)),
                  pl.BlockSpec((1,) + wl.shape[1:], lambda b, hp, t: (hp, 0, 0, 0))],
        out_specs=pl.BlockSpec((t_blk, LANES), lambda b, hp, t: (b * n_t + t, hp)),
        scratch_shapes=[
            pltpu.VMEM((5, 8, LANES), jnp.float32),
            pltpu.VMEM((LANES, LANES), jnp.float32),
            pltpu.VMEM((n_c, LANES, LANES), jnp.bfloat16),
            pltpu.VMEM((n_c, LANES, LANES), jnp.bfloat16),
            pltpu.VMEM((n_c, LANES, LANES), jnp.float32),
            pltpu.VMEM((n_c, LANES, LANES), jnp.float32),
            pltpu.VMEM((n_c, 8, LANES), jnp.float32),
            pltpu.VMEM((t_blk, LANES), jnp.float32),
        ],
        compiler_params=pltpu.CompilerParams(
            dimension_semantics=("parallel", "parallel", "arbitrary"), vmem_limit_bytes=VMEM_LIMIT),
        name="rwkv",
    )(slabs, slabs, slabs, slabs, slabs, pr, sh, wl)


def _rwkv_params(shift_mu, w0, a0, k_k, k_a, r_k, ln_w, ln_b, w_w2, w_a2, w_g2):
    def pairs(vec):
        return vec.reshape(N_HEAD_PAIRS, LANES)

    mu_r, mu_k, mu_v = (pairs(shift_mu[i * RWKV_WIDTH:(i + 1) * RWKV_WIDTH]) for i in range(3))
    rows = [mu_r, mu_k, mu_v, pairs(w0), pairs(a0), pairs(k_k), pairs(k_a), pairs(r_k.reshape(-1)),
            pairs(ln_w), pairs(ln_b)]
    pr = jnp.stack(rows, axis=1)
    pr = jnp.pad(pr, ((0, 0), (0, 16 - pr.shape[1]), (0, 0)))
    mu_rest = shift_mu[3 * RWKV_WIDTH:]
    sh = jnp.pad(mu_rest.reshape(2, LANES), ((0, 6), (0, 0)))
    zeros = jnp.zeros((DECAY_LORA, RWKV_WIDTH), jnp.float32)
    wd = jnp.concatenate([w_w2, zeros], axis=0)
    wa = jnp.concatenate([zeros, w_a2], axis=0)
    wl = jnp.stack([wd, wa, w_g2], axis=0)
    wl = wl.reshape(3, LANES, N_HEAD_PAIRS, LANES).transpose(2, 0, 1, 3)
    return pr, sh, _bf16(wl)


def _rms(x, g):
    ms = jnp.mean(x * x, axis=-1, keepdims=True)
    return x * lax.rsqrt(ms + NORM_EPS) * g


def _merge_body(x_ref, att_ref, rw_ref, gate_ref, bg_ref, wa_ref, wr_ref, wo_ref, gn_ref, o_ref):
    n_gs = D_MODEL // LANES
    f0 = jnp.concatenate([gate_ref[s] for s in range(n_gs)], axis=1)
    f1 = jnp.concatenate([gate_ref[n_gs + s] for s in range(n_gs)], axis=1)
    g0 = jax.nn.sigmoid(f0 + bg_ref[:, :D_MODEL])
    g1 = jax.nn.sigmoid(f1 + bg_ref[:, D_MODEL:])
    merged = g0 * _dot(_bf16(att_ref[...]), wa_ref[...]) + g1 * _dot(_bf16(rw_ref[...]), wr_ref[...])
    z = _dot(_bf16(merged), wo_ref[...])
    o_ref[...] = x_ref[...] + _rms(z, gn_ref[...])


def _merge(x2, o_att, o_rwkv, slabs, b_gate, wa, wr, wo, g_post, tm):
    m = x2.shape[0]
    n_gate_slabs = GATE_COLS // LANES
    const = lambda shape: pl.BlockSpec(shape, lambda i: (0, 0))
    return pl.pallas_call(
        _merge_body,
        out_shape=jax.ShapeDtypeStruct((m, D_MODEL), jnp.float32),
        grid=(m // tm,),
        in_specs=[
            pl.BlockSpec((tm, D_MODEL), lambda i: (i, 0)),
            pl.BlockSpec((tm, ATT_OUT_WIDTH), lambda i: (i, 0)),
            pl.BlockSpec((tm, RWKV_WIDTH), lambda i: (i, 0)),
            pl.BlockSpec((n_gate_slabs, tm, LANES), lambda i: (GATE_SLAB0 // n_gate_slabs, i, 0)),
            const((1, GATE_COLS)), const(wa.shape), const(wr.shape), const(wo.shape), const((1, D_MODEL)),
        ],
        out_specs=pl.BlockSpec((tm, D_MODEL), lambda i: (i, 0)),
        compiler_params=pltpu.CompilerParams(
            dimension_semantics=("parallel",), vmem_limit_bytes=VMEM_LIMIT),
        name="merge",
    )(x2, o_att, o_rwkv, slabs, b_gate, wa, wr, wo, g_post)


def _ffn_body(x_ref, gpre_ref, w1_ref, w2_ref, gpost_ref, o_ref, *, tf):
    x = x_ref[...]
    h = _bf16(_rms(x, gpre_ref[...]))
    acc = jnp.zeros(x.shape, jnp.float32)
    for c in range(D_FF // tf):
        u = jnp.maximum(_dot(h, w1_ref[:, c * tf:(c + 1) * tf]), 0.0)
        acc = acc + _dot(_bf16(u * u), w2_ref[c * tf:(c + 1) * tf, :])
    o_ref[...] = x + _rms(acc, gpost_ref[...])


def _ffn(x2, g_pre, w1, w2, g_post, tm, tf=1024):
    m = x2.shape[0]
    const = lambda shape: pl.BlockSpec(shape, lambda i: (0, 0))
    return pl.pallas_call(
        functools.partial(_ffn_body, tf=tf),
        out_shape=jax.ShapeDtypeStruct((m, D_MODEL), jnp.float32),
        grid=(m // tm,),
        in_specs=[pl.BlockSpec((tm, D_MODEL), lambda i: (i, 0)), const((1, D_MODEL)),
                  const(w1.shape), const(w2.shape), const((1, D_MODEL))],
        out_specs=pl.BlockSpec((tm, D_MODEL), lambda i: (i, 0)),
        compiler_params=pltpu.CompilerParams(
            dimension_semantics=("parallel",), vmem_limit_bytes=VMEM_LIMIT),
        name="ffn",
    )(x2, g_pre, w1, w2, g_post)


def kernel(x, rel_bias, norm_mix_pre, norm_mix_post, norm_ffn_pre, norm_ffn_post, w_in, b_gate, shift_mu, w0, w_w2, a0, w_a2, w_g2, k_k, k_a, r_k, ln_x_w, ln_x_b, w_att_branch, w_rwkv_branch, w_out, w_ffn1, w_ffn2):
    batch, seq, d_model = x.shape
    assert d_model == D_MODEL and seq % ATT_CHUNK == 0 and seq % RWKV_T == 0
    m = batch * seq
    tm = 512
    assert m % tm == 0
    bias_tiles = _bias_tiles(rel_bias)
    row = lambda vec: vec.reshape(1, -1)
    x2 = x.reshape(m, D_MODEL)
    for l in range(w_in.shape[0]):
        slabs = _proj(x2, row(norm_mix_pre[l]), _bf16(w_in[l]), tm)
        o_att = _attn(slabs, bias_tiles, batch, seq)
        pr, sh, wl = _rwkv_params(shift_mu[l], w0[l], a0[l], k_k[l], k_a[l], r_k[l], ln_x_w[l], ln_x_b[l],
                                  w_w2[l], w_a2[l], w_g2[l])
        o_rwkv = _rwkv(slabs, pr, sh, wl, batch, seq)
        x2 = _merge(x2, o_att, o_rwkv, slabs, row(b_gate[l]), _bf16(w_att_branch[l]), _bf16(w_rwkv_branch[l]),
                    _bf16(w_out[l]), row(norm_mix_post[l]), tm)
        x2 = _ffn(x2, row(norm_ffn_pre[l]), _bf16(w_ffn1[l]), _bf16(w_ffn2[l]), row(norm_ffn_post[l]), tm)
    return x2.reshape(batch, seq, D_MODEL)
```

```python
import functools
import math

import jax
import jax.numpy as jnp
from jax import lax
from jax.experimental import pallas as pl
from jax.experimental.pallas import tpu as pltpu

D_MODEL = 1024
HEAD_DIM = 64
DILATIONS = (1, 4, 16)
KEYS_PER_QUERY = 128
N_GROUPS = len(DILATIONS)
HEADS_PER_GROUP = 4
ATT_HEADS = N_GROUPS * HEADS_PER_GROUP
ATT_WIDTH = ATT_HEADS * HEAD_DIM
ATT_OUT_WIDTH = HEADS_PER_GROUP * HEAD_DIM
N_BUCKETS = 32
MAX_DISTANCE = KEYS_PER_QUERY * DILATIONS[-1]
RWKV_WIDTH = D_MODEL
DECAY_LORA = 64
ICLR_LORA = 64
GATE_LORA = 128
RWKV_COLS = 3 * RWKV_WIDTH + DECAY_LORA + ICLR_LORA + GATE_LORA
N_BRANCHES = 2
IN_COLS = 3 * ATT_WIDTH + RWKV_COLS + N_BRANCHES * D_MODEL
D_FF = 4 * D_MODEL
NORM_EPS = 1e-6
LN_X_EPS = 64e-5
L2_EPS = 1e-12

LANES = 128
VMEM_LIMIT = 56 * 1024 * 1024
NEG = -1e30

GATE_COLS = N_BRANCHES * D_MODEL
COL_ROT = IN_COLS - GATE_COLS
N_SLABS = IN_COLS // LANES
GATE_SLAB0 = 0
Q_SLAB0 = GATE_COLS // LANES
K_SLAB0 = Q_SLAB0 + ATT_WIDTH // LANES
V_SLAB0 = K_SLAB0 + ATT_WIDTH // LANES
R_SLAB0 = V_SLAB0 + ATT_WIDTH // LANES
RK_SLAB0 = R_SLAB0 + RWKV_WIDTH // LANES
RV_SLAB0 = RK_SLAB0 + RWKV_WIDTH // LANES
LORA_SLAB = RV_SLAB0 + RWKV_WIDTH // LANES
FG_SLAB = LORA_SLAB + 1
N_HEAD_PAIRS = RWKV_WIDTH // LANES

PROJ_TN = 512
PROJ_TM = 1024
ROW_TILE = 512
ATT_CHUNK = KEYS_PER_QUERY * DILATIONS[-1]
QB = KEYS_PER_QUERY
RWKV_CHUNK = 64
RWKV_T = 512


def _bf16(x):
    return x.astype(jnp.bfloat16)


def _dot(a, b):
    return jnp.dot(a, b, preferred_element_type=jnp.float32)


def _dot_nt(a, b):
    return lax.dot_general(a, b, (((1,), (1,)), ((), ())), preferred_element_type=jnp.float32)


def _dot_tn(a, b):
    return lax.dot_general(a, b, (((0,), (0,)), ((), ())), preferred_element_type=jnp.float32)


def _proj_body(x_ref, g_ref, w_ref, o_ref, h_scr):
    @pl.when(pl.program_id(1) == 0)
    def _():
        x = x_ref[...]
        ms = jnp.mean(x * x, axis=-1, keepdims=True)
        h_scr[...] = _bf16(x * lax.rsqrt(ms + NORM_EPS) * g_ref[...])

    acc = _dot(h_scr[...], w_ref[0])
    for s in range(PROJ_TN // LANES):
        o_ref[s] = acc[:, s * LANES:(s + 1) * LANES]


def _proj(x2, g, w_bf, tm):
    m = x2.shape[0]
    n_col_blocks = IN_COLS // PROJ_TN
    rot_blocks = COL_ROT // PROJ_TN
    w_blk = w_bf.reshape(D_MODEL, n_col_blocks, PROJ_TN).transpose(1, 0, 2)
    return pl.pallas_call(
        _proj_body,
        out_shape=jax.ShapeDtypeStruct((N_SLABS, m, LANES), jnp.float32),
        grid=(m // tm, n_col_blocks),
        in_specs=[
            pl.BlockSpec((tm, D_MODEL), lambda i, j: (i, 0)),
            pl.BlockSpec((1, D_MODEL), lambda i, j: (0, 0)),
            pl.BlockSpec((1, D_MODEL, PROJ_TN), lambda i, j: ((j + rot_blocks) % n_col_blocks, 0, 0)),
        ],
        out_specs=pl.BlockSpec((PROJ_TN // LANES, tm, LANES), lambda i, j: (j, i, 0)),
        scratch_shapes=[pltpu.VMEM((tm, D_MODEL), jnp.bfloat16)],
        compiler_params=pltpu.CompilerParams(
            dimension_semantics=("parallel", "arbitrary"), vmem_limit_bytes=VMEM_LIMIT),
        name="proj",
    )(x2, g, w_blk)


def _attn_unit(d, q, k, v, bias0, bias1, lo):
    scale = HEAD_DIM ** -0.5
    qs = q * scale
    kb = _bf16(k)
    zero = jnp.zeros_like(qs)
    s0 = _dot_nt(_bf16(jnp.where(lo, qs, zero)), kb) + bias0
    s1 = _dot_nt(_bf16(jnp.where(lo, zero, qs)), kb) + bias1
    m0 = jnp.max(s0, axis=-1, keepdims=True)
    m1 = jnp.max(s1, axis=-1, keepdims=True)
    p0 = _bf16(jnp.exp(s0 - m0))
    p1 = _bf16(jnp.exp(s1 - m1))
    zv = jnp.zeros_like(v)
    ones = jnp.ones_like(v)
    rhs0 = _bf16(jnp.concatenate([jnp.where(lo, v, zv), jnp.where(lo, ones, zv)], axis=1))
    rhs1 = _bf16(jnp.concatenate([jnp.where(lo, zv, v), jnp.where(lo, zv, ones)], axis=1))
    ol = _dot(p0, rhs0) + _dot(p1, rhs1)
    o = ol[:, :LANES]
    l = ol[:, LANES:]
    m = jnp.where(lo, m0, m1)
    return o / l, m + jnp.log(l)


def _attn_body(q_ref, kc_ref, kp_ref, vc_ref, vp_ref, bias_ref, o_ref, o_scr, l_scr):
    c = pl.program_id(1)
    g = pl.program_id(2)
    lo = lax.broadcasted_iota(jnp.int32, (1, LANES), 1) < HEAD_DIM
    first = jnp.where(c == 0, 1, 0)

    def rows(start, n, d):
        return pl.ds(start, n) if d == 1 else pl.ds(start, n, stride=d)

    def group(gi):
        d = DILATIONS[gi]
        n_qb = ATT_CHUNK // (QB * d)

        def unit(p, start_q, k, v, variant):
            h0 = gi * HEADS_PER_GROUP + 2 * p
            q = q_ref[p, rows(start_q, QB, d), :]
            o, lse = _attn_unit(d, q, k, v, bias_ref[variant, h0], bias_ref[variant, h0 + 1], lo)
            o_scr[gi, p, rows(start_q, QB, d), :] = o
            l_scr[gi, p, rows(start_q, QB, d), :] = lse

        def first_blocks(r, carry):
            prev_start = r + QB * d * (n_qb - 1)
            for p in range(2):
                k = jnp.concatenate([kp_ref[p, rows(prev_start, QB, d), :], kc_ref[p, rows(r, QB, d), :]], axis=0)
                v = jnp.concatenate([vp_ref[p, rows(prev_start, QB, d), :], vc_ref[p, rows(r, QB, d), :]], axis=0)
                unit(p, r, k, v, first)
            return carry

        lax.fori_loop(0, d, first_blocks, 0)

        if n_qb > 1:
            def later_blocks(u, carry):
                r = u % d
                qb = 1 + u // d
                start_q = r + QB * d * qb
                for p in range(2):
                    k = kc_ref[p, rows(start_q - QB * d, 2 * QB, d), :]
                    v = vc_ref[p, rows(start_q - QB * d, 2 * QB, d), :]
                    unit(p, start_q, k, v, 0)
                return carry

            lax.fori_loop(0, d * (n_qb - 1), later_blocks, 0)

    for gi in range(N_GROUPS):
        pl.when(g == gi)(functools.partial(group, gi))

    @pl.when(g == N_GROUPS - 1)
    def _():
        tile = 256

        def comb(i, carry):
            rs = pl.ds(pl.multiple_of(i * tile, tile), tile)
            for p in range(2):
                ls = [l_scr[gi, p, rs, :] for gi in range(N_GROUPS)]
                mx = jnp.maximum(jnp.maximum(ls[0], ls[1]), ls[2])
                ws = [jnp.exp(l - mx) for l in ls]
                num = ws[0] * o_scr[0, p, rs, :] + ws[1] * o_scr[1, p, rs, :] + ws[2] * o_scr[2, p, rs, :]
                o_ref[rs, p * LANES:(p + 1) * LANES] = num / (ws[0] + ws[1] + ws[2])
            return carry

        lax.fori_loop(0, ATT_CHUNK // tile, comb, 0)


def _attn(slabs, bias_tiles, batch, seq):
    m = slabs.shape[1]
    n_chunks = seq // ATT_CHUNK
    blk = (2, ATT_CHUNK, LANES)

    def cur(slab0):
        return pl.BlockSpec(blk, lambda b, c, g: (slab0 // 2 + g, b * n_chunks + c, 0))

    def prev(slab0):
        return pl.BlockSpec(blk, lambda b, c, g: (slab0 // 2 + g, b * n_chunks + jnp.maximum(c - 1, 0), 0))

    return pl.pallas_call(
        _attn_body,
        out_shape=jax.ShapeDtypeStruct((m, ATT_OUT_WIDTH), jnp.float32),
        grid=(batch, n_chunks, N_GROUPS),
        in_specs=[cur(Q_SLAB0), cur(K_SLAB0), prev(K_SLAB0), cur(V_SLAB0), prev(V_SLAB0),
                  pl.BlockSpec(bias_tiles.shape, lambda b, c, g: (0, 0, 0, 0))],
        out_specs=pl.BlockSpec((ATT_CHUNK, ATT_OUT_WIDTH), lambda b, c, g: (b * n_chunks + c, 0)),
        scratch_shapes=[pltpu.VMEM((N_GROUPS, 2, ATT_CHUNK, LANES), jnp.float32),
                        pltpu.VMEM((N_GROUPS, 2, ATT_CHUNK, LANES), jnp.float32)],
        compiler_params=pltpu.CompilerParams(
            dimension_semantics=("parallel", "parallel", "arbitrary"), vmem_limit_bytes=VMEM_LIMIT),
        name="attn",
    )(slabs, slabs, slabs, slabs, slabs, bias_tiles)


def _t5_bucket(dist):
    max_exact = N_BUCKETS // 2
    d_f = jnp.maximum(dist, 1).astype(jnp.float32)
    large = max_exact + (jnp.log(d_f / max_exact) / math.log(MAX_DISTANCE / max_exact)
                         * (N_BUCKETS - max_exact)).astype(jnp.int32)
    large = jnp.minimum(large, N_BUCKETS - 1)
    return jnp.where(dist < max_exact, dist, large)


def _bias_tiles(rel_bias):
    dil = jnp.array(DILATIONS, jnp.int32)
    dist = dil[:, None] * jnp.arange(KEYS_PER_QUERY + 1, dtype=jnp.int32)[None, :]
    bucket = _t5_bucket(dist)
    bias = rel_bias.reshape(N_BUCKETS, N_GROUPS, HEADS_PER_GROUP)[bucket, jnp.arange(N_GROUPS)[:, None]]
    bias = jnp.transpose(bias, (0, 2, 1)).astype(jnp.float32).reshape(ATT_HEADS, KEYS_PER_QUERY + 1)
    n = 3 * QB - 1
    neg = lambda w: jnp.full((ATT_HEADS, w), NEG, jnp.float32)
    e = jnp.concatenate([neg(QB - 1), bias[:, ::-1], neg(QB - 1), neg(1)], axis=1)
    e = jnp.roll(e, -(QB - 1), axis=1)
    t0 = jnp.tile(e, (1, QB))[:, :QB * n].reshape(ATT_HEADS, QB, n)[:, :, :2 * QB]
    col = jnp.arange(2 * QB)[None, None, :]
    t1 = jnp.where(col >= QB, t0, NEG)
    return jnp.stack([t0, t1])


def _rwkv_body(r_ref, k_ref, v_ref, lora_ref, fg_ref, pr_ref, sh_ref, wl_ref, o_ref,
               carry, st, u_s, bp_s, z_s, rt_s, gam_s, y_s):
    t_blk = r_ref.shape[1]
    n_chunks = t_blk // RWKV_CHUNK
    cc = RWKV_CHUNK

    @pl.when(pl.program_id(2) == 0)
    def _():
        carry[...] = jnp.zeros_like(carry)
        st[...] = jnp.zeros_like(st)

    row = lax.broadcasted_iota(jnp.int32, (t_blk, LANES), 0)
    pr = pr_ref[0]
    sh = sh_ref[...]

    def shifted(ref, idx, mu):
        x = ref[0]
        prev = pltpu.roll(x, 1, 0)
        prev = jnp.where(row == 0, carry[idx, 7:8, :], prev)
        carry[idx] = x[t_blk - 8:, :]
        return x + (prev - x) * mu

    r = shifted(r_ref, 0, pr[0:1])
    k = shifted(k_ref, 1, pr[1:2])
    v = shifted(v_ref, 2, pr[2:3])
    f_lora = shifted(lora_ref, 3, sh[0:1])
    f_g = shifted(fg_ref, 4, sh[1:2])
    w0, a0, k_k, k_a, r_k, ln_w, ln_b = (pr[i:i + 1] for i in range(3, 10))

    idx = lax.broadcasted_iota(jnp.int32, (LANES, LANES), 0)
    jdx = lax.broadcasted_iota(jnp.int32, (LANES, LANES), 1)
    same_head = (idx // HEAD_DIM) == (jdx // HEAD_DIM)
    head_ones = _bf16(jnp.where(same_head, 1.0, 0.0))

    def head_sum(x):
        hi = _bf16(x)
        lo_ = _bf16(x - hi.astype(jnp.float32))
        return _dot(hi, head_ones) + _dot(lo_, head_ones)

    wd = w0 + _dot(_bf16(jnp.tanh(f_lora)), wl_ref[0, 0])
    nwd = -wd
    softplus = jnp.maximum(nwd, 0.0) + jnp.log(1.0 + jnp.exp(-jnp.abs(nwd)))
    logw = -jnp.exp(-softplus - 0.5)
    a = jax.nn.sigmoid(a0 + _dot(_bf16(f_lora), wl_ref[0, 1]))
    gate = _dot(_bf16(jax.nn.sigmoid(f_g)), wl_ref[0, 2])
    kk = k * k_k
    kk = kk / jnp.maximum(jnp.sqrt(head_sum(kk * kk)), L2_EPS)
    k2 = k * (1.0 + (a - 1.0) * k_a)
    bb = kk * a

    lane_lo = lax.broadcasted_iota(jnp.int32, (1, LANES), 1) < HEAD_DIM

    def stack(x):
        z = jnp.zeros_like(x)
        return jnp.concatenate([jnp.where(lane_lo, x, z), jnp.where(lane_lo, z, x)], axis=0)

    ti = idx % cc
    tj = jdx % cc
    strict_lower = same_head & (ti > tj)
    lower = same_head & (ti >= tj)
    eye = idx == jdx
    ci = lax.broadcasted_iota(jnp.int32, (cc, cc), 0)
    cj = lax.broadcasted_iota(jnp.int32, (cc, cc), 1)
    tri = _bf16(jnp.where(ci >= cj, 1.0, 0.0))
    zero_m = jnp.zeros((LANES, LANES), jnp.float32)

    cs = range(n_chunks)

    def each(fn, *lists):
        return [fn(*args) for args in zip(*lists)]

    def split3(x):
        hi = _bf16(x)
        r1 = x - hi.astype(jnp.float32)
        mid = _bf16(r1)
        return hi, mid, _bf16(r1 - mid.astype(jnp.float32))

    lw = [logw[c * cc:(c + 1) * cc] for c in cs]
    parts = each(split3, lw)
    big_l = each(lambda p: _dot(tri, p[0]) + _dot(tri, p[1]) + _dot(tri, p[2]), parts)
    l_end = each(lambda l: l[cc - 1:cc], big_l)
    e_l = each(jnp.exp, big_l)
    e_lm = each(lambda l, w: jnp.exp(l - w), big_l, lw)
    e_nl = each(lambda l: jnp.exp(-l), big_l)
    e_c = each(lambda le, l: jnp.exp(le - l), l_end, big_l)
    rows_of = lambda x: [x[c * cc:(c + 1) * cc] for c in cs]
    a_kk = each(lambda x, e: _bf16(stack(x * e)), rows_of(kk), e_lm)
    a_r_f = each(lambda x, e: stack(x * e), rows_of(r), e_l)
    a_r = each(_bf16, a_r_f)
    kb = each(lambda xk, xb, e: _bf16(jnp.concatenate([stack(xk * e), stack(xb * e)], axis=0)),
              rows_of(k2), rows_of(bb), e_nl)
    kcb = each(lambda xk, xb, e: _bf16(jnp.concatenate([stack(xk * e), stack(-xb * e)], axis=0)),
               rows_of(k2), rows_of(bb), e_c)
    bc = each(lambda xb, e: _bf16(stack(xb * e)), rows_of(bb), e_c)
    vs = each(lambda x: _bf16(stack(x)), rows_of(v))
    s1 = each(_dot_nt, a_kk, kb)
    s2 = each(_dot_nt, a_r, kb)
    mk = each(lambda s: _bf16(jnp.where(strict_lower, s[:, :LANES], zero_m)), s1)
    mb = each(lambda s: jnp.where(strict_lower, s[:, LANES:], zero_m), s1)
    nkb = each(lambda s: _bf16(jnp.concatenate([jnp.where(lower, s[:, :LANES], zero_m),
                                                 jnp.where(lower, -s[:, LANES:], zero_m)], axis=1)), s2)

    eye_f = jnp.where(eye, 1.0, 0.0)
    leaf = (idx // 2) == (jdx // 2)
    x = each(lambda n: eye_f - jnp.where(leaf, n, zero_m), mb)
    blk = 4
    while blk <= cc:
        half = blk // 2
        sel = ((idx // blk) == (jdx // blk)) & ((idx % blk) >= half) & ((jdx % blk) < half)
        xb = each(_bf16, x)
        t1 = each(lambda xc, n: _bf16(_dot(xc, _bf16(jnp.where(sel, n, zero_m)))), xb, mb)
        x = each(lambda xf, t, xc: xf - _dot(t, xc), x, t1, xb)
        blk *= 2
    tinv = each(_bf16, x)

    pm = each(lambda t, ak: _bf16(_dot(t, ak)), tinv, a_kk)
    mkv = each(lambda m_, v_: _bf16(_dot(m_, v_)), mk, vs)
    q = each(lambda t, m_: _bf16(_dot(t, m_)), tinv, mkv)
    vq = each(lambda v_, q_: jnp.concatenate([v_, q_], axis=0), vs, q)
    for c in cs:
        bp_s[c] = _bf16(_dot_tn(bc[c], pm[c]))
    for c in cs:
        rt_s[c] = _dot_tn(vq[c], kcb[c])
    for c in cs:
        u_s[c] = _bf16(a_r_f[c] + _dot(nkb[c][:, LANES:], pm[c]))
    for c in cs:
        z_s[c] = _dot(nkb[c], vq[c])
    for c in cs:
        gam_s[c] = jnp.broadcast_to(jnp.exp(l_end[c]), (8, LANES))

    for c in range(n_chunks):
        s_f = st[...]
        s_b = _bf16(s_f)
        y_st = _dot_nt(u_s[c], s_b) + z_s[c]
        y_s[c * cc:(c + 1) * cc, :] = y_st[:cc] + y_st[cc:]
        st[...] = s_f * gam_s[c, 0:1, :] - _dot_nt(s_b, bp_s[c]) + rt_s[c]

    y = y_s[...]
    mu = head_sum(y) * (1.0 / HEAD_DIM)
    yc = y - mu
    var = head_sum(yc * yc) * (1.0 / HEAD_DIM)
    yn = yc * lax.rsqrt(var + LN_X_EPS) * ln_w + ln_b
    bonus = head_sum(r * k2 * r_k) * v
    o_ref[...] = (yn + bonus) * gate


def _rwkv(slabs, pr, sh, wl, batch, seq):
    m = slabs.shape[1]
    t_blk = RWKV_T
    n_t = seq // t_blk
    n_c = t_blk // RWKV_CHUNK

    def tok(slab0, per_pair):
        if per_pair:
            return pl.BlockSpec((1, t_blk, LANES), lambda b, hp, t: (slab0 + hp, b * n_t + t, 0))
        return pl.BlockSpec((1, t_blk, LANES), lambda b, hp, t: (slab0, b * n_t + t, 0))

    return pl.pallas_call(
        _rwkv_body,
        out_shape=jax.ShapeDtypeStruct((m, RWKV_WIDTH), jnp.float32),
        grid=(batch, N_HEAD_PAIRS, n_t),
        in_specs=[tok(R_SLAB0, True), tok(RK_SLAB0, True), tok(RV_SLAB0, True),
                  tok(LORA_SLAB, False), tok(FG_SLAB, False),
                  pl.BlockSpec((1,) + pr.shape[1:], lambda b, hp, t: (hp, 0, 0)),
                  pl.BlockSpec(sh.shape, lambda b, hp, t: (0, 0)),
                  pl.BlockSpec((1,) + wl.shape[1:], lambda b, hp, t: (hp, 0, 0, 0))],
        out_specs=pl.BlockSpec((t_blk, LANES), lambda b, hp, t: (b * n_t + t, hp)),
        scratch_shapes=[
            pltpu.VMEM((5, 8, LANES), jnp.float32),
            pltpu.VMEM((LANES, LANES), jnp.float32),
            pltpu.VMEM((n_c, LANES, LANES), jnp.bfloat16),
            pltpu.VMEM((n_c, LANES, LANES), jnp.bfloat16),
            pltpu.VMEM((n_c, LANES, LANES), jnp.float32),
            pltpu.VMEM((n_c, LANES, LANES), jnp.float32),
            pltpu.VMEM((n_c, 8, LANES), jnp.float32),
            pltpu.VMEM((t_blk, LANES), jnp.float32),
        ],
        compiler_params=pltpu.CompilerParams(
            dimension_semantics=("parallel", "parallel", "arbitrary"), vmem_limit_bytes=VMEM_LIMIT),
        name="rwkv",
    )(slabs, slabs, slabs, slabs, slabs, pr, sh, wl)


def _rwkv_params(shift_mu, w0, a0, k_k, k_a, r_k, ln_w, ln_b, w_w2, w_a2, w_g2):
    def pairs(vec):
        return vec.reshape(N_HEAD_PAIRS, LANES)

    mu_r, mu_k, mu_v = (pairs(shift_mu[i * RWKV_WIDTH:(i + 1) * RWKV_WIDTH]) for i in range(3))
    rows = [mu_r, mu_k, mu_v, pairs(w0), pairs(a0), pairs(k_k), pairs(k_a), pairs(r_k.reshape(-1)),
            pairs(ln_w), pairs(ln_b)]
    pr = jnp.stack(rows, axis=1)
    pr = jnp.pad(pr, ((0, 0), (0, 16 - pr.shape[1]), (0, 0)))
    mu_rest = shift_mu[3 * RWKV_WIDTH:]
    sh = jnp.pad(mu_rest.reshape(2, LANES), ((0, 6), (0, 0)))
    zeros = jnp.zeros((DECAY_LORA, RWKV_WIDTH), jnp.float32)
    wd = jnp.concatenate([w_w2, zeros], axis=0)
    wa = jnp.concatenate([zeros, w_a2], axis=0)
    wl = jnp.stack([wd, wa, w_g2], axis=0)
    wl = wl.reshape(3, LANES, N_HEAD_PAIRS, LANES).transpose(2, 0, 1, 3)
    return pr, sh, _bf16(wl)


def _rms(x, g):
    ms = jnp.mean(x * x, axis=-1, keepdims=True)
    return x * lax.rsqrt(ms + NORM_EPS) * g


def _merge_body(x_ref, att_ref, rw_ref, gate_ref, bg_ref, wa_ref, wr_ref, wo_ref, gn_ref, o_ref):
    n_gs = D_MODEL // LANES
    f0 = jnp.concatenate([gate_ref[s] for s in range(n_gs)], axis=1)
    f1 = jnp.concatenate([gate_ref[n_gs + s] for s in range(n_gs)], axis=1)
    g0 = jax.nn.sigmoid(f0 + bg_ref[:, :D_MODEL])
    g1 = jax.nn.sigmoid(f1 + bg_ref[:, D_MODEL:])
    merged = g0 * _dot(_bf16(att_ref[...]), wa_ref[...]) + g1 * _dot(_bf16(rw_ref[...]), wr_ref[...])
    z = _dot(_bf16(merged), wo_ref[...])
    o_ref[...] = x_ref[...] + _rms(z, gn_ref[...])


def _merge(x2, o_att, o_rwkv, slabs, b_gate, wa, wr, wo, g_post, tm):
    m = x2.shape[0]
    n_gate_slabs = GATE_COLS // LANES
    const = lambda shape: pl.BlockSpec(shape, lambda i: (0, 0))
    return pl.pallas_call(
        _merge_body,
        out_shape=jax.ShapeDtypeStruct((m, D_MODEL), jnp.float32),
        grid=(m // tm,),
        in_specs=[
            pl.BlockSpec((tm, D_MODEL), lambda i: (i, 0)),
            pl.BlockSpec((tm, ATT_OUT_WIDTH), lambda i: (i, 0)),
            pl.BlockSpec((tm, RWKV_WIDTH), lambda i: (i, 0)),
            pl.BlockSpec((n_gate_slabs, tm, LANES), lambda i: (GATE_SLAB0 // n_gate_slabs, i, 0)),
            const((1, GATE_COLS)), const(wa.shape), const(wr.shape), const(wo.shape), const((1, D_MODEL)),
        ],
        out_specs=pl.BlockSpec((tm, D_MODEL), lambda i: (i, 0)),
        compiler_params=pltpu.CompilerParams(
            dimension_semantics=("parallel",), vmem_limit_bytes=VMEM_LIMIT),
        name="merge",
    )(x2, o_att, o_rwkv, slabs, b_gate, wa, wr, wo, g_post)


def _ffn_body(x_ref, gpre_ref, w1_ref, w2_ref, gpost_ref, o_ref, *, tf):
    x = x_ref[...]
    h = _bf16(_rms(x, gpre_ref[...]))
    acc = jnp.zeros(x.shape, jnp.float32)
    for c in range(D_FF // tf):
        u = jnp.maximum(_dot(h, w1_ref[:, c * tf:(c + 1) * tf]), 0.0)
        acc = acc + _dot(_bf16(u * u), w2_ref[c * tf:(c + 1) * tf, :])
    o_ref[...] = x + _rms(acc, gpost_ref[...])


def _ffn(x2, g_pre, w1, w2, g_post, tm, tf=1024):
    m = x2.shape[0]
    const = lambda shape: pl.BlockSpec(shape, lambda i: (0, 0))
    return pl.pallas_call(
        functools.partial(_ffn_body, tf=tf),
        out_shape=jax.ShapeDtypeStruct((m, D_MODEL), jnp.float32),
        grid=(m // tm,),
        in_specs=[pl.BlockSpec((tm, D_MODEL), lambda i: (i, 0)), const((1, D_MODEL)),
                  const(w1.shape), const(w2.shape), const((1, D_MODEL))],
        out_specs=pl.BlockSpec((tm, D_MODEL), lambda i: (i, 0)),
        compiler_params=pltpu.CompilerParams(
            dimension_semantics=("parallel",), vmem_limit_bytes=VMEM_LIMIT),
        name="ffn",
    )(x2, g_pre, w1, w2, g_post)


def kernel(x, rel_bias, norm_mix_pre, norm_mix_post, norm_ffn_pre, norm_ffn_post, w_in, b_gate, shift_mu, w0, w_w2, a0, w_a2, w_g2, k_k, k_a, r_k, ln_x_w, ln_x_b, w_att_branch, w_rwkv_branch, w_out, w_ffn1, w_ffn2):
    batch, seq, d_model = x.shape
    assert d_model == D_MODEL and seq % ATT_CHUNK == 0 and seq % RWKV_T == 0
    m = batch * seq
    tm = ROW_TILE
    assert m % PROJ_TM == 0 and m % tm == 0
    bias_tiles = _bias_tiles(rel_bias)
    row = lambda vec: vec.reshape(1, -1)
    x2 = x.reshape(m, D_MODEL)
    for l in range(w_in.shape[0]):
        slabs = _proj(x2, row(norm_mix_pre[l]), _bf16(w_in[l]), PROJ_TM)
        o_att = _attn(slabs, bias_tiles, batch, seq)
        pr, sh, wl = _rwkv_params(shift_mu[l], w0[l], a0[l], k_k[l], k_a[l], r_k[l], ln_x_w[l], ln_x_b[l],
                                  w_w2[l], w_a2[l], w_g2[l])
        o_rwkv = _rwkv(slabs, pr, sh, wl, batch, seq)
        x2 = _merge(x2, o_att, o_rwkv, slabs, row(b_gate[l]), _bf16(w_att_branch[l]), _bf16(w_rwkv_branch[l]),
                    _bf16(w_out[l]), row(norm_mix_post[l]), tm)
        x2 = _ffn(x2, row(norm_ffn_pre[l]), _bf16(w_ffn1[l]), _bf16(w_ffn2[l]), row(norm_ffn_post[l]), tm)
    return x2.reshape(batch, seq, D_MODEL)
```

```python
import functools
import math

import jax
import jax.numpy as jnp
from jax import lax
from jax.experimental import pallas as pl
from jax.experimental.pallas import tpu as pltpu

D_MODEL = 1024
HEAD_DIM = 64
DILATIONS = (1, 4, 16)
KEYS_PER_QUERY = 128
N_GROUPS = len(DILATIONS)
HEADS_PER_GROUP = 4
ATT_HEADS = N_GROUPS * HEADS_PER_GROUP
ATT_WIDTH = ATT_HEADS * HEAD_DIM
ATT_OUT_WIDTH = HEADS_PER_GROUP * HEAD_DIM
N_BUCKETS = 32
MAX_DISTANCE = KEYS_PER_QUERY * DILATIONS[-1]
RWKV_WIDTH = D_MODEL
DECAY_LORA = 64
ICLR_LORA = 64
GATE_LORA = 128
RWKV_COLS = 3 * RWKV_WIDTH + DECAY_LORA + ICLR_LORA + GATE_LORA
N_BRANCHES = 2
IN_COLS = 3 * ATT_WIDTH + RWKV_COLS + N_BRANCHES * D_MODEL
D_FF = 4 * D_MODEL
NORM_EPS = 1e-6
LN_X_EPS = 64e-5
L2_EPS = 1e-12

LANES = 128
VMEM_LIMIT = 56 * 1024 * 1024
NEG = -1e30

GATE_COLS = N_BRANCHES * D_MODEL
COL_ROT = IN_COLS - GATE_COLS
N_SLABS = IN_COLS // LANES
GATE_SLAB0 = 0
Q_SLAB0 = GATE_COLS // LANES
K_SLAB0 = Q_SLAB0 + ATT_WIDTH // LANES
V_SLAB0 = K_SLAB0 + ATT_WIDTH // LANES
R_SLAB0 = V_SLAB0 + ATT_WIDTH // LANES
RK_SLAB0 = R_SLAB0 + RWKV_WIDTH // LANES
RV_SLAB0 = RK_SLAB0 + RWKV_WIDTH // LANES
LORA_SLAB = RV_SLAB0 + RWKV_WIDTH // LANES
FG_SLAB = LORA_SLAB + 1
RWKV_QUAD = 4
QW = RWKV_QUAD * HEAD_DIM
N_QUADS = RWKV_WIDTH // QW

PROJ_TN = 512
PROJ_TM = 1024
ROW_TILE = 512
ATT_CHUNK = KEYS_PER_QUERY * DILATIONS[-1]
QB = KEYS_PER_QUERY
RWKV_CHUNK = 64
RWKV_T = 512


def _bf16(x):
    return x.astype(jnp.bfloat16)


def _dot(a, b):
    return jnp.dot(a, b, preferred_element_type=jnp.float32)


def _dot_nt(a, b):
    return lax.dot_general(a, b, (((1,), (1,)), ((), ())), preferred_element_type=jnp.float32)


def _proj_body(x_ref, g_ref, w_ref, o_ref, h_scr):
    @pl.when(pl.program_id(1) == 0)
    def _():
        x = x_ref[...]
        ms = jnp.mean(x * x, axis=-1, keepdims=True)
        h_scr[...] = _bf16(x * lax.rsqrt(ms + NORM_EPS) * g_ref[...])

    acc = _dot(h_scr[...], w_ref[0])
    for s in range(PROJ_TN // LANES):
        o_ref[s] = acc[:, s * LANES:(s + 1) * LANES]


def _proj(x2, g, w_bf, tm):
    m = x2.shape[0]
    n_col_blocks = IN_COLS // PROJ_TN
    rot_blocks = COL_ROT // PROJ_TN
    w_blk = w_bf.reshape(D_MODEL, n_col_blocks, PROJ_TN).transpose(1, 0, 2)
    return pl.pallas_call(
        _proj_body,
        out_shape=jax.ShapeDtypeStruct((N_SLABS, m, LANES), jnp.float32),
        grid=(m // tm, n_col_blocks),
        in_specs=[
            pl.BlockSpec((tm, D_MODEL), lambda i, j: (i, 0)),
            pl.BlockSpec((1, D_MODEL), lambda i, j: (0, 0)),
            pl.BlockSpec((1, D_MODEL, PROJ_TN), lambda i, j: ((j + rot_blocks) % n_col_blocks, 0, 0)),
        ],
        out_specs=pl.BlockSpec((PROJ_TN // LANES, tm, LANES), lambda i, j: (j, i, 0)),
        scratch_shapes=[pltpu.VMEM((tm, D_MODEL), jnp.bfloat16)],
        compiler_params=pltpu.CompilerParams(
            dimension_semantics=("parallel", "arbitrary"), vmem_limit_bytes=VMEM_LIMIT),
        name="proj",
    )(x2, g, w_blk)


def _attn_unit(d, q, k, v, bias0, bias1, lo):
    scale = HEAD_DIM ** -0.5
    qs = q * scale
    kb = _bf16(k)
    zero = jnp.zeros_like(qs)
    s0 = _dot_nt(_bf16(jnp.where(lo, qs, zero)), kb) + bias0
    s1 = _dot_nt(_bf16(jnp.where(lo, zero, qs)), kb) + bias1
    m0 = jnp.max(s0, axis=-1, keepdims=True)
    m1 = jnp.max(s1, axis=-1, keepdims=True)
    p0 = _bf16(jnp.exp(s0 - m0))
    p1 = _bf16(jnp.exp(s1 - m1))
    zv = jnp.zeros_like(v)
    ones = jnp.ones_like(v)
    rhs0 = _bf16(jnp.concatenate([jnp.where(lo, v, zv), jnp.where(lo, ones, zv)], axis=1))
    rhs1 = _bf16(jnp.concatenate([jnp.where(lo, zv, v), jnp.where(lo, zv, ones)], axis=1))
    ol = _dot(p0, rhs0) + _dot(p1, rhs1)
    o = ol[:, :LANES]
    l = ol[:, LANES:]
    m = jnp.where(lo, m0, m1)
    return o / l, m + jnp.log(l)


def _attn_body(q_ref, kc_ref, kp_ref, vc_ref, vp_ref, bias_ref, o_ref, o_scr, l_scr):
    c = pl.program_id(1)
    g = pl.program_id(2)
    lo = lax.broadcasted_iota(jnp.int32, (1, LANES), 1) < HEAD_DIM
    first = jnp.where(c == 0, 1, 0)

    def rows(start, n, d):
        return pl.ds(start, n) if d == 1 else pl.ds(start, n, stride=d)

    def group(gi):
        d = DILATIONS[gi]
        n_qb = ATT_CHUNK // (QB * d)

        def unit(p, start_q, k, v, variant):
            h0 = gi * HEADS_PER_GROUP + 2 * p
            q = q_ref[p, rows(start_q, QB, d), :]
            o, lse = _attn_unit(d, q, k, v, bias_ref[variant, h0], bias_ref[variant, h0 + 1], lo)
            o_scr[gi, p, rows(start_q, QB, d), :] = o
            l_scr[gi, p, rows(start_q, QB, d), :] = lse

        def first_blocks(r, carry):
            prev_start = r + QB * d * (n_qb - 1)
            for p in range(2):
                k = jnp.concatenate([kp_ref[p, rows(prev_start, QB, d), :], kc_ref[p, rows(r, QB, d), :]], axis=0)
                v = jnp.concatenate([vp_ref[p, rows(prev_start, QB, d), :], vc_ref[p, rows(r, QB, d), :]], axis=0)
                unit(p, r, k, v, first)
            return carry

        lax.fori_loop(0, d, first_blocks, 0)

        if n_qb > 1:
            def later_blocks(u, carry):
                r = u % d
                qb = 1 + u // d
                start_q = r + QB * d * qb
                for p in range(2):
                    k = kc_ref[p, rows(start_q - QB * d, 2 * QB, d), :]
                    v = vc_ref[p, rows(start_q - QB * d, 2 * QB, d), :]
                    unit(p, start_q, k, v, 0)
                return carry

            lax.fori_loop(0, d * (n_qb - 1), later_blocks, 0)

    for gi in range(N_GROUPS):
        pl.when(g == gi)(functools.partial(group, gi))

    @pl.when(g == N_GROUPS - 1)
    def _():
        tile = 256

        def comb(i, carry):
            rs = pl.ds(pl.multiple_of(i * tile, tile), tile)
            for p in range(2):
                ls = [l_scr[gi, p, rs, :] for gi in range(N_GROUPS)]
                mx = jnp.maximum(jnp.maximum(ls[0], ls[1]), ls[2])
                ws = [jnp.exp(l - mx) for l in ls]
                num = ws[0] * o_scr[0, p, rs, :] + ws[1] * o_scr[1, p, rs, :] + ws[2] * o_scr[2, p, rs, :]
                o_ref[rs, p * LANES:(p + 1) * LANES] = num / (ws[0] + ws[1] + ws[2])
            return carry

        lax.fori_loop(0, ATT_CHUNK // tile, comb, 0)


def _attn(slabs, bias_tiles, batch, seq):
    m = slabs.shape[1]
    n_chunks = seq // ATT_CHUNK
    blk = (2, ATT_CHUNK, LANES)

    def cur(slab0):
        return pl.BlockSpec(blk, lambda b, c, g: (slab0 // 2 + g, b * n_chunks + c, 0))

    def prev(slab0):
        return pl.BlockSpec(blk, lambda b, c, g: (slab0 // 2 + g, b * n_chunks + jnp.maximum(c - 1, 0), 0))

    return pl.pallas_call(
        _attn_body,
        out_shape=jax.ShapeDtypeStruct((m, ATT_OUT_WIDTH), jnp.float32),
        grid=(batch, n_chunks, N_GROUPS),
        in_specs=[cur(Q_SLAB0), cur(K_SLAB0), prev(K_SLAB0), cur(V_SLAB0), prev(V_SLAB0),
                  pl.BlockSpec(bias_tiles.shape, lambda b, c, g: (0, 0, 0, 0))],
        out_specs=pl.BlockSpec((ATT_CHUNK, ATT_OUT_WIDTH), lambda b, c, g: (b * n_chunks + c, 0)),
        scratch_shapes=[pltpu.VMEM((N_GROUPS, 2, ATT_CHUNK, LANES), jnp.float32),
                        pltpu.VMEM((N_GROUPS, 2, ATT_CHUNK, LANES), jnp.float32)],
        compiler_params=pltpu.CompilerParams(
            dimension_semantics=("parallel", "parallel", "arbitrary"), vmem_limit_bytes=VMEM_LIMIT),
        name="attn",
    )(slabs, slabs, slabs, slabs, slabs, bias_tiles)


def _t5_bucket(dist):
    max_exact = N_BUCKETS // 2
    d_f = jnp.maximum(dist, 1).astype(jnp.float32)
    large = max_exact + (jnp.log(d_f / max_exact) / math.log(MAX_DISTANCE / max_exact)
                         * (N_BUCKETS - max_exact)).astype(jnp.int32)
    large = jnp.minimum(large, N_BUCKETS - 1)
    return jnp.where(dist < max_exact, dist, large)


def _bias_tiles(rel_bias):
    dil = jnp.array(DILATIONS, jnp.int32)
    dist = dil[:, None] * jnp.arange(KEYS_PER_QUERY + 1, dtype=jnp.int32)[None, :]
    bucket = _t5_bucket(dist)
    bias = rel_bias.reshape(N_BUCKETS, N_GROUPS, HEADS_PER_GROUP)[bucket, jnp.arange(N_GROUPS)[:, None]]
    bias = jnp.transpose(bias, (0, 2, 1)).astype(jnp.float32).reshape(ATT_HEADS, KEYS_PER_QUERY + 1)
    n = 3 * QB - 1
    neg = lambda w: jnp.full((ATT_HEADS, w), NEG, jnp.float32)
    e = jnp.concatenate([neg(QB - 1), bias[:, ::-1], neg(QB - 1), neg(1)], axis=1)
    e = jnp.roll(e, -(QB - 1), axis=1)
    t0 = jnp.tile(e, (1, QB))[:, :QB * n].reshape(ATT_HEADS, QB, n)[:, :, :2 * QB]
    col = jnp.arange(2 * QB)[None, None, :]
    t1 = jnp.where(col >= QB, t0, NEG)
    return jnp.stack([t0, t1])


def _interleave(*gens):
    gens = list(gens)
    while gens:
        for gen in list(gens):
            try:
                next(gen)
            except StopIteration:
                gens.remove(gen)


def _rwkv_body(r_ref, k_ref, v_ref, lora_ref, fg_ref, pr_ref, sh_ref, wl_ref, o_ref,
               carry, h_st, up_s, zq_s, kcbt_s, g_s, vb_s, post_s, y_s):
    t_blk = o_ref.shape[0]
    cc = RWKV_CHUNK
    cs = range(t_blk // cc)
    i = pl.program_id(2)
    n_t = pl.num_programs(2) - 1
    slot_w = i % 2
    slot_r = 1 - slot_w

    @pl.when(i == 0)
    def _():
        for ref in (carry, h_st, up_s, zq_s, kcbt_s, g_s, vb_s, post_s):
            ref[...] = jnp.zeros(ref.shape, ref.dtype)

    pr = pr_ref[0]
    ln_w, ln_b = pr[8:9], pr[9:10]
    lane_head = lax.broadcasted_iota(jnp.int32, (QW, QW), 1) // HEAD_DIM
    row_head = lax.broadcasted_iota(jnp.int32, (QW, QW), 0) // HEAD_DIM
    same_head = lane_head == row_head
    head_ones = _bf16(jnp.where(same_head, 1.0, 0.0))
    chunk_lane_head = lax.broadcasted_iota(jnp.int32, (cc, QW), 1) // HEAD_DIM
    keep = [_bf16(jnp.where(chunk_lane_head == h, 1.0, 0.0)) for h in range(RWKV_QUAD)]

    def head_sum(x):
        return _dot(_bf16(x), head_ones)

    def stack4(xb):
        return jnp.concatenate([xb * keep[h] for h in range(RWKV_QUAD)], axis=0)

    def each(fn, *lists):
        return [fn(*args) for args in zip(*lists)]

    def split3(x):
        hi = _bf16(x)
        r1 = x - hi.astype(jnp.float32)
        mid = _bf16(r1)
        return hi, mid, _bf16(r1 - mid.astype(jnp.float32))

    def precompute():
        sh = sh_ref[...]

        def shifted(x, idx, mu):
            width = x.shape[1]
            row = lax.broadcasted_iota(jnp.int32, x.shape, 0)
            prev = pltpu.roll(x, 1, 0)
            prev = jnp.where(row == 0, carry[idx, 7:8, :width], prev)
            carry[idx, :, :width] = x[t_blk - 8:, :]
            return x + (prev - x) * mu

        wide = lambda ref: jnp.concatenate([ref[0], ref[1]], axis=1)
        r = shifted(wide(r_ref), 0, pr[0:1])
        k = shifted(wide(k_ref), 1, pr[1:2])
        v = shifted(wide(v_ref), 2, pr[2:3])
        f_lora = shifted(lora_ref[0], 3, sh[0:1, :LANES])
        f_g = shifted(fg_ref[0], 4, sh[1:2, :LANES])
        w0, a0, k_k, k_a, r_k = (pr[j:j + 1] for j in range(3, 8))

        wd = w0 + _dot(_bf16(jnp.tanh(f_lora)), wl_ref[0, 0])
        nwd = -wd
        softplus = jnp.maximum(nwd, 0.0) + jnp.log(1.0 + jnp.exp(-jnp.abs(nwd)))
        logw = -jnp.exp(-softplus - 0.5)
        a = jax.nn.sigmoid(a0 + _dot(_bf16(f_lora), wl_ref[0, 1]))
        kk = k * k_k
        kk = kk / jnp.maximum(jnp.sqrt(head_sum(kk * kk)), L2_EPS)
        k2 = k * (1.0 + (a - 1.0) * k_a)
        bb = kk * a
        post_s[slot_w, 0] = head_sum(r * k2 * r_k) * v
        post_s[slot_w, 1] = _dot(_bf16(jax.nn.sigmoid(f_g)), wl_ref[0, 2])
        yield

        t_i = lax.broadcasted_iota(jnp.int32, (cc, QW), 0)
        s_i = lax.broadcasted_iota(jnp.int32, (cc, QW), 1) % cc
        strict_lower = t_i > s_i
        lower = t_i >= s_i
        eye_f = jnp.where(t_i == s_i, 1.0, 0.0)
        zero_m = jnp.zeros((cc, QW), jnp.float32)
        ci = lax.broadcasted_iota(jnp.int32, (cc, cc), 0)
        cj = lax.broadcasted_iota(jnp.int32, (cc, cc), 1)
        tri = _bf16(jnp.where(ci >= cj, 1.0, 0.0))
        rows_of = lambda x: [x[c * cc:(c + 1) * cc] for c in cs]

        lw = rows_of(logw)
        parts = each(split3, lw)
        big_l = each(lambda p: _dot(tri, p[0]) + _dot(tri, p[1]) + _dot(tri, p[2]), parts)
        yield
        l_end = each(lambda l: l[cc - 1:cc], big_l)
        e_l = each(jnp.exp, big_l)
        e_lm = each(lambda l, w: jnp.exp(l - w), big_l, lw)
        e_nl = each(lambda l: jnp.exp(-l), big_l)
        e_c = each(lambda le, l: jnp.exp(le - l), l_end, big_l)
        a_kk = each(lambda x, e: _bf16(x * e), rows_of(kk), e_lm)
        a_r_f = each(lambda x, e: x * e, rows_of(r), e_l)
        a_r = each(_bf16, a_r_f)
        kb = each(lambda xk, xb, e: jnp.concatenate([stack4(_bf16(xk * e)), stack4(_bf16(xb * e))], axis=0),
                  rows_of(k2), rows_of(bb), e_nl)
        vb = each(_bf16, rows_of(v))
        vst = each(stack4, vb)
        yield
        s1 = each(_dot_nt, a_kk, kb)
        yield
        s2 = each(_dot_nt, a_r, kb)
        yield
        mk = each(lambda s: _bf16(jnp.where(strict_lower, s[:, :QW], zero_m)), s1)
        mb = each(lambda s: jnp.where(strict_lower, s[:, QW:], zero_m), s1)
        nkb = each(lambda s: _bf16(jnp.concatenate([jnp.where(lower, s[:, :QW], zero_m),
                                                     jnp.where(lower, -s[:, QW:], zero_m)], axis=1)), s2)

        x = each(lambda n: eye_f - jnp.where((t_i // 2) == (s_i // 2), n, zero_m), mb)
        blk = 4
        while blk <= cc:
            half = blk // 2
            sel = ((t_i // blk) == (s_i // blk)) & ((t_i % blk) >= half) & ((s_i % blk) < half)
            xb = each(_bf16, x)
            t1 = each(lambda xc, n: _bf16(_dot(xc, stack4(_bf16(jnp.where(sel, n, zero_m))))), xb, mb)
            yield
            x = each(lambda xf, t, xc: xf - _dot(t, stack4(xc)), x, t1, xb)
            yield
            blk *= 2
        tinv = each(_bf16, x)

        pm = each(lambda t, ak: _dot(t, stack4(ak)), tinv, a_kk)
        mkv = each(lambda m_, v_: _bf16(_dot(m_, v_)), mk, vst)
        yield
        q = each(lambda t, m_: _dot(t, stack4(m_)), tinv, mkv)
        yield
        pmb = each(_bf16, pm)
        u = each(lambda af, n, p: af + _dot(n[:, QW:], stack4(p)), a_r_f, nkb, pmb)
        z = each(lambda n, v_, q_: _dot(n, jnp.concatenate([v_, stack4(_bf16(q_))], axis=0)), nkb, vst, q)
        yield
        for c in cs:
            up_s[slot_w, c] = jnp.concatenate([_bf16(u[c]), pmb[c]], axis=0)
            zq_s[slot_w, c, 0] = z[c]
            zq_s[slot_w, c, 1] = q[c]
            vb_s[slot_w, c] = vb[c]
        yield
        kcg = each(lambda xk, xb, e, le: jnp.concatenate(
            [xk * e, -(xb * e), jnp.broadcast_to(jnp.exp(le), (2 * cc, QW))], axis=0).T,
            rows_of(k2), rows_of(bb), e_c, l_end)
        for c in cs:
            kcbt_s[slot_w, c] = _bf16(kcg[c][:, :2 * cc])
            g_s[slot_w, c] = kcg[c][:, 2 * cc:]
        yield

    def sequential():
        h = h_st[...]
        for c in cs:
            ys = _dot(up_s[slot_r, c], _bf16(h))
            y_s[c * cc:(c + 1) * cc, :] = ys[:cc] + zq_s[slot_r, c, 0]
            sig = ys[cc:] + zq_s[slot_r, c, 1]
            yield
            hu = _dot(kcbt_s[slot_r, c], jnp.concatenate([vb_s[slot_r, c], _bf16(sig)], axis=0))
            g = g_s[slot_r, c]
            h = h * jnp.concatenate([g, g], axis=1) + jnp.where(same_head, hu, jnp.zeros_like(hu))
            yield
        h_st[...] = h
        y = y_s[...]
        mu = head_sum(y) * (1.0 / HEAD_DIM)
        yc = y - mu
        var = head_sum(yc * yc) * (1.0 / HEAD_DIM)
        yn = yc * lax.rsqrt(var + LN_X_EPS) * ln_w + ln_b
        o_ref[...] = (yn + post_s[slot_r, 0]) * post_s[slot_r, 1]
        yield

    @pl.when(i < n_t)
    def _():
        _interleave(precompute(), sequential())

    @pl.when(i == n_t)
    def _():
        _interleave(sequential())


def _rwkv(slabs, pr, sh, wl, batch, seq):
    m = slabs.shape[1]
    t_blk = RWKV_T
    n_t = seq // t_blk
    n_c = t_blk // RWKV_CHUNK
    slabs_per_quad = QW // LANES
    t_in = lambda b, t: b * n_t + jnp.minimum(t, n_t - 1)

    def tok(slab0, per_quad):
        if per_quad:
            return pl.BlockSpec((slabs_per_quad, t_blk, LANES),
                                lambda b, qd, t: (slab0 // slabs_per_quad + qd, t_in(b, t), 0))
        return pl.BlockSpec((1, t_blk, LANES), lambda b, qd, t: (slab0, t_in(b, t), 0))

    return pl.pallas_call(
        _rwkv_body,
        out_shape=jax.ShapeDtypeStruct((m, RWKV_WIDTH), jnp.float32),
        grid=(batch, N_QUADS, n_t + 1),
        in_specs=[tok(R_SLAB0, True), tok(RK_SLAB0, True), tok(RV_SLAB0, True),
                  tok(LORA_SLAB, False), tok(FG_SLAB, False),
                  pl.BlockSpec((1,) + pr.shape[1:], lambda b, qd, t: (qd, 0, 0)),
                  pl.BlockSpec(sh.shape, lambda b, qd, t: (0, 0)),
                  pl.BlockSpec((1,) + wl.shape[1:], lambda b, qd, t: (qd, 0, 0, 0))],
        out_specs=pl.BlockSpec((t_blk, QW), lambda b, qd, t: (b * n_t + jnp.maximum(t - 1, 0), qd)),
        scratch_shapes=[
            pltpu.VMEM((5, 8, QW), jnp.float32),
            pltpu.VMEM((QW, QW), jnp.float32),
            pltpu.VMEM((2, n_c, 2 * RWKV_CHUNK, QW), jnp.bfloat16),
            pltpu.VMEM((2, n_c, 2, RWKV_CHUNK, QW), jnp.float32),
            pltpu.VMEM((2, n_c, QW, LANES), jnp.bfloat16),
            pltpu.VMEM((2, n_c, QW, LANES), jnp.float32),
            pltpu.VMEM((2, n_c, RWKV_CHUNK, QW), jnp.bfloat16),
            pltpu.VMEM((2, 2, t_blk, QW), jnp.float32),
            pltpu.VMEM((t_blk, QW), jnp.float32),
        ],
        compiler_params=pltpu.CompilerParams(
            dimension_semantics=("parallel", "parallel", "arbitrary"), vmem_limit_bytes=VMEM_LIMIT),
        name="rwkv",
    )(slabs, slabs, slabs, slabs, slabs, pr, sh, wl)


def _rwkv_params(shift_mu, w0, a0, k_k, k_a, r_k, ln_w, ln_b, w_w2, w_a2, w_g2):
    def quads(vec):
        return vec.reshape(N_QUADS, QW)

    mu_r, mu_k, mu_v = (quads(shift_mu[j * RWKV_WIDTH:(j + 1) * RWKV_WIDTH]) for j in range(3))
    rows = [mu_r, mu_k, mu_v, quads(w0), quads(a0), quads(k_k), quads(k_a), quads(r_k.reshape(-1)),
            quads(ln_w), quads(ln_b)]
    pr = jnp.stack(rows, axis=1)
    pr = jnp.pad(pr, ((0, 0), (0, 16 - pr.shape[1]), (0, 0)))
    mu_rest = shift_mu[3 * RWKV_WIDTH:]
    sh = jnp.pad(mu_rest.reshape(2, LANES), ((0, 6), (0, QW - LANES)))
    zeros = jnp.zeros((DECAY_LORA, RWKV_WIDTH), jnp.float32)
    wd = jnp.concatenate([w_w2, zeros], axis=0)
    wa = jnp.concatenate([zeros, w_a2], axis=0)
    wl = jnp.stack([wd, wa, w_g2], axis=0)
    wl = wl.reshape(3, LANES, N_QUADS, QW).transpose(2, 0, 1, 3)
    return pr, sh, _bf16(wl)


def _rms(x, g):
    ms = jnp.mean(x * x, axis=-1, keepdims=True)
    return x * lax.rsqrt(ms + NORM_EPS) * g


def _merge_body(x_ref, att_ref, rw_ref, gate_ref, bg_ref, wa_ref, wr_ref, wo_ref, gn_ref, o_ref):
    n_gs = D_MODEL // LANES
    f0 = jnp.concatenate([gate_ref[s] for s in range(n_gs)], axis=1)
    f1 = jnp.concatenate([gate_ref[n_gs + s] for s in range(n_gs)], axis=1)
    g0 = jax.nn.sigmoid(f0 + bg_ref[:, :D_MODEL])
    g1 = jax.nn.sigmoid(f1 + bg_ref[:, D_MODEL:])
    merged = g0 * _dot(_bf16(att_ref[...]), wa_ref[...]) + g1 * _dot(_bf16(rw_ref[...]), wr_ref[...])
    z = _dot(_bf16(merged), wo_ref[...])
    o_ref[...] = x_ref[...] + _rms(z, gn_ref[...])


def _merge(x2, o_att, o_rwkv, slabs, b_gate, wa, wr, wo, g_post, tm):
    m = x2.shape[0]
    n_gate_slabs = GATE_COLS // LANES
    const = lambda shape: pl.BlockSpec(shape, lambda i: (0, 0))
    return pl.pallas_call(
        _merge_body,
        out_shape=jax.ShapeDtypeStruct((m, D_MODEL), jnp.float32),
        grid=(m // tm,),
        in_specs=[
            pl.BlockSpec((tm, D_MODEL), lambda i: (i, 0)),
            pl.BlockSpec((tm, ATT_OUT_WIDTH), lambda i: (i, 0)),
            pl.BlockSpec((tm, RWKV_WIDTH), lambda i: (i, 0)),
            pl.BlockSpec((n_gate_slabs, tm, LANES), lambda i: (GATE_SLAB0 // n_gate_slabs, i, 0)),
            const((1, GATE_COLS)), const(wa.shape), const(wr.shape), const(wo.shape), const((1, D_MODEL)),
        ],
        out_specs=pl.BlockSpec((tm, D_MODEL), lambda i: (i, 0)),
        compiler_params=pltpu.CompilerParams(
            dimension_semantics=("parallel",), vmem_limit_bytes=VMEM_LIMIT),
        name="merge",
    )(x2, o_att, o_rwkv, slabs, b_gate, wa, wr, wo, g_post)


def _ffn_body(x_ref, gpre_ref, w1_ref, w2_ref, gpost_ref, o_ref, *, tf):
    x = x_ref[...]
    h = _bf16(_rms(x, gpre_ref[...]))
    acc = jnp.zeros(x.shape, jnp.float32)
    for c in range(D_FF // tf):
        u = jnp.maximum(_dot(h, w1_ref[:, c * tf:(c + 1) * tf]), 0.0)
        acc = acc + _dot(_bf16(u * u), w2_ref[c * tf:(c + 1) * tf, :])
    o_ref[...] = x + _rms(acc, gpost_ref[...])


def _ffn(x2, g_pre, w1, w2, g_post, tm, tf=1024):
    m = x2.shape[0]
    const = lambda shape: pl.BlockSpec(shape, lambda i: (0, 0))
    return pl.pallas_call(
        functools.partial(_ffn_body, tf=tf),
        out_shape=jax.ShapeDtypeStruct((m, D_MODEL), jnp.float32),
        grid=(m // tm,),
        in_specs=[pl.BlockSpec((tm, D_MODEL), lambda i: (i, 0)), const((1, D_MODEL)),
                  const(w1.shape), const(w2.shape), const((1, D_MODEL))],
        out_specs=pl.BlockSpec((tm, D_MODEL), lambda i: (i, 0)),
        compiler_params=pltpu.CompilerParams(
            dimension_semantics=("parallel",), vmem_limit_bytes=VMEM_LIMIT),
        name="ffn",
    )(x2, g_pre, w1, w2, g_post)


def kernel(x, rel_bias, norm_mix_pre, norm_mix_post, norm_ffn_pre, norm_ffn_post, w_in, b_gate, shift_mu, w0, w_w2, a0, w_a2, w_g2, k_k, k_a, r_k, ln_x_w, ln_x_b, w_att_branch, w_rwkv_branch, w_out, w_ffn1, w_ffn2):
    batch, seq, d_model = x.shape
    assert d_model == D_MODEL and seq % ATT_CHUNK == 0 and seq % RWKV_T == 0
    m = batch * seq
    tm = ROW_TILE
    assert m % PROJ_TM == 0 and m % tm == 0
    bias_tiles = _bias_tiles(rel_bias)
    row = lambda vec: vec.reshape(1, -1)
    x2 = x.reshape(m, D_MODEL)
    for l in range(w_in.shape[0]):
        slabs = _proj(x2, row(norm_mix_pre[l]), _bf16(w_in[l]), PROJ_TM)
        o_att = _attn(slabs, bias_tiles, batch, seq)
        pr, sh, wl = _rwkv_params(shift_mu[l], w0[l], a0[l], k_k[l], k_a[l], r_k[l], ln_x_w[l], ln_x_b[l],
                                  w_w2[l], w_a2[l], w_g2[l])
        o_rwkv = _rwkv(slabs, pr, sh, wl, batch, seq)
        x2 = _merge(x2, o_att, o_rwkv, slabs, row(b_gate[l]), _bf16(w_att_branch[l]), _bf16(w_rwkv_branch[l]),
                    _bf16(w_out[l]), row(norm_mix_post[l]), tm)
        x2 = _ffn(x2, row(norm_ffn_pre[l]), _bf16(w_ffn1[l]), _bf16(w_ffn2[l]), row(norm_ffn_post[l]), tm)
    return x2.reshape(batch, seq, D_MODEL)
```

```python
import functools
import math

import jax
import jax.numpy as jnp
from jax import lax
from jax.experimental import pallas as pl
from jax.experimental.pallas import tpu as pltpu

D_MODEL = 1024
HEAD_DIM = 64
DILATIONS = (1, 4, 16)
KEYS_PER_QUERY = 128
N_GROUPS = len(DILATIONS)
HEADS_PER_GROUP = 4
ATT_HEADS = N_GROUPS * HEADS_PER_GROUP
ATT_WIDTH = ATT_HEADS * HEAD_DIM
ATT_OUT_WIDTH = HEADS_PER_GROUP * HEAD_DIM
N_BUCKETS = 32
MAX_DISTANCE = KEYS_PER_QUERY * DILATIONS[-1]
RWKV_WIDTH = D_MODEL
DECAY_LORA = 64
ICLR_LORA = 64
GATE_LORA = 128
RWKV_COLS = 3 * RWKV_WIDTH + DECAY_LORA + ICLR_LORA + GATE_LORA
N_BRANCHES = 2
IN_COLS = 3 * ATT_WIDTH + RWKV_COLS + N_BRANCHES * D_MODEL
D_FF = 4 * D_MODEL
NORM_EPS = 1e-6
LN_X_EPS = 64e-5
L2_EPS = 1e-12

LANES = 128
VMEM_LIMIT = 56 * 1024 * 1024
NEG = -1e30

GATE_COLS = N_BRANCHES * D_MODEL
Q_SLAB0 = 0
K_SLAB0 = Q_SLAB0 + ATT_WIDTH // LANES
V_SLAB0 = K_SLAB0 + ATT_WIDTH // LANES
LORA_SLAB = V_SLAB0 + ATT_WIDTH // LANES
FG_SLAB = LORA_SLAB + 1
N_SLABS_F32 = FG_SLAB + 1
GATE_SLAB0 = 0
R_SLAB0 = GATE_COLS // LANES
RK_SLAB0 = R_SLAB0 + RWKV_WIDTH // LANES
RV_SLAB0 = RK_SLAB0 + RWKV_WIDTH // LANES
N_SLABS_BF16 = RV_SLAB0 + RWKV_WIDTH // LANES
RWKV_QUAD = 4
QW = RWKV_QUAD * HEAD_DIM
N_QUADS = RWKV_WIDTH // QW

PROJ_TN = 512
PROJ_TM = 2048
ROW_TILE = 512
ATT_CHUNK = KEYS_PER_QUERY * DILATIONS[-1]
QB = KEYS_PER_QUERY
RWKV_CHUNK = 64
RWKV_T = 512


def _bf16(x):
    return x.astype(jnp.bfloat16)


def _dot(a, b):
    return jnp.dot(a, b, preferred_element_type=jnp.float32)


def _dot_nt(a, b):
    return lax.dot_general(a, b, (((1,), (1,)), ((), ())), preferred_element_type=jnp.float32)


def _proj_body(x_ref, g_ref, w_ref, o32_ref, o16_ref, h_scr, *, n_f32_blocks):
    j = pl.program_id(1)

    @pl.when(j == 0)
    def _():
        x = x_ref[...]
        ms = jnp.mean(x * x, axis=-1, keepdims=True)
        h_scr[...] = _bf16(x * lax.rsqrt(ms + NORM_EPS) * g_ref[...])

    acc = _dot(h_scr[...], w_ref[0])

    @pl.when(j < n_f32_blocks)
    def _():
        for s in range(PROJ_TN // LANES):
            o32_ref[s] = acc[:, s * LANES:(s + 1) * LANES]

    @pl.when(j >= n_f32_blocks)
    def _():
        for s in range(PROJ_TN // LANES):
            o16_ref[s] = _bf16(acc[:, s * LANES:(s + 1) * LANES])


def _proj(x2, g, w_in, tm):
    m = x2.shape[0]
    n_col_blocks = IN_COLS // PROJ_TN
    slabs_per_block = PROJ_TN // LANES
    n_f32_blocks = N_SLABS_F32 // slabs_per_block
    att_end = 3 * ATT_WIDTH
    rkv_end = att_end + 3 * RWKV_WIDTH
    lora_end = att_end + RWKV_COLS
    w_cols = jnp.concatenate([w_in[:, :att_end], w_in[:, rkv_end:lora_end], w_in[:, lora_end:],
                              w_in[:, att_end:rkv_end]], axis=1)
    w_blk = _bf16(w_cols).reshape(D_MODEL, n_col_blocks, PROJ_TN).transpose(1, 0, 2)
    blk = (slabs_per_block, tm, LANES)
    return pl.pallas_call(
        functools.partial(_proj_body, n_f32_blocks=n_f32_blocks),
        out_shape=(jax.ShapeDtypeStruct((N_SLABS_F32, m, LANES), jnp.float32),
                   jax.ShapeDtypeStruct((N_SLABS_BF16, m, LANES), jnp.bfloat16)),
        grid=(m // tm, n_col_blocks),
        in_specs=[
            pl.BlockSpec((tm, D_MODEL), lambda i, j: (i, 0)),
            pl.BlockSpec((1, D_MODEL), lambda i, j: (0, 0)),
            pl.BlockSpec((1, D_MODEL, PROJ_TN), lambda i, j: (j, 0, 0)),
        ],
        out_specs=(pl.BlockSpec(blk, lambda i, j: (jnp.minimum(j, n_f32_blocks - 1), i, 0)),
                   pl.BlockSpec(blk, lambda i, j: (jnp.maximum(j - n_f32_blocks, 0), i, 0))),
        scratch_shapes=[pltpu.VMEM((tm, D_MODEL), jnp.bfloat16)],
        compiler_params=pltpu.CompilerParams(
            dimension_semantics=("parallel", "arbitrary"), vmem_limit_bytes=VMEM_LIMIT),
        name="proj",
    )(x2, g, w_blk)


def _attn_unit(d, q, k, v, bias0, bias1, lo):
    scale = HEAD_DIM ** -0.5
    qs = q * scale
    kb = _bf16(k)
    zero = jnp.zeros_like(qs)
    s0 = _dot_nt(_bf16(jnp.where(lo, qs, zero)), kb) + bias0
    s1 = _dot_nt(_bf16(jnp.where(lo, zero, qs)), kb) + bias1
    m0 = jnp.max(s0, axis=-1, keepdims=True)
    m1 = jnp.max(s1, axis=-1, keepdims=True)
    p0 = _bf16(jnp.exp(s0 - m0))
    p1 = _bf16(jnp.exp(s1 - m1))
    zv = jnp.zeros_like(v)
    ones = jnp.ones_like(v)
    rhs0 = _bf16(jnp.concatenate([jnp.where(lo, v, zv), jnp.where(lo, ones, zv)], axis=1))
    rhs1 = _bf16(jnp.concatenate([jnp.where(lo, zv, v), jnp.where(lo, zv, ones)], axis=1))
    ol = _dot(p0, rhs0) + _dot(p1, rhs1)
    o = ol[:, :LANES]
    l = ol[:, LANES:]
    m = jnp.where(lo, m0, m1)
    return o / l, m + jnp.log(l)


def _attn_body(q_ref, kc_ref, kp_ref, vc_ref, vp_ref, bias_ref, o_ref, o_scr, l_scr):
    c = pl.program_id(1)
    g = pl.program_id(2)
    lo = lax.broadcasted_iota(jnp.int32, (1, LANES), 1) < HEAD_DIM
    first = jnp.where(c == 0, 1, 0)

    def rows(start, n, d):
        return pl.ds(start, n) if d == 1 else pl.ds(start, n, stride=d)

    def group(gi):
        d = DILATIONS[gi]
        n_qb = ATT_CHUNK // (QB * d)

        def unit(p, start_q, k, v, variant):
            h0 = gi * HEADS_PER_GROUP + 2 * p
            q = q_ref[p, rows(start_q, QB, d), :]
            o, lse = _attn_unit(d, q, k, v, bias_ref[variant, h0], bias_ref[variant, h0 + 1], lo)
            o_scr[gi, p, rows(start_q, QB, d), :] = o
            l_scr[gi, p, rows(start_q, QB, d), :] = lse

        def first_blocks(r, carry):
            prev_start = r + QB * d * (n_qb - 1)
            for p in range(2):
                k = jnp.concatenate([kp_ref[p, rows(prev_start, QB, d), :], kc_ref[p, rows(r, QB, d), :]], axis=0)
                v = jnp.concatenate([vp_ref[p, rows(prev_start, QB, d), :], vc_ref[p, rows(r, QB, d), :]], axis=0)
                unit(p, r, k, v, first)
            return carry

        lax.fori_loop(0, d, first_blocks, 0)

        if n_qb > 1:
            def later_blocks(u, carry):
                r = u % d
                qb = 1 + u // d
                start_q = r + QB * d * qb
                for p in range(2):
                    k = kc_ref[p, rows(start_q - QB * d, 2 * QB, d), :]
                    v = vc_ref[p, rows(start_q - QB * d, 2 * QB, d), :]
                    unit(p, start_q, k, v, 0)
                return carry

            lax.fori_loop(0, d * (n_qb - 1), later_blocks, 0)

    for gi in range(N_GROUPS):
        pl.when(g == gi)(functools.partial(group, gi))

    @pl.when(g == N_GROUPS - 1)
    def _():
        tile = 256

        def comb(i, carry):
            rs = pl.ds(pl.multiple_of(i * tile, tile), tile)
            for p in range(2):
                ls = [l_scr[gi, p, rs, :] for gi in range(N_GROUPS)]
                mx = jnp.maximum(jnp.maximum(ls[0], ls[1]), ls[2])
                ws = [jnp.exp(l - mx) for l in ls]
                num = ws[0] * o_scr[0, p, rs, :] + ws[1] * o_scr[1, p, rs, :] + ws[2] * o_scr[2, p, rs, :]
                o_ref[rs, p * LANES:(p + 1) * LANES] = _bf16(num / (ws[0] + ws[1] + ws[2]))
            return carry

        lax.fori_loop(0, ATT_CHUNK // tile, comb, 0)


def _attn(slabs, bias_tiles, batch, seq):
    m = slabs.shape[1]
    n_chunks = seq // ATT_CHUNK
    blk = (2, ATT_CHUNK, LANES)

    def cur(slab0):
        return pl.BlockSpec(blk, lambda b, c, g: (slab0 // 2 + g, b * n_chunks + c, 0))

    def prev(slab0):
        return pl.BlockSpec(blk, lambda b, c, g: (slab0 // 2 + g, b * n_chunks + jnp.maximum(c - 1, 0), 0))

    return pl.pallas_call(
        _attn_body,
        out_shape=jax.ShapeDtypeStruct((m, ATT_OUT_WIDTH), jnp.bfloat16),
        grid=(batch, n_chunks, N_GROUPS),
        in_specs=[cur(Q_SLAB0), cur(K_SLAB0), prev(K_SLAB0), cur(V_SLAB0), prev(V_SLAB0),
                  pl.BlockSpec(bias_tiles.shape, lambda b, c, g: (0, 0, 0, 0))],
        out_specs=pl.BlockSpec((ATT_CHUNK, ATT_OUT_WIDTH), lambda b, c, g: (b * n_chunks + c, 0)),
        scratch_shapes=[pltpu.VMEM((N_GROUPS, 2, ATT_CHUNK, LANES), jnp.float32),
                        pltpu.VMEM((N_GROUPS, 2, ATT_CHUNK, LANES), jnp.float32)],
        compiler_params=pltpu.CompilerParams(
            dimension_semantics=("parallel", "parallel", "arbitrary"), vmem_limit_bytes=VMEM_LIMIT),
        name="attn",
    )(slabs, slabs, slabs, slabs, slabs, bias_tiles)


def _t5_bucket(dist):
    max_exact = N_BUCKETS // 2
    d_f = jnp.maximum(dist, 1).astype(jnp.float32)
    large = max_exact + (jnp.log(d_f / max_exact) / math.log(MAX_DISTANCE / max_exact)
                         * (N_BUCKETS - max_exact)).astype(jnp.int32)
    large = jnp.minimum(large, N_BUCKETS - 1)
    return jnp.where(dist < max_exact, dist, large)


def _bias_tiles(rel_bias):
    dil = jnp.array(DILATIONS, jnp.int32)
    dist = dil[:, None] * jnp.arange(KEYS_PER_QUERY + 1, dtype=jnp.int32)[None, :]
    bucket = _t5_bucket(dist)
    bias = rel_bias.reshape(N_BUCKETS, N_GROUPS, HEADS_PER_GROUP)[bucket, jnp.arange(N_GROUPS)[:, None]]
    bias = jnp.transpose(bias, (0, 2, 1)).astype(jnp.float32).reshape(ATT_HEADS, KEYS_PER_QUERY + 1)
    n = 3 * QB - 1
    neg = lambda w: jnp.full((ATT_HEADS, w), NEG, jnp.float32)
    e = jnp.concatenate([neg(QB - 1), bias[:, ::-1], neg(QB - 1), neg(1)], axis=1)
    e = jnp.roll(e, -(QB - 1), axis=1)
    t0 = jnp.tile(e, (1, QB))[:, :QB * n].reshape(ATT_HEADS, QB, n)[:, :, :2 * QB]
    col = jnp.arange(2 * QB)[None, None, :]
    t1 = jnp.where(col >= QB, t0, NEG)
    return jnp.stack([t0, t1])


def _interleave(*gens):
    gens = list(gens)
    while gens:
        for gen in list(gens):
            try:
                next(gen)
            except StopIteration:
                gens.remove(gen)


def _rwkv_body(r_ref, k_ref, v_ref, lora_ref, fg_ref, pr_ref, sh_ref, wl_ref, o_ref,
               carry, h_st, up_s, zq_s, kcbt_s, g_s, vb_s, post_s, y_s):
    t_blk = o_ref.shape[0]
    cc = RWKV_CHUNK
    cs = range(t_blk // cc)
    i = pl.program_id(2)
    n_t = pl.num_programs(2) - 1
    slot_w = i % 2
    slot_r = 1 - slot_w

    @pl.when(i == 0)
    def _():
        for ref in (carry, h_st, up_s, zq_s, kcbt_s, g_s, vb_s, post_s):
            ref[...] = jnp.zeros(ref.shape, ref.dtype)

    pr = pr_ref[0]
    ln_w, ln_b = pr[8:9], pr[9:10]
    lane_head = lax.broadcasted_iota(jnp.int32, (QW, QW), 1) // HEAD_DIM
    row_head = lax.broadcasted_iota(jnp.int32, (QW, QW), 0) // HEAD_DIM
    same_head = lane_head == row_head
    head_ones = _bf16(jnp.where(same_head, 1.0, 0.0))
    chunk_lane_head = lax.broadcasted_iota(jnp.int32, (cc, QW), 1) // HEAD_DIM
    keep = [_bf16(jnp.where(chunk_lane_head == h, 1.0, 0.0)) for h in range(RWKV_QUAD)]

    def head_sum(x):
        return _dot(_bf16(x), head_ones)

    def stack4(xb):
        return jnp.concatenate([xb * keep[h] for h in range(RWKV_QUAD)], axis=0)

    def each(fn, *lists):
        return [fn(*args) for args in zip(*lists)]

    def split3(x):
        hi = _bf16(x)
        r1 = x - hi.astype(jnp.float32)
        mid = _bf16(r1)
        return hi, mid, _bf16(r1 - mid.astype(jnp.float32))

    def precompute():
        sh = sh_ref[...]

        def shifted(x, idx, mu):
            width = x.shape[1]
            row = lax.broadcasted_iota(jnp.int32, x.shape, 0)
            prev = pltpu.roll(x, 1, 0)
            prev = jnp.where(row == 0, carry[idx, 7:8, :width], prev)
            carry[idx, :, :width] = x[t_blk - 8:, :]
            return x + (prev - x) * mu

        wide = lambda ref: jnp.concatenate([ref[0], ref[1]], axis=1).astype(jnp.float32)
        r = shifted(wide(r_ref), 0, pr[0:1])
        k = shifted(wide(k_ref), 1, pr[1:2])
        v = shifted(wide(v_ref), 2, pr[2:3])
        f_lora = shifted(lora_ref[0], 3, sh[0:1, :LANES])
        f_g = shifted(fg_ref[0], 4, sh[1:2, :LANES])
        w0, a0, k_k, k_a, r_k = (pr[j:j + 1] for j in range(3, 8))

        wd = w0 + _dot(_bf16(jnp.tanh(f_lora)), wl_ref[0, 0])
        nwd = -wd
        softplus = jnp.maximum(nwd, 0.0) + jnp.log(1.0 + jnp.exp(-jnp.abs(nwd)))
        logw = -jnp.exp(-softplus - 0.5)
        a = jax.nn.sigmoid(a0 + _dot(_bf16(f_lora), wl_ref[0, 1]))
        kk = k * k_k
        kk = kk / jnp.maximum(jnp.sqrt(head_sum(kk * kk)), L2_EPS)
        k2 = k * (1.0 + (a - 1.0) * k_a)
        bb = kk * a
        post_s[slot_w, 0] = head_sum(r * k2 * r_k) * v
        post_s[slot_w, 1] = _dot(_bf16(jax.nn.sigmoid(f_g)), wl_ref[0, 2])
        yield

        t_i = lax.broadcasted_iota(jnp.int32, (cc, QW), 0)
        s_i = lax.broadcasted_iota(jnp.int32, (cc, QW), 1) % cc
        strict_lower = t_i > s_i
        lower = t_i >= s_i
        eye_f = jnp.where(t_i == s_i, 1.0, 0.0)
        zero_m = jnp.zeros((cc, QW), jnp.float32)
        ci = lax.broadcasted_iota(jnp.int32, (cc, cc), 0)
        cj = lax.broadcasted_iota(jnp.int32, (cc, cc), 1)
        tri = _bf16(jnp.where(ci >= cj, 1.0, 0.0))
        rows_of = lambda x: [x[c * cc:(c + 1) * cc] for c in cs]

        lw = rows_of(logw)
        parts = each(split3, lw)
        big_l = each(lambda p: _dot(tri, p[0]) + _dot(tri, p[1]) + _dot(tri, p[2]), parts)
        yield
        l_end = each(lambda l: l[cc - 1:cc], big_l)
        e_l = each(jnp.exp, big_l)
        e_lm = each(lambda l, w: jnp.exp(l - w), big_l, lw)
        e_nl = each(lambda l: jnp.exp(-l), big_l)
        e_c = each(lambda le, l: jnp.exp(le - l), l_end, big_l)
        a_kk = each(lambda x, e: _bf16(x * e), rows_of(kk), e_lm)
        a_r_f = each(lambda x, e: x * e, rows_of(r), e_l)
        a_r = each(_bf16, a_r_f)
        kb = each(lambda xk, xb, e: jnp.concatenate([stack4(_bf16(xk * e)), stack4(_bf16(xb * e))], axis=0),
                  rows_of(k2), rows_of(bb), e_nl)
        vb = each(_bf16, rows_of(v))
        vst = each(stack4, vb)
        yield
        s1 = each(_dot_nt, a_kk, kb)
        yield
        s2 = each(_dot_nt, a_r, kb)
        yield
        mk = each(lambda s: _bf16(jnp.where(strict_lower, s[:, :QW], zero_m)), s1)
        mb = each(lambda s: jnp.where(strict_lower, s[:, QW:], zero_m), s1)
        nkb = each(lambda s: _bf16(jnp.concatenate([jnp.where(lower, s[:, :QW], zero_m),
                                                     jnp.where(lower, -s[:, QW:], zero_m)], axis=1)), s2)

        x = each(lambda n: eye_f - jnp.where((t_i // 2) == (s_i // 2), n, zero_m), mb)
        blk = 4
        while blk <= cc:
            half = blk // 2
            sel = ((t_i // blk) == (s_i // blk)) & ((t_i % blk) >= half) & ((s_i % blk) < half)
            xb = each(_bf16, x)
            t1 = each(lambda xc, n: _bf16(_dot(xc, stack4(_bf16(jnp.where(sel, n, zero_m))))), xb, mb)
            yield
            x = each(lambda xf, t, xc: xf - _dot(t, stack4(xc)), x, t1, xb)
            yield
            blk *= 2
        tinv = each(_bf16, x)

        pm = each(lambda t, ak: _dot(t, stack4(ak)), tinv, a_kk)
        mkv = each(lambda m_, v_: _bf16(_dot(m_, v_)), mk, vst)
        yield
        q = each(lambda t, m_: _dot(t, stack4(m_)), tinv, mkv)
        yield
        pmb = each(_bf16, pm)
        u = each(lambda af, n, p: af + _dot(n[:, QW:], stack4(p)), a_r_f, nkb, pmb)
        z = each(lambda n, v_, q_: _dot(n, jnp.concatenate([v_, stack4(_bf16(q_))], axis=0)), nkb, vst, q)
        yield
        for c in cs:
            up_s[slot_w, c] = jnp.concatenate([_bf16(u[c]), pmb[c]], axis=0)
            zq_s[slot_w, c, 0] = z[c]
            zq_s[slot_w, c, 1] = q[c]
            vb_s[slot_w, c] = vb[c]
        yield
        kcg = each(lambda xk, xb, e, le: jnp.concatenate(
            [xk * e, -(xb * e), jnp.broadcast_to(jnp.exp(le), (2 * cc, QW))], axis=0).T,
            rows_of(k2), rows_of(bb), e_c, l_end)
        for c in cs:
            kcbt_s[slot_w, c] = _bf16(kcg[c][:, :2 * cc])
            g_s[slot_w, c] = kcg[c][:, 2 * cc:]
        yield

    def sequential():
        h = h_st[...]
        for c in cs:
            ys = _dot(up_s[slot_r, c], _bf16(h))
            y_s[c * cc:(c + 1) * cc, :] = ys[:cc] + zq_s[slot_r, c, 0]
            sig = ys[cc:] + zq_s[slot_r, c, 1]
            yield
            hu = _dot(kcbt_s[slot_r, c], jnp.concatenate([vb_s[slot_r, c], _bf16(sig)], axis=0))
            g = g_s[slot_r, c]
            h = h * jnp.concatenate([g, g], axis=1) + jnp.where(same_head, hu, jnp.zeros_like(hu))
            yield
        h_st[...] = h
        y = y_s[...]
        mu = head_sum(y) * (1.0 / HEAD_DIM)
        yc = y - mu
        var = head_sum(yc * yc) * (1.0 / HEAD_DIM)
        yn = yc * lax.rsqrt(var + LN_X_EPS) * ln_w + ln_b
        o_ref[...] = _bf16((yn + post_s[slot_r, 0]) * post_s[slot_r, 1])
        yield

    @pl.when(i < n_t)
    def _():
        _interleave(precompute(), sequential())

    @pl.when(i == n_t)
    def _():
        _interleave(sequential())


def _rwkv(slabs32, slabs16, pr, sh, wl, batch, seq):
    m = slabs32.shape[1]
    t_blk = RWKV_T
    n_t = seq // t_blk
    n_c = t_blk // RWKV_CHUNK
    slabs_per_quad = QW // LANES
    t_in = lambda b, t: b * n_t + jnp.minimum(t, n_t - 1)

    def tok(slab0, per_quad):
        if per_quad:
            return pl.BlockSpec((slabs_per_quad, t_blk, LANES),
                                lambda b, qd, t: (slab0 // slabs_per_quad + qd, t_in(b, t), 0))
        return pl.BlockSpec((1, t_blk, LANES), lambda b, qd, t: (slab0, t_in(b, t), 0))

    return pl.pallas_call(
        _rwkv_body,
        out_shape=jax.ShapeDtypeStruct((m, RWKV_WIDTH), jnp.bfloat16),
        grid=(batch, N_QUADS, n_t + 1),
        in_specs=[tok(R_SLAB0, True), tok(RK_SLAB0, True), tok(RV_SLAB0, True),
                  tok(LORA_SLAB, False), tok(FG_SLAB, False),
                  pl.BlockSpec((1,) + pr.shape[1:], lambda b, qd, t: (qd, 0, 0)),
                  pl.BlockSpec(sh.shape, lambda b, qd, t: (0, 0)),
                  pl.BlockSpec((1,) + wl.shape[1:], lambda b, qd, t: (qd, 0, 0, 0))],
        out_specs=pl.BlockSpec((t_blk, QW), lambda b, qd, t: (b * n_t + jnp.maximum(t - 1, 0), qd)),
        scratch_shapes=[
            pltpu.VMEM((5, 8, QW), jnp.float32),
            pltpu.VMEM((QW, QW), jnp.float32),
            pltpu.VMEM((2, n_c, 2 * RWKV_CHUNK, QW), jnp.bfloat16),
            pltpu.VMEM((2, n_c, 2, RWKV_CHUNK, QW), jnp.float32),
            pltpu.VMEM((2, n_c, QW, LANES), jnp.bfloat16),
            pltpu.VMEM((2, n_c, QW, LANES), jnp.float32),
            pltpu.VMEM((2, n_c, RWKV_CHUNK, QW), jnp.bfloat16),
            pltpu.VMEM((2, 2, t_blk, QW), jnp.float32),
            pltpu.VMEM((t_blk, QW), jnp.float32),
        ],
        compiler_params=pltpu.CompilerParams(
            dimension_semantics=("parallel", "parallel", "arbitrary"), vmem_limit_bytes=VMEM_LIMIT),
        name="rwkv",
    )(slabs16, slabs16, slabs16, slabs32, slabs32, pr, sh, wl)


def _rwkv_params(shift_mu, w0, a0, k_k, k_a, r_k, ln_w, ln_b, w_w2, w_a2, w_g2):
    def quads(vec):
        return vec.reshape(N_QUADS, QW)

    mu_r, mu_k, mu_v = (quads(shift_mu[j * RWKV_WIDTH:(j + 1) * RWKV_WIDTH]) for j in range(3))
    rows = [mu_r, mu_k, mu_v, quads(w0), quads(a0), quads(k_k), quads(k_a), quads(r_k.reshape(-1)),
            quads(ln_w), quads(ln_b)]
    pr = jnp.stack(rows, axis=1)
    pr = jnp.pad(pr, ((0, 0), (0, 16 - pr.shape[1]), (0, 0)))
    mu_rest = shift_mu[3 * RWKV_WIDTH:]
    sh = jnp.pad(mu_rest.reshape(2, LANES), ((0, 6), (0, QW - LANES)))
    zeros = jnp.zeros((DECAY_LORA, RWKV_WIDTH), jnp.float32)
    wd = jnp.concatenate([w_w2, zeros], axis=0)
    wa = jnp.concatenate([zeros, w_a2], axis=0)
    wl = jnp.stack([wd, wa, w_g2], axis=0)
    wl = wl.reshape(3, LANES, N_QUADS, QW).transpose(2, 0, 1, 3)
    return pr, sh, _bf16(wl)


def _rms(x, g):
    ms = jnp.mean(x * x, axis=-1, keepdims=True)
    return x * lax.rsqrt(ms + NORM_EPS) * g


def _merge_body(x_ref, att_ref, rw_ref, gate_ref, bg_ref, wa_ref, wr_ref, wo_ref, gn_ref, o_ref):
    n_gs = D_MODEL // LANES
    f0 = jnp.concatenate([gate_ref[s] for s in range(n_gs)], axis=1).astype(jnp.float32)
    f1 = jnp.concatenate([gate_ref[n_gs + s] for s in range(n_gs)], axis=1).astype(jnp.float32)
    g0 = jax.nn.sigmoid(f0 + bg_ref[:, :D_MODEL])
    g1 = jax.nn.sigmoid(f1 + bg_ref[:, D_MODEL:])
    merged = g0 * _dot(att_ref[...], wa_ref[...]) + g1 * _dot(rw_ref[...], wr_ref[...])
    z = _dot(_bf16(merged), wo_ref[...])
    o_ref[...] = x_ref[...] + _rms(z, gn_ref[...])


def _merge(x2, o_att, o_rwkv, slabs, b_gate, wa, wr, wo, g_post, tm):
    m = x2.shape[0]
    n_gate_slabs = GATE_COLS // LANES
    const = lambda shape: pl.BlockSpec(shape, lambda i: (0, 0))
    return pl.pallas_call(
        _merge_body,
        out_shape=jax.ShapeDtypeStruct((m, D_MODEL), jnp.float32),
        grid=(m // tm,),
        in_specs=[
            pl.BlockSpec((tm, D_MODEL), lambda i: (i, 0)),
            pl.BlockSpec((tm, ATT_OUT_WIDTH), lambda i: (i, 0)),
            pl.BlockSpec((tm, RWKV_WIDTH), lambda i: (i, 0)),
            pl.BlockSpec((n_gate_slabs, tm, LANES), lambda i: (GATE_SLAB0 // n_gate_slabs, i, 0)),
            const((1, GATE_COLS)), const(wa.shape), const(wr.shape), const(wo.shape), const((1, D_MODEL)),
        ],
        out_specs=pl.BlockSpec((tm, D_MODEL), lambda i: (i, 0)),
        compiler_params=pltpu.CompilerParams(
            dimension_semantics=("parallel",), vmem_limit_bytes=VMEM_LIMIT),
        name="merge",
    )(x2, o_att, o_rwkv, slabs, b_gate, wa, wr, wo, g_post)


def _ffn_body(x_ref, gpre_ref, w1_ref, w2_ref, gpost_ref, o_ref, *, tf):
    x = x_ref[...]
    h = _bf16(_rms(x, gpre_ref[...]))
    acc = jnp.zeros(x.shape, jnp.float32)
    for c in range(D_FF // tf):
        u = jnp.maximum(_dot(h, w1_ref[:, c * tf:(c + 1) * tf]), 0.0)
        acc = acc + _dot(_bf16(u * u), w2_ref[c * tf:(c + 1) * tf, :])
    o_ref[...] = x + _rms(acc, gpost_ref[...])


def _ffn(x2, g_pre, w1, w2, g_post, tm, tf=1024):
    m = x2.shape[0]
    const = lambda shape: pl.BlockSpec(shape, lambda i: (0, 0))
    return pl.pallas_call(
        functools.partial(_ffn_body, tf=tf),
        out_shape=jax.ShapeDtypeStruct((m, D_MODEL), jnp.float32),
        grid=(m // tm,),
        in_specs=[pl.BlockSpec((tm, D_MODEL), lambda i: (i, 0)), const((1, D_MODEL)),
                  const(w1.shape), const(w2.shape), const((1, D_MODEL))],
        out_specs=pl.BlockSpec((tm, D_MODEL), lambda i: (i, 0)),
        compiler_params=pltpu.CompilerParams(
            dimension_semantics=("parallel",), vmem_limit_bytes=VMEM_LIMIT),
        name="ffn",
    )(x2, g_pre, w1, w2, g_post)


def kernel(x, rel_bias, norm_mix_pre, norm_mix_post, norm_ffn_pre, norm_ffn_post, w_in, b_gate, shift_mu, w0, w_w2, a0, w_a2, w_g2, k_k, k_a, r_k, ln_x_w, ln_x_b, w_att_branch, w_rwkv_branch, w_out, w_ffn1, w_ffn2):
    batch, seq, d_model = x.shape
    assert d_model == D_MODEL and seq % ATT_CHUNK == 0 and seq % RWKV_T == 0
    m = batch * seq
    tm = ROW_TILE
    assert m % PROJ_TM == 0 and m % tm == 0
    bias_tiles = _bias_tiles(rel_bias)
    row = lambda vec: vec.reshape(1, -1)
    x2 = x.reshape(m, D_MODEL)
    for l in range(w_in.shape[0]):
        slabs32, slabs16 = _proj(x2, row(norm_mix_pre[l]), w_in[l], PROJ_TM)
        o_att = _attn(slabs32, bias_tiles, batch, seq)
        pr, sh, wl = _rwkv_params(shift_mu[l], w0[l], a0[l], k_k[l], k_a[l], r_k[l], ln_x_w[l], ln_x_b[l],
                                  w_w2[l], w_a2[l], w_g2[l])
        o_rwkv = _rwkv(slabs32, slabs16, pr, sh, wl, batch, seq)
        x2 = _merge(x2, o_att, o_rwkv, slabs16, row(b_gate[l]), _bf16(w_att_branch[l]), _bf16(w_rwkv_branch[l]),
                    _bf16(w_out[l]), row(norm_mix_post[l]), tm)
        x2 = _ffn(x2, row(norm_ffn_pre[l]), _bf16(w_ffn1[l]), _bf16(w_ffn2[l]), row(norm_ffn_post[l]), tm)
    return x2.reshape(batch, seq, D_MODEL)
```

```python
import functools
import math

import jax
import jax.numpy as jnp
from jax import lax
from jax.experimental import pallas as pl
from jax.experimental.pallas import tpu as pltpu

D_MODEL = 1024
HEAD_DIM = 64
DILATIONS = (1, 4, 16)
KEYS_PER_QUERY = 128
N_GROUPS = len(DILATIONS)
HEADS_PER_GROUP = 4
ATT_HEADS = N_GROUPS * HEADS_PER_GROUP
ATT_WIDTH = ATT_HEADS * HEAD_DIM
ATT_OUT_WIDTH = HEADS_PER_GROUP * HEAD_DIM
N_BUCKETS = 32
MAX_DISTANCE = KEYS_PER_QUERY * DILATIONS[-1]
RWKV_WIDTH = D_MODEL
DECAY_LORA = 64
ICLR_LORA = 64
GATE_LORA = 128
RWKV_COLS = 3 * RWKV_WIDTH + DECAY_LORA + ICLR_LORA + GATE_LORA
N_BRANCHES = 2
IN_COLS = 3 * ATT_WIDTH + RWKV_COLS + N_BRANCHES * D_MODEL
D_FF = 4 * D_MODEL
NORM_EPS = 1e-6
LN_X_EPS = 64e-5
L2_EPS = 1e-12

LANES = 128
VMEM_LIMIT = 56 * 1024 * 1024
NEG = -1e30

GATE_COLS = N_BRANCHES * D_MODEL
Q_SLAB0 = 0
K_SLAB0 = Q_SLAB0 + ATT_WIDTH // LANES
V_SLAB0 = K_SLAB0 + ATT_WIDTH // LANES
LORA_SLAB = V_SLAB0 + ATT_WIDTH // LANES
FG_SLAB = LORA_SLAB + 1
N_SLABS_F32 = FG_SLAB + 1
GATE_SLAB0 = 0
R_SLAB0 = GATE_COLS // LANES
RK_SLAB0 = R_SLAB0 + RWKV_WIDTH // LANES
RV_SLAB0 = RK_SLAB0 + RWKV_WIDTH // LANES
N_SLABS_BF16 = RV_SLAB0 + RWKV_WIDTH // LANES
RWKV_QUAD = 4
QW = RWKV_QUAD * HEAD_DIM
N_QUADS = RWKV_WIDTH // QW

PROJ_TN = 512
PROJ_TM = 2048
ROW_TILE = 512
ATT_CHUNK = KEYS_PER_QUERY * DILATIONS[-1]
QB = KEYS_PER_QUERY
RWKV_CHUNK = 64
RWKV_T = 512


def _bf16(x):
    return x.astype(jnp.bfloat16)


def _dot(a, b):
    return jnp.dot(a, b, preferred_element_type=jnp.float32)


def _dot_nt(a, b):
    return lax.dot_general(a, b, (((1,), (1,)), ((), ())), preferred_element_type=jnp.float32)


def _proj_body(x_ref, g_ref, w_ref, o32_ref, o16_ref, h_scr, *, n_f32_blocks):
    j = pl.program_id(1)

    @pl.when(j == 0)
    def _():
        x = x_ref[...]
        ms = jnp.mean(x * x, axis=-1, keepdims=True)
        h_scr[...] = _bf16(x * lax.rsqrt(ms + NORM_EPS) * g_ref[...])

    acc = _dot(h_scr[...], w_ref[0])

    @pl.when(j < n_f32_blocks)
    def _():
        for s in range(PROJ_TN // LANES):
            o32_ref[s] = acc[:, s * LANES:(s + 1) * LANES]

    @pl.when(j >= n_f32_blocks)
    def _():
        for s in range(PROJ_TN // LANES):
            o16_ref[s] = _bf16(acc[:, s * LANES:(s + 1) * LANES])


def _proj(x2, g, w_in, tm):
    m = x2.shape[0]
    n_col_blocks = IN_COLS // PROJ_TN
    slabs_per_block = PROJ_TN // LANES
    n_f32_blocks = N_SLABS_F32 // slabs_per_block
    att_end = 3 * ATT_WIDTH
    rkv_end = att_end + 3 * RWKV_WIDTH
    lora_end = att_end + RWKV_COLS
    w_cols = jnp.concatenate([w_in[:, :att_end], w_in[:, rkv_end:lora_end], w_in[:, lora_end:],
                              w_in[:, att_end:rkv_end]], axis=1)
    w_blk = _bf16(w_cols).reshape(D_MODEL, n_col_blocks, PROJ_TN).transpose(1, 0, 2)
    blk = (slabs_per_block, tm, LANES)
    return pl.pallas_call(
        functools.partial(_proj_body, n_f32_blocks=n_f32_blocks),
        out_shape=(jax.ShapeDtypeStruct((N_SLABS_F32, m, LANES), jnp.float32),
                   jax.ShapeDtypeStruct((N_SLABS_BF16, m, LANES), jnp.bfloat16)),
        grid=(m // tm, n_col_blocks),
        in_specs=[
            pl.BlockSpec((tm, D_MODEL), lambda i, j: (i, 0)),
            pl.BlockSpec((1, D_MODEL), lambda i, j: (0, 0)),
            pl.BlockSpec((1, D_MODEL, PROJ_TN), lambda i, j: (j, 0, 0)),
        ],
        out_specs=(pl.BlockSpec(blk, lambda i, j: (jnp.minimum(j, n_f32_blocks - 1), i, 0)),
                   pl.BlockSpec(blk, lambda i, j: (jnp.maximum(j - n_f32_blocks, 0), i, 0))),
        scratch_shapes=[pltpu.VMEM((tm, D_MODEL), jnp.bfloat16)],
        compiler_params=pltpu.CompilerParams(
            dimension_semantics=("parallel", "arbitrary"), vmem_limit_bytes=VMEM_LIMIT),
        name="proj",
    )(x2, g, w_blk)


def _attn_unit(d, q, k, v, bias0, bias1, lo):
    scale = HEAD_DIM ** -0.5
    qs = q * scale
    kb = _bf16(k)
    zero = jnp.zeros_like(qs)
    s0 = _dot_nt(_bf16(jnp.where(lo, qs, zero)), kb) + bias0
    s1 = _dot_nt(_bf16(jnp.where(lo, zero, qs)), kb) + bias1
    m0 = jnp.max(s0, axis=-1, keepdims=True)
    m1 = jnp.max(s1, axis=-1, keepdims=True)
    p0 = _bf16(jnp.exp(s0 - m0))
    p1 = _bf16(jnp.exp(s1 - m1))
    zv = jnp.zeros_like(v)
    ones = jnp.ones_like(v)
    rhs0 = _bf16(jnp.concatenate([jnp.where(lo, v, zv), jnp.where(lo, ones, zv)], axis=1))
    rhs1 = _bf16(jnp.concatenate([jnp.where(lo, zv, v), jnp.where(lo, zv, ones)], axis=1))
    ol = _dot(p0, rhs0) + _dot(p1, rhs1)
    o = ol[:, :LANES]
    l = ol[:, LANES:]
    m = jnp.where(lo, m0, m1)
    return o / l, m + jnp.log(l)


def _attn_body(q_ref, kc_ref, kp_ref, vc_ref, vp_ref, bias_ref, o_ref, o_scr, l_scr):
    c = pl.program_id(1)
    g = pl.program_id(2)
    lo = lax.broadcasted_iota(jnp.int32, (1, LANES), 1) < HEAD_DIM
    first = jnp.where(c == 0, 1, 0)

    def rows(start, n, d):
        return pl.ds(start, n) if d == 1 else pl.ds(start, n, stride=d)

    def group(gi):
        d = DILATIONS[gi]
        n_qb = ATT_CHUNK // (QB * d)

        def unit(p, start_q, k, v, variant):
            h0 = gi * HEADS_PER_GROUP + 2 * p
            q = q_ref[p, rows(start_q, QB, d), :]
            o, lse = _attn_unit(d, q, k, v, bias_ref[variant, h0], bias_ref[variant, h0 + 1], lo)
            o_scr[gi, p, rows(start_q, QB, d), :] = o
            l_scr[gi, p, rows(start_q, QB, d), :] = lse

        def first_blocks(r, carry):
            prev_start = r + QB * d * (n_qb - 1)
            for p in range(2):
                k = jnp.concatenate([kp_ref[p, rows(prev_start, QB, d), :], kc_ref[p, rows(r, QB, d), :]], axis=0)
                v = jnp.concatenate([vp_ref[p, rows(prev_start, QB, d), :], vc_ref[p, rows(r, QB, d), :]], axis=0)
                unit(p, r, k, v, first)
            return carry

        lax.fori_loop(0, d, first_blocks, 0)

        if n_qb > 1:
            def later_blocks(u, carry):
                r = u % d
                qb = 1 + u // d
                start_q = r + QB * d * qb
                for p in range(2):
                    k = kc_ref[p, rows(start_q - QB * d, 2 * QB, d), :]
                    v = vc_ref[p, rows(start_q - QB * d, 2 * QB, d), :]
                    unit(p, start_q, k, v, 0)
                return carry

            lax.fori_loop(0, d * (n_qb - 1), later_blocks, 0)

    for gi in range(N_GROUPS):
        pl.when(g == gi)(functools.partial(group, gi))

    @pl.when(g == N_GROUPS - 1)
    def _():
        tile = 256

        def comb(i, carry):
            rs = pl.ds(pl.multiple_of(i * tile, tile), tile)
            for p in range(2):
                ls = [l_scr[gi, p, rs, :] for gi in range(N_GROUPS)]
                mx = jnp.maximum(jnp.maximum(ls[0], ls[1]), ls[2])
                ws = [jnp.exp(l - mx) for l in ls]
                num = ws[0] * o_scr[0, p, rs, :] + ws[1] * o_scr[1, p, rs, :] + ws[2] * o_scr[2, p, rs, :]
                o_ref[rs, p * LANES:(p + 1) * LANES] = _bf16(num / (ws[0] + ws[1] + ws[2]))
            return carry

        lax.fori_loop(0, ATT_CHUNK // tile, comb, 0)


def _attn(slabs, bias_tiles, batch, seq):
    m = slabs.shape[1]
    n_chunks = seq // ATT_CHUNK
    blk = (2, ATT_CHUNK, LANES)

    def cur(slab0):
        return pl.BlockSpec(blk, lambda b, c, g: (slab0 // 2 + g, b * n_chunks + c, 0))

    def prev(slab0):
        return pl.BlockSpec(blk, lambda b, c, g: (slab0 // 2 + g, b * n_chunks + jnp.maximum(c - 1, 0), 0))

    return pl.pallas_call(
        _attn_body,
        out_shape=jax.ShapeDtypeStruct((m, ATT_OUT_WIDTH), jnp.bfloat16),
        grid=(batch, n_chunks, N_GROUPS),
        in_specs=[cur(Q_SLAB0), cur(K_SLAB0), prev(K_SLAB0), cur(V_SLAB0), prev(V_SLAB0),
                  pl.BlockSpec(bias_tiles.shape, lambda b, c, g: (0, 0, 0, 0))],
        out_specs=pl.BlockSpec((ATT_CHUNK, ATT_OUT_WIDTH), lambda b, c, g: (b * n_chunks + c, 0)),
        scratch_shapes=[pltpu.VMEM((N_GROUPS, 2, ATT_CHUNK, LANES), jnp.float32),
                        pltpu.VMEM((N_GROUPS, 2, ATT_CHUNK, LANES), jnp.float32)],
        compiler_params=pltpu.CompilerParams(
            dimension_semantics=("parallel", "parallel", "arbitrary"), vmem_limit_bytes=VMEM_LIMIT),
        name="attn",
    )(slabs, slabs, slabs, slabs, slabs, bias_tiles)


def _t5_bucket(dist):
    max_exact = N_BUCKETS // 2
    d_f = jnp.maximum(dist, 1).astype(jnp.float32)
    large = max_exact + (jnp.log(d_f / max_exact) / math.log(MAX_DISTANCE / max_exact)
                         * (N_BUCKETS - max_exact)).astype(jnp.int32)
    large = jnp.minimum(large, N_BUCKETS - 1)
    return jnp.where(dist < max_exact, dist, large)


def _bias_tiles(rel_bias):
    dil = jnp.array(DILATIONS, jnp.int32)
    dist = dil[:, None] * jnp.arange(KEYS_PER_QUERY + 1, dtype=jnp.int32)[None, :]
    bucket = _t5_bucket(dist)
    bias = rel_bias.reshape(N_BUCKETS, N_GROUPS, HEADS_PER_GROUP)[bucket, jnp.arange(N_GROUPS)[:, None]]
    bias = jnp.transpose(bias, (0, 2, 1)).astype(jnp.float32).reshape(ATT_HEADS, KEYS_PER_QUERY + 1)
    n = 3 * QB - 1
    neg = lambda w: jnp.full((ATT_HEADS, w), NEG, jnp.float32)
    e = jnp.concatenate([neg(QB - 1), bias[:, ::-1], neg(QB - 1), neg(1)], axis=1)
    e = jnp.roll(e, -(QB - 1), axis=1)
    t0 = jnp.tile(e, (1, QB))[:, :QB * n].reshape(ATT_HEADS, QB, n)[:, :, :2 * QB]
    col = jnp.arange(2 * QB)[None, None, :]
    t1 = jnp.where(col >= QB, t0, NEG)
    return jnp.stack([t0, t1])


def _interleave(*gens):
    gens = list(gens)
    while gens:
        for gen in list(gens):
            try:
                next(gen)
            except StopIteration:
                gens.remove(gen)


def _rwkv_body(r_ref, k_ref, v_ref, lora_ref, fg_ref, pra_ref, prs_ref, sh_ref, wl_ref, o_ref,
               carry, h_st, tok_s, post_s, up_s, zq_s, kcbt_s, g_s, vb_s, y_s, *, n_t):
    t_blk = o_ref.shape[0]
    cc = RWKV_CHUNK
    cs = range(t_blk // cc)
    i = pl.program_id(0)
    tok_w, tok_r = i % 2, (i + 1) % 2
    chk_w, chk_r = (i + 1) % 2, i % 2
    post_w, post_r = i % 3, (i + 1) % 3
    first_tok = (i % n_t) == 0
    first_seq = ((i + n_t - 2) % n_t) == 0

    @pl.when(i == 0)
    def _():
        for ref in (carry, h_st, tok_s, post_s, up_s, zq_s, kcbt_s, g_s, vb_s):
            ref[...] = jnp.zeros(ref.shape, ref.dtype)

    lane_head = lax.broadcasted_iota(jnp.int32, (QW, QW), 1) // HEAD_DIM
    row_head = lax.broadcasted_iota(jnp.int32, (QW, QW), 0) // HEAD_DIM
    same_head = lane_head == row_head
    head_ones = _bf16(jnp.where(same_head, 1.0, 0.0))
    chunk_lane_head = lax.broadcasted_iota(jnp.int32, (cc, QW), 1) // HEAD_DIM
    keep = [_bf16(jnp.where(chunk_lane_head == h, 1.0, 0.0)) for h in range(RWKV_QUAD)]

    def head_sum(x):
        return _dot(_bf16(x), head_ones)

    def stack4(xb):
        return jnp.concatenate([xb * keep[h] for h in range(RWKV_QUAD)], axis=0)

    def each(fn, *lists):
        return [fn(*args) for args in zip(*lists)]

    def split3(x):
        hi = _bf16(x)
        r1 = x - hi.astype(jnp.float32)
        mid = _bf16(r1)
        return hi, mid, _bf16(r1 - mid.astype(jnp.float32))

    def per_token():
        pr = pra_ref[0]
        sh = sh_ref[...]

        def shifted(x, idx, mu):
            width = x.shape[1]
            row = lax.broadcasted_iota(jnp.int32, x.shape, 0)
            last = jnp.where(first_tok, jnp.zeros((1, width), jnp.float32), carry[idx, 7:8, :width])
            prev = jnp.where(row == 0, last, pltpu.roll(x, 1, 0))
            carry[idx, :, :width] = x[t_blk - 8:, :]
            return x + (prev - x) * mu

        wide = lambda ref: jnp.concatenate([ref[0], ref[1]], axis=1).astype(jnp.float32)
        r = shifted(wide(r_ref), 0, pr[0:1])
        k = shifted(wide(k_ref), 1, pr[1:2])
        v = shifted(wide(v_ref), 2, pr[2:3])
        tok_s[tok_w, 0] = r
        tok_s[tok_w, 4] = v
        yield
        f_lora = shifted(lora_ref[0], 3, sh[0:1, :LANES])
        f_g = shifted(fg_ref[0], 4, sh[1:2, :LANES])
        w0, a0, k_k, k_a, r_k = (pr[j:j + 1] for j in range(3, 8))
        wd = w0 + _dot(_bf16(jnp.tanh(f_lora)), wl_ref[0, 0])
        tok_s[tok_w, 5] = -math.exp(-0.5) * jax.nn.sigmoid(wd)
        a = jax.nn.sigmoid(a0 + _dot(_bf16(f_lora), wl_ref[0, 1]))
        post_s[post_w, 1] = _dot(_bf16(jax.nn.sigmoid(f_g)), wl_ref[0, 2])
        yield
        kk = k * k_k
        kk = kk * lax.rsqrt(jnp.maximum(head_sum(kk * kk), L2_EPS * L2_EPS))
        tok_s[tok_w, 1] = kk
        tok_s[tok_w, 3] = kk * a
        yield
        k2 = k * (1.0 + (a - 1.0) * k_a)
        tok_s[tok_w, 2] = k2
        post_s[post_w, 0] = head_sum(r * k2 * r_k) * v
        yield

    def per_chunk():
        t_i = lax.broadcasted_iota(jnp.int32, (cc, QW), 0)
        s_i = lax.broadcasted_iota(jnp.int32, (cc, QW), 1) % cc
        strict_lower = t_i > s_i
        lower = t_i >= s_i
        eye_f = jnp.where(t_i == s_i, 1.0, 0.0)
        zero_m = jnp.zeros((cc, QW), jnp.float32)
        ci = lax.broadcasted_iota(jnp.int32, (cc, cc), 0)
        cj = lax.broadcasted_iota(jnp.int32, (cc, cc), 1)
        tri = _bf16(jnp.where(ci >= cj, 1.0, 0.0))
        rows_of = lambda j: [tok_s[tok_r, j, c * cc:(c + 1) * cc, :] for c in cs]
        r, kk, k2, bb, v, lw = (rows_of(j) for j in range(6))

        parts = each(split3, lw)
        big_l = each(lambda p: _dot(tri, p[0]) + _dot(tri, p[1]) + _dot(tri, p[2]), parts)
        yield
        l_end = each(lambda l: l[cc - 1:cc], big_l)
        e_l = each(jnp.exp, big_l)
        e_lm = each(lambda l, w: jnp.exp(l - w), big_l, lw)
        e_nl = each(lambda l: jnp.exp(-l), big_l)
        e_c = each(lambda le, l: jnp.exp(le - l), l_end, big_l)
        a_kk = each(lambda x, e: _bf16(x * e), kk, e_lm)
        a_r_f = each(lambda x, e: x * e, r, e_l)
        a_r = each(_bf16, a_r_f)
        kb = each(lambda xk, xb, e: jnp.concatenate([stack4(_bf16(xk * e)), stack4(_bf16(xb * e))], axis=0),
                  k2, bb, e_nl)
        vb = each(_bf16, v)
        vst = each(stack4, vb)
        yield
        s12 = each(lambda ak, ar, kb_: _dot_nt(jnp.concatenate([ak, ar], axis=0), kb_), a_kk, a_r, kb)
        s1 = each(lambda s: s[:cc], s12)
        s2 = each(lambda s: s[cc:], s12)
        yield
        mk = each(lambda s: _bf16(jnp.where(strict_lower, s[:, :QW], zero_m)), s1)
        mb = each(lambda s: jnp.where(strict_lower, s[:, QW:], zero_m), s1)
        nkb = each(lambda s: _bf16(jnp.concatenate([jnp.where(lower, s[:, :QW], zero_m),
                                                     jnp.where(lower, -s[:, QW:], zero_m)], axis=1)), s2)

        x = each(lambda n: eye_f - jnp.where((t_i // 2) == (s_i // 2), n, zero_m), mb)
        blk = 4
        while blk <= cc:
            half = blk // 2
            sel = ((t_i // blk) == (s_i // blk)) & ((t_i % blk) >= half) & ((s_i % blk) < half)
            xb = each(_bf16, x)
            t1 = each(lambda xc, n: _bf16(_dot(xc, stack4(_bf16(jnp.where(sel, n, zero_m))))), xb, mb)
            yield
            x = each(lambda xf, t, xc: xf - _dot(t, stack4(xc)), x, t1, xb)
            yield
            blk *= 2
        tinv = each(_bf16, x)

        pm = each(lambda t, ak: _dot(t, stack4(ak)), tinv, a_kk)
        mkv = each(lambda m_, v_: _bf16(_dot(m_, v_)), mk, vst)
        yield
        q = each(lambda t, m_: _dot(t, stack4(m_)), tinv, mkv)
        yield
        pmb = each(_bf16, pm)
        u = each(lambda af, n, p: af + _dot(n[:, QW:], stack4(p)), a_r_f, nkb, pmb)
        z = each(lambda n, v_, q_: _dot(n, jnp.concatenate([v_, stack4(_bf16(q_))], axis=0)), nkb, vst, q)
        yield
        for c in cs:
            up_s[chk_w, c] = jnp.concatenate([_bf16(u[c]), pmb[c]], axis=0)
            zq_s[chk_w, c, 0] = z[c]
            zq_s[chk_w, c, 1] = q[c]
            vb_s[chk_w, c] = vb[c]
        yield
        kcg = each(lambda xk, xb, e, le: jnp.concatenate(
            [xk * e, -(xb * e), jnp.broadcast_to(jnp.exp(le), (2 * cc, QW))], axis=0).T,
            k2, bb, e_c, l_end)
        for c in cs:
            kcbt_s[chk_w, c] = _bf16(kcg[c][:, :2 * cc])
            g_s[chk_w, c] = kcg[c][:, 2 * cc:]
        yield

    def sequential():
        h = jnp.where(first_seq, jnp.zeros((QW, QW), jnp.float32), h_st[...])
        for c in cs:
            ys = _dot(up_s[chk_r, c], _bf16(h))
            y_s[c * cc:(c + 1) * cc, :] = ys[:cc] + zq_s[chk_r, c, 0]
            sig = ys[cc:] + zq_s[chk_r, c, 1]
            yield
            hu = _dot(kcbt_s[chk_r, c], jnp.concatenate([vb_s[chk_r, c], _bf16(sig)], axis=0))
            g = g_s[chk_r, c]
            h = h * jnp.concatenate([g, g], axis=1) + jnp.where(same_head, hu, jnp.zeros_like(hu))
            yield
        h_st[...] = h
        prs = prs_ref[0]
        y = y_s[...]
        mu = head_sum(y) * (1.0 / HEAD_DIM)
        yc = y - mu
        var = head_sum(yc * yc) * (1.0 / HEAD_DIM)
        yn = yc * lax.rsqrt(var + LN_X_EPS) * prs[8:9] + prs[9:10]
        o_ref[...] = _bf16((yn + post_s[post_r, 0]) * post_s[post_r, 1])
        yield

    _interleave(per_chunk(), sequential(), per_token())


def _rwkv(slabs32, slabs16, pr, sh, wl, batch, seq):
    m = slabs32.shape[1]
    t_blk = RWKV_T
    n_t = seq // t_blk
    n_c = t_blk // RWKV_CHUNK
    n_blocks = batch * N_QUADS * n_t
    slabs_per_quad = QW // LANES

    def where(blk):
        blk = jnp.clip(blk, 0, n_blocks - 1)
        seq_id, t = blk // n_t, blk % n_t
        return seq_id % N_QUADS, (seq_id // N_QUADS) * n_t + t

    def tok(slab0, per_quad):
        if per_quad:
            return pl.BlockSpec((slabs_per_quad, t_blk, LANES),
                                lambda i: (slab0 // slabs_per_quad + where(i)[0], where(i)[1], 0))
        return pl.BlockSpec((1, t_blk, LANES), lambda i: (slab0, where(i)[1], 0))

    return pl.pallas_call(
        functools.partial(_rwkv_body, n_t=n_t),
        out_shape=jax.ShapeDtypeStruct((m, RWKV_WIDTH), jnp.bfloat16),
        grid=(n_blocks + 2,),
        in_specs=[tok(R_SLAB0, True), tok(RK_SLAB0, True), tok(RV_SLAB0, True),
                  tok(LORA_SLAB, False), tok(FG_SLAB, False),
                  pl.BlockSpec((1,) + pr.shape[1:], lambda i: (where(i)[0], 0, 0)),
                  pl.BlockSpec((1,) + pr.shape[1:], lambda i: (where(i - 2)[0], 0, 0)),
                  pl.BlockSpec(sh.shape, lambda i: (0, 0)),
                  pl.BlockSpec((1,) + wl.shape[1:], lambda i: (where(i)[0], 0, 0, 0))],
        out_specs=pl.BlockSpec((t_blk, QW), lambda i: (where(i - 2)[1], where(i - 2)[0])),
        scratch_shapes=[
            pltpu.VMEM((5, 8, QW), jnp.float32),
            pltpu.VMEM((QW, QW), jnp.float32),
            pltpu.VMEM((2, 6, t_blk, QW), jnp.float32),
            pltpu.VMEM((3, 2, t_blk, QW), jnp.float32),
            pltpu.VMEM((2, n_c, 2 * RWKV_CHUNK, QW), jnp.bfloat16),
            pltpu.VMEM((2, n_c, 2, RWKV_CHUNK, QW), jnp.float32),
            pltpu.VMEM((2, n_c, QW, LANES), jnp.bfloat16),
            pltpu.VMEM((2, n_c, QW, LANES), jnp.float32),
            pltpu.VMEM((2, n_c, RWKV_CHUNK, QW), jnp.bfloat16),
            pltpu.VMEM((t_blk, QW), jnp.float32),
        ],
        compiler_params=pltpu.CompilerParams(
            dimension_semantics=("arbitrary",), vmem_limit_bytes=VMEM_LIMIT),
        name="rwkv",
    )(slabs16, slabs16, slabs16, slabs32, slabs32, pr, pr, sh, wl)


def _rwkv_params(shift_mu, w0, a0, k_k, k_a, r_k, ln_w, ln_b, w_w2, w_a2, w_g2):
    def quads(vec):
        return vec.reshape(N_QUADS, QW)

    mu_r, mu_k, mu_v = (quads(shift_mu[j * RWKV_WIDTH:(j + 1) * RWKV_WIDTH]) for j in range(3))
    rows = [mu_r, mu_k, mu_v, quads(w0), quads(a0), quads(k_k), quads(k_a), quads(r_k.reshape(-1)),
            quads(ln_w), quads(ln_b)]
    pr = jnp.stack(rows, axis=1)
    pr = jnp.pad(pr, ((0, 0), (0, 16 - pr.shape[1]), (0, 0)))
    mu_rest = shift_mu[3 * RWKV_WIDTH:]
    sh = jnp.pad(mu_rest.reshape(2, LANES), ((0, 6), (0, QW - LANES)))
    zeros = jnp.zeros((DECAY_LORA, RWKV_WIDTH), jnp.float32)
    wd = jnp.concatenate([w_w2, zeros], axis=0)
    wa = jnp.concatenate([zeros, w_a2], axis=0)
    wl = jnp.stack([wd, wa, w_g2], axis=0)
    wl = wl.reshape(3, LANES, N_QUADS, QW).transpose(2, 0, 1, 3)
    return pr, sh, _bf16(wl)


def _rms(x, g):
    ms = jnp.mean(x * x, axis=-1, keepdims=True)
    return x * lax.rsqrt(ms + NORM_EPS) * g


def _merge_body(x_ref, att_ref, rw_ref, gate_ref, bg_ref, wa_ref, wr_ref, wo_ref, gn_ref, o_ref):
    n_gs = D_MODEL // LANES
    f0 = jnp.concatenate([gate_ref[s] for s in range(n_gs)], axis=1).astype(jnp.float32)
    f1 = jnp.concatenate([gate_ref[n_gs + s] for s in range(n_gs)], axis=1).astype(jnp.float32)
    g0 = jax.nn.sigmoid(f0 + bg_ref[:, :D_MODEL])
    g1 = jax.nn.sigmoid(f1 + bg_ref[:, D_MODEL:])
    merged = g0 * _dot(att_ref[...], wa_ref[...]) + g1 * _dot(rw_ref[...], wr_ref[...])
    z = _dot(_bf16(merged), wo_ref[...])
    o_ref[...] = x_ref[...] + _rms(z, gn_ref[...])


def _merge(x2, o_att, o_rwkv, slabs, b_gate, wa, wr, wo, g_post, tm):
    m = x2.shape[0]
    n_gate_slabs = GATE_COLS // LANES
    const = lambda shape: pl.BlockSpec(shape, lambda i: (0, 0))
    return pl.pallas_call(
        _merge_body,
        out_shape=jax.ShapeDtypeStruct((m, D_MODEL), jnp.float32),
        grid=(m // tm,),
        in_specs=[
            pl.BlockSpec((tm, D_MODEL), lambda i: (i, 0)),
            pl.BlockSpec((tm, ATT_OUT_WIDTH), lambda i: (i, 0)),
            pl.BlockSpec((tm, RWKV_WIDTH), lambda i: (i, 0)),
            pl.BlockSpec((n_gate_slabs, tm, LANES), lambda i: (GATE_SLAB0 // n_gate_slabs, i, 0)),
            const((1, GATE_COLS)), const(wa.shape), const(wr.shape), const(wo.shape), const((1, D_MODEL)),
        ],
        out_specs=pl.BlockSpec((tm, D_MODEL), lambda i: (i, 0)),
        compiler_params=pltpu.CompilerParams(
            dimension_semantics=("parallel",), vmem_limit_bytes=VMEM_LIMIT),
        name="merge",
    )(x2, o_att, o_rwkv, slabs, b_gate, wa, wr, wo, g_post)


def _ffn_body(x_ref, gpre_ref, w1_ref, w2_ref, gpost_ref, o_ref, *, tf):
    x = x_ref[...]
    h = _bf16(_rms(x, gpre_ref[...]))
    acc = jnp.zeros(x.shape, jnp.float32)
    for c in range(D_FF // tf):
        u = jnp.maximum(_dot(h, w1_ref[:, c * tf:(c + 1) * tf]), 0.0)
        acc = acc + _dot(_bf16(u * u), w2_ref[c * tf:(c + 1) * tf, :])
    o_ref[...] = x + _rms(acc, gpost_ref[...])


def _ffn(x2, g_pre, w1, w2, g_post, tm, tf=1024):
    m = x2.shape[0]
    const = lambda shape: pl.BlockSpec(shape, lambda i: (0, 0))
    return pl.pallas_call(
        functools.partial(_ffn_body, tf=tf),
        out_shape=jax.ShapeDtypeStruct((m, D_MODEL), jnp.float32),
        grid=(m // tm,),
        in_specs=[pl.BlockSpec((tm, D_MODEL), lambda i: (i, 0)), const((1, D_MODEL)),
                  const(w1.shape), const(w2.shape), const((1, D_MODEL))],
        out_specs=pl.BlockSpec((tm, D_MODEL), lambda i: (i, 0)),
        compiler_params=pltpu.CompilerParams(
            dimension_semantics=("parallel",), vmem_limit_bytes=VMEM_LIMIT),
        name="ffn",
    )(x2, g_pre, w1, w2, g_post)


def kernel(x, rel_bias, norm_mix_pre, norm_mix_post, norm_ffn_pre, norm_ffn_post, w_in, b_gate, shift_mu, w0, w_w2, a0, w_a2, w_g2, k_k, k_a, r_k, ln_x_w, ln_x_b, w_att_branch, w_rwkv_branch, w_out, w_ffn1, w_ffn2):
    batch, seq, d_model = x.shape
    assert d_model == D_MODEL and seq % ATT_CHUNK == 0 and seq % RWKV_T == 0
    m = batch * seq
    tm = ROW_TILE
    assert m % PROJ_TM == 0 and m % tm == 0
    bias_tiles = _bias_tiles(rel_bias)
    row = lambda vec: vec.reshape(1, -1)
    x2 = x.reshape(m, D_MODEL)
    for l in range(w_in.shape[0]):
        slabs32, slabs16 = _proj(x2, row(norm_mix_pre[l]), w_in[l], PROJ_TM)
        o_att = _attn(slabs32, bias_tiles, batch, seq)
        pr, sh, wl = _rwkv_params(shift_mu[l], w0[l], a0[l], k_k[l], k_a[l], r_k[l], ln_x_w[l], ln_x_b[l],
                                  w_w2[l], w_a2[l], w_g2[l])
        o_rwkv = _rwkv(slabs32, slabs16, pr, sh, wl, batch, seq)
        x2 = _merge(x2, o_att, o_rwkv, slabs16, row(b_gate[l]), _bf16(w_att_branch[l]), _bf16(w_rwkv_branch[l]),
                    _bf16(w_out[l]), row(norm_mix_post[l]), tm)
        x2 = _ffn(x2, row(norm_ffn_pre[l]), _bf16(w_ffn1[l]), _bf16(w_ffn2[l]), row(norm_ffn_post[l]), tm)
    return x2.reshape(batch, seq, D_MODEL)
```

```python
import functools
import math

import jax
import jax.numpy as jnp
from jax import lax
from jax.experimental import pallas as pl
from jax.experimental.pallas import tpu as pltpu

D_MODEL = 1024
HEAD_DIM = 64
DILATIONS = (1, 4, 16)
KEYS_PER_QUERY = 128
N_GROUPS = len(DILATIONS)
HEADS_PER_GROUP = 4
ATT_HEADS = N_GROUPS * HEADS_PER_GROUP
ATT_WIDTH = ATT_HEADS * HEAD_DIM
ATT_OUT_WIDTH = HEADS_PER_GROUP * HEAD_DIM
N_BUCKETS = 32
MAX_DISTANCE = KEYS_PER_QUERY * DILATIONS[-1]
RWKV_WIDTH = D_MODEL
DECAY_LORA = 64
ICLR_LORA = 64
GATE_LORA = 128
RWKV_COLS = 3 * RWKV_WIDTH + DECAY_LORA + ICLR_LORA + GATE_LORA
N_BRANCHES = 2
IN_COLS = 3 * ATT_WIDTH + RWKV_COLS + N_BRANCHES * D_MODEL
D_FF = 4 * D_MODEL
NORM_EPS = 1e-6
LN_X_EPS = 64e-5
L2_EPS = 1e-12

LANES = 128
VMEM_LIMIT = 56 * 1024 * 1024
NEG = -1e30

GATE_COLS = N_BRANCHES * D_MODEL
Q_SLAB0 = 0
K_SLAB0 = Q_SLAB0 + ATT_WIDTH // LANES
V_SLAB0 = K_SLAB0 + ATT_WIDTH // LANES
LORA_SLAB = V_SLAB0 + ATT_WIDTH // LANES
FG_SLAB = LORA_SLAB + 1
N_SLABS_F32 = FG_SLAB + 1
GATE_SLAB0 = 0
R_SLAB0 = GATE_COLS // LANES
RK_SLAB0 = R_SLAB0 + RWKV_WIDTH // LANES
RV_SLAB0 = RK_SLAB0 + RWKV_WIDTH // LANES
N_SLABS_BF16 = RV_SLAB0 + RWKV_WIDTH // LANES
RWKV_QUAD = 4
QW = RWKV_QUAD * HEAD_DIM
N_QUADS = RWKV_WIDTH // QW

PROJ_TN = 512
PROJ_TM = 2048
ROW_TILE = 512
ATT_CHUNK = KEYS_PER_QUERY * DILATIONS[-1]
QB = KEYS_PER_QUERY
ATT_UNROLL = 3
RWKV_CHUNK = 64
RWKV_T = 512


def _bf16(x):
    return x.astype(jnp.bfloat16)


def _dot(a, b):
    return jnp.dot(a, b, preferred_element_type=jnp.float32)


def _dot_nt(a, b):
    return lax.dot_general(a, b, (((1,), (1,)), ((), ())), preferred_element_type=jnp.float32)


def _proj_body(x_ref, g_ref, w_ref, o32_ref, o16_ref, h_scr, *, n_f32_blocks):
    j = pl.program_id(1)

    @pl.when(j == 0)
    def _():
        x = x_ref[...]
        ms = jnp.mean(x * x, axis=-1, keepdims=True)
        h_scr[...] = _bf16(x * lax.rsqrt(ms + NORM_EPS) * g_ref[...])

    acc = _dot(h_scr[...], w_ref[0])

    @pl.when(j < n_f32_blocks)
    def _():
        for s in range(PROJ_TN // LANES):
            o32_ref[s] = acc[:, s * LANES:(s + 1) * LANES]

    @pl.when(j >= n_f32_blocks)
    def _():
        for s in range(PROJ_TN // LANES):
            o16_ref[s] = _bf16(acc[:, s * LANES:(s + 1) * LANES])


def _proj(x2, g, w_in, tm):
    m = x2.shape[0]
    n_col_blocks = IN_COLS // PROJ_TN
    slabs_per_block = PROJ_TN // LANES
    n_f32_blocks = N_SLABS_F32 // slabs_per_block
    att_end = 3 * ATT_WIDTH
    rkv_end = att_end + 3 * RWKV_WIDTH
    lora_end = att_end + RWKV_COLS
    w_cols = jnp.concatenate([w_in[:, :att_end], w_in[:, rkv_end:lora_end], w_in[:, lora_end:],
                              w_in[:, att_end:rkv_end]], axis=1)
    w_blk = _bf16(w_cols).reshape(D_MODEL, n_col_blocks, PROJ_TN).transpose(1, 0, 2)
    blk = (slabs_per_block, tm, LANES)
    return pl.pallas_call(
        functools.partial(_proj_body, n_f32_blocks=n_f32_blocks),
        out_shape=(jax.ShapeDtypeStruct((N_SLABS_F32, m, LANES), jnp.float32),
                   jax.ShapeDtypeStruct((N_SLABS_BF16, m, LANES), jnp.bfloat16)),
        grid=(m // tm, n_col_blocks),
        in_specs=[
            pl.BlockSpec((tm, D_MODEL), lambda i, j: (i, 0)),
            pl.BlockSpec((1, D_MODEL), lambda i, j: (0, 0)),
            pl.BlockSpec((1, D_MODEL, PROJ_TN), lambda i, j: (j, 0, 0)),
        ],
        out_specs=(pl.BlockSpec(blk, lambda i, j: (jnp.minimum(j, n_f32_blocks - 1), i, 0)),
                   pl.BlockSpec(blk, lambda i, j: (jnp.maximum(j - n_f32_blocks, 0), i, 0))),
        scratch_shapes=[pltpu.VMEM((tm, D_MODEL), jnp.bfloat16)],
        compiler_params=pltpu.CompilerParams(
            dimension_semantics=("parallel", "arbitrary"), vmem_limit_bytes=VMEM_LIMIT),
        name="proj",
    )(x2, g, w_blk)


def _attn_units(qs, ks, vs, biases, lo):
    scale = HEAD_DIM ** -0.5

    def logits(q, k):
        qs_ = q * scale
        zero = jnp.zeros_like(qs_)
        lhs = _bf16(jnp.concatenate([jnp.where(lo, qs_, zero), jnp.where(lo, zero, qs_)], axis=0))
        return _dot_nt(lhs, _bf16(k))

    s = [logits(q, k) for q, k in zip(qs, ks)]
    s0 = [x[:QB] + b[0] for x, b in zip(s, biases)]
    s1 = [x[QB:] + b[1] for x, b in zip(s, biases)]
    m0 = [jnp.max(x, axis=-1, keepdims=True) for x in s0]
    m1 = [jnp.max(x, axis=-1, keepdims=True) for x in s1]
    p0 = [_bf16(jnp.exp(x - m)) for x, m in zip(s0, m0)]
    p1 = [_bf16(jnp.exp(x - m)) for x, m in zip(s1, m1)]

    def weighted(p0_, p1_, v):
        zv = jnp.zeros_like(v)
        ones = jnp.ones_like(v)
        rhs0 = _bf16(jnp.concatenate([jnp.where(lo, v, zv), jnp.where(lo, ones, zv)], axis=1))
        rhs1 = _bf16(jnp.concatenate([jnp.where(lo, zv, v), jnp.where(lo, zv, ones)], axis=1))
        return _dot(p0_, rhs0) + _dot(p1_, rhs1)

    ol = [weighted(a, b, v) for a, b, v in zip(p0, p1, vs)]
    out = []
    for x, a, b in zip(ol, m0, m1):
        l = x[:, LANES:]
        out.append((x[:, :LANES] / l, jnp.where(lo, a, b) + jnp.log(l)))
    return out


def _attn_body(q_ref, kc_ref, kp_ref, vc_ref, vp_ref, bias_ref, o_ref, o_scr, l_scr):
    c = pl.program_id(1)
    g = pl.program_id(2)
    lo = lax.broadcasted_iota(jnp.int32, (1, LANES), 1) < HEAD_DIM
    first = jnp.where(c == 0, 1, 0)

    def rows(start, n, d):
        return pl.ds(start, n) if d == 1 else pl.ds(start, n, stride=d)

    def group(gi):
        d = DILATIONS[gi]
        n_qb = ATT_CHUNK // (QB * d)

        def run(units):
            qs = [q_ref[p, rows(start_q, QB, d), :] for p, start_q, _, _, _ in units]
            biases = [(bias_ref[variant, gi * HEADS_PER_GROUP + 2 * p],
                       bias_ref[variant, gi * HEADS_PER_GROUP + 2 * p + 1]) for p, _, _, _, variant in units]
            res = _attn_units(qs, [u[2] for u in units], [u[3] for u in units], biases, lo)
            for (p, start_q, _, _, _), (o, lse) in zip(units, res):
                o_scr[gi, p, rows(start_q, QB, d), :] = o
                l_scr[gi, p, rows(start_q, QB, d), :] = lse

        n_first = max(j for j in range(1, ATT_UNROLL + 1) if d % j == 0)

        def first_blocks(it, carry):
            units = []
            for j in range(n_first):
                r = it * n_first + j
                prev_start = r + QB * d * (n_qb - 1)
                for p in range(2):
                    k = jnp.concatenate([kp_ref[p, rows(prev_start, QB, d), :], kc_ref[p, rows(r, QB, d), :]], axis=0)
                    v = jnp.concatenate([vp_ref[p, rows(prev_start, QB, d), :], vc_ref[p, rows(r, QB, d), :]], axis=0)
                    units.append((p, r, k, v, first))
            run(units)
            return carry

        lax.fori_loop(0, d // n_first, first_blocks, 0)

        n_later = d * (n_qb - 1)
        if n_later:
            n_par = max(j for j in range(1, ATT_UNROLL + 1) if n_later % j == 0)

            def later_blocks(it, carry):
                units = []
                for j in range(n_par):
                    u = it * n_par + j
                    start_q = u % d + QB * d * (1 + u // d)
                    for p in range(2):
                        k = kc_ref[p, rows(start_q - QB * d, 2 * QB, d), :]
                        v = vc_ref[p, rows(start_q - QB * d, 2 * QB, d), :]
                        units.append((p, start_q, k, v, 0))
                run(units)
                return carry

            lax.fori_loop(0, n_later // n_par, later_blocks, 0)

    for gi in range(N_GROUPS):
        pl.when(g == gi)(functools.partial(group, gi))

    @pl.when(g == N_GROUPS - 1)
    def _():
        tile = 256

        def comb(i, carry):
            rs = pl.ds(pl.multiple_of(i * tile, tile), tile)
            for p in range(2):
                ls = [l_scr[gi, p, rs, :] for gi in range(N_GROUPS)]
                mx = jnp.maximum(jnp.maximum(ls[0], ls[1]), ls[2])
                ws = [jnp.exp(l - mx) for l in ls]
                num = ws[0] * o_scr[0, p, rs, :] + ws[1] * o_scr[1, p, rs, :] + ws[2] * o_scr[2, p, rs, :]
                o_ref[rs, p * LANES:(p + 1) * LANES] = _bf16(num / (ws[0] + ws[1] + ws[2]))
            return carry

        lax.fori_loop(0, ATT_CHUNK // tile, comb, 0)


def _attn(slabs, bias_tiles, batch, seq):
    m = slabs.shape[1]
    n_chunks = seq // ATT_CHUNK
    blk = (2, ATT_CHUNK, LANES)

    def cur(slab0):
        return pl.BlockSpec(blk, lambda b, c, g: (slab0 // 2 + g, b * n_chunks + c, 0))

    def prev(slab0):
        return pl.BlockSpec(blk, lambda b, c, g: (slab0 // 2 + g, b * n_chunks + jnp.maximum(c - 1, 0), 0))

    return pl.pallas_call(
        _attn_body,
        out_shape=jax.ShapeDtypeStruct((m, ATT_OUT_WIDTH), jnp.bfloat16),
        grid=(batch, n_chunks, N_GROUPS),
        in_specs=[cur(Q_SLAB0), cur(K_SLAB0), prev(K_SLAB0), cur(V_SLAB0), prev(V_SLAB0),
                  pl.BlockSpec(bias_tiles.shape, lambda b, c, g: (0, 0, 0, 0))],
        out_specs=pl.BlockSpec((ATT_CHUNK, ATT_OUT_WIDTH), lambda b, c, g: (b * n_chunks + c, 0)),
        scratch_shapes=[pltpu.VMEM((N_GROUPS, 2, ATT_CHUNK, LANES), jnp.float32),
                        pltpu.VMEM((N_GROUPS, 2, ATT_CHUNK, LANES), jnp.float32)],
        compiler_params=pltpu.CompilerParams(
            dimension_semantics=("parallel", "parallel", "arbitrary"), vmem_limit_bytes=VMEM_LIMIT),
        name="attn",
    )(slabs, slabs, slabs, slabs, slabs, bias_tiles)


def _t5_bucket(dist):
    max_exact = N_BUCKETS // 2
    d_f = jnp.maximum(dist, 1).astype(jnp.float32)
    large = max_exact + (jnp.log(d_f / max_exact) / math.log(MAX_DISTANCE / max_exact)
                         * (N_BUCKETS - max_exact)).astype(jnp.int32)
    large = jnp.minimum(large, N_BUCKETS - 1)
    return jnp.where(dist < max_exact, dist, large)


def _bias_tiles(rel_bias):
    dil = jnp.array(DILATIONS, jnp.int32)
    dist = dil[:, None] * jnp.arange(KEYS_PER_QUERY + 1, dtype=jnp.int32)[None, :]
    bucket = _t5_bucket(dist)
    bias = rel_bias.reshape(N_BUCKETS, N_GROUPS, HEADS_PER_GROUP)[bucket, jnp.arange(N_GROUPS)[:, None]]
    bias = jnp.transpose(bias, (0, 2, 1)).astype(jnp.float32).reshape(ATT_HEADS, KEYS_PER_QUERY + 1)
    n = 3 * QB - 1
    neg = lambda w: jnp.full((ATT_HEADS, w), NEG, jnp.float32)
    e = jnp.concatenate([neg(QB - 1), bias[:, ::-1], neg(QB - 1), neg(1)], axis=1)
    e = jnp.roll(e, -(QB - 1), axis=1)
    t0 = jnp.tile(e, (1, QB))[:, :QB * n].reshape(ATT_HEADS, QB, n)[:, :, :2 * QB]
    col = jnp.arange(2 * QB)[None, None, :]
    t1 = jnp.where(col >= QB, t0, NEG)
    return jnp.stack([t0, t1])


def _interleave(*gens):
    gens = list(gens)
    while gens:
        for gen in list(gens):
            try:
                next(gen)
            except StopIteration:
                gens.remove(gen)


def _rwkv_body(r_ref, k_ref, v_ref, lora_ref, fg_ref, pra_ref, prs_ref, sh_ref, wl_ref, o_ref,
               carry, h_st, tok_s, post_s, up_s, zq_s, kcbt_s, g_s, vb_s, y_s, *, n_t):
    t_blk = o_ref.shape[0]
    cc = RWKV_CHUNK
    cs = range(t_blk // cc)
    i = pl.program_id(0)
    tok_w, tok_r = i % 2, (i + 1) % 2
    chk_w, chk_r = (i + 1) % 2, i % 2
    post_w, post_r = i % 3, (i + 1) % 3
    first_tok = (i % n_t) == 0
    first_seq = ((i + n_t - 2) % n_t) == 0

    @pl.when(i == 0)
    def _():
        for ref in (carry, h_st, tok_s, post_s, up_s, zq_s, kcbt_s, g_s, vb_s):
            ref[...] = jnp.zeros(ref.shape, ref.dtype)

    lane_head = lax.broadcasted_iota(jnp.int32, (QW, QW), 1) // HEAD_DIM
    row_head = lax.broadcasted_iota(jnp.int32, (QW, QW), 0) // HEAD_DIM
    same_head = lane_head == row_head
    head_ones = _bf16(jnp.where(same_head, 1.0, 0.0))
    chunk_lane_head = lax.broadcasted_iota(jnp.int32, (cc, QW), 1) // HEAD_DIM
    keep = [_bf16(jnp.where(chunk_lane_head == h, 1.0, 0.0)) for h in range(RWKV_QUAD)]

    def head_sum(x):
        return _dot(_bf16(x), head_ones)

    def stack4(xb):
        return jnp.concatenate([xb * keep[h] for h in range(RWKV_QUAD)], axis=0)

    def each(fn, *lists):
        return [fn(*args) for args in zip(*lists)]

    def split3(x):
        hi = _bf16(x)
        r1 = x - hi.astype(jnp.float32)
        mid = _bf16(r1)
        return hi, mid, _bf16(r1 - mid.astype(jnp.float32))

    def per_token():
        pr = pra_ref[0]
        sh = sh_ref[...]

        def shifted(x, idx, mu):
            width = x.shape[1]
            row = lax.broadcasted_iota(jnp.int32, x.shape, 0)
            last = jnp.where(first_tok, jnp.zeros((1, width), jnp.float32), carry[idx, 7:8, :width])
            prev = jnp.where(row == 0, last, pltpu.roll(x, 1, 0))
            carry[idx, :, :width] = x[t_blk - 8:, :]
            return x + (prev - x) * mu

        wide = lambda ref: jnp.concatenate([ref[0], ref[1]], axis=1).astype(jnp.float32)
        r = shifted(wide(r_ref), 0, pr[0:1])
        k = shifted(wide(k_ref), 1, pr[1:2])
        v = shifted(wide(v_ref), 2, pr[2:3])
        tok_s[tok_w, 0] = r
        tok_s[tok_w, 4] = v
        yield
        f_lora = shifted(lora_ref[0], 3, sh[0:1, :LANES])
        f_g = shifted(fg_ref[0], 4, sh[1:2, :LANES])
        w0, a0, k_k, k_a, r_k = (pr[j:j + 1] for j in range(3, 8))
        wd = w0 + _dot(_bf16(jnp.tanh(f_lora)), wl_ref[0, 0])
        tok_s[tok_w, 5] = -math.exp(-0.5) * jax.nn.sigmoid(wd)
        a = jax.nn.sigmoid(a0 + _dot(_bf16(f_lora), wl_ref[0, 1]))
        post_s[post_w, 1] = _dot(_bf16(jax.nn.sigmoid(f_g)), wl_ref[0, 2])
        yield
        kk = k * k_k
        kk = kk * lax.rsqrt(jnp.maximum(head_sum(kk * kk), L2_EPS * L2_EPS))
        tok_s[tok_w, 1] = kk
        tok_s[tok_w, 3] = kk * a
        yield
        k2 = k * (1.0 + (a - 1.0) * k_a)
        tok_s[tok_w, 2] = k2
        post_s[post_w, 0] = head_sum(r * k2 * r_k) * v
        yield

    def per_chunk():
        t_i = lax.broadcasted_iota(jnp.int32, (cc, QW), 0)
        s_i = lax.broadcasted_iota(jnp.int32, (cc, QW), 1) % cc
        strict_lower = t_i > s_i
        lower = t_i >= s_i
        eye_f = jnp.where(t_i == s_i, 1.0, 0.0)
        zero_m = jnp.zeros((cc, QW), jnp.float32)
        ci = lax.broadcasted_iota(jnp.int32, (cc, cc), 0)
        cj = lax.broadcasted_iota(jnp.int32, (cc, cc), 1)
        tri = _bf16(jnp.where(ci >= cj, 1.0, 0.0))
        rows_of = lambda j: [tok_s[tok_r, j, c * cc:(c + 1) * cc, :] for c in cs]
        r, kk, k2, bb, v, lw = (rows_of(j) for j in range(6))

        parts = each(split3, lw)
        big_l = each(lambda p: _dot(tri, p[0]) + _dot(tri, p[1]) + _dot(tri, p[2]), parts)
        yield
        l_end = each(lambda l: l[cc - 1:cc], big_l)
        e_l = each(jnp.exp, big_l)
        e_lm = each(lambda l, w: jnp.exp(l - w), big_l, lw)
        e_nl = each(lambda l: jnp.exp(-l), big_l)
        e_c = each(lambda le, l: jnp.exp(le - l), l_end, big_l)
        a_kk = each(lambda x, e: _bf16(x * e), kk, e_lm)
        a_r_f = each(lambda x, e: x * e, r, e_l)
        a_r = each(_bf16, a_r_f)
        kb = each(lambda xk, xb, e: jnp.concatenate([stack4(_bf16(xk * e)), stack4(_bf16(xb * e))], axis=0),
                  k2, bb, e_nl)
        vb = each(_bf16, v)
        vst = each(stack4, vb)
        yield
        s12 = each(lambda ak, ar, kb_: _dot_nt(jnp.concatenate([ak, ar], axis=0), kb_), a_kk, a_r, kb)
        s1 = each(lambda s: s[:cc], s12)
        s2 = each(lambda s: s[cc:], s12)
        yield
        mk = each(lambda s: _bf16(jnp.where(strict_lower, s[:, :QW], zero_m)), s1)
        mb = each(lambda s: jnp.where(strict_lower, s[:, QW:], zero_m), s1)
        nkb = each(lambda s: _bf16(jnp.concatenate([jnp.where(lower, s[:, :QW], zero_m),
                                                     jnp.where(lower, -s[:, QW:], zero_m)], axis=1)), s2)

        x = each(lambda n: eye_f - jnp.where((t_i // 2) == (s_i // 2), n, zero_m), mb)
        blk = 4
        while blk <= cc:
            half = blk // 2
            sel = ((t_i // blk) == (s_i // blk)) & ((t_i % blk) >= half) & ((s_i % blk) < half)
            xb = each(_bf16, x)
            t1 = each(lambda xc, n: _bf16(_dot(xc, stack4(_bf16(jnp.where(sel, n, zero_m))))), xb, mb)
            yield
            x = each(lambda xf, t, xc: xf - _dot(t, stack4(xc)), x, t1, xb)
            yield
            blk *= 2
        tinv = each(_bf16, x)

        pm = each(lambda t, ak: _dot(t, stack4(ak)), tinv, a_kk)
        mkv = each(lambda m_, v_: _bf16(_dot(m_, v_)), mk, vst)
        yield
        q = each(lambda t, m_: _dot(t, stack4(m_)), tinv, mkv)
        yield
        pmb = each(_bf16, pm)
        u = each(lambda af, n, p: af + _dot(n[:, QW:], stack4(p)), a_r_f, nkb, pmb)
        z = each(lambda n, v_, q_: _dot(n, jnp.concatenate([v_, stack4(_bf16(q_))], axis=0)), nkb, vst, q)
        yield
        for c in cs:
            up_s[chk_w, c] = jnp.concatenate([_bf16(u[c]), pmb[c]], axis=0)
            zq_s[chk_w, c, 0] = z[c]
            zq_s[chk_w, c, 1] = q[c]
            vb_s[chk_w, c] = vb[c]
        yield
        kcg = each(lambda xk, xb, e, le: jnp.concatenate(
            [xk * e, -(xb * e), jnp.broadcast_to(jnp.exp(le), (2 * cc, QW))], axis=0).T,
            k2, bb, e_c, l_end)
        for c in cs:
            kcbt_s[chk_w, c] = _bf16(kcg[c][:, :2 * cc])
            g_s[chk_w, c] = kcg[c][:, 2 * cc:]
        yield

    def sequential():
        h = jnp.where(first_seq, jnp.zeros((QW, QW), jnp.float32), h_st[...])
        for c in cs:
            ys = _dot(up_s[chk_r, c], _bf16(h))
            y_s[c * cc:(c + 1) * cc, :] = ys[:cc] + zq_s[chk_r, c, 0]
            sig = ys[cc:] + zq_s[chk_r, c, 1]
            yield
            hu = _dot(kcbt_s[chk_r, c], jnp.concatenate([vb_s[chk_r, c], _bf16(sig)], axis=0))
            g = g_s[chk_r, c]
            h = h * jnp.concatenate([g, g], axis=1) + jnp.where(same_head, hu, jnp.zeros_like(hu))
            yield
        h_st[...] = h
        prs = prs_ref[0]
        y = y_s[...]
        mu = head_sum(y) * (1.0 / HEAD_DIM)
        yc = y - mu
        var = head_sum(yc * yc) * (1.0 / HEAD_DIM)
        yn = yc * lax.rsqrt(var + LN_X_EPS) * prs[8:9] + prs[9:10]
        o_ref[...] = _bf16((yn + post_s[post_r, 0]) * post_s[post_r, 1])
        yield

    _interleave(per_chunk(), sequential(), per_token())


def _rwkv(slabs32, slabs16, pr, sh, wl, batch, seq):
    m = slabs32.shape[1]
    t_blk = RWKV_T
    n_t = seq // t_blk
    n_c = t_blk // RWKV_CHUNK
    n_blocks = batch * N_QUADS * n_t
    slabs_per_quad = QW // LANES

    def where(blk):
        blk = jnp.clip(blk, 0, n_blocks - 1)
        seq_id, t = blk // n_t, blk % n_t
        return seq_id % N_QUADS, (seq_id // N_QUADS) * n_t + t

    def tok(slab0, per_quad):
        if per_quad:
            return pl.BlockSpec((slabs_per_quad, t_blk, LANES),
                                lambda i: (slab0 // slabs_per_quad + where(i)[0], where(i)[1], 0))
        return pl.BlockSpec((1, t_blk, LANES), lambda i: (slab0, where(i)[1], 0))

    return pl.pallas_call(
        functools.partial(_rwkv_body, n_t=n_t),
        out_shape=jax.ShapeDtypeStruct((m, RWKV_WIDTH), jnp.bfloat16),
        grid=(n_blocks + 2,),
        in_specs=[tok(R_SLAB0, True), tok(RK_SLAB0, True), tok(RV_SLAB0, True),
                  tok(LORA_SLAB, False), tok(FG_SLAB, False),
                  pl.BlockSpec((1,) + pr.shape[1:], lambda i: (where(i)[0], 0, 0)),
                  pl.BlockSpec((1,) + pr.shape[1:], lambda i: (where(i - 2)[0], 0, 0)),
                  pl.BlockSpec(sh.shape, lambda i: (0, 0)),
                  pl.BlockSpec((1,) + wl.shape[1:], lambda i: (where(i)[0], 0, 0, 0))],
        out_specs=pl.BlockSpec((t_blk, QW), lambda i: (where(i - 2)[1], where(i - 2)[0])),
        scratch_shapes=[
            pltpu.VMEM((5, 8, QW), jnp.float32),
            pltpu.VMEM((QW, QW), jnp.float32),
            pltpu.VMEM((2, 6, t_blk, QW), jnp.float32),
            pltpu.VMEM((3, 2, t_blk, QW), jnp.float32),
            pltpu.VMEM((2, n_c, 2 * RWKV_CHUNK, QW), jnp.bfloat16),
            pltpu.VMEM((2, n_c, 2, RWKV_CHUNK, QW), jnp.float32),
            pltpu.VMEM((2, n_c, QW, LANES), jnp.bfloat16),
            pltpu.VMEM((2, n_c, QW, LANES), jnp.float32),
            pltpu.VMEM((2, n_c, RWKV_CHUNK, QW), jnp.bfloat16),
            pltpu.VMEM((t_blk, QW), jnp.float32),
        ],
        compiler_params=pltpu.CompilerParams(
            dimension_semantics=("arbitrary",), vmem_limit_bytes=VMEM_LIMIT),
        name="rwkv",
    )(slabs16, slabs16, slabs16, slabs32, slabs32, pr, pr, sh, wl)


def _rwkv_params(shift_mu, w0, a0, k_k, k_a, r_k, ln_w, ln_b, w_w2, w_a2, w_g2):
    def quads(vec):
        return vec.reshape(N_QUADS, QW)

    mu_r, mu_k, mu_v = (quads(shift_mu[j * RWKV_WIDTH:(j + 1) * RWKV_WIDTH]) for j in range(3))
    rows = [mu_r, mu_k, mu_v, quads(w0), quads(a0), quads(k_k), quads(k_a), quads(r_k.reshape(-1)),
            quads(ln_w), quads(ln_b)]
    pr = jnp.stack(rows, axis=1)
    pr = jnp.pad(pr, ((0, 0), (0, 16 - pr.shape[1]), (0, 0)))
    mu_rest = shift_mu[3 * RWKV_WIDTH:]
    sh = jnp.pad(mu_rest.reshape(2, LANES), ((0, 6), (0, QW - LANES)))
    zeros = jnp.zeros((DECAY_LORA, RWKV_WIDTH), jnp.float32)
    wd = jnp.concatenate([w_w2, zeros], axis=0)
    wa = jnp.concatenate([zeros, w_a2], axis=0)
    wl = jnp.stack([wd, wa, w_g2], axis=0)
    wl = wl.reshape(3, LANES, N_QUADS, QW).transpose(2, 0, 1, 3)
    return pr, sh, _bf16(wl)


def _rms(x, g):
    ms = jnp.mean(x * x, axis=-1, keepdims=True)
    return x * lax.rsqrt(ms + NORM_EPS) * g


def _merge_body(x_ref, att_ref, rw_ref, gate_ref, bg_ref, wa_ref, wr_ref, wo_ref, gn_ref, o_ref):
    n_gs = D_MODEL // LANES
    f0 = jnp.concatenate([gate_ref[s] for s in range(n_gs)], axis=1).astype(jnp.float32)
    f1 = jnp.concatenate([gate_ref[n_gs + s] for s in range(n_gs)], axis=1).astype(jnp.float32)
    g0 = jax.nn.sigmoid(f0 + bg_ref[:, :D_MODEL])
    g1 = jax.nn.sigmoid(f1 + bg_ref[:, D_MODEL:])
    merged = g0 * _dot(att_ref[...], wa_ref[...]) + g1 * _dot(rw_ref[...], wr_ref[...])
    z = _dot(_bf16(merged), wo_ref[...])
    o_ref[...] = x_ref[...] + _rms(z, gn_ref[...])


def _merge(x2, o_att, o_rwkv, slabs, b_gate, wa, wr, wo, g_post, tm):
    m = x2.shape[0]
    n_gate_slabs = GATE_COLS // LANES
    const = lambda shape: pl.BlockSpec(shape, lambda i: (0, 0))
    return pl.pallas_call(
        _merge_body,
        out_shape=jax.ShapeDtypeStruct((m, D_MODEL), jnp.float32),
        grid=(m // tm,),
        in_specs=[
            pl.BlockSpec((tm, D_MODEL), lambda i: (i, 0)),
            pl.BlockSpec((tm, ATT_OUT_WIDTH), lambda i: (i, 0)),
            pl.BlockSpec((tm, RWKV_WIDTH), lambda i: (i, 0)),
            pl.BlockSpec((n_gate_slabs, tm, LANES), lambda i: (GATE_SLAB0 // n_gate_slabs, i, 0)),
            const((1, GATE_COLS)), const(wa.shape), const(wr.shape), const(wo.shape), const((1, D_MODEL)),
        ],
        out_specs=pl.BlockSpec((tm, D_MODEL), lambda i: (i, 0)),
        compiler_params=pltpu.CompilerParams(
            dimension_semantics=("parallel",), vmem_limit_bytes=VMEM_LIMIT),
        name="merge",
    )(x2, o_att, o_rwkv, slabs, b_gate, wa, wr, wo, g_post)


def _ffn_body(x_ref, gpre_ref, w1_ref, w2_ref, gpost_ref, o_ref, *, tf):
    x = x_ref[...]
    h = _bf16(_rms(x, gpre_ref[...]))
    acc = jnp.zeros(x.shape, jnp.float32)
    for c in range(D_FF // tf):
        u = jnp.maximum(_dot(h, w1_ref[:, c * tf:(c + 1) * tf]), 0.0)
        acc = acc + _dot(_bf16(u * u), w2_ref[c * tf:(c + 1) * tf, :])
    o_ref[...] = x + _rms(acc, gpost_ref[...])


def _ffn(x2, g_pre, w1, w2, g_post, tm, tf=1024):
    m = x2.shape[0]
    const = lambda shape: pl.BlockSpec(shape, lambda i: (0, 0))
    return pl.pallas_call(
        functools.partial(_ffn_body, tf=tf),
        out_shape=jax.ShapeDtypeStruct((m, D_MODEL), jnp.float32),
        grid=(m // tm,),
        in_specs=[pl.BlockSpec((tm, D_MODEL), lambda i: (i, 0)), const((1, D_MODEL)),
                  const(w1.shape), const(w2.shape), const((1, D_MODEL))],
        out_specs=pl.BlockSpec((tm, D_MODEL), lambda i: (i, 0)),
        compiler_params=pltpu.CompilerParams(
            dimension_semantics=("parallel",), vmem_limit_bytes=VMEM_LIMIT),
        name="ffn",
    )(x2, g_pre, w1, w2, g_post)


def kernel(x, rel_bias, norm_mix_pre, norm_mix_post, norm_ffn_pre, norm_ffn_post, w_in, b_gate, shift_mu, w0, w_w2, a0, w_a2, w_g2, k_k, k_a, r_k, ln_x_w, ln_x_b, w_att_branch, w_rwkv_branch, w_out, w_ffn1, w_ffn2):
    batch, seq, d_model = x.shape
    assert d_model == D_MODEL and seq % ATT_CHUNK == 0 and seq % RWKV_T == 0
    m = batch * seq
    tm = ROW_TILE
    assert m % PROJ_TM == 0 and m % tm == 0
    bias_tiles = _bias_tiles(rel_bias)
    row = lambda vec: vec.reshape(1, -1)
    x2 = x.reshape(m, D_MODEL)
    for l in range(w_in.shape[0]):
        slabs32, slabs16 = _proj(x2, row(norm_mix_pre[l]), w_in[l], PROJ_TM)
        o_att = _attn(slabs32, bias_tiles, batch, seq)
        pr, sh, wl = _rwkv_params(shift_mu[l], w0[l], a0[l], k_k[l], k_a[l], r_k[l], ln_x_w[l], ln_x_b[l],
                                  w_w2[l], w_a2[l], w_g2[l])
        o_rwkv = _rwkv(slabs32, slabs16, pr, sh, wl, batch, seq)
        x2 = _merge(x2, o_att, o_rwkv, slabs16, row(b_gate[l]), _bf16(w_att_branch[l]), _bf16(w_rwkv_branch[l]),
                    _bf16(w_out[l]), row(norm_mix_post[l]), tm)
        x2 = _ffn(x2, row(norm_ffn_pre[l]), _bf16(w_ffn1[l]), _bf16(w_ffn2[l]), row(norm_ffn_post[l]), tm)
    return x2.reshape(batch, seq, D_MODEL)
```

```python
import functools
import math

import jax
import jax.numpy as jnp
from jax import lax
from jax.experimental import pallas as pl
from jax.experimental.pallas import tpu as pltpu

D_MODEL = 1024
HEAD_DIM = 64
DILATIONS = (1, 4, 16)
KEYS_PER_QUERY = 128
N_GROUPS = len(DILATIONS)
HEADS_PER_GROUP = 4
ATT_HEADS = N_GROUPS * HEADS_PER_GROUP
ATT_WIDTH = ATT_HEADS * HEAD_DIM
ATT_OUT_WIDTH = HEADS_PER_GROUP * HEAD_DIM
N_BUCKETS = 32
MAX_DISTANCE = KEYS_PER_QUERY * DILATIONS[-1]
RWKV_WIDTH = D_MODEL
DECAY_LORA = 64
ICLR_LORA = 64
GATE_LORA = 128
RWKV_COLS = 3 * RWKV_WIDTH + DECAY_LORA + ICLR_LORA + GATE_LORA
N_BRANCHES = 2
IN_COLS = 3 * ATT_WIDTH + RWKV_COLS + N_BRANCHES * D_MODEL
D_FF = 4 * D_MODEL
NORM_EPS = 1e-6
LN_X_EPS = 64e-5
L2_EPS = 1e-12

LANES = 128
VMEM_LIMIT = 56 * 1024 * 1024
NEG = -1e30

GATE_COLS = N_BRANCHES * D_MODEL
Q_SLAB0 = 0
K_SLAB0 = Q_SLAB0 + ATT_WIDTH // LANES
V_SLAB0 = K_SLAB0 + ATT_WIDTH // LANES
LORA_SLAB = V_SLAB0 + ATT_WIDTH // LANES
FG_SLAB = LORA_SLAB + 1
N_SLABS_F32 = FG_SLAB + 1
GATE_SLAB0 = 0
R_SLAB0 = GATE_COLS // LANES
RK_SLAB0 = R_SLAB0 + RWKV_WIDTH // LANES
RV_SLAB0 = RK_SLAB0 + RWKV_WIDTH // LANES
N_SLABS_BF16 = RV_SLAB0 + RWKV_WIDTH // LANES
RWKV_QUAD = 4
QW = RWKV_QUAD * HEAD_DIM
N_QUADS = RWKV_WIDTH // QW

PROJ_TN = 512
PROJ_TM = 2048
ROW_TILE = 512
ATT_CHUNK = KEYS_PER_QUERY * DILATIONS[-1]
QB = KEYS_PER_QUERY
ATT_UNROLL = 3
RWKV_CHUNK = 64
RWKV_T = 512


def _bf16(x):
    return x.astype(jnp.bfloat16)


def _dot(a, b):
    return jnp.dot(a, b, preferred_element_type=jnp.float32)


def _dot_nt(a, b):
    return lax.dot_general(a, b, (((1,), (1,)), ((), ())), preferred_element_type=jnp.float32)


def _proj_body(x_ref, g_ref, w_ref, o32_ref, o16_ref, h_scr, *, n_f32_blocks):
    j = pl.program_id(1)

    @pl.when(j == 0)
    def _():
        x = x_ref[...]
        ms = jnp.mean(x * x, axis=-1, keepdims=True)
        h_scr[...] = _bf16(x * lax.rsqrt(ms + NORM_EPS) * g_ref[...])

    acc = _dot(h_scr[...], w_ref[0])

    @pl.when(j < n_f32_blocks)
    def _():
        for s in range(PROJ_TN // LANES):
            o32_ref[s] = acc[:, s * LANES:(s + 1) * LANES]

    @pl.when(j >= n_f32_blocks)
    def _():
        for s in range(PROJ_TN // LANES):
            o16_ref[s] = _bf16(acc[:, s * LANES:(s + 1) * LANES])


def _proj(x2, g, w_in, tm):
    m = x2.shape[0]
    n_col_blocks = IN_COLS // PROJ_TN
    slabs_per_block = PROJ_TN // LANES
    n_f32_blocks = N_SLABS_F32 // slabs_per_block
    att_end = 3 * ATT_WIDTH
    rkv_end = att_end + 3 * RWKV_WIDTH
    lora_end = att_end + RWKV_COLS
    w_cols = jnp.concatenate([w_in[:, :att_end], w_in[:, rkv_end:lora_end], w_in[:, lora_end:],
                              w_in[:, att_end:rkv_end]], axis=1)
    w_blk = _bf16(w_cols).reshape(D_MODEL, n_col_blocks, PROJ_TN).transpose(1, 0, 2)
    blk = (slabs_per_block, tm, LANES)
    return pl.pallas_call(
        functools.partial(_proj_body, n_f32_blocks=n_f32_blocks),
        out_shape=(jax.ShapeDtypeStruct((N_SLABS_F32, m, LANES), jnp.float32),
                   jax.ShapeDtypeStruct((N_SLABS_BF16, m, LANES), jnp.bfloat16)),
        grid=(m // tm, n_col_blocks),
        in_specs=[
            pl.BlockSpec((tm, D_MODEL), lambda i, j: (i, 0)),
            pl.BlockSpec((1, D_MODEL), lambda i, j: (0, 0)),
            pl.BlockSpec((1, D_MODEL, PROJ_TN), lambda i, j: (j, 0, 0)),
        ],
        out_specs=(pl.BlockSpec(blk, lambda i, j: (jnp.minimum(j, n_f32_blocks - 1), i, 0)),
                   pl.BlockSpec(blk, lambda i, j: (jnp.maximum(j - n_f32_blocks, 0), i, 0))),
        scratch_shapes=[pltpu.VMEM((tm, D_MODEL), jnp.bfloat16)],
        compiler_params=pltpu.CompilerParams(
            dimension_semantics=("parallel", "arbitrary"), vmem_limit_bytes=VMEM_LIMIT),
        name="proj",
    )(x2, g, w_blk)


def _attn_units(qs, ks, vs, biases, lo):
    scale = HEAD_DIM ** -0.5

    def logits(q, k):
        qs_ = q * scale
        zero = jnp.zeros_like(qs_)
        lhs = _bf16(jnp.concatenate([jnp.where(lo, qs_, zero), jnp.where(lo, zero, qs_)], axis=0))
        return _dot_nt(lhs, _bf16(k))

    s = [logits(q, k) for q, k in zip(qs, ks)]
    s0 = [x[:QB] + b[0] for x, b in zip(s, biases)]
    s1 = [x[QB:] + b[1] for x, b in zip(s, biases)]
    m0 = [jnp.max(x, axis=-1, keepdims=True) for x in s0]
    m1 = [jnp.max(x, axis=-1, keepdims=True) for x in s1]
    p0 = [_bf16(jnp.exp(x - m)) for x, m in zip(s0, m0)]
    p1 = [_bf16(jnp.exp(x - m)) for x, m in zip(s1, m1)]

    def weighted(p0_, p1_, v):
        zv = jnp.zeros_like(v)
        ones = jnp.ones_like(v)
        rhs0 = _bf16(jnp.concatenate([jnp.where(lo, v, zv), jnp.where(lo, ones, zv)], axis=1))
        rhs1 = _bf16(jnp.concatenate([jnp.where(lo, zv, v), jnp.where(lo, zv, ones)], axis=1))
        return _dot(p0_, rhs0) + _dot(p1_, rhs1)

    ol = [weighted(a, b, v) for a, b, v in zip(p0, p1, vs)]
    out = []
    for x, a, b in zip(ol, m0, m1):
        l = x[:, LANES:]
        out.append((x[:, :LANES] / l, jnp.where(lo, a, b) + jnp.log(l)))
    return out


def _attn_body(q_ref, kc_ref, kp_ref, vc_ref, vp_ref, bias_ref, o_ref, o_scr, l_scr):
    c = pl.program_id(1)
    g = pl.program_id(2)
    lo = lax.broadcasted_iota(jnp.int32, (1, LANES), 1) < HEAD_DIM
    first = jnp.where(c == 0, 1, 0)

    def rows(start, n, d):
        return pl.ds(start, n) if d == 1 else pl.ds(start, n, stride=d)

    def group(gi):
        d = DILATIONS[gi]
        n_qb = ATT_CHUNK // (QB * d)

        def run(units):
            qs = [q_ref[p, rows(start_q, QB, d), :] for p, start_q, _, _, _ in units]
            biases = [(bias_ref[variant, gi * HEADS_PER_GROUP + 2 * p],
                       bias_ref[variant, gi * HEADS_PER_GROUP + 2 * p + 1]) for p, _, _, _, variant in units]
            res = _attn_units(qs, [u[2] for u in units], [u[3] for u in units], biases, lo)
            for (p, start_q, _, _, _), (o, lse) in zip(units, res):
                o_scr[gi, p, rows(start_q, QB, d), :] = o
                l_scr[gi, p, rows(start_q, QB, d), :] = lse

        n_first = max(j for j in range(1, ATT_UNROLL + 1) if d % j == 0)

        def first_blocks(it, carry):
            units = []
            for j in range(n_first):
                r = it * n_first + j
                prev_start = r + QB * d * (n_qb - 1)
                for p in range(2):
                    k = jnp.concatenate([kp_ref[p, rows(prev_start, QB, d), :], kc_ref[p, rows(r, QB, d), :]], axis=0)
                    v = jnp.concatenate([vp_ref[p, rows(prev_start, QB, d), :], vc_ref[p, rows(r, QB, d), :]], axis=0)
                    units.append((p, r, k, v, first))
            run(units)
            return carry

        lax.fori_loop(0, d // n_first, first_blocks, 0)

        n_later = d * (n_qb - 1)
        if n_later:
            n_par = max(j for j in range(1, ATT_UNROLL + 1) if n_later % j == 0)

            def later_blocks(it, carry):
                units = []
                for j in range(n_par):
                    u = it * n_par + j
                    start_q = u % d + QB * d * (1 + u // d)
                    for p in range(2):
                        k = kc_ref[p, rows(start_q - QB * d, 2 * QB, d), :]
                        v = vc_ref[p, rows(start_q - QB * d, 2 * QB, d), :]
                        units.append((p, start_q, k, v, 0))
                run(units)
                return carry

            lax.fori_loop(0, n_later // n_par, later_blocks, 0)

    for gi in range(N_GROUPS):
        pl.when(g == gi)(functools.partial(group, gi))

    @pl.when(g == N_GROUPS - 1)
    def _():
        tile = 256

        def comb(i, carry):
            rs = pl.ds(pl.multiple_of(i * tile, tile), tile)
            for p in range(2):
                ls = [l_scr[gi, p, rs, :] for gi in range(N_GROUPS)]
                mx = jnp.maximum(jnp.maximum(ls[0], ls[1]), ls[2])
                ws = [jnp.exp(l - mx) for l in ls]
                num = ws[0] * o_scr[0, p, rs, :] + ws[1] * o_scr[1, p, rs, :] + ws[2] * o_scr[2, p, rs, :]
                o_ref[rs, p * LANES:(p + 1) * LANES] = _bf16(num / (ws[0] + ws[1] + ws[2]))
            return carry

        lax.fori_loop(0, ATT_CHUNK // tile, comb, 0)


def _attn(slabs, bias_tiles, batch, seq):
    m = slabs.shape[1]
    n_chunks = seq // ATT_CHUNK
    blk = (2, ATT_CHUNK, LANES)

    def cur(slab0):
        return pl.BlockSpec(blk, lambda b, c, g: (slab0 // 2 + g, b * n_chunks + c, 0))

    def prev(slab0):
        return pl.BlockSpec(blk, lambda b, c, g: (slab0 // 2 + g, b * n_chunks + jnp.maximum(c - 1, 0), 0))

    return pl.pallas_call(
        _attn_body,
        out_shape=jax.ShapeDtypeStruct((m, ATT_OUT_WIDTH), jnp.bfloat16),
        grid=(batch, n_chunks, N_GROUPS),
        in_specs=[cur(Q_SLAB0), cur(K_SLAB0), prev(K_SLAB0), cur(V_SLAB0), prev(V_SLAB0),
                  pl.BlockSpec(bias_tiles.shape, lambda b, c, g: (0, 0, 0, 0))],
        out_specs=pl.BlockSpec((ATT_CHUNK, ATT_OUT_WIDTH), lambda b, c, g: (b * n_chunks + c, 0)),
        scratch_shapes=[pltpu.VMEM((N_GROUPS, 2, ATT_CHUNK, LANES), jnp.float32),
                        pltpu.VMEM((N_GROUPS, 2, ATT_CHUNK, LANES), jnp.float32)],
        compiler_params=pltpu.CompilerParams(
            dimension_semantics=("parallel", "parallel", "arbitrary"), vmem_limit_bytes=VMEM_LIMIT),
        name="attn",
    )(slabs, slabs, slabs, slabs, slabs, bias_tiles)


def _t5_bucket(dist):
    max_exact = N_BUCKETS // 2
    d_f = jnp.maximum(dist, 1).astype(jnp.float32)
    large = max_exact + (jnp.log(d_f / max_exact) / math.log(MAX_DISTANCE / max_exact)
                         * (N_BUCKETS - max_exact)).astype(jnp.int32)
    large = jnp.minimum(large, N_BUCKETS - 1)
    return jnp.where(dist < max_exact, dist, large)


def _bias_tiles(rel_bias):
    dil = jnp.array(DILATIONS, jnp.int32)
    dist = dil[:, None] * jnp.arange(KEYS_PER_QUERY + 1, dtype=jnp.int32)[None, :]
    bucket = _t5_bucket(dist)
    bias = rel_bias.reshape(N_BUCKETS, N_GROUPS, HEADS_PER_GROUP)[bucket, jnp.arange(N_GROUPS)[:, None]]
    bias = jnp.transpose(bias, (0, 2, 1)).astype(jnp.float32).reshape(ATT_HEADS, KEYS_PER_QUERY + 1)
    n = 3 * QB - 1
    neg = lambda w: jnp.full((ATT_HEADS, w), NEG, jnp.float32)
    e = jnp.concatenate([neg(QB - 1), bias[:, ::-1], neg(QB - 1), neg(1)], axis=1)
    e = jnp.roll(e, -(QB - 1), axis=1)
    t0 = jnp.tile(e, (1, QB))[:, :QB * n].reshape(ATT_HEADS, QB, n)[:, :, :2 * QB]
    col = jnp.arange(2 * QB)[None, None, :]
    t1 = jnp.where(col >= QB, t0, NEG)
    return jnp.stack([t0, t1])


def _interleave(*gens):
    gens = list(gens)
    while gens:
        for gen in list(gens):
            try:
                next(gen)
            except StopIteration:
                gens.remove(gen)


def _rwkv_body(r_ref, k_ref, v_ref, lora_ref, fg_ref, pra_ref, prs_ref, sh_ref, wl_ref, o_ref,
               carry, h_st, tok_s, post_s, up_s, zq_s, kcbt_s, g_s, vb_s, y_s, *, n_t):
    t_blk = o_ref.shape[0]
    cc = RWKV_CHUNK
    cs = range(t_blk // cc)
    i = pl.program_id(0)
    first_tok = (i % n_t) == 0
    first_seq = ((i + n_t - 2) % n_t) == 0

    @pl.when(i == 0)
    def _():
        for ref in (carry, h_st, tok_s, post_s, up_s, zq_s, kcbt_s, g_s, vb_s):
            ref[...] = jnp.zeros(ref.shape, ref.dtype)

    lane_head = lax.broadcasted_iota(jnp.int32, (QW, QW), 1) // HEAD_DIM
    row_head = lax.broadcasted_iota(jnp.int32, (QW, QW), 0) // HEAD_DIM
    same_head = lane_head == row_head
    head_ones = _bf16(jnp.where(same_head, 1.0, 0.0))
    chunk_lane_head = lax.broadcasted_iota(jnp.int32, (cc, QW), 1) // HEAD_DIM
    keep = [_bf16(jnp.where(chunk_lane_head == h, 1.0, 0.0)) for h in range(RWKV_QUAD)]

    def head_sum(x):
        return _dot(_bf16(x), head_ones)

    def stack4(xb):
        return jnp.concatenate([xb * keep[h] for h in range(RWKV_QUAD)], axis=0)

    def each(fn, *lists):
        return [fn(*args) for args in zip(*lists)]

    def split3(x):
        hi = _bf16(x)
        r1 = x - hi.astype(jnp.float32)
        mid = _bf16(r1)
        return hi, mid, _bf16(r1 - mid.astype(jnp.float32))

    def per_token(tok_w):
        pr = pra_ref[0]
        sh = sh_ref[...]

        def shifted(x, idx, mu):
            width = x.shape[1]
            row = lax.broadcasted_iota(jnp.int32, x.shape, 0)
            last = jnp.where(first_tok, jnp.zeros((1, width), jnp.float32), carry[idx, 7:8, :width])
            prev = jnp.where(row == 0, last, pltpu.roll(x, 1, 0))
            carry[idx, :, :width] = x[t_blk - 8:, :]
            return x + (prev - x) * mu

        wide = lambda ref: jnp.concatenate([ref[0], ref[1]], axis=1).astype(jnp.float32)
        r = shifted(wide(r_ref), 0, pr[0:1])
        k = shifted(wide(k_ref), 1, pr[1:2])
        v = shifted(wide(v_ref), 2, pr[2:3])
        tok_s[tok_w, 0] = r
        tok_s[tok_w, 4] = v
        yield
        f_lora = shifted(lora_ref[0], 3, sh[0:1, :LANES])
        f_g = shifted(fg_ref[0], 4, sh[1:2, :LANES])
        w0, a0, k_k, k_a, r_k = (pr[j:j + 1] for j in range(3, 8))
        wd = w0 + _dot(_bf16(jnp.tanh(f_lora)), wl_ref[0, 0])
        tok_s[tok_w, 5] = -math.exp(-0.5) * jax.nn.sigmoid(wd)
        a = jax.nn.sigmoid(a0 + _dot(_bf16(f_lora), wl_ref[0, 1]))
        tok_s[tok_w, 7] = _dot(_bf16(jax.nn.sigmoid(f_g)), wl_ref[0, 2])
        yield
        kk = k * k_k
        kk = kk * lax.rsqrt(jnp.maximum(head_sum(kk * kk), L2_EPS * L2_EPS))
        tok_s[tok_w, 1] = kk
        tok_s[tok_w, 3] = kk * a
        yield
        k2 = k * (1.0 + (a - 1.0) * k_a)
        tok_s[tok_w, 2] = k2
        tok_s[tok_w, 6] = head_sum(r * k2 * r_k) * v
        yield

    def per_chunk(tok_r, chk_w):
        t_i = lax.broadcasted_iota(jnp.int32, (cc, QW), 0)
        s_i = lax.broadcasted_iota(jnp.int32, (cc, QW), 1) % cc
        strict_lower = t_i > s_i
        lower = t_i >= s_i
        eye_f = jnp.where(t_i == s_i, 1.0, 0.0)
        zero_m = jnp.zeros((cc, QW), jnp.float32)
        ci = lax.broadcasted_iota(jnp.int32, (cc, cc), 0)
        cj = lax.broadcasted_iota(jnp.int32, (cc, cc), 1)
        tri = _bf16(jnp.where(ci >= cj, 1.0, 0.0))
        rows_of = lambda j: [tok_s[tok_r, j, c * cc:(c + 1) * cc, :] for c in cs]
        r, kk, k2, bb, v, lw = (rows_of(j) for j in range(6))

        parts = each(split3, lw)
        big_l = each(lambda p: _dot(tri, p[0]) + _dot(tri, p[1]) + _dot(tri, p[2]), parts)
        yield
        l_end = each(lambda l: l[cc - 1:cc], big_l)
        e_l = each(jnp.exp, big_l)
        e_lm = each(lambda l, w: jnp.exp(l - w), big_l, lw)
        e_nl = each(lambda l: jnp.exp(-l), big_l)
        e_c = each(lambda le, l: jnp.exp(le - l), l_end, big_l)
        a_kk = each(lambda x, e: _bf16(x * e), kk, e_lm)
        a_r_f = each(lambda x, e: x * e, r, e_l)
        a_r = each(_bf16, a_r_f)
        kb = each(lambda xk, xb, e: jnp.concatenate([stack4(_bf16(xk * e)), stack4(_bf16(xb * e))], axis=0),
                  k2, bb, e_nl)
        vb = each(_bf16, v)
        vst = each(stack4, vb)
        yield
        s12 = each(lambda ak, ar, kb_: _dot_nt(jnp.concatenate([ak, ar], axis=0), kb_), a_kk, a_r, kb)
        s1 = each(lambda s: s[:cc], s12)
        s2 = each(lambda s: s[cc:], s12)
        yield
        mk = each(lambda s: _bf16(jnp.where(strict_lower, s[:, :QW], zero_m)), s1)
        mb = each(lambda s: jnp.where(strict_lower, s[:, QW:], zero_m), s1)
        nkb = each(lambda s: _bf16(jnp.concatenate([jnp.where(lower, s[:, :QW], zero_m),
                                                     jnp.where(lower, -s[:, QW:], zero_m)], axis=1)), s2)

        x = each(lambda n: eye_f - jnp.where((t_i // 2) == (s_i // 2), n, zero_m), mb)
        blk = 4
        while blk <= cc:
            half = blk // 2
            sel = ((t_i // blk) == (s_i // blk)) & ((t_i % blk) >= half) & ((s_i % blk) < half)
            xb = each(_bf16, x)
            t1 = each(lambda xc, n: _bf16(_dot(xc, stack4(_bf16(jnp.where(sel, n, zero_m))))), xb, mb)
            yield
            x = each(lambda xf, t, xc: xf - _dot(t, stack4(xc)), x, t1, xb)
            yield
            blk *= 2
        tinv = each(_bf16, x)

        pm = each(lambda t, ak: _dot(t, stack4(ak)), tinv, a_kk)
        mkv = each(lambda m_, v_: _bf16(_dot(m_, v_)), mk, vst)
        yield
        q = each(lambda t, m_: _dot(t, stack4(m_)), tinv, mkv)
        yield
        pmb = each(_bf16, pm)
        u = each(lambda af, n, p: af + _dot(n[:, QW:], stack4(p)), a_r_f, nkb, pmb)
        z = each(lambda n, v_, q_: _dot(n, jnp.concatenate([v_, stack4(_bf16(q_))], axis=0)), nkb, vst, q)
        yield
        for c in cs:
            up_s[chk_w, c] = jnp.concatenate([_bf16(u[c]), pmb[c]], axis=0)
            zq_s[chk_w, c, 0] = z[c]
            zq_s[chk_w, c, 1] = q[c]
            vb_s[chk_w, c] = vb[c]
        yield
        kcg = each(lambda xk, xb, e, le: jnp.concatenate(
            [xk * e, -(xb * e), jnp.broadcast_to(jnp.exp(le), (2 * cc, QW))], axis=0).T,
            k2, bb, e_c, l_end)
        for c in cs:
            kcbt_s[chk_w, c] = _bf16(kcg[c][:, :2 * cc])
            g_s[chk_w, c] = kcg[c][:, 2 * cc:]
        yield
        post_s[chk_w, 0] = tok_s[tok_r, 6]
        post_s[chk_w, 1] = tok_s[tok_r, 7]
        yield

    def sequential(chk_r):
        h = jnp.where(first_seq, jnp.zeros((QW, QW), jnp.float32), h_st[...])
        for c in cs:
            ys = _dot(up_s[chk_r, c], _bf16(h))
            y_s[c * cc:(c + 1) * cc, :] = ys[:cc] + zq_s[chk_r, c, 0]
            sig = ys[cc:] + zq_s[chk_r, c, 1]
            yield
            hu = _dot(kcbt_s[chk_r, c], jnp.concatenate([vb_s[chk_r, c], _bf16(sig)], axis=0))
            g = g_s[chk_r, c]
            h = h * jnp.concatenate([g, g], axis=1) + jnp.where(same_head, hu, jnp.zeros_like(hu))
            yield
        h_st[...] = h
        prs = prs_ref[0]
        y = y_s[...]
        mu = head_sum(y) * (1.0 / HEAD_DIM)
        yc = y - mu
        var = head_sum(yc * yc) * (1.0 / HEAD_DIM)
        yn = yc * lax.rsqrt(var + LN_X_EPS) * prs[8:9] + prs[9:10]
        o_ref[...] = _bf16((yn + post_s[chk_r, 0]) * post_s[chk_r, 1])
        yield

    for parity in (0, 1):
        @pl.when(i % 2 == parity)
        def _(parity=parity):
            _interleave(per_chunk(1 - parity, 1 - parity), sequential(parity), per_token(parity))


def _rwkv(slabs32, slabs16, pr, sh, wl, batch, seq):
    m = slabs32.shape[1]
    t_blk = RWKV_T
    n_t = seq // t_blk
    n_c = t_blk // RWKV_CHUNK
    n_blocks = batch * N_QUADS * n_t
    slabs_per_quad = QW // LANES

    def where(blk):
        blk = jnp.clip(blk, 0, n_blocks - 1)
        seq_id, t = blk // n_t, blk % n_t
        return seq_id % N_QUADS, (seq_id // N_QUADS) * n_t + t

    def tok(slab0, per_quad):
        if per_quad:
            return pl.BlockSpec((slabs_per_quad, t_blk, LANES),
                                lambda i: (slab0 // slabs_per_quad + where(i)[0], where(i)[1], 0))
        return pl.BlockSpec((1, t_blk, LANES), lambda i: (slab0, where(i)[1], 0))

    return pl.pallas_call(
        functools.partial(_rwkv_body, n_t=n_t),
        out_shape=jax.ShapeDtypeStruct((m, RWKV_WIDTH), jnp.bfloat16),
        grid=(n_blocks + 2,),
        in_specs=[tok(R_SLAB0, True), tok(RK_SLAB0, True), tok(RV_SLAB0, True),
                  tok(LORA_SLAB, False), tok(FG_SLAB, False),
                  pl.BlockSpec((1,) + pr.shape[1:], lambda i: (where(i)[0], 0, 0)),
                  pl.BlockSpec((1,) + pr.shape[1:], lambda i: (where(i - 2)[0], 0, 0)),
                  pl.BlockSpec(sh.shape, lambda i: (0, 0)),
                  pl.BlockSpec((1,) + wl.shape[1:], lambda i: (where(i)[0], 0, 0, 0))],
        out_specs=pl.BlockSpec((t_blk, QW), lambda i: (where(i - 2)[1], where(i - 2)[0])),
        scratch_shapes=[
            pltpu.VMEM((5, 8, QW), jnp.float32),
            pltpu.VMEM((QW, QW), jnp.float32),
            pltpu.VMEM((2, 8, t_blk, QW), jnp.float32),
            pltpu.VMEM((2, 2, t_blk, QW), jnp.float32),
            pltpu.VMEM((2, n_c, 2 * RWKV_CHUNK, QW), jnp.bfloat16),
            pltpu.VMEM((2, n_c, 2, RWKV_CHUNK, QW), jnp.float32),
            pltpu.VMEM((2, n_c, QW, LANES), jnp.bfloat16),
            pltpu.VMEM((2, n_c, QW, LANES), jnp.float32),
            pltpu.VMEM((2, n_c, RWKV_CHUNK, QW), jnp.bfloat16),
            pltpu.VMEM((t_blk, QW), jnp.float32),
        ],
        compiler_params=pltpu.CompilerParams(
            dimension_semantics=("arbitrary",), vmem_limit_bytes=VMEM_LIMIT),
        name="rwkv",
    )(slabs16, slabs16, slabs16, slabs32, slabs32, pr, pr, sh, wl)


def _rwkv_params(shift_mu, w0, a0, k_k, k_a, r_k, ln_w, ln_b, w_w2, w_a2, w_g2):
    def quads(vec):
        return vec.reshape(N_QUADS, QW)

    mu_r, mu_k, mu_v = (quads(shift_mu[j * RWKV_WIDTH:(j + 1) * RWKV_WIDTH]) for j in range(3))
    rows = [mu_r, mu_k, mu_v, quads(w0), quads(a0), quads(k_k), quads(k_a), quads(r_k.reshape(-1)),
            quads(ln_w), quads(ln_b)]
    pr = jnp.stack(rows, axis=1)
    pr = jnp.pad(pr, ((0, 0), (0, 16 - pr.shape[1]), (0, 0)))
    mu_rest = shift_mu[3 * RWKV_WIDTH:]
    sh = jnp.pad(mu_rest.reshape(2, LANES), ((0, 6), (0, QW - LANES)))
    zeros = jnp.zeros((DECAY_LORA, RWKV_WIDTH), jnp.float32)
    wd = jnp.concatenate([w_w2, zeros], axis=0)
    wa = jnp.concatenate([zeros, w_a2], axis=0)
    wl = jnp.stack([wd, wa, w_g2], axis=0)
    wl = wl.reshape(3, LANES, N_QUADS, QW).transpose(2, 0, 1, 3)
    return pr, sh, _bf16(wl)


def _rms(x, g):
    ms = jnp.mean(x * x, axis=-1, keepdims=True)
    return x * lax.rsqrt(ms + NORM_EPS) * g


def _merge_body(x_ref, att_ref, rw_ref, gate_ref, bg_ref, wa_ref, wr_ref, wo_ref, gn_ref, o_ref):
    n_gs = D_MODEL // LANES
    f0 = jnp.concatenate([gate_ref[s] for s in range(n_gs)], axis=1).astype(jnp.float32)
    f1 = jnp.concatenate([gate_ref[n_gs + s] for s in range(n_gs)], axis=1).astype(jnp.float32)
    g0 = jax.nn.sigmoid(f0 + bg_ref[:, :D_MODEL])
    g1 = jax.nn.sigmoid(f1 + bg_ref[:, D_MODEL:])
    merged = g0 * _dot(att_ref[...], wa_ref[...]) + g1 * _dot(rw_ref[...], wr_ref[...])
    z = _dot(_bf16(merged), wo_ref[...])
    o_ref[...] = x_ref[...] + _rms(z, gn_ref[...])


def _merge(x2, o_att, o_rwkv, slabs, b_gate, wa, wr, wo, g_post, tm):
    m = x2.shape[0]
    n_gate_slabs = GATE_COLS // LANES
    const = lambda shape: pl.BlockSpec(shape, lambda i: (0, 0))
    return pl.pallas_call(
        _merge_body,
        out_shape=jax.ShapeDtypeStruct((m, D_MODEL), jnp.float32),
        grid=(m // tm,),
        in_specs=[
            pl.BlockSpec((tm, D_MODEL), lambda i: (i, 0)),
            pl.BlockSpec((tm, ATT_OUT_WIDTH), lambda i: (i, 0)),
            pl.BlockSpec((tm, RWKV_WIDTH), lambda i: (i, 0)),
            pl.BlockSpec((n_gate_slabs, tm, LANES), lambda i: (GATE_SLAB0 // n_gate_slabs, i, 0)),
            const((1, GATE_COLS)), const(wa.shape), const(wr.shape), const(wo.shape), const((1, D_MODEL)),
        ],
        out_specs=pl.BlockSpec((tm, D_MODEL), lambda i: (i, 0)),
        compiler_params=pltpu.CompilerParams(
            dimension_semantics=("parallel",), vmem_limit_bytes=VMEM_LIMIT),
        name="merge",
    )(x2, o_att, o_rwkv, slabs, b_gate, wa, wr, wo, g_post)


def _ffn_body(x_ref, gpre_ref, w1_ref, w2_ref, gpost_ref, o_ref, *, tf):
    x = x_ref[...]
    h = _bf16(_rms(x, gpre_ref[...]))
    acc = jnp.zeros(x.shape, jnp.float32)
    for c in range(D_FF // tf):
        u = jnp.maximum(_dot(h, w1_ref[:, c * tf:(c + 1) * tf]), 0.0)
        acc = acc + _dot(_bf16(u * u), w2_ref[c * tf:(c + 1) * tf, :])
    o_ref[...] = x + _rms(acc, gpost_ref[...])


def _ffn(x2, g_pre, w1, w2, g_post, tm, tf=1024):
    m = x2.shape[0]
    const = lambda shape: pl.BlockSpec(shape, lambda i: (0, 0))
    return pl.pallas_call(
        functools.partial(_ffn_body, tf=tf),
        out_shape=jax.ShapeDtypeStruct((m, D_MODEL), jnp.float32),
        grid=(m // tm,),
        in_specs=[pl.BlockSpec((tm, D_MODEL), lambda i: (i, 0)), const((1, D_MODEL)),
                  const(w1.shape), const(w2.shape), const((1, D_MODEL))],
        out_specs=pl.BlockSpec((tm, D_MODEL), lambda i: (i, 0)),
        compiler_params=pltpu.CompilerParams(
            dimension_semantics=("parallel",), vmem_limit_bytes=VMEM_LIMIT),
        name="ffn",
    )(x2, g_pre, w1, w2, g_post)


def kernel(x, rel_bias, norm_mix_pre, norm_mix_post, norm_ffn_pre, norm_ffn_post, w_in, b_gate, shift_mu, w0, w_w2, a0, w_a2, w_g2, k_k, k_a, r_k, ln_x_w, ln_x_b, w_att_branch, w_rwkv_branch, w_out, w_ffn1, w_ffn2):
    batch, seq, d_model = x.shape
    assert d_model == D_MODEL and seq % ATT_CHUNK == 0 and seq % RWKV_T == 0
    m = batch * seq
    tm = ROW_TILE
    assert m % PROJ_TM == 0 and m % tm == 0
    bias_tiles = _bias_tiles(rel_bias)
    row = lambda vec: vec.reshape(1, -1)
    x2 = x.reshape(m, D_MODEL)
    for l in range(w_in.shape[0]):
        slabs32, slabs16 = _proj(x2, row(norm_mix_pre[l]), w_in[l], PROJ_TM)
        o_att = _attn(slabs32, bias_tiles, batch, seq)
        pr, sh, wl = _rwkv_params(shift_mu[l], w0[l], a0[l], k_k[l], k_a[l], r_k[l], ln_x_w[l], ln_x_b[l],
                                  w_w2[l], w_a2[l], w_g2[l])
        o_rwkv = _rwkv(slabs32, slabs16, pr, sh, wl, batch, seq)
        x2 = _merge(x2, o_att, o_rwkv, slabs16, row(b_gate[l]), _bf16(w_att_branch[l]), _bf16(w_rwkv_branch[l]),
                    _bf16(w_out[l]), row(norm_mix_post[l]), tm)
        x2 = _ffn(x2, row(norm_ffn_pre[l]), _bf16(w_ffn1[l]), _bf16(w_ffn2[l]), row(norm_ffn_post[l]), tm)
    return x2.reshape(batch, seq, D_MODEL)
```

```python
import functools
import math

import jax
import jax.numpy as jnp
from jax import lax
from jax.experimental import pallas as pl
from jax.experimental.pallas import tpu as pltpu

D_MODEL = 1024
HEAD_DIM = 64
DILATIONS = (1, 4, 16)
KEYS_PER_QUERY = 128
N_GROUPS = len(DILATIONS)
HEADS_PER_GROUP = 4
ATT_HEADS = N_GROUPS * HEADS_PER_GROUP
ATT_WIDTH = ATT_HEADS * HEAD_DIM
ATT_OUT_WIDTH = HEADS_PER_GROUP * HEAD_DIM
N_BUCKETS = 32
MAX_DISTANCE = KEYS_PER_QUERY * DILATIONS[-1]
RWKV_WIDTH = D_MODEL
DECAY_LORA = 64
ICLR_LORA = 64
GATE_LORA = 128
RWKV_COLS = 3 * RWKV_WIDTH + DECAY_LORA + ICLR_LORA + GATE_LORA
N_BRANCHES = 2
IN_COLS = 3 * ATT_WIDTH + RWKV_COLS + N_BRANCHES * D_MODEL
D_FF = 4 * D_MODEL
NORM_EPS = 1e-6
LN_X_EPS = 64e-5
L2_EPS = 1e-12

LANES = 128
VMEM_LIMIT = 56 * 1024 * 1024
NEG = -1e30

GATE_COLS = N_BRANCHES * D_MODEL
Q_SLAB0 = 0
K_SLAB0 = Q_SLAB0 + ATT_WIDTH // LANES
V_SLAB0 = K_SLAB0 + ATT_WIDTH // LANES
LORA_SLAB = V_SLAB0 + ATT_WIDTH // LANES
FG_SLAB = LORA_SLAB + 1
N_SLABS_F32 = FG_SLAB + 1
GATE_SLAB0 = 0
R_SLAB0 = GATE_COLS // LANES
RK_SLAB0 = R_SLAB0 + RWKV_WIDTH // LANES
RV_SLAB0 = RK_SLAB0 + RWKV_WIDTH // LANES
N_SLABS_BF16 = RV_SLAB0 + RWKV_WIDTH // LANES
RWKV_QUAD = 4
QW = RWKV_QUAD * HEAD_DIM
N_QUADS = RWKV_WIDTH // QW

PROJ_TN = 512
PROJ_TM = 2048
ROW_TILE = 512
FFN_TM = 1024
ATT_CHUNK = KEYS_PER_QUERY * DILATIONS[-1]
QB = KEYS_PER_QUERY
ATT_UNROLL = 3
RWKV_CHUNK = 64
RWKV_T = 512
RWKV_GROUP = 8


def _bf16(x):
    return x.astype(jnp.bfloat16)


def _dot(a, b):
    return jnp.dot(a, b, preferred_element_type=jnp.float32)


def _dot_nt(a, b):
    return lax.dot_general(a, b, (((1,), (1,)), ((), ())), preferred_element_type=jnp.float32)


def _proj_body(x_ref, g_ref, w_ref, o32_ref, o16_ref, h_scr, *, n_f32_blocks):
    j = pl.program_id(1)

    @pl.when(j == 0)
    def _():
        x = x_ref[...]
        ms = jnp.mean(x * x, axis=-1, keepdims=True)
        h_scr[...] = _bf16(x * lax.rsqrt(ms + NORM_EPS) * g_ref[...])

    acc = _dot(h_scr[...], w_ref[0])

    @pl.when(j < n_f32_blocks)
    def _():
        for s in range(PROJ_TN // LANES):
            o32_ref[s] = acc[:, s * LANES:(s + 1) * LANES]

    @pl.when(j >= n_f32_blocks)
    def _():
        for s in range(PROJ_TN // LANES):
            o16_ref[s] = _bf16(acc[:, s * LANES:(s + 1) * LANES])


def _proj(x2, g, w_in, tm):
    m = x2.shape[0]
    n_col_blocks = IN_COLS // PROJ_TN
    slabs_per_block = PROJ_TN // LANES
    n_f32_blocks = N_SLABS_F32 // slabs_per_block
    att_end = 3 * ATT_WIDTH
    rkv_end = att_end + 3 * RWKV_WIDTH
    lora_end = att_end + RWKV_COLS
    w_cols = jnp.concatenate([w_in[:, :att_end], w_in[:, rkv_end:lora_end], w_in[:, lora_end:],
                              w_in[:, att_end:rkv_end]], axis=1)
    w_blk = _bf16(w_cols).reshape(D_MODEL, n_col_blocks, PROJ_TN).transpose(1, 0, 2)
    blk = (slabs_per_block, tm, LANES)
    return pl.pallas_call(
        functools.partial(_proj_body, n_f32_blocks=n_f32_blocks),
        out_shape=(jax.ShapeDtypeStruct((N_SLABS_F32, m, LANES), jnp.float32),
                   jax.ShapeDtypeStruct((N_SLABS_BF16, m, LANES), jnp.bfloat16)),
        grid=(m // tm, n_col_blocks),
        in_specs=[
            pl.BlockSpec((tm, D_MODEL), lambda i, j: (i, 0)),
            pl.BlockSpec((1, D_MODEL), lambda i, j: (0, 0)),
            pl.BlockSpec((1, D_MODEL, PROJ_TN), lambda i, j: (j, 0, 0)),
        ],
        out_specs=(pl.BlockSpec(blk, lambda i, j: (jnp.minimum(j, n_f32_blocks - 1), i, 0)),
                   pl.BlockSpec(blk, lambda i, j: (jnp.maximum(j - n_f32_blocks, 0), i, 0))),
        scratch_shapes=[pltpu.VMEM((tm, D_MODEL), jnp.bfloat16)],
        compiler_params=pltpu.CompilerParams(
            dimension_semantics=("parallel", "arbitrary"), vmem_limit_bytes=VMEM_LIMIT),
        name="proj",
    )(x2, g, w_blk)


def _attn_units(qs, ks, vs, biases, lo):
    scale = HEAD_DIM ** -0.5

    def logits(q, k):
        qs_ = q * scale
        zero = jnp.zeros_like(qs_)
        lhs = _bf16(jnp.concatenate([jnp.where(lo, qs_, zero), jnp.where(lo, zero, qs_)], axis=0))
        return _dot_nt(lhs, _bf16(k))

    s = [logits(q, k) for q, k in zip(qs, ks)]
    s0 = [x[:QB] + b[0] for x, b in zip(s, biases)]
    s1 = [x[QB:] + b[1] for x, b in zip(s, biases)]
    m0 = [jnp.max(x, axis=-1, keepdims=True) for x in s0]
    m1 = [jnp.max(x, axis=-1, keepdims=True) for x in s1]
    p0 = [_bf16(jnp.exp(x - m)) for x, m in zip(s0, m0)]
    p1 = [_bf16(jnp.exp(x - m)) for x, m in zip(s1, m1)]

    def weighted(p0_, p1_, v):
        zv = jnp.zeros_like(v)
        ones = jnp.ones_like(v)
        rhs0 = _bf16(jnp.concatenate([jnp.where(lo, v, zv), jnp.where(lo, ones, zv)], axis=1))
        rhs1 = _bf16(jnp.concatenate([jnp.where(lo, zv, v), jnp.where(lo, zv, ones)], axis=1))
        return _dot(p0_, rhs0) + _dot(p1_, rhs1)

    ol = [weighted(a, b, v) for a, b, v in zip(p0, p1, vs)]
    out = []
    for x, a, b in zip(ol, m0, m1):
        l = x[:, LANES:]
        out.append((x[:, :LANES] / l, jnp.where(lo, a, b) + jnp.log(l)))
    return out


def _attn_body(q_ref, kc_ref, kp_ref, vc_ref, vp_ref, bias_ref, o_ref, o_scr, l_scr):
    c = pl.program_id(1)
    g = pl.program_id(2)
    lo = lax.broadcasted_iota(jnp.int32, (1, LANES), 1) < HEAD_DIM
    first = jnp.where(c == 0, 1, 0)

    def rows(start, n, d):
        return pl.ds(start, n) if d == 1 else pl.ds(start, n, stride=d)

    def group(gi):
        d = DILATIONS[gi]
        n_qb = ATT_CHUNK // (QB * d)

        def run(units):
            qs = [q_ref[p, rows(start_q, QB, d), :] for p, start_q, _, _, _ in units]
            biases = [(bias_ref[variant, gi * HEADS_PER_GROUP + 2 * p],
                       bias_ref[variant, gi * HEADS_PER_GROUP + 2 * p + 1]) for p, _, _, _, variant in units]
            res = _attn_units(qs, [u[2] for u in units], [u[3] for u in units], biases, lo)
            for (p, start_q, _, _, _), (o, lse) in zip(units, res):
                o_scr[gi, p, rows(start_q, QB, d), :] = o
                l_scr[gi, p, rows(start_q, QB, d), :] = lse

        n_first = max(j for j in range(1, ATT_UNROLL + 1) if d % j == 0)

        def first_blocks(it, carry):
            units = []
            for j in range(n_first):
                r = it * n_first + j
                prev_start = r + QB * d * (n_qb - 1)
                for p in range(2):
                    k = jnp.concatenate([kp_ref[p, rows(prev_start, QB, d), :], kc_ref[p, rows(r, QB, d), :]], axis=0)
                    v = jnp.concatenate([vp_ref[p, rows(prev_start, QB, d), :], vc_ref[p, rows(r, QB, d), :]], axis=0)
                    units.append((p, r, k, v, first))
            run(units)
            return carry

        lax.fori_loop(0, d // n_first, first_blocks, 0)

        n_later = d * (n_qb - 1)
        if n_later:
            n_par = max(j for j in range(1, ATT_UNROLL + 1) if n_later % j == 0)

            def later_blocks(it, carry):
                units = []
                for j in range(n_par):
                    u = it * n_par + j
                    start_q = u % d + QB * d * (1 + u // d)
                    for p in range(2):
                        k = kc_ref[p, rows(start_q - QB * d, 2 * QB, d), :]
                        v = vc_ref[p, rows(start_q - QB * d, 2 * QB, d), :]
                        units.append((p, start_q, k, v, 0))
                run(units)
                return carry

            lax.fori_loop(0, n_later // n_par, later_blocks, 0)

    for gi in range(N_GROUPS):
        pl.when(g == gi)(functools.partial(group, gi))

    @pl.when(g == N_GROUPS - 1)
    def _():
        tile = 256

        def comb(i, carry):
            rs = pl.ds(pl.multiple_of(i * tile, tile), tile)
            for p in range(2):
                ls = [l_scr[gi, p, rs, :] for gi in range(N_GROUPS)]
                mx = jnp.maximum(jnp.maximum(ls[0], ls[1]), ls[2])
                ws = [jnp.exp(l - mx) for l in ls]
                num = ws[0] * o_scr[0, p, rs, :] + ws[1] * o_scr[1, p, rs, :] + ws[2] * o_scr[2, p, rs, :]
                o_ref[rs, p * LANES:(p + 1) * LANES] = _bf16(num / (ws[0] + ws[1] + ws[2]))
            return carry

        lax.fori_loop(0, ATT_CHUNK // tile, comb, 0)


def _attn(slabs, bias_tiles, batch, seq):
    m = slabs.shape[1]
    n_chunks = seq // ATT_CHUNK
    blk = (2, ATT_CHUNK, LANES)

    def cur(slab0):
        return pl.BlockSpec(blk, lambda b, c, g: (slab0 // 2 + g, b * n_chunks + c, 0))

    def prev(slab0):
        return pl.BlockSpec(blk, lambda b, c, g: (slab0 // 2 + g, b * n_chunks + jnp.maximum(c - 1, 0), 0))

    return pl.pallas_call(
        _attn_body,
        out_shape=jax.ShapeDtypeStruct((m, ATT_OUT_WIDTH), jnp.bfloat16),
        grid=(batch, n_chunks, N_GROUPS),
        in_specs=[cur(Q_SLAB0), cur(K_SLAB0), prev(K_SLAB0), cur(V_SLAB0), prev(V_SLAB0),
                  pl.BlockSpec(bias_tiles.shape, lambda b, c, g: (0, 0, 0, 0))],
        out_specs=pl.BlockSpec((ATT_CHUNK, ATT_OUT_WIDTH), lambda b, c, g: (b * n_chunks + c, 0)),
        scratch_shapes=[pltpu.VMEM((N_GROUPS, 2, ATT_CHUNK, LANES), jnp.float32),
                        pltpu.VMEM((N_GROUPS, 2, ATT_CHUNK, LANES), jnp.float32)],
        compiler_params=pltpu.CompilerParams(
            dimension_semantics=("parallel", "parallel", "arbitrary"), vmem_limit_bytes=VMEM_LIMIT),
        name="attn",
    )(slabs, slabs, slabs, slabs, slabs, bias_tiles)


def _t5_bucket(dist):
    max_exact = N_BUCKETS // 2
    d_f = jnp.maximum(dist, 1).astype(jnp.float32)
    large = max_exact + (jnp.log(d_f / max_exact) / math.log(MAX_DISTANCE / max_exact)
                         * (N_BUCKETS - max_exact)).astype(jnp.int32)
    large = jnp.minimum(large, N_BUCKETS - 1)
    return jnp.where(dist < max_exact, dist, large)


def _bias_tiles(rel_bias):
    dil = jnp.array(DILATIONS, jnp.int32)
    dist = dil[:, None] * jnp.arange(KEYS_PER_QUERY + 1, dtype=jnp.int32)[None, :]
    bucket = _t5_bucket(dist)
    bias = rel_bias.reshape(N_BUCKETS, N_GROUPS, HEADS_PER_GROUP)[bucket, jnp.arange(N_GROUPS)[:, None]]
    bias = jnp.transpose(bias, (0, 2, 1)).astype(jnp.float32).reshape(ATT_HEADS, KEYS_PER_QUERY + 1)
    n = 3 * QB - 1
    neg = lambda w: jnp.full((ATT_HEADS, w), NEG, jnp.float32)
    e = jnp.concatenate([neg(QB - 1), bias[:, ::-1], neg(QB - 1), neg(1)], axis=1)
    e = jnp.roll(e, -(QB - 1), axis=1)
    t0 = jnp.tile(e, (1, QB))[:, :QB * n].reshape(ATT_HEADS, QB, n)[:, :, :2 * QB]
    col = jnp.arange(2 * QB)[None, None, :]
    t1 = jnp.where(col >= QB, t0, NEG)
    return jnp.stack([t0, t1])


def _interleave(*gens):
    gens = list(gens)
    while gens:
        for gen in list(gens):
            try:
                next(gen)
            except StopIteration:
                gens.remove(gen)


def _rwkv_body(r_ref, k_ref, v_ref, lora_ref, fg_ref, pra_ref, prs_ref, sh_ref, wl_ref, o_ref,
               carry, h_st, tok_s, post_s, up_s, zq_s, kcbt_s, g_s, vb_s, y_s, *, n_t):
    t_blk = o_ref.shape[0]
    cc = RWKV_CHUNK
    cs = range(t_blk // cc)
    i = pl.program_id(0)
    first_tok = (i % n_t) == 0
    first_seq = ((i + n_t - 2) % n_t) == 0

    @pl.when(i == 0)
    def _():
        for ref in (carry, h_st, tok_s, post_s, up_s, zq_s, kcbt_s, g_s, vb_s):
            ref[...] = jnp.zeros(ref.shape, ref.dtype)

    lane_head = lax.broadcasted_iota(jnp.int32, (QW, QW), 1) // HEAD_DIM
    row_head = lax.broadcasted_iota(jnp.int32, (QW, QW), 0) // HEAD_DIM
    same_head = lane_head == row_head
    head_ones = _bf16(jnp.where(same_head, 1.0, 0.0))
    sw = RWKV_QUAD * cc
    chunk_lane_head = lax.broadcasted_iota(jnp.int32, (cc, QW), 1) // HEAD_DIM
    keep = [chunk_lane_head == h for h in range(RWKV_QUAD)]
    chunk_col_head = lax.broadcasted_iota(jnp.int32, (cc, sw), 1) // cc
    keep_sbs = [chunk_col_head == h for h in range(RWKV_QUAD)]

    def head_sum(x):
        return _dot(_bf16(x), head_ones)

    def stack4(x):
        masks = keep_sbs if x.shape[1] == sw else keep
        z = jnp.zeros_like(x)
        return _bf16(jnp.concatenate([jnp.where(masks[h], x, z) for h in range(RWKV_QUAD)], axis=0))

    def each(fn, *lists):
        return [fn(*args) for args in zip(*lists)]

    def split3(x):
        hi = _bf16(x)
        r1 = x - hi.astype(jnp.float32)
        mid = _bf16(r1)
        return hi, mid, _bf16(r1 - mid.astype(jnp.float32))

    def per_token(tok_w):
        pr = pra_ref[0]
        sh = sh_ref[...]

        def shifted(x, idx, mu):
            width = x.shape[1]
            row = lax.broadcasted_iota(jnp.int32, x.shape, 0)
            last = jnp.where(first_tok, jnp.zeros((1, width), jnp.float32), carry[idx, 7:8, :width])
            prev = jnp.where(row == 0, last, pltpu.roll(x, 1, 0))
            carry[idx, :, :width] = x[t_blk - 8:, :]
            return x + (prev - x) * mu

        wide = lambda ref: jnp.concatenate([ref[0], ref[1]], axis=1).astype(jnp.float32)
        r = shifted(wide(r_ref), 0, pr[0:1])
        k = shifted(wide(k_ref), 1, pr[1:2])
        v = shifted(wide(v_ref), 2, pr[2:3])
        tok_s[tok_w, 0] = r
        tok_s[tok_w, 4] = v
        yield
        f_lora = shifted(lora_ref[0], 3, sh[0:1, :LANES])
        f_g = shifted(fg_ref[0], 4, sh[1:2, :LANES])
        w0, a0, k_k, k_a, r_k = (pr[j:j + 1] for j in range(3, 8))
        wd = w0 + _dot(_bf16(jnp.tanh(f_lora)), wl_ref[0, 0])
        tok_s[tok_w, 5] = -math.exp(-0.5) * jax.nn.sigmoid(wd)
        a = jax.nn.sigmoid(a0 + _dot(_bf16(f_lora), wl_ref[0, 1]))
        tok_s[tok_w, 7] = _dot(_bf16(jax.nn.sigmoid(f_g)), wl_ref[0, 2])
        yield
        kk = k * k_k
        kk = kk * lax.rsqrt(jnp.maximum(head_sum(kk * kk), L2_EPS * L2_EPS))
        tok_s[tok_w, 1] = kk
        tok_s[tok_w, 3] = kk * a
        yield
        k2 = k * (1.0 + (a - 1.0) * k_a)
        tok_s[tok_w, 2] = k2
        tok_s[tok_w, 6] = head_sum(r * k2 * r_k) * v
        yield

    def per_chunk(tok_r, chk_w, cs):
        t_i = lax.broadcasted_iota(jnp.int32, (cc, sw), 0)
        s_i = lax.broadcasted_iota(jnp.int32, (cc, sw), 1) % cc
        strict_lower = t_i > s_i
        lower = t_i >= s_i
        eye_f = jnp.where(t_i == s_i, 1.0, 0.0)
        zero_m = jnp.zeros((cc, sw), jnp.float32)
        ci = lax.broadcasted_iota(jnp.int32, (cc, cc), 0)
        cj = lax.broadcasted_iota(jnp.int32, (cc, cc), 1)
        tri = _bf16(jnp.where(ci >= cj, 1.0, 0.0))
        rows_of = lambda j: [tok_s[tok_r, j, c * cc:(c + 1) * cc, :] for c in cs]
        r, kk, k2, bb, v, lw = (rows_of(j) for j in range(6))

        parts = each(split3, lw)
        big_l = each(lambda p: _dot(tri, p[0]) + _dot(tri, p[1]) + _dot(tri, p[2]), parts)
        yield
        l_end = each(lambda l: l[cc - 1:cc], big_l)
        e_l = each(jnp.exp, big_l)
        e_lm = each(lambda l, w: jnp.exp(l - w), big_l, lw)
        e_nl = each(lambda l: jnp.exp(-l), big_l)
        e_c = each(lambda le, l: jnp.exp(le - l), l_end, big_l)
        a_kk_f = each(lambda x, e: x * e, kk, e_lm)
        a_kk = each(_bf16, a_kk_f)
        a_r_f = each(lambda x, e: x * e, r, e_l)
        a_r = each(_bf16, a_r_f)
        kb = each(lambda xk, xb, e: jnp.concatenate([stack4(xk * e), stack4(xb * e)], axis=0),
                  k2, bb, e_nl)
        vb = each(_bf16, v)
        vst = each(stack4, v)
        yield
        s12 = each(lambda ak, ar, kb_: _dot_nt(jnp.concatenate([ak, ar], axis=0), kb_), a_kk, a_r, kb)
        s1 = each(lambda s: s[:cc], s12)
        s2 = each(lambda s: s[cc:], s12)
        yield
        mk = each(lambda s: _bf16(jnp.where(strict_lower, s[:, :sw], zero_m)), s1)
        mb = each(lambda s: jnp.where(strict_lower, s[:, sw:], zero_m), s1)
        nkb = each(lambda s: _bf16(jnp.concatenate([jnp.where(lower, s[:, :sw], zero_m),
                                                     jnp.where(lower, -s[:, sw:], zero_m)], axis=1)), s2)

        x = each(lambda n: eye_f - jnp.where((t_i // 2) == (s_i // 2), n, zero_m), mb)
        blk = 4
        while blk <= cc:
            half = blk // 2
            sel = ((t_i // blk) == (s_i // blk)) & ((t_i % blk) >= half) & ((s_i % blk) < half)
            xb = each(_bf16, x)
            t1 = each(lambda xc, n: _bf16(_dot(xc, stack4(jnp.where(sel, n, zero_m)))), xb, mb)
            yield
            x = each(lambda xf, t: xf - _dot(t, stack4(xf)), x, t1)
            yield
            blk *= 2
        tinv = each(_bf16, x)

        pm = each(lambda t, ak: _dot(t, stack4(ak)), tinv, a_kk_f)
        mkv = each(lambda m_, v_: _dot(m_, v_), mk, vst)
        yield
        q = each(lambda t, m_: _dot(t, stack4(m_)), tinv, mkv)
        yield
        pmb = each(_bf16, pm)
        u = each(lambda af, n, p: af + _dot(n[:, sw:], stack4(p)), a_r_f, nkb, pm)
        z = each(lambda n, v_, q_: _dot(n, jnp.concatenate([v_, stack4(q_)], axis=0)), nkb, vst, q)
        yield
        for j, c in enumerate(cs):
            up_s[chk_w, c] = jnp.concatenate([pmb[j], _bf16(u[j])], axis=0)
            zq_s[chk_w, c, 0] = z[j]
            zq_s[chk_w, c, 1] = q[j]
            vb_s[chk_w, c] = vb[j]
        yield
        pad = [jnp.zeros((LANES - 2 * cc, QW), jnp.float32)] if 2 * cc < LANES else []
        kcg = each(lambda xk, xb, e, le: jnp.concatenate(
            [xk * e, -(xb * e)] + pad + [jnp.broadcast_to(jnp.exp(le), (LANES, QW))], axis=0).T,
            k2, bb, e_c, l_end)
        for j, c in enumerate(cs):
            kcbt_s[chk_w, c] = _bf16(kcg[j][:, :LANES])
            g_s[chk_w, c] = kcg[j][:, LANES:]
        yield

    def per_chunk_groups(tok_r, chk_w):
        for first in range(0, len(cs), RWKV_GROUP):
            yield from per_chunk(tok_r, chk_w, list(cs[first:first + RWKV_GROUP]))
        post_s[chk_w, 0] = tok_s[tok_r, 6]
        post_s[chk_w, 1] = tok_s[tok_r, 7]
        yield

    def sequential(chk_r):
        h = jnp.where(first_seq, jnp.zeros((QW, QW), jnp.float32), h_st[...])
        pad_b = [jnp.zeros((LANES - 2 * cc, QW), jnp.bfloat16)] if 2 * cc < LANES else []
        for c in cs:
            ys = _dot(up_s[chk_r, c], _bf16(h))
            sig = ys[:cc] + zq_s[chk_r, c, 1]
            y_s[c * cc:(c + 1) * cc, :] = ys[cc:] + zq_s[chk_r, c, 0]
            yield
            hu = _dot(kcbt_s[chk_r, c], jnp.concatenate([vb_s[chk_r, c], _bf16(sig)] + pad_b, axis=0))
            g = g_s[chk_r, c]
            h = h * jnp.concatenate([g, g], axis=1) + jnp.where(same_head, hu, jnp.zeros_like(hu))
            yield
        h_st[...] = h
        prs = prs_ref[0]
        y = y_s[...]
        mu = head_sum(y) * (1.0 / HEAD_DIM)
        yc = y - mu
        var = head_sum(yc * yc) * (1.0 / HEAD_DIM)
        yn = yc * lax.rsqrt(var + LN_X_EPS) * prs[8:9] + prs[9:10]
        o_ref[...] = _bf16((yn + post_s[chk_r, 0]) * post_s[chk_r, 1])
        yield

    for parity in (0, 1):
        @pl.when(i % 2 == parity)
        def _(parity=parity):
            _interleave(per_chunk_groups(1 - parity, 1 - parity), sequential(parity), per_token(parity))


def _rwkv(slabs32, slabs16, pr, sh, wl, batch, seq):
    m = slabs32.shape[1]
    t_blk = RWKV_T
    n_t = seq // t_blk
    n_c = t_blk // RWKV_CHUNK
    n_blocks = batch * N_QUADS * n_t
    slabs_per_quad = QW // LANES

    def where(blk):
        blk = jnp.clip(blk, 0, n_blocks - 1)
        seq_id, t = blk // n_t, blk % n_t
        return seq_id % N_QUADS, (seq_id // N_QUADS) * n_t + t

    def tok(slab0, per_quad):
        if per_quad:
            return pl.BlockSpec((slabs_per_quad, t_blk, LANES),
                                lambda i: (slab0 // slabs_per_quad + where(i)[0], where(i)[1], 0))
        return pl.BlockSpec((1, t_blk, LANES), lambda i: (slab0, where(i)[1], 0))

    return pl.pallas_call(
        functools.partial(_rwkv_body, n_t=n_t),
        out_shape=jax.ShapeDtypeStruct((m, RWKV_WIDTH), jnp.bfloat16),
        grid=(n_blocks + 2,),
        in_specs=[tok(R_SLAB0, True), tok(RK_SLAB0, True), tok(RV_SLAB0, True),
                  tok(LORA_SLAB, False), tok(FG_SLAB, False),
                  pl.BlockSpec((1,) + pr.shape[1:], lambda i: (where(i)[0], 0, 0)),
                  pl.BlockSpec((1,) + pr.shape[1:], lambda i: (where(i - 2)[0], 0, 0)),
                  pl.BlockSpec(sh.shape, lambda i: (0, 0)),
                  pl.BlockSpec((1,) + wl.shape[1:], lambda i: (where(i)[0], 0, 0, 0))],
        out_specs=pl.BlockSpec((t_blk, QW), lambda i: (where(i - 2)[1], where(i - 2)[0])),
        scratch_shapes=[
            pltpu.VMEM((5, 8, QW), jnp.float32),
            pltpu.VMEM((QW, QW), jnp.float32),
            pltpu.VMEM((2, 8, t_blk, QW), jnp.float32),
            pltpu.VMEM((2, 2, t_blk, QW), jnp.float32),
            pltpu.VMEM((2, n_c, 2 * RWKV_CHUNK, QW), jnp.bfloat16),
            pltpu.VMEM((2, n_c, 2, RWKV_CHUNK, QW), jnp.float32),
            pltpu.VMEM((2, n_c, QW, LANES), jnp.bfloat16),
            pltpu.VMEM((2, n_c, QW, LANES), jnp.float32),
            pltpu.VMEM((2, n_c, RWKV_CHUNK, QW), jnp.bfloat16),
            pltpu.VMEM((t_blk, QW), jnp.float32),
        ],
        compiler_params=pltpu.CompilerParams(
            dimension_semantics=("arbitrary",), vmem_limit_bytes=VMEM_LIMIT),
        name="rwkv",
    )(slabs16, slabs16, slabs16, slabs32, slabs32, pr, pr, sh, wl)


def _rwkv_params(shift_mu, w0, a0, k_k, k_a, r_k, ln_w, ln_b, w_w2, w_a2, w_g2):
    def quads(vec):
        return vec.reshape(N_QUADS, QW)

    mu_r, mu_k, mu_v = (quads(shift_mu[j * RWKV_WIDTH:(j + 1) * RWKV_WIDTH]) for j in range(3))
    rows = [mu_r, mu_k, mu_v, quads(w0), quads(a0), quads(k_k), quads(k_a), quads(r_k.reshape(-1)),
            quads(ln_w), quads(ln_b)]
    pr = jnp.stack(rows, axis=1)
    pr = jnp.pad(pr, ((0, 0), (0, 16 - pr.shape[1]), (0, 0)))
    mu_rest = shift_mu[3 * RWKV_WIDTH:]
    sh = jnp.pad(mu_rest.reshape(2, LANES), ((0, 6), (0, QW - LANES)))
    zeros = jnp.zeros((DECAY_LORA, RWKV_WIDTH), jnp.float32)
    wd = jnp.concatenate([w_w2, zeros], axis=0)
    wa = jnp.concatenate([zeros, w_a2], axis=0)
    wl = jnp.stack([wd, wa, w_g2], axis=0)
    wl = wl.reshape(3, LANES, N_QUADS, QW).transpose(2, 0, 1, 3)
    return pr, sh, _bf16(wl)


def _rms(x, g):
    ms = jnp.mean(x * x, axis=-1, keepdims=True)
    return x * lax.rsqrt(ms + NORM_EPS) * g


def _merge_body(x_ref, att_ref, rw_ref, gate_ref, bg_ref, wa_ref, wr_ref, wo_ref, gn_ref, o_ref):
    n_gs = D_MODEL // LANES
    f0 = jnp.concatenate([gate_ref[s] for s in range(n_gs)], axis=1).astype(jnp.float32)
    f1 = jnp.concatenate([gate_ref[n_gs + s] for s in range(n_gs)], axis=1).astype(jnp.float32)
    g0 = jax.nn.sigmoid(f0 + bg_ref[:, :D_MODEL])
    g1 = jax.nn.sigmoid(f1 + bg_ref[:, D_MODEL:])
    merged = g0 * _dot(att_ref[...], wa_ref[...]) + g1 * _dot(rw_ref[...], wr_ref[...])
    z = _dot(_bf16(merged), wo_ref[...])
    o_ref[...] = x_ref[...] + _rms(z, gn_ref[...])


def _merge(x2, o_att, o_rwkv, slabs, b_gate, wa, wr, wo, g_post, tm):
    m = x2.shape[0]
    n_gate_slabs = GATE_COLS // LANES
    const = lambda shape: pl.BlockSpec(shape, lambda i: (0, 0))
    return pl.pallas_call(
        _merge_body,
        out_shape=jax.ShapeDtypeStruct((m, D_MODEL), jnp.float32),
        grid=(m // tm,),
        in_specs=[
            pl.BlockSpec((tm, D_MODEL), lambda i: (i, 0)),
            pl.BlockSpec((tm, ATT_OUT_WIDTH), lambda i: (i, 0)),
            pl.BlockSpec((tm, RWKV_WIDTH), lambda i: (i, 0)),
            pl.BlockSpec((n_gate_slabs, tm, LANES), lambda i: (GATE_SLAB0 // n_gate_slabs, i, 0)),
            const((1, GATE_COLS)), const(wa.shape), const(wr.shape), const(wo.shape), const((1, D_MODEL)),
        ],
        out_specs=pl.BlockSpec((tm, D_MODEL), lambda i: (i, 0)),
        compiler_params=pltpu.CompilerParams(
            dimension_semantics=("parallel",), vmem_limit_bytes=VMEM_LIMIT),
        name="merge",
    )(x2, o_att, o_rwkv, slabs, b_gate, wa, wr, wo, g_post)


def _ffn_body(x_ref, gpre_ref, w1_ref, w2_ref, gpost_ref, o_ref, *, tf):
    x = x_ref[...]
    h = _bf16(_rms(x, gpre_ref[...]))
    acc = jnp.zeros(x.shape, jnp.float32)
    for c in range(D_FF // tf):
        u = jnp.maximum(_dot(h, w1_ref[:, c * tf:(c + 1) * tf]), 0.0)
        acc = acc + _dot(_bf16(u * u), w2_ref[c * tf:(c + 1) * tf, :])
    o_ref[...] = x + _rms(acc, gpost_ref[...])


def _ffn(x2, g_pre, w1, w2, g_post, tm, tf=1024):
    m = x2.shape[0]
    const = lambda shape: pl.BlockSpec(shape, lambda i: (0, 0), pipeline_mode=pl.Buffered(1))
    return pl.pallas_call(
        functools.partial(_ffn_body, tf=tf),
        out_shape=jax.ShapeDtypeStruct((m, D_MODEL), jnp.float32),
        grid=(m // tm,),
        in_specs=[pl.BlockSpec((tm, D_MODEL), lambda i: (i, 0)), const((1, D_MODEL)),
                  const(w1.shape), const(w2.shape), const((1, D_MODEL))],
        out_specs=pl.BlockSpec((tm, D_MODEL), lambda i: (i, 0)),
        compiler_params=pltpu.CompilerParams(
            dimension_semantics=("parallel",), vmem_limit_bytes=VMEM_LIMIT),
        name="ffn",
    )(x2, g_pre, w1, w2, g_post)


def kernel(x, rel_bias, norm_mix_pre, norm_mix_post, norm_ffn_pre, norm_ffn_post, w_in, b_gate, shift_mu, w0, w_w2, a0, w_a2, w_g2, k_k, k_a, r_k, ln_x_w, ln_x_b, w_att_branch, w_rwkv_branch, w_out, w_ffn1, w_ffn2):
    batch, seq, d_model = x.shape
    assert d_model == D_MODEL and seq % ATT_CHUNK == 0 and seq % RWKV_T == 0
    m = batch * seq
    tm = ROW_TILE
    assert m % PROJ_TM == 0 and m % tm == 0 and m % FFN_TM == 0
    bias_tiles = _bias_tiles(rel_bias)
    row = lambda vec: vec.reshape(1, -1)
    x2 = x.reshape(m, D_MODEL)
    for l in range(w_in.shape[0]):
        slabs32, slabs16 = _proj(x2, row(norm_mix_pre[l]), w_in[l], PROJ_TM)
        o_att = _attn(slabs32, bias_tiles, batch, seq)
        pr, sh, wl = _rwkv_params(shift_mu[l], w0[l], a0[l], k_k[l], k_a[l], r_k[l], ln_x_w[l], ln_x_b[l],
                                  w_w2[l], w_a2[l], w_g2[l])
        o_rwkv = _rwkv(slabs32, slabs16, pr, sh, wl, batch, seq)
        x2 = _merge(x2, o_att, o_rwkv, slabs16, row(b_gate[l]), _bf16(w_att_branch[l]), _bf16(w_rwkv_branch[l]),
                    _bf16(w_out[l]), row(norm_mix_post[l]), tm)
        x2 = _ffn(x2, row(norm_ffn_pre[l]), _bf16(w_ffn1[l]), _bf16(w_ffn2[l]), row(norm_ffn_post[l]), FFN_TM)
    return x2.reshape(batch, seq, D_MODEL)
```

```python
import functools
import math

import jax
import jax.numpy as jnp
from jax import lax
from jax.experimental import pallas as pl
from jax.experimental.pallas import tpu as pltpu

D_MODEL = 1024
HEAD_DIM = 64
DILATIONS = (1, 4, 16)
KEYS_PER_QUERY = 128
N_GROUPS = len(DILATIONS)
HEADS_PER_GROUP = 4
ATT_HEADS = N_GROUPS * HEADS_PER_GROUP
ATT_WIDTH = ATT_HEADS * HEAD_DIM
ATT_OUT_WIDTH = HEADS_PER_GROUP * HEAD_DIM
N_BUCKETS = 32
MAX_DISTANCE = KEYS_PER_QUERY * DILATIONS[-1]
RWKV_WIDTH = D_MODEL
DECAY_LORA = 64
ICLR_LORA = 64
GATE_LORA = 128
RWKV_COLS = 3 * RWKV_WIDTH + DECAY_LORA + ICLR_LORA + GATE_LORA
N_BRANCHES = 2
IN_COLS = 3 * ATT_WIDTH + RWKV_COLS + N_BRANCHES * D_MODEL
D_FF = 4 * D_MODEL
NORM_EPS = 1e-6
LN_X_EPS = 64e-5
L2_EPS = 1e-12

LANES = 128
VMEM_LIMIT = 56 * 1024 * 1024
NEG = -1e30

GATE_COLS = N_BRANCHES * D_MODEL
Q_SLAB0 = 0
K_SLAB0 = Q_SLAB0 + ATT_WIDTH // LANES
V_SLAB0 = K_SLAB0 + ATT_WIDTH // LANES
LORA_SLAB = V_SLAB0 + ATT_WIDTH // LANES
FG_SLAB = LORA_SLAB + 1
N_SLABS_F32 = FG_SLAB + 1
GATE_SLAB0 = 0
R_SLAB0 = GATE_COLS // LANES
RK_SLAB0 = R_SLAB0 + RWKV_WIDTH // LANES
RV_SLAB0 = RK_SLAB0 + RWKV_WIDTH // LANES
N_SLABS_BF16 = RV_SLAB0 + RWKV_WIDTH // LANES
RWKV_QUAD = 4
QW = RWKV_QUAD * HEAD_DIM
N_QUADS = RWKV_WIDTH // QW

PROJ_TN = 512
PROJ_TM = 2048
ROW_TILE = 512
FFN_TM = 1024
ATT_CHUNK = KEYS_PER_QUERY * DILATIONS[-1]
QB = KEYS_PER_QUERY
ATT_UNROLL = 3
RWKV_CHUNK = 64
RWKV_T = 512
RWKV_GROUP = 8


def _bf16(x):
    return x.astype(jnp.bfloat16)


def _dot(a, b):
    return jnp.dot(a, b, preferred_element_type=jnp.float32)


def _dot_nt(a, b):
    return lax.dot_general(a, b, (((1,), (1,)), ((), ())), preferred_element_type=jnp.float32)


def _proj_body(x_ref, g_ref, w_ref, o32_ref, o16_ref, h_scr, *, n_f32_blocks):
    j = pl.program_id(1)

    @pl.when(j == 0)
    def _():
        x = x_ref[...]
        ms = jnp.mean(x * x, axis=-1, keepdims=True)
        h_scr[...] = _bf16(x * lax.rsqrt(ms + NORM_EPS) * g_ref[...])

    acc = _dot(h_scr[...], w_ref[...])

    @pl.when(j < n_f32_blocks)
    def _():
        for s in range(PROJ_TN // LANES):
            o32_ref[s] = acc[:, s * LANES:(s + 1) * LANES]

    @pl.when(j >= n_f32_blocks)
    def _():
        for s in range(PROJ_TN // LANES):
            o16_ref[s] = _bf16(acc[:, s * LANES:(s + 1) * LANES])


def _proj(x2, g, w_in, tm):
    m = x2.shape[0]
    n_col_blocks = IN_COLS // PROJ_TN
    slabs_per_block = PROJ_TN // LANES
    n_f32_blocks = N_SLABS_F32 // slabs_per_block
    att_end = 3 * ATT_WIDTH
    rkv_end = att_end + 3 * RWKV_WIDTH
    lora_end = att_end + RWKV_COLS
    w_cols = _bf16(jnp.concatenate([w_in[:, :att_end], w_in[:, rkv_end:lora_end], w_in[:, lora_end:],
                                    w_in[:, att_end:rkv_end]], axis=1))
    blk = (slabs_per_block, tm, LANES)
    return pl.pallas_call(
        functools.partial(_proj_body, n_f32_blocks=n_f32_blocks),
        out_shape=(jax.ShapeDtypeStruct((N_SLABS_F32, m, LANES), jnp.float32),
                   jax.ShapeDtypeStruct((N_SLABS_BF16, m, LANES), jnp.bfloat16)),
        grid=(m // tm, n_col_blocks),
        in_specs=[
            pl.BlockSpec((tm, D_MODEL), lambda i, j: (i, 0)),
            pl.BlockSpec((1, D_MODEL), lambda i, j: (0, 0)),
            pl.BlockSpec((D_MODEL, PROJ_TN), lambda i, j: (0, j)),
        ],
        out_specs=(pl.BlockSpec(blk, lambda i, j: (jnp.minimum(j, n_f32_blocks - 1), i, 0)),
                   pl.BlockSpec(blk, lambda i, j: (jnp.maximum(j - n_f32_blocks, 0), i, 0))),
        scratch_shapes=[pltpu.VMEM((tm, D_MODEL), jnp.bfloat16)],
        compiler_params=pltpu.CompilerParams(
            dimension_semantics=("parallel", "arbitrary"), vmem_limit_bytes=VMEM_LIMIT),
        name="proj",
    )(x2, g, w_cols)


def _attn_units(qs, ks, vs, biases, lo):
    scale = HEAD_DIM ** -0.5

    def logits(q, k):
        qs_ = q * scale
        zero = jnp.zeros_like(qs_)
        lhs = _bf16(jnp.concatenate([jnp.where(lo, qs_, zero), jnp.where(lo, zero, qs_)], axis=0))
        return _dot_nt(lhs, _bf16(k))

    s = [logits(q, k) for q, k in zip(qs, ks)]
    s0 = [x[:QB] + b[0] for x, b in zip(s, biases)]
    s1 = [x[QB:] + b[1] for x, b in zip(s, biases)]
    m0 = [jnp.max(x, axis=-1, keepdims=True) for x in s0]
    m1 = [jnp.max(x, axis=-1, keepdims=True) for x in s1]
    p0 = [_bf16(jnp.exp(x - m)) for x, m in zip(s0, m0)]
    p1 = [_bf16(jnp.exp(x - m)) for x, m in zip(s1, m1)]

    def weighted(p0_, p1_, v):
        zv = jnp.zeros_like(v)
        ones = jnp.ones_like(v)
        rhs0 = _bf16(jnp.concatenate([jnp.where(lo, v, zv), jnp.where(lo, ones, zv)], axis=1))
        rhs1 = _bf16(jnp.concatenate([jnp.where(lo, zv, v), jnp.where(lo, zv, ones)], axis=1))
        return _dot(p0_, rhs0) + _dot(p1_, rhs1)

    ol = [weighted(a, b, v) for a, b, v in zip(p0, p1, vs)]
    out = []
    for x, a, b in zip(ol, m0, m1):
        l = x[:, LANES:]
        out.append((x[:, :LANES] / l, jnp.where(lo, a, b) + jnp.log(l)))
    return out


def _attn_body(q_ref, kc_ref, kp_ref, vc_ref, vp_ref, bias_ref, o_ref, o_scr, l_scr):
    c = pl.program_id(1)
    g = pl.program_id(2)
    lo = lax.broadcasted_iota(jnp.int32, (1, LANES), 1) < HEAD_DIM
    before_start = (c == 0) & (lax.broadcasted_iota(jnp.int32, (QB, 2 * QB), 1) < QB)

    def rows(start, n, d):
        return pl.ds(start, n) if d == 1 else pl.ds(start, n, stride=d)

    def group(gi):
        d = DILATIONS[gi]
        n_qb = ATT_CHUNK // (QB * d)

        def run(units):
            qs = [q_ref[p, rows(start_q, QB, d), :] for p, start_q, _, _, _ in units]

            def bias(head, is_first):
                tile = bias_ref[head]
                return jnp.where(before_start, NEG, tile) if is_first else tile

            biases = [(bias(gi * HEADS_PER_GROUP + 2 * p, is_first), bias(gi * HEADS_PER_GROUP + 2 * p + 1, is_first))
                      for p, _, _, _, is_first in units]
            res = _attn_units(qs, [u[2] for u in units], [u[3] for u in units], biases, lo)
            for (p, start_q, _, _, _), (o, lse) in zip(units, res):
                o_scr[gi, p, rows(start_q, QB, d), :] = o
                l_scr[gi, p, rows(start_q, QB, d), :] = lse

        n_first = max(j for j in range(1, ATT_UNROLL + 1) if d % j == 0)

        def first_blocks(it, carry):
            units = []
            for j in range(n_first):
                r = it * n_first + j
                prev_start = r + QB * d * (n_qb - 1)
                for p in range(2):
                    k = jnp.concatenate([kp_ref[p, rows(prev_start, QB, d), :], kc_ref[p, rows(r, QB, d), :]], axis=0)
                    v = jnp.concatenate([vp_ref[p, rows(prev_start, QB, d), :], vc_ref[p, rows(r, QB, d), :]], axis=0)
                    units.append((p, r, k, v, True))
            run(units)
            return carry

        lax.fori_loop(0, d // n_first, first_blocks, 0)

        n_later = d * (n_qb - 1)
        if n_later:
            n_par = max(j for j in range(1, ATT_UNROLL + 1) if n_later % j == 0)

            def later_blocks(it, carry):
                units = []
                for j in range(n_par):
                    u = it * n_par + j
                    start_q = u % d + QB * d * (1 + u // d)
                    for p in range(2):
                        k = kc_ref[p, rows(start_q - QB * d, 2 * QB, d), :]
                        v = vc_ref[p, rows(start_q - QB * d, 2 * QB, d), :]
                        units.append((p, start_q, k, v, False))
                run(units)
                return carry

            lax.fori_loop(0, n_later // n_par, later_blocks, 0)

    for gi in range(N_GROUPS):
        pl.when(g == gi)(functools.partial(group, gi))

    @pl.when(g == N_GROUPS - 1)
    def _():
        tile = 256

        def comb(i, carry):
            rs = pl.ds(pl.multiple_of(i * tile, tile), tile)
            for p in range(2):
                ls = [l_scr[gi, p, rs, :] for gi in range(N_GROUPS)]
                mx = jnp.maximum(jnp.maximum(ls[0], ls[1]), ls[2])
                ws = [jnp.exp(l - mx) for l in ls]
                num = ws[0] * o_scr[0, p, rs, :] + ws[1] * o_scr[1, p, rs, :] + ws[2] * o_scr[2, p, rs, :]
                o_ref[rs, p * LANES:(p + 1) * LANES] = _bf16(num / (ws[0] + ws[1] + ws[2]))
            return carry

        lax.fori_loop(0, ATT_CHUNK // tile, comb, 0)


def _attn(slabs, bias_tiles, batch, seq):
    m = slabs.shape[1]
    n_chunks = seq // ATT_CHUNK
    blk = (2, ATT_CHUNK, LANES)

    def cur(slab0):
        return pl.BlockSpec(blk, lambda b, c, g: (slab0 // 2 + g, b * n_chunks + c, 0))

    def prev(slab0):
        return pl.BlockSpec(blk, lambda b, c, g: (slab0 // 2 + g, b * n_chunks + jnp.maximum(c - 1, 0), 0))

    return pl.pallas_call(
        _attn_body,
        out_shape=jax.ShapeDtypeStruct((m, ATT_OUT_WIDTH), jnp.bfloat16),
        grid=(batch, n_chunks, N_GROUPS),
        in_specs=[cur(Q_SLAB0), cur(K_SLAB0), prev(K_SLAB0), cur(V_SLAB0), prev(V_SLAB0),
                  pl.BlockSpec(bias_tiles.shape, lambda b, c, g: (0, 0, 0))],
        out_specs=pl.BlockSpec((ATT_CHUNK, ATT_OUT_WIDTH), lambda b, c, g: (b * n_chunks + c, 0)),
        scratch_shapes=[pltpu.VMEM((N_GROUPS, 2, ATT_CHUNK, LANES), jnp.float32),
                        pltpu.VMEM((N_GROUPS, 2, ATT_CHUNK, LANES), jnp.float32)],
        compiler_params=pltpu.CompilerParams(
            dimension_semantics=("parallel", "parallel", "arbitrary"), vmem_limit_bytes=VMEM_LIMIT),
        name="attn",
    )(slabs, slabs, slabs, slabs, slabs, bias_tiles)


def _t5_bucket(dist):
    max_exact = N_BUCKETS // 2
    d_f = jnp.maximum(dist, 1).astype(jnp.float32)
    large = max_exact + (jnp.log(d_f / max_exact) / math.log(MAX_DISTANCE / max_exact)
                         * (N_BUCKETS - max_exact)).astype(jnp.int32)
    large = jnp.minimum(large, N_BUCKETS - 1)
    return jnp.where(dist < max_exact, dist, large)


def _bias_tiles(rel_bias):
    dil = jnp.array(DILATIONS, jnp.int32)
    dist = dil[:, None] * jnp.arange(KEYS_PER_QUERY + 1, dtype=jnp.int32)[None, :]
    bucket = _t5_bucket(dist)
    bias = rel_bias.reshape(N_BUCKETS, N_GROUPS, HEADS_PER_GROUP)[bucket, jnp.arange(N_GROUPS)[:, None]]
    bias = jnp.transpose(bias, (0, 2, 1)).astype(jnp.float32).reshape(ATT_HEADS, KEYS_PER_QUERY + 1)
    n = 3 * QB - 1
    neg = lambda w: jnp.full((ATT_HEADS, w), NEG, jnp.float32)
    e = jnp.concatenate([neg(QB - 1), bias[:, ::-1], neg(QB - 1), neg(1)], axis=1)
    e = jnp.roll(e, -(QB - 1), axis=1)
    return jnp.tile(e, (1, QB))[:, :QB * n].reshape(ATT_HEADS, QB, n)[:, :, :2 * QB]


def _interleave(*gens):
    gens = list(gens)
    while gens:
        for gen in list(gens):
            try:
                next(gen)
            except StopIteration:
                gens.remove(gen)


def _rwkv_body(r_ref, k_ref, v_ref, lora_ref, fg_ref, pra_ref, prs_ref, sh_ref, wl_ref, o_ref,
               carry, h_st, tok_s, post_s, up_s, zq_s, kcbt_s, g_s, vb_s, y_s, *, n_t):
    t_blk = o_ref.shape[0]
    cc = RWKV_CHUNK
    cs = range(t_blk // cc)
    i = pl.program_id(0)
    first_tok = (i % n_t) == 0
    first_seq = ((i + n_t - 2) % n_t) == 0

    @pl.when(i == 0)
    def _():
        for ref in (carry, h_st, tok_s, post_s, up_s, zq_s, kcbt_s, g_s, vb_s):
            ref[...] = jnp.zeros(ref.shape, ref.dtype)

    lane_head = lax.broadcasted_iota(jnp.int32, (QW, QW), 1) // HEAD_DIM
    row_head = lax.broadcasted_iota(jnp.int32, (QW, QW), 0) // HEAD_DIM
    same_head = lane_head == row_head
    head_ones = _bf16(jnp.where(same_head, 1.0, 0.0))
    sw = RWKV_QUAD * cc
    chunk_lane_head = lax.broadcasted_iota(jnp.int32, (cc, QW), 1) // HEAD_DIM
    keep = [chunk_lane_head == h for h in range(RWKV_QUAD)]
    chunk_col_head = lax.broadcasted_iota(jnp.int32, (cc, sw), 1) // cc
    keep_sbs = [chunk_col_head == h for h in range(RWKV_QUAD)]

    def head_sum(x):
        return _dot(_bf16(x), head_ones)

    def stack4(x):
        masks = keep_sbs if x.shape[1] == sw else keep
        z = jnp.zeros_like(x)
        return _bf16(jnp.concatenate([jnp.where(masks[h], x, z) for h in range(RWKV_QUAD)], axis=0))

    def each(fn, *lists):
        return [fn(*args) for args in zip(*lists)]

    def split3(x):
        hi = _bf16(x)
        r1 = x - hi.astype(jnp.float32)
        mid = _bf16(r1)
        return hi, mid, _bf16(r1 - mid.astype(jnp.float32))

    def per_token(tok_w):
        pr = pra_ref[0]
        sh = sh_ref[...]

        def shifted(x, idx, mu):
            width = x.shape[1]
            row = lax.broadcasted_iota(jnp.int32, x.shape, 0)
            last = jnp.where(first_tok, jnp.zeros((1, width), jnp.float32), carry[idx, 7:8, :width])
            prev = jnp.where(row == 0, last, pltpu.roll(x, 1, 0))
            carry[idx, :, :width] = x[t_blk - 8:, :]
            return x + (prev - x) * mu

        wide = lambda ref: jnp.concatenate([ref[0], ref[1]], axis=1).astype(jnp.float32)
        r = shifted(wide(r_ref), 0, pr[0:1])
        k = shifted(wide(k_ref), 1, pr[1:2])
        v = shifted(wide(v_ref), 2, pr[2:3])
        tok_s[tok_w, 0] = r
        tok_s[tok_w, 4] = v
        yield
        f_lora = shifted(lora_ref[0], 3, sh[0:1, :LANES])
        f_g = shifted(fg_ref[0], 4, sh[1:2, :LANES])
        w0, a0, k_k, k_a, r_k = (pr[j:j + 1] for j in range(3, 8))
        wd = w0 + _dot(_bf16(jnp.tanh(f_lora)), wl_ref[0, 0])
        tok_s[tok_w, 5] = -math.exp(-0.5) * jax.nn.sigmoid(wd)
        a = jax.nn.sigmoid(a0 + _dot(_bf16(f_lora), wl_ref[0, 1]))
        tok_s[tok_w, 7] = _dot(_bf16(jax.nn.sigmoid(f_g)), wl_ref[0, 2])
        yield
        kk = k * k_k
        kk = kk * lax.rsqrt(jnp.maximum(head_sum(kk * kk), L2_EPS * L2_EPS))
        tok_s[tok_w, 1] = kk
        tok_s[tok_w, 3] = kk * a
        yield
        k2 = k * (1.0 + (a - 1.0) * k_a)
        tok_s[tok_w, 2] = k2
        tok_s[tok_w, 6] = head_sum(r * k2 * r_k) * v
        yield

    def per_chunk(tok_r, chk_w, cs):
        t_i = lax.broadcasted_iota(jnp.int32, (cc, sw), 0)
        s_i = lax.broadcasted_iota(jnp.int32, (cc, sw), 1) % cc
        strict_lower = t_i > s_i
        lower = t_i >= s_i
        eye_f = jnp.where(t_i == s_i, 1.0, 0.0)
        zero_m = jnp.zeros((cc, sw), jnp.float32)
        ci = lax.broadcasted_iota(jnp.int32, (cc, cc), 0)
        cj = lax.broadcasted_iota(jnp.int32, (cc, cc), 1)
        tri = _bf16(jnp.where(ci >= cj, 1.0, 0.0))
        rows_of = lambda j: [tok_s[tok_r, j, c * cc:(c + 1) * cc, :] for c in cs]
        r, kk, k2, bb, v, lw = (rows_of(j) for j in range(6))

        parts = each(split3, lw)
        big_l = each(lambda p: _dot(tri, p[0]) + _dot(tri, p[1]) + _dot(tri, p[2]), parts)
        yield
        l_end = each(lambda l: l[cc - 1:cc], big_l)
        e_l = each(jnp.exp, big_l)
        e_lm = each(lambda l, w: jnp.exp(l - w), big_l, lw)
        e_nl = each(lambda l: jnp.exp(-l), big_l)
        e_c = each(lambda le, l: jnp.exp(le - l), l_end, big_l)
        a_kk_f = each(lambda x, e: x * e, kk, e_lm)
        a_kk = each(_bf16, a_kk_f)
        a_r_f = each(lambda x, e: x * e, r, e_l)
        a_r = each(_bf16, a_r_f)
        kb = each(lambda xk, xb, e: jnp.concatenate([stack4(xk * e), stack4(xb * e)], axis=0),
                  k2, bb, e_nl)
        vb = each(_bf16, v)
        vst = each(stack4, v)
        yield
        s12 = each(lambda ak, ar, kb_: _dot_nt(jnp.concatenate([ak, ar], axis=0), kb_), a_kk, a_r, kb)
        s1 = each(lambda s: s[:cc], s12)
        s2 = each(lambda s: s[cc:], s12)
        yield
        mk = each(lambda s: _bf16(jnp.where(strict_lower, s[:, :sw], zero_m)), s1)
        mb = each(lambda s: jnp.where(strict_lower, s[:, sw:], zero_m), s1)
        nkb = each(lambda s: _bf16(jnp.concatenate([jnp.where(lower, s[:, :sw], zero_m),
                                                     jnp.where(lower, -s[:, sw:], zero_m)], axis=1)), s2)

        x = each(lambda n: eye_f - jnp.where((t_i // 2) == (s_i // 2), n, zero_m), mb)
        blk = 4
        while blk <= cc:
            half = blk // 2
            sel = ((t_i // blk) == (s_i // blk)) & ((t_i % blk) >= half) & ((s_i % blk) < half)
            xb = each(_bf16, x)
            t1 = each(lambda xc, n: _bf16(_dot(xc, stack4(jnp.where(sel, n, zero_m)))), xb, mb)
            yield
            x = each(lambda xf, t: xf - _dot(t, stack4(xf)), x, t1)
            yield
            blk *= 2
        tinv = each(_bf16, x)

        pm = each(lambda t, ak: _dot(t, stack4(ak)), tinv, a_kk_f)
        mkv = each(lambda m_, v_: _dot(m_, v_), mk, vst)
        yield
        q = each(lambda t, m_: _dot(t, stack4(m_)), tinv, mkv)
        yield
        pmb = each(_bf16, pm)
        u = each(lambda af, n, p: af + _dot(n[:, sw:], stack4(p)), a_r_f, nkb, pm)
        z = each(lambda n, v_, q_: _dot(n, jnp.concatenate([v_, stack4(q_)], axis=0)), nkb, vst, q)
        yield
        for j, c in enumerate(cs):
            up_s[chk_w, c] = jnp.concatenate([pmb[j], _bf16(u[j])], axis=0)
            zq_s[chk_w, c, 0] = z[j]
            zq_s[chk_w, c, 1] = q[j]
            vb_s[chk_w, c] = vb[j]
        yield
        pad = [jnp.zeros((LANES - 2 * cc, QW), jnp.float32)] if 2 * cc < LANES else []
        kcg = each(lambda xk, xb, e, le: jnp.concatenate(
            [xk * e, -(xb * e)] + pad + [jnp.broadcast_to(jnp.exp(le), (LANES, QW))], axis=0).T,
            k2, bb, e_c, l_end)
        for j, c in enumerate(cs):
            kcbt_s[chk_w, c] = _bf16(kcg[j][:, :LANES])
            g_s[chk_w, c] = kcg[j][:, LANES:]
        yield

    def per_chunk_groups(tok_r, chk_w):
        for first in range(0, len(cs), RWKV_GROUP):
            yield from per_chunk(tok_r, chk_w, list(cs[first:first + RWKV_GROUP]))
        post_s[chk_w, 0] = tok_s[tok_r, 6]
        post_s[chk_w, 1] = tok_s[tok_r, 7]
        yield

    def sequential(chk_r):
        h = jnp.where(first_seq, jnp.zeros((QW, QW), jnp.float32), h_st[...])
        pad_b = [jnp.zeros((LANES - 2 * cc, QW), jnp.bfloat16)] if 2 * cc < LANES else []
        for c in cs:
            ys = _dot(up_s[chk_r, c], _bf16(h))
            sig = ys[:cc] + zq_s[chk_r, c, 1]
            y_s[c * cc:(c + 1) * cc, :] = ys[cc:] + zq_s[chk_r, c, 0]
            yield
            hu = _dot(kcbt_s[chk_r, c], jnp.concatenate([vb_s[chk_r, c], _bf16(sig)] + pad_b, axis=0))
            g = g_s[chk_r, c]
            h = h * jnp.concatenate([g, g], axis=1) + jnp.where(same_head, hu, jnp.zeros_like(hu))
            yield
        h_st[...] = h
        prs = prs_ref[0]
        y = y_s[...]
        mu = head_sum(y) * (1.0 / HEAD_DIM)
        yc = y - mu
        var = head_sum(yc * yc) * (1.0 / HEAD_DIM)
        yn = yc * lax.rsqrt(var + LN_X_EPS) * prs[8:9] + prs[9:10]
        o_ref[...] = _bf16((yn + post_s[chk_r, 0]) * post_s[chk_r, 1])
        yield

    for parity in (0, 1):
        @pl.when(i % 2 == parity)
        def _(parity=parity):
            _interleave(per_chunk_groups(1 - parity, 1 - parity), sequential(parity), per_token(parity))


def _rwkv(slabs32, slabs16, pr, sh, wl, batch, seq):
    m = slabs32.shape[1]
    t_blk = RWKV_T
    n_t = seq // t_blk
    n_c = t_blk // RWKV_CHUNK
    n_blocks = batch * N_QUADS * n_t
    slabs_per_quad = QW // LANES

    def where(blk):
        blk = jnp.clip(blk, 0, n_blocks - 1)
        seq_id, t = blk // n_t, blk % n_t
        return seq_id % N_QUADS, (seq_id // N_QUADS) * n_t + t

    def tok(slab0, per_quad):
        if per_quad:
            return pl.BlockSpec((slabs_per_quad, t_blk, LANES),
                                lambda i: (slab0 // slabs_per_quad + where(i)[0], where(i)[1], 0))
        return pl.BlockSpec((1, t_blk, LANES), lambda i: (slab0, where(i)[1], 0))

    return pl.pallas_call(
        functools.partial(_rwkv_body, n_t=n_t),
        out_shape=jax.ShapeDtypeStruct((m, RWKV_WIDTH), jnp.bfloat16),
        grid=(n_blocks + 2,),
        in_specs=[tok(R_SLAB0, True), tok(RK_SLAB0, True), tok(RV_SLAB0, True),
                  tok(LORA_SLAB, False), tok(FG_SLAB, False),
                  pl.BlockSpec((1,) + pr.shape[1:], lambda i: (where(i)[0], 0, 0)),
                  pl.BlockSpec((1,) + pr.shape[1:], lambda i: (where(i - 2)[0], 0, 0)),
                  pl.BlockSpec(sh.shape, lambda i: (0, 0)),
                  pl.BlockSpec((1,) + wl.shape[1:], lambda i: (where(i)[0], 0, 0, 0))],
        out_specs=pl.BlockSpec((t_blk, QW), lambda i: (where(i - 2)[1], where(i - 2)[0])),
        scratch_shapes=[
            pltpu.VMEM((5, 8, QW), jnp.float32),
            pltpu.VMEM((QW, QW), jnp.float32),
            pltpu.VMEM((2, 8, t_blk, QW), jnp.float32),
            pltpu.VMEM((2, 2, t_blk, QW), jnp.float32),
            pltpu.VMEM((2, n_c, 2 * RWKV_CHUNK, QW), jnp.bfloat16),
            pltpu.VMEM((2, n_c, 2, RWKV_CHUNK, QW), jnp.float32),
            pltpu.VMEM((2, n_c, QW, LANES), jnp.bfloat16),
            pltpu.VMEM((2, n_c, QW, LANES), jnp.float32),
            pltpu.VMEM((2, n_c, RWKV_CHUNK, QW), jnp.bfloat16),
            pltpu.VMEM((t_blk, QW), jnp.float32),
        ],
        compiler_params=pltpu.CompilerParams(
            dimension_semantics=("arbitrary",), vmem_limit_bytes=VMEM_LIMIT),
        name="rwkv",
    )(slabs16, slabs16, slabs16, slabs32, slabs32, pr, pr, sh, wl)


def _rwkv_params(shift_mu, w0, a0, k_k, k_a, r_k, ln_w, ln_b, w_w2, w_a2, w_g2):
    def quads(vec):
        return vec.reshape(N_QUADS, QW)

    mu_r, mu_k, mu_v = (quads(shift_mu[j * RWKV_WIDTH:(j + 1) * RWKV_WIDTH]) for j in range(3))
    rows = [mu_r, mu_k, mu_v, quads(w0), quads(a0), quads(k_k), quads(k_a), quads(r_k.reshape(-1)),
            quads(ln_w), quads(ln_b)]
    pr = jnp.stack(rows, axis=1)
    pr = jnp.pad(pr, ((0, 0), (0, 16 - pr.shape[1]), (0, 0)))
    mu_rest = shift_mu[3 * RWKV_WIDTH:]
    sh = jnp.pad(mu_rest.reshape(2, LANES), ((0, 6), (0, QW - LANES)))
    zeros = jnp.zeros((DECAY_LORA, RWKV_WIDTH), jnp.float32)
    wd = jnp.concatenate([w_w2, zeros], axis=0)
    wa = jnp.concatenate([zeros, w_a2], axis=0)
    wl = jnp.stack([wd, wa, w_g2], axis=0)
    wl = wl.reshape(3, LANES, N_QUADS, QW).transpose(2, 0, 1, 3)
    return pr, sh, _bf16(wl)


def _rms(x, g):
    ms = jnp.mean(x * x, axis=-1, keepdims=True)
    return x * lax.rsqrt(ms + NORM_EPS) * g


def _merge_body(x_ref, att_ref, rw_ref, gate_ref, bg_ref, wa_ref, wr_ref, wo_ref, gn_ref, o_ref):
    n_gs = D_MODEL // LANES
    f0 = jnp.concatenate([gate_ref[s] for s in range(n_gs)], axis=1).astype(jnp.float32)
    f1 = jnp.concatenate([gate_ref[n_gs + s] for s in range(n_gs)], axis=1).astype(jnp.float32)
    g0 = jax.nn.sigmoid(f0 + bg_ref[:, :D_MODEL])
    g1 = jax.nn.sigmoid(f1 + bg_ref[:, D_MODEL:])
    merged = g0 * _dot(att_ref[...], wa_ref[...]) + g1 * _dot(rw_ref[...], wr_ref[...])
    z = _dot(_bf16(merged), wo_ref[...])
    o_ref[...] = x_ref[...] + _rms(z, gn_ref[...])


def _merge(x2, o_att, o_rwkv, slabs, b_gate, wa, wr, wo, g_post, tm):
    m = x2.shape[0]
    n_gate_slabs = GATE_COLS // LANES
    const = lambda shape: pl.BlockSpec(shape, lambda i: (0, 0))
    return pl.pallas_call(
        _merge_body,
        out_shape=jax.ShapeDtypeStruct((m, D_MODEL), jnp.float32),
        grid=(m // tm,),
        in_specs=[
            pl.BlockSpec((tm, D_MODEL), lambda i: (i, 0)),
            pl.BlockSpec((tm, ATT_OUT_WIDTH), lambda i: (i, 0)),
            pl.BlockSpec((tm, RWKV_WIDTH), lambda i: (i, 0)),
            pl.BlockSpec((n_gate_slabs, tm, LANES), lambda i: (GATE_SLAB0 // n_gate_slabs, i, 0)),
            const((1, GATE_COLS)), const(wa.shape), const(wr.shape), const(wo.shape), const((1, D_MODEL)),
        ],
        out_specs=pl.BlockSpec((tm, D_MODEL), lambda i: (i, 0)),
        compiler_params=pltpu.CompilerParams(
            dimension_semantics=("parallel",), vmem_limit_bytes=VMEM_LIMIT),
        name="merge",
    )(x2, o_att, o_rwkv, slabs, b_gate, wa, wr, wo, g_post)


def _ffn_body(x_ref, gpre_ref, w1_ref, w2_ref, gpost_ref, o_ref, *, tf):
    x = x_ref[...]
    h = _bf16(_rms(x, gpre_ref[...]))
    acc = jnp.zeros(x.shape, jnp.float32)
    for c in range(D_FF // tf):
        u = jnp.maximum(_dot(h, w1_ref[:, c * tf:(c + 1) * tf]), 0.0)
        acc = acc + _dot(_bf16(u * u), w2_ref[c * tf:(c + 1) * tf, :])
    o_ref[...] = x + _rms(acc, gpost_ref[...])


def _ffn(x2, g_pre, w1, w2, g_post, tm, tf=1024):
    m = x2.shape[0]
    const = lambda shape: pl.BlockSpec(shape, lambda i: (0, 0), pipeline_mode=pl.Buffered(1))
    return pl.pallas_call(
        functools.partial(_ffn_body, tf=tf),
        out_shape=jax.ShapeDtypeStruct((m, D_MODEL), jnp.float32),
        grid=(m // tm,),
        in_specs=[pl.BlockSpec((tm, D_MODEL), lambda i: (i, 0)), const((1, D_MODEL)),
                  const(w1.shape), const(w2.shape), const((1, D_MODEL))],
        out_specs=pl.BlockSpec((tm, D_MODEL), lambda i: (i, 0)),
        compiler_params=pltpu.CompilerParams(
            dimension_semantics=("parallel",), vmem_limit_bytes=VMEM_LIMIT),
        name="ffn",
    )(x2, g_pre, w1, w2, g_post)


def kernel(x, rel_bias, norm_mix_pre, norm_mix_post, norm_ffn_pre, norm_ffn_post, w_in, b_gate, shift_mu, w0, w_w2, a0, w_a2, w_g2, k_k, k_a, r_k, ln_x_w, ln_x_b, w_att_branch, w_rwkv_branch, w_out, w_ffn1, w_ffn2):
    batch, seq, d_model = x.shape
    assert d_model == D_MODEL and seq % ATT_CHUNK == 0 and seq % RWKV_T == 0
    m = batch * seq
    tm = ROW_TILE
    assert m % PROJ_TM == 0 and m % tm == 0 and m % FFN_TM == 0
    bias_tiles = _bias_tiles(rel_bias)
    row = lambda vec: vec.reshape(1, -1)
    x2 = x.reshape(m, D_MODEL)
    for l in range(w_in.shape[0]):
        slabs32, slabs16 = _proj(x2, row(norm_mix_pre[l]), w_in[l], PROJ_TM)
        o_att = _attn(slabs32, bias_tiles, batch, seq)
        pr, sh, wl = _rwkv_params(shift_mu[l], w0[l], a0[l], k_k[l], k_a[l], r_k[l], ln_x_w[l], ln_x_b[l],
                                  w_w2[l], w_a2[l], w_g2[l])
        o_rwkv = _rwkv(slabs32, slabs16, pr, sh, wl, batch, seq)
        x2 = _merge(x2, o_att, o_rwkv, slabs16, row(b_gate[l]), _bf16(w_att_branch[l]), _bf16(w_rwkv_branch[l]),
                    _bf16(w_out[l]), row(norm_mix_post[l]), tm)
        x2 = _ffn(x2, row(norm_ffn_pre[l]), _bf16(w_ffn1[l]), _bf16(w_ffn2[l]), row(norm_ffn_post[l]), FFN_TM)
    return x2.reshape(batch, seq, D_MODEL)
```

```python
import functools
import math

import jax
import jax.numpy as jnp
from jax import lax
from jax.experimental import pallas as pl
from jax.experimental.pallas import tpu as pltpu

D_MODEL = 1024
HEAD_DIM = 64
DILATIONS = (1, 4, 16)
KEYS_PER_QUERY = 128
N_GROUPS = len(DILATIONS)
HEADS_PER_GROUP = 4
ATT_HEADS = N_GROUPS * HEADS_PER_GROUP
ATT_WIDTH = ATT_HEADS * HEAD_DIM
ATT_OUT_WIDTH = HEADS_PER_GROUP * HEAD_DIM
N_BUCKETS = 32
MAX_DISTANCE = KEYS_PER_QUERY * DILATIONS[-1]
RWKV_WIDTH = D_MODEL
DECAY_LORA = 64
ICLR_LORA = 64
GATE_LORA = 128
RWKV_COLS = 3 * RWKV_WIDTH + DECAY_LORA + ICLR_LORA + GATE_LORA
N_BRANCHES = 2
IN_COLS = 3 * ATT_WIDTH + RWKV_COLS + N_BRANCHES * D_MODEL
D_FF = 4 * D_MODEL
NORM_EPS = 1e-6
LN_X_EPS = 64e-5
L2_EPS = 1e-12

LANES = 128
VMEM_LIMIT = 56 * 1024 * 1024
NEG = -1e30

GATE_COLS = N_BRANCHES * D_MODEL
Q_SLAB0 = 0
K_SLAB0 = Q_SLAB0 + ATT_WIDTH // LANES
V_SLAB0 = K_SLAB0 + ATT_WIDTH // LANES
LORA_SLAB = V_SLAB0 + ATT_WIDTH // LANES
FG_SLAB = LORA_SLAB + 1
N_SLABS_F32 = FG_SLAB + 1
GATE_SLAB0 = 0
R_SLAB0 = GATE_COLS // LANES
RK_SLAB0 = R_SLAB0 + RWKV_WIDTH // LANES
RV_SLAB0 = RK_SLAB0 + RWKV_WIDTH // LANES
N_SLABS_BF16 = RV_SLAB0 + RWKV_WIDTH // LANES
RWKV_QUAD = 4
QW = RWKV_QUAD * HEAD_DIM
N_QUADS = RWKV_WIDTH // QW

PROJ_TN = 512
PROJ_TM = 2048
ROW_TILE = 512
FFN_TM = 1024
ATT_CHUNK = KEYS_PER_QUERY * DILATIONS[-1]
QB = KEYS_PER_QUERY
ATT_UNROLL = 3
RWKV_CHUNK = 64
RWKV_T = 512
RWKV_GROUP = 8


def _bf16(x):
    return x.astype(jnp.bfloat16)


def _dot(a, b):
    return jnp.dot(a, b, preferred_element_type=jnp.float32)


def _dot_nt(a, b):
    return lax.dot_general(a, b, (((1,), (1,)), ((), ())), preferred_element_type=jnp.float32)


def _proj_body(x_ref, g_ref, w_ref, o32_ref, o16_ref, h_scr):
    @pl.when(pl.program_id(1) == 0)
    def _():
        x = x_ref[...]
        ms = jnp.mean(x * x, axis=-1, keepdims=True)
        h_scr[...] = _bf16(x * lax.rsqrt(ms + NORM_EPS) * g_ref[...])

    acc = _dot(h_scr[...], w_ref[...])
    for s in range(PROJ_TN // LANES):
        o32_ref[s] = acc[:, s * LANES:(s + 1) * LANES]
        o16_ref[s] = _bf16(acc[:, s * LANES:(s + 1) * LANES])


def _proj(x2, g, w_in, tm):
    m = x2.shape[0]
    n_col_blocks = IN_COLS // PROJ_TN
    slabs_per_block = PROJ_TN // LANES
    n_f32_blocks = N_SLABS_F32 // slabs_per_block
    att_end = 3 * ATT_WIDTH
    rkv_end = att_end + 3 * RWKV_WIDTH
    lora_end = att_end + RWKV_COLS
    w_cols = _bf16(jnp.concatenate([w_in[:, :att_end], w_in[:, rkv_end:lora_end], w_in[:, lora_end:],
                                    w_in[:, att_end:rkv_end]], axis=1))
    blk = (slabs_per_block, tm, LANES)
    return pl.pallas_call(
        _proj_body,
        out_shape=(jax.ShapeDtypeStruct((N_SLABS_F32 + slabs_per_block, m, LANES), jnp.float32),
                   jax.ShapeDtypeStruct((N_SLABS_BF16, m, LANES), jnp.bfloat16)),
        grid=(m // tm, n_col_blocks),
        in_specs=[
            pl.BlockSpec((tm, D_MODEL), lambda i, j: (i, 0)),
            pl.BlockSpec((1, D_MODEL), lambda i, j: (0, 0)),
            pl.BlockSpec((D_MODEL, PROJ_TN), lambda i, j: (0, j)),
        ],
        out_specs=(pl.BlockSpec(blk, lambda i, j: (jnp.minimum(j, n_f32_blocks), i, 0)),
                   pl.BlockSpec(blk, lambda i, j: (jnp.maximum(j - n_f32_blocks, 0), i, 0))),
        scratch_shapes=[pltpu.VMEM((tm, D_MODEL), jnp.bfloat16)],
        compiler_params=pltpu.CompilerParams(
            dimension_semantics=("parallel", "arbitrary"), vmem_limit_bytes=VMEM_LIMIT),
        name="proj",
    )(x2, g, w_cols)


def _attn_units(qs, ks, vs, biases, lo):
    scale = HEAD_DIM ** -0.5

    def logits(q, k):
        qs_ = q * scale
        zero = jnp.zeros_like(qs_)
        lhs = _bf16(jnp.concatenate([jnp.where(lo, qs_, zero), jnp.where(lo, zero, qs_)], axis=0))
        return _dot_nt(lhs, _bf16(k))

    s = [logits(q, k) for q, k in zip(qs, ks)]
    s0 = [x[:QB] + b[0] for x, b in zip(s, biases)]
    s1 = [x[QB:] + b[1] for x, b in zip(s, biases)]
    m0 = [jnp.max(x, axis=-1, keepdims=True) for x in s0]
    m1 = [jnp.max(x, axis=-1, keepdims=True) for x in s1]
    p0 = [_bf16(jnp.exp(x - m)) for x, m in zip(s0, m0)]
    p1 = [_bf16(jnp.exp(x - m)) for x, m in zip(s1, m1)]

    def weighted(p0_, p1_, v):
        zv = jnp.zeros_like(v)
        ones = jnp.ones_like(v)
        rhs0 = _bf16(jnp.concatenate([jnp.where(lo, v, zv), jnp.where(lo, ones, zv)], axis=1))
        rhs1 = _bf16(jnp.concatenate([jnp.where(lo, zv, v), jnp.where(lo, zv, ones)], axis=1))
        return _dot(p0_, rhs0) + _dot(p1_, rhs1)

    ol = [weighted(a, b, v) for a, b, v in zip(p0, p1, vs)]
    out = []
    for x, a, b in zip(ol, m0, m1):
        l = x[:, LANES:]
        out.append((x[:, :LANES] / l, jnp.where(lo, a, b) + jnp.log(l)))
    return out


def _attn_body(q_ref, kc_ref, kp_ref, vc_ref, vp_ref, bias_ref, o_ref, o_scr, l_scr):
    c = pl.program_id(1)
    g = pl.program_id(2)
    lo = lax.broadcasted_iota(jnp.int32, (1, LANES), 1) < HEAD_DIM
    before_start = (c == 0) & (lax.broadcasted_iota(jnp.int32, (QB, 2 * QB), 1) < QB)

    def rows(start, n, d):
        return pl.ds(start, n) if d == 1 else pl.ds(start, n, stride=d)

    def group(gi):
        d = DILATIONS[gi]
        n_qb = ATT_CHUNK // (QB * d)

        def run(units):
            qs = [q_ref[p, rows(start_q, QB, d), :] for p, start_q, _, _, _ in units]

            def bias(head, is_first):
                tile = bias_ref[head]
                return jnp.where(before_start, NEG, tile) if is_first else tile

            biases = [(bias(gi * HEADS_PER_GROUP + 2 * p, is_first), bias(gi * HEADS_PER_GROUP + 2 * p + 1, is_first))
                      for p, _, _, _, is_first in units]
            res = _attn_units(qs, [u[2] for u in units], [u[3] for u in units], biases, lo)
            for (p, start_q, _, _, _), (o, lse) in zip(units, res):
                o_scr[gi, p, rows(start_q, QB, d), :] = o
                l_scr[gi, p, rows(start_q, QB, d), :] = lse

        n_first = max(j for j in range(1, ATT_UNROLL + 1) if d % j == 0)

        def first_blocks(it, carry):
            units = []
            for j in range(n_first):
                r = it * n_first + j
                prev_start = r + QB * d * (n_qb - 1)
                for p in range(2):
                    k = jnp.concatenate([kp_ref[p, rows(prev_start, QB, d), :], kc_ref[p, rows(r, QB, d), :]], axis=0)
                    v = jnp.concatenate([vp_ref[p, rows(prev_start, QB, d), :], vc_ref[p, rows(r, QB, d), :]], axis=0)
                    units.append((p, r, k, v, True))
            run(units)
            return carry

        lax.fori_loop(0, d // n_first, first_blocks, 0)

        n_later = d * (n_qb - 1)
        if n_later:
            n_par = max(j for j in range(1, ATT_UNROLL + 1) if n_later % j == 0)

            def later_blocks(it, carry):
                units = []
                for j in range(n_par):
                    u = it * n_par + j
                    start_q = u % d + QB * d * (1 + u // d)
                    for p in range(2):
                        k = kc_ref[p, rows(start_q - QB * d, 2 * QB, d), :]
                        v = vc_ref[p, rows(start_q - QB * d, 2 * QB, d), :]
                        units.append((p, start_q, k, v, False))
                run(units)
                return carry

            lax.fori_loop(0, n_later // n_par, later_blocks, 0)

    for gi in range(N_GROUPS):
        pl.when(g == gi)(functools.partial(group, gi))

    @pl.when(g == N_GROUPS - 1)
    def _():
        tile = 256

        def comb(i, carry):
            rs = pl.ds(pl.multiple_of(i * tile, tile), tile)
            for p in range(2):
                ls = [l_scr[gi, p, rs, :] for gi in range(N_GROUPS)]
                mx = jnp.maximum(jnp.maximum(ls[0], ls[1]), ls[2])
                ws = [jnp.exp(l - mx) for l in ls]
                num = ws[0] * o_scr[0, p, rs, :] + ws[1] * o_scr[1, p, rs, :] + ws[2] * o_scr[2, p, rs, :]
                o_ref[rs, p * LANES:(p + 1) * LANES] = _bf16(num / (ws[0] + ws[1] + ws[2]))
            return carry

        lax.fori_loop(0, ATT_CHUNK // tile, comb, 0)


def _attn(slabs, bias_tiles, batch, seq):
    m = slabs.shape[1]
    n_chunks = seq // ATT_CHUNK
    blk = (2, ATT_CHUNK, LANES)

    def cur(slab0):
        return pl.BlockSpec(blk, lambda b, c, g: (slab0 // 2 + g, b * n_chunks + c, 0))

    def prev(slab0):
        return pl.BlockSpec(blk, lambda b, c, g: (slab0 // 2 + g, b * n_chunks + jnp.maximum(c - 1, 0), 0))

    return pl.pallas_call(
        _attn_body,
        out_shape=jax.ShapeDtypeStruct((m, ATT_OUT_WIDTH), jnp.bfloat16),
        grid=(batch, n_chunks, N_GROUPS),
        in_specs=[cur(Q_SLAB0), cur(K_SLAB0), prev(K_SLAB0), cur(V_SLAB0), prev(V_SLAB0),
                  pl.BlockSpec(bias_tiles.shape, lambda b, c, g: (0, 0, 0))],
        out_specs=pl.BlockSpec((ATT_CHUNK, ATT_OUT_WIDTH), lambda b, c, g: (b * n_chunks + c, 0)),
        scratch_shapes=[pltpu.VMEM((N_GROUPS, 2, ATT_CHUNK, LANES), jnp.float32),
                        pltpu.VMEM((N_GROUPS, 2, ATT_CHUNK, LANES), jnp.float32)],
        compiler_params=pltpu.CompilerParams(
            dimension_semantics=("parallel", "parallel", "arbitrary"), vmem_limit_bytes=VMEM_LIMIT),
        name="attn",
    )(slabs, slabs, slabs, slabs, slabs, bias_tiles)


def _t5_bucket(dist):
    max_exact = N_BUCKETS // 2
    d_f = jnp.maximum(dist, 1).astype(jnp.float32)
    large = max_exact + (jnp.log(d_f / max_exact) / math.log(MAX_DISTANCE / max_exact)
                         * (N_BUCKETS - max_exact)).astype(jnp.int32)
    large = jnp.minimum(large, N_BUCKETS - 1)
    return jnp.where(dist < max_exact, dist, large)


def _bias_tiles(rel_bias):
    dil = jnp.array(DILATIONS, jnp.int32)
    dist = dil[:, None] * jnp.arange(KEYS_PER_QUERY + 1, dtype=jnp.int32)[None, :]
    bucket = _t5_bucket(dist)
    bias = rel_bias.reshape(N_BUCKETS, N_GROUPS, HEADS_PER_GROUP)[bucket, jnp.arange(N_GROUPS)[:, None]]
    bias = jnp.transpose(bias, (0, 2, 1)).astype(jnp.float32).reshape(ATT_HEADS, KEYS_PER_QUERY + 1)
    n = 3 * QB - 1
    neg = lambda w: jnp.full((ATT_HEADS, w), NEG, jnp.float32)
    e = jnp.concatenate([neg(QB - 1), bias[:, ::-1], neg(QB - 1), neg(1)], axis=1)
    e = jnp.roll(e, -(QB - 1), axis=1)
    return jnp.tile(e, (1, QB))[:, :QB * n].reshape(ATT_HEADS, QB, n)[:, :, :2 * QB]


def _interleave(*gens):
    gens = list(gens)
    while gens:
        for gen in list(gens):
            try:
                next(gen)
            except StopIteration:
                gens.remove(gen)


def _rwkv_body(r_ref, k_ref, v_ref, lora_ref, fg_ref, pra_ref, prs_ref, sh_ref, wl_ref, o_ref,
               carry, h_st, tok_s, post_s, up_s, zq_s, kcbt_s, g_s, vb_s, y_s, *, n_t):
    t_blk = o_ref.shape[0]
    cc = RWKV_CHUNK
    cs = range(t_blk // cc)
    i = pl.program_id(0)
    first_tok = (i % n_t) == 0
    first_seq = ((i + n_t - 2) % n_t) == 0

    @pl.when(i == 0)
    def _():
        for ref in (carry, h_st, tok_s, post_s, up_s, zq_s, kcbt_s, g_s, vb_s):
            ref[...] = jnp.zeros(ref.shape, ref.dtype)

    lane_head = lax.broadcasted_iota(jnp.int32, (QW, QW), 1) // HEAD_DIM
    row_head = lax.broadcasted_iota(jnp.int32, (QW, QW), 0) // HEAD_DIM
    same_head = lane_head == row_head
    head_ones = _bf16(jnp.where(same_head, 1.0, 0.0))
    sw = RWKV_QUAD * cc
    chunk_lane_head = lax.broadcasted_iota(jnp.int32, (cc, QW), 1) // HEAD_DIM
    keep = [chunk_lane_head == h for h in range(RWKV_QUAD)]
    chunk_col_head = lax.broadcasted_iota(jnp.int32, (cc, sw), 1) // cc
    keep_sbs = [chunk_col_head == h for h in range(RWKV_QUAD)]

    def head_sum(x):
        return _dot(_bf16(x), head_ones)

    def stack4(x):
        masks = keep_sbs if x.shape[1] == sw else keep
        z = jnp.zeros_like(x)
        return _bf16(jnp.concatenate([jnp.where(masks[h], x, z) for h in range(RWKV_QUAD)], axis=0))

    def each(fn, *lists):
        return [fn(*args) for args in zip(*lists)]

    def split3(x):
        hi = _bf16(x)
        r1 = x - hi.astype(jnp.float32)
        mid = _bf16(r1)
        return hi, mid, _bf16(r1 - mid.astype(jnp.float32))

    def per_token(tok_w):
        pr = pra_ref[0]
        sh = sh_ref[...]

        def shifted(x, idx, mu):
            width = x.shape[1]
            row = lax.broadcasted_iota(jnp.int32, x.shape, 0)
            last = jnp.where(first_tok, jnp.zeros((1, width), jnp.float32), carry[idx, 7:8, :width])
            prev = jnp.where(row == 0, last, pltpu.roll(x, 1, 0))
            carry[idx, :, :width] = x[t_blk - 8:, :]
            return x + (prev - x) * mu

        wide = lambda ref: jnp.concatenate([ref[0], ref[1]], axis=1).astype(jnp.float32)
        r = shifted(wide(r_ref), 0, pr[0:1])
        k = shifted(wide(k_ref), 1, pr[1:2])
        v = shifted(wide(v_ref), 2, pr[2:3])
        tok_s[tok_w, 0] = r
        tok_s[tok_w, 4] = v
        yield
        f_lora = shifted(lora_ref[0], 3, sh[0:1, :LANES])
        f_g = shifted(fg_ref[0], 4, sh[1:2, :LANES])
        w0, a0, k_k, k_a, r_k = (pr[j:j + 1] for j in range(3, 8))
        wd = w0 + _dot(_bf16(jnp.tanh(f_lora)), wl_ref[0, 0])
        tok_s[tok_w, 5] = -math.exp(-0.5) * jax.nn.sigmoid(wd)
        a = jax.nn.sigmoid(a0 + _dot(_bf16(f_lora), wl_ref[0, 1]))
        tok_s[tok_w, 7] = _dot(_bf16(jax.nn.sigmoid(f_g)), wl_ref[0, 2])
        yield
        kk = k * k_k
        kk = kk * lax.rsqrt(jnp.maximum(head_sum(kk * kk), L2_EPS * L2_EPS))
        tok_s[tok_w, 1] = kk
        tok_s[tok_w, 3] = kk * a
        yield
        k2 = k * (1.0 + (a - 1.0) * k_a)
        tok_s[tok_w, 2] = k2
        tok_s[tok_w, 6] = head_sum(r * k2 * r_k) * v
        yield

    def per_chunk(tok_r, chk_w, cs):
        t_i = lax.broadcasted_iota(jnp.int32, (cc, sw), 0)
        s_i = lax.broadcasted_iota(jnp.int32, (cc, sw), 1) % cc
        strict_lower = t_i > s_i
        lower = t_i >= s_i
        eye_f = jnp.where(t_i == s_i, 1.0, 0.0)
        zero_m = jnp.zeros((cc, sw), jnp.float32)
        ci = lax.broadcasted_iota(jnp.int32, (cc, cc), 0)
        cj = lax.broadcasted_iota(jnp.int32, (cc, cc), 1)
        tri = _bf16(jnp.where(ci >= cj, 1.0, 0.0))
        rows_of = lambda j: [tok_s[tok_r, j, c * cc:(c + 1) * cc, :] for c in cs]
        r, kk, k2, bb, v, lw = (rows_of(j) for j in range(6))

        parts = each(split3, lw)
        big_l = each(lambda p: _dot(tri, p[0]) + _dot(tri, p[1]) + _dot(tri, p[2]), parts)
        yield
        l_end = each(lambda l: l[cc - 1:cc], big_l)
        e_l = each(jnp.exp, big_l)
        e_lm = each(lambda l, w: jnp.exp(l - w), big_l, lw)
        e_nl = each(lambda l: jnp.exp(-l), big_l)
        e_c = each(lambda le, l: jnp.exp(le - l), l_end, big_l)
        a_kk_f = each(lambda x, e: x * e, kk, e_lm)
        a_kk = each(_bf16, a_kk_f)
        a_r_f = each(lambda x, e: x * e, r, e_l)
        a_r = each(_bf16, a_r_f)
        kb = each(lambda xk, xb, e: jnp.concatenate([stack4(xk * e), stack4(xb * e)], axis=0),
                  k2, bb, e_nl)
        vb = each(_bf16, v)
        vst = each(stack4, v)
        yield
        s12 = each(lambda ak, ar, kb_: _dot_nt(jnp.concatenate([ak, ar], axis=0), kb_), a_kk, a_r, kb)
        s1 = each(lambda s: s[:cc], s12)
        s2 = each(lambda s: s[cc:], s12)
        yield
        mk = each(lambda s: _bf16(jnp.where(strict_lower, s[:, :sw], zero_m)), s1)
        mb = each(lambda s: jnp.where(strict_lower, s[:, sw:], zero_m), s1)
        nkb = each(lambda s: _bf16(jnp.concatenate([jnp.where(lower, s[:, :sw], zero_m),
                                                     jnp.where(lower, -s[:, sw:], zero_m)], axis=1)), s2)

        x = each(lambda n: eye_f - jnp.where((t_i // 2) == (s_i // 2), n, zero_m), mb)
        blk = 4
        while blk <= cc:
            half = blk // 2
            sel = ((t_i // blk) == (s_i // blk)) & ((t_i % blk) >= half) & ((s_i % blk) < half)
            xb = each(_bf16, x)
            t1 = each(lambda xc, n: _bf16(_dot(xc, stack4(jnp.where(sel, n, zero_m)))), xb, mb)
            yield
            x = each(lambda xf, t: xf - _dot(t, stack4(xf)), x, t1)
            yield
            blk *= 2
        tinv = each(_bf16, x)

        pm = each(lambda t, ak: _dot(t, stack4(ak)), tinv, a_kk_f)
        mkv = each(lambda m_, v_: _dot(m_, v_), mk, vst)
        yield
        q = each(lambda t, m_: _dot(t, stack4(m_)), tinv, mkv)
        yield
        pmb = each(_bf16, pm)
        u = each(lambda af, n, p: af + _dot(n[:, sw:], stack4(p)), a_r_f, nkb, pm)
        z = each(lambda n, v_, q_: _dot(n, jnp.concatenate([v_, stack4(q_)], axis=0)), nkb, vst, q)
        yield
        for j, c in enumerate(cs):
            up_s[chk_w, c] = jnp.concatenate([pmb[j], _bf16(u[j])], axis=0)
            zq_s[chk_w, c, 0] = z[j]
            zq_s[chk_w, c, 1] = q[j]
            vb_s[chk_w, c] = vb[j]
        yield
        pad = [jnp.zeros((LANES - 2 * cc, QW), jnp.float32)] if 2 * cc < LANES else []
        kcg = each(lambda xk, xb, e, le: jnp.concatenate(
            [xk * e, -(xb * e)] + pad + [jnp.broadcast_to(jnp.exp(le), (LANES, QW))], axis=0).T,
            k2, bb, e_c, l_end)
        for j, c in enumerate(cs):
            kcbt_s[chk_w, c] = _bf16(kcg[j][:, :LANES])
            g_s[chk_w, c] = kcg[j][:, LANES:]
        yield

    def per_chunk_groups(tok_r, chk_w):
        for first in range(0, len(cs), RWKV_GROUP):
            yield from per_chunk(tok_r, chk_w, list(cs[first:first + RWKV_GROUP]))
        post_s[chk_w, 0] = tok_s[tok_r, 6]
        post_s[chk_w, 1] = tok_s[tok_r, 7]
        yield

    def sequential(chk_r):
        h = jnp.where(first_seq, jnp.zeros((QW, QW), jnp.float32), h_st[...])
        pad_b = [jnp.zeros((LANES - 2 * cc, QW), jnp.bfloat16)] if 2 * cc < LANES else []
        for c in cs:
            ys = _dot(up_s[chk_r, c], _bf16(h))
            sig = ys[:cc] + zq_s[chk_r, c, 1]
            y_s[c * cc:(c + 1) * cc, :] = ys[cc:] + zq_s[chk_r, c, 0]
            yield
            hu = _dot(kcbt_s[chk_r, c], jnp.concatenate([vb_s[chk_r, c], _bf16(sig)] + pad_b, axis=0))
            g = g_s[chk_r, c]
            h = h * jnp.concatenate([g, g], axis=1) + jnp.where(same_head, hu, jnp.zeros_like(hu))
            yield
        h_st[...] = h
        prs = prs_ref[0]
        y = y_s[...]
        mu = head_sum(y) * (1.0 / HEAD_DIM)
        yc = y - mu
        var = head_sum(yc * yc) * (1.0 / HEAD_DIM)
        yn = yc * lax.rsqrt(var + LN_X_EPS) * prs[8:9] + prs[9:10]
        o_ref[...] = _bf16((yn + post_s[chk_r, 0]) * post_s[chk_r, 1])
        yield

    for parity in (0, 1):
        @pl.when(i % 2 == parity)
        def _(parity=parity):
            _interleave(per_chunk_groups(1 - parity, 1 - parity), sequential(parity), per_token(parity))


def _rwkv(slabs32, slabs16, pr, sh, wl, batch, seq):
    m = slabs32.shape[1]
    t_blk = RWKV_T
    n_t = seq // t_blk
    n_c = t_blk // RWKV_CHUNK
    n_blocks = batch * N_QUADS * n_t
    slabs_per_quad = QW // LANES

    def where(blk):
        blk = jnp.clip(blk, 0, n_blocks - 1)
        seq_id, t = blk // n_t, blk % n_t
        return seq_id % N_QUADS, (seq_id // N_QUADS) * n_t + t

    def tok(slab0, per_quad):
        if per_quad:
            return pl.BlockSpec((slabs_per_quad, t_blk, LANES),
                                lambda i: (slab0 // slabs_per_quad + where(i)[0], where(i)[1], 0))
        return pl.BlockSpec((1, t_blk, LANES), lambda i: (slab0, where(i)[1], 0))

    return pl.pallas_call(
        functools.partial(_rwkv_body, n_t=n_t),
        out_shape=jax.ShapeDtypeStruct((m, RWKV_WIDTH), jnp.bfloat16),
        grid=(n_blocks + 2,),
        in_specs=[tok(R_SLAB0, True), tok(RK_SLAB0, True), tok(RV_SLAB0, True),
                  tok(LORA_SLAB, False), tok(FG_SLAB, False),
                  pl.BlockSpec((1,) + pr.shape[1:], lambda i: (where(i)[0], 0, 0)),
                  pl.BlockSpec((1,) + pr.shape[1:], lambda i: (where(i - 2)[0], 0, 0)),
                  pl.BlockSpec(sh.shape, lambda i: (0, 0)),
                  pl.BlockSpec((1,) + wl.shape[1:], lambda i: (where(i)[0], 0, 0, 0))],
        out_specs=pl.BlockSpec((t_blk, QW), lambda i: (where(i - 2)[1], where(i - 2)[0])),
        scratch_shapes=[
            pltpu.VMEM((5, 8, QW), jnp.float32),
            pltpu.VMEM((QW, QW), jnp.float32),
            pltpu.VMEM((2, 8, t_blk, QW), jnp.float32),
            pltpu.VMEM((2, 2, t_blk, QW), jnp.float32),
            pltpu.VMEM((2, n_c, 2 * RWKV_CHUNK, QW), jnp.bfloat16),
            pltpu.VMEM((2, n_c, 2, RWKV_CHUNK, QW), jnp.float32),
            pltpu.VMEM((2, n_c, QW, LANES), jnp.bfloat16),
            pltpu.VMEM((2, n_c, QW, LANES), jnp.float32),
            pltpu.VMEM((2, n_c, RWKV_CHUNK, QW), jnp.bfloat16),
            pltpu.VMEM((t_blk, QW), jnp.float32),
        ],
        compiler_params=pltpu.CompilerParams(
            dimension_semantics=("arbitrary",), vmem_limit_bytes=VMEM_LIMIT),
        name="rwkv",
    )(slabs16, slabs16, slabs16, slabs32, slabs32, pr, pr, sh, wl)


def _rwkv_params(shift_mu, w0, a0, k_k, k_a, r_k, ln_w, ln_b, w_w2, w_a2, w_g2):
    def quads(vec):
        return vec.reshape(N_QUADS, QW)

    mu_r, mu_k, mu_v = (quads(shift_mu[j * RWKV_WIDTH:(j + 1) * RWKV_WIDTH]) for j in range(3))
    rows = [mu_r, mu_k, mu_v, quads(w0), quads(a0), quads(k_k), quads(k_a), quads(r_k.reshape(-1)),
            quads(ln_w), quads(ln_b)]
    pr = jnp.stack(rows, axis=1)
    pr = jnp.pad(pr, ((0, 0), (0, 16 - pr.shape[1]), (0, 0)))
    mu_rest = shift_mu[3 * RWKV_WIDTH:]
    sh = jnp.pad(mu_rest.reshape(2, LANES), ((0, 6), (0, QW - LANES)))
    zeros = jnp.zeros((DECAY_LORA, RWKV_WIDTH), jnp.float32)
    wd = jnp.concatenate([w_w2, zeros], axis=0)
    wa = jnp.concatenate([zeros, w_a2], axis=0)
    wl = jnp.stack([wd, wa, w_g2], axis=0)
    wl = wl.reshape(3, LANES, N_QUADS, QW).transpose(2, 0, 1, 3)
    return pr, sh, _bf16(wl)


def _rms(x, g):
    ms = jnp.mean(x * x, axis=-1, keepdims=True)
    return x * lax.rsqrt(ms + NORM_EPS) * g


def _merge_body(x_ref, att_ref, rw_ref, gate_ref, bg_ref, wa_ref, wr_ref, wo_ref, gn_ref, o_ref):
    n_gs = D_MODEL // LANES
    f0 = jnp.concatenate([gate_ref[s] for s in range(n_gs)], axis=1).astype(jnp.float32)
    f1 = jnp.concatenate([gate_ref[n_gs + s] for s in range(n_gs)], axis=1).astype(jnp.float32)
    g0 = jax.nn.sigmoid(f0 + bg_ref[:, :D_MODEL])
    g1 = jax.nn.sigmoid(f1 + bg_ref[:, D_MODEL:])
    merged = g0 * _dot(att_ref[...], wa_ref[...]) + g1 * _dot(rw_ref[...], wr_ref[...])
    z = _dot(_bf16(merged), wo_ref[...])
    o_ref[...] = x_ref[...] + _rms(z, gn_ref[...])


def _merge(x2, o_att, o_rwkv, slabs, b_gate, wa, wr, wo, g_post, tm):
    m = x2.shape[0]
    n_gate_slabs = GATE_COLS // LANES
    const = lambda shape: pl.BlockSpec(shape, lambda i: (0, 0))
    return pl.pallas_call(
        _merge_body,
        out_shape=jax.ShapeDtypeStruct((m, D_MODEL), jnp.float32),
        grid=(m // tm,),
        in_specs=[
            pl.BlockSpec((tm, D_MODEL), lambda i: (i, 0)),
            pl.BlockSpec((tm, ATT_OUT_WIDTH), lambda i: (i, 0)),
            pl.BlockSpec((tm, RWKV_WIDTH), lambda i: (i, 0)),
            pl.BlockSpec((n_gate_slabs, tm, LANES), lambda i: (GATE_SLAB0 // n_gate_slabs, i, 0)),
            const((1, GATE_COLS)), const(wa.shape), const(wr.shape), const(wo.shape), const((1, D_MODEL)),
        ],
        out_specs=pl.BlockSpec((tm, D_MODEL), lambda i: (i, 0)),
        compiler_params=pltpu.CompilerParams(
            dimension_semantics=("parallel",), vmem_limit_bytes=VMEM_LIMIT),
        name="merge",
    )(x2, o_att, o_rwkv, slabs, b_gate, wa, wr, wo, g_post)


def _ffn_body(x_ref, gpre_ref, w1_ref, w2_ref, gpost_ref, o_ref, *, tf):
    x = x_ref[...]
    h = _bf16(_rms(x, gpre_ref[...]))
    acc = jnp.zeros(x.shape, jnp.float32)
    for c in range(D_FF // tf):
        u = jnp.maximum(_dot(h, w1_ref[:, c * tf:(c + 1) * tf]), 0.0)
        acc = acc + _dot(_bf16(u * u), w2_ref[c * tf:(c + 1) * tf, :])
    o_ref[...] = x + _rms(acc, gpost_ref[...])


def _ffn(x2, g_pre, w1, w2, g_post, tm, tf=1024):
    m = x2.shape[0]
    const = lambda shape: pl.BlockSpec(shape, lambda i: (0, 0), pipeline_mode=pl.Buffered(1))
    return pl.pallas_call(
        functools.partial(_ffn_body, tf=tf),
        out_shape=jax.ShapeDtypeStruct((m, D_MODEL), jnp.float32),
        grid=(m // tm,),
        in_specs=[pl.BlockSpec((tm, D_MODEL), lambda i: (i, 0)), const((1, D_MODEL)),
                  const(w1.shape), const(w2.shape), const((1, D_MODEL))],
        out_specs=pl.BlockSpec((tm, D_MODEL), lambda i: (i, 0)),
        compiler_params=pltpu.CompilerParams(
            dimension_semantics=("parallel",), vmem_limit_bytes=VMEM_LIMIT),
        name="ffn",
    )(x2, g_pre, w1, w2, g_post)


def kernel(x, rel_bias, norm_mix_pre, norm_mix_post, norm_ffn_pre, norm_ffn_post, w_in, b_gate, shift_mu, w0, w_w2, a0, w_a2, w_g2, k_k, k_a, r_k, ln_x_w, ln_x_b, w_att_branch, w_rwkv_branch, w_out, w_ffn1, w_ffn2):
    batch, seq, d_model = x.shape
    assert d_model == D_MODEL and seq % ATT_CHUNK == 0 and seq % RWKV_T == 0
    m = batch * seq
    tm = ROW_TILE
    assert m % PROJ_TM == 0 and m % tm == 0 and m % FFN_TM == 0
    bias_tiles = _bias_tiles(rel_bias)
    row = lambda vec: vec.reshape(1, -1)
    x2 = x.reshape(m, D_MODEL)
    for l in range(w_in.shape[0]):
        slabs32, slabs16 = _proj(x2, row(norm_mix_pre[l]), w_in[l], PROJ_TM)
        o_att = _attn(slabs32, bias_tiles, batch, seq)
        pr, sh, wl = _rwkv_params(shift_mu[l], w0[l], a0[l], k_k[l], k_a[l], r_k[l], ln_x_w[l], ln_x_b[l],
                                  w_w2[l], w_a2[l], w_g2[l])
        o_rwkv = _rwkv(slabs32, slabs16, pr, sh, wl, batch, seq)
        x2 = _merge(x2, o_att, o_rwkv, slabs16, row(b_gate[l]), _bf16(w_att_branch[l]), _bf16(w_rwkv_branch[l]),
                    _bf16(w_out[l]), row(norm_mix_post[l]), tm)
        x2 = _ffn(x2, row(norm_ffn_pre[l]), _bf16(w_ffn1[l]), _bf16(w_ffn2[l]), row(norm_ffn_post[l]), FFN_TM)
    return x2.reshape(batch, seq, D_MODEL)
```

```python
import functools
import math

import jax
import jax.numpy as jnp
from jax import lax
from jax.experimental import pallas as pl
from jax.experimental.pallas import tpu as pltpu

D_MODEL = 1024
HEAD_DIM = 64
DILATIONS = (1, 4, 16)
KEYS_PER_QUERY = 128
N_GROUPS = len(DILATIONS)
HEADS_PER_GROUP = 4
ATT_HEADS = N_GROUPS * HEADS_PER_GROUP
ATT_WIDTH = ATT_HEADS * HEAD_DIM
ATT_OUT_WIDTH = HEADS_PER_GROUP * HEAD_DIM
N_BUCKETS = 32
MAX_DISTANCE = KEYS_PER_QUERY * DILATIONS[-1]
RWKV_WIDTH = D_MODEL
DECAY_LORA = 64
ICLR_LORA = 64
GATE_LORA = 128
RWKV_COLS = 3 * RWKV_WIDTH + DECAY_LORA + ICLR_LORA + GATE_LORA
N_BRANCHES = 2
IN_COLS = 3 * ATT_WIDTH + RWKV_COLS + N_BRANCHES * D_MODEL
D_FF = 4 * D_MODEL
NORM_EPS = 1e-6
LN_X_EPS = 64e-5
L2_EPS = 1e-12

LANES = 128
VMEM_LIMIT = 56 * 1024 * 1024
NEG = -1e30

GATE_COLS = N_BRANCHES * D_MODEL
Q_SLAB0 = 0
K_SLAB0 = Q_SLAB0 + ATT_WIDTH // LANES
V_SLAB0 = K_SLAB0 + ATT_WIDTH // LANES
LORA_SLAB = V_SLAB0 + ATT_WIDTH // LANES
FG_SLAB = LORA_SLAB + 1
N_SLABS_F32 = FG_SLAB + 1
GATE_SLAB0 = 0
R_SLAB0 = GATE_COLS // LANES
RK_SLAB0 = R_SLAB0 + RWKV_WIDTH // LANES
RV_SLAB0 = RK_SLAB0 + RWKV_WIDTH // LANES
N_SLABS_BF16 = RV_SLAB0 + RWKV_WIDTH // LANES
RWKV_QUAD = 4
QW = RWKV_QUAD * HEAD_DIM
N_QUADS = RWKV_WIDTH // QW

PROJ_TN = 512
PROJ_TM = 2048
ROW_TILE = 512
ATT_CHUNK = KEYS_PER_QUERY * DILATIONS[-1]
QB = KEYS_PER_QUERY
ATT_UNROLL = 3
RWKV_CHUNK = 64
RWKV_T = 512
RWKV_GROUP = 8


def _bf16(x):
    return x.astype(jnp.bfloat16)


def _dot(a, b):
    return jnp.dot(a, b, preferred_element_type=jnp.float32)


def _dot_nt(a, b):
    return lax.dot_general(a, b, (((1,), (1,)), ((), ())), preferred_element_type=jnp.float32)


def _proj_body(x_ref, g_ref, w_ref, o32_ref, o16_ref, h_scr):
    @pl.when(pl.program_id(1) == 0)
    def _():
        x = x_ref[...]
        ms = jnp.mean(x * x, axis=-1, keepdims=True)
        h_scr[...] = _bf16(x * lax.rsqrt(ms + NORM_EPS) * g_ref[...])

    acc = _dot(h_scr[...], w_ref[...])
    for s in range(PROJ_TN // LANES):
        o32_ref[s] = acc[:, s * LANES:(s + 1) * LANES]
        o16_ref[s] = _bf16(acc[:, s * LANES:(s + 1) * LANES])


def _proj(x2, g, w_in, tm):
    m = x2.shape[0]
    n_col_blocks = IN_COLS // PROJ_TN
    slabs_per_block = PROJ_TN // LANES
    n_f32_blocks = N_SLABS_F32 // slabs_per_block
    att_end = 3 * ATT_WIDTH
    rkv_end = att_end + 3 * RWKV_WIDTH
    lora_end = att_end + RWKV_COLS
    w_cols = _bf16(jnp.concatenate([w_in[:, :att_end], w_in[:, rkv_end:lora_end], w_in[:, lora_end:],
                                    w_in[:, att_end:rkv_end]], axis=1))
    blk = (slabs_per_block, tm, LANES)
    return pl.pallas_call(
        _proj_body,
        out_shape=(jax.ShapeDtypeStruct((N_SLABS_F32 + slabs_per_block, m, LANES), jnp.float32),
                   jax.ShapeDtypeStruct((N_SLABS_BF16, m, LANES), jnp.bfloat16)),
        grid=(m // tm, n_col_blocks),
        in_specs=[
            pl.BlockSpec((tm, D_MODEL), lambda i, j: (i, 0)),
            pl.BlockSpec((1, D_MODEL), lambda i, j: (0, 0)),
            pl.BlockSpec((D_MODEL, PROJ_TN), lambda i, j: (0, j)),
        ],
        out_specs=(pl.BlockSpec(blk, lambda i, j: (jnp.minimum(j, n_f32_blocks), i, 0)),
                   pl.BlockSpec(blk, lambda i, j: (jnp.maximum(j - n_f32_blocks, 0), i, 0))),
        scratch_shapes=[pltpu.VMEM((tm, D_MODEL), jnp.bfloat16)],
        compiler_params=pltpu.CompilerParams(
            dimension_semantics=("parallel", "arbitrary"), vmem_limit_bytes=VMEM_LIMIT),
        name="proj",
    )(x2, g, w_cols)


def _attn_units(qs, ks, vs, biases, lo):
    scale = HEAD_DIM ** -0.5

    def logits(q, k):
        qs_ = q * scale
        zero = jnp.zeros_like(qs_)
        lhs = _bf16(jnp.concatenate([jnp.where(lo, qs_, zero), jnp.where(lo, zero, qs_)], axis=0))
        return _dot_nt(lhs, _bf16(k))

    s = [logits(q, k) for q, k in zip(qs, ks)]
    s0 = [x[:QB] + b[0] for x, b in zip(s, biases)]
    s1 = [x[QB:] + b[1] for x, b in zip(s, biases)]
    m0 = [jnp.max(x, axis=-1, keepdims=True) for x in s0]
    m1 = [jnp.max(x, axis=-1, keepdims=True) for x in s1]
    p0 = [_bf16(jnp.exp(x - m)) for x, m in zip(s0, m0)]
    p1 = [_bf16(jnp.exp(x - m)) for x, m in zip(s1, m1)]

    def weighted(p0_, p1_, v):
        zv = jnp.zeros_like(v)
        ones = jnp.ones_like(v)
        rhs0 = _bf16(jnp.concatenate([jnp.where(lo, v, zv), jnp.where(lo, ones, zv)], axis=1))
        rhs1 = _bf16(jnp.concatenate([jnp.where(lo, zv, v), jnp.where(lo, zv, ones)], axis=1))
        return _dot(p0_, rhs0) + _dot(p1_, rhs1)

    ol = [weighted(a, b, v) for a, b, v in zip(p0, p1, vs)]
    out = []
    for x, a, b in zip(ol, m0, m1):
        l = x[:, LANES:]
        out.append((x[:, :LANES] / l, jnp.where(lo, a, b) + jnp.log(l)))
    return out


def _attn_body(q_ref, kc_ref, kp_ref, vc_ref, vp_ref, bias_ref, o_ref, o_scr, l_scr):
    c = pl.program_id(1)
    g = pl.program_id(2)
    lo = lax.broadcasted_iota(jnp.int32, (1, LANES), 1) < HEAD_DIM
    before_start = (c == 0) & (lax.broadcasted_iota(jnp.int32, (QB, 2 * QB), 1) < QB)

    def rows(start, n, d):
        return pl.ds(start, n) if d == 1 else pl.ds(start, n, stride=d)

    def group(gi):
        d = DILATIONS[gi]
        n_qb = ATT_CHUNK // (QB * d)

        def run(units):
            qs = [q_ref[p, rows(start_q, QB, d), :] for p, start_q, _, _, _ in units]

            def bias(head, is_first):
                tile = bias_ref[head]
                return jnp.where(before_start, NEG, tile) if is_first else tile

            biases = [(bias(gi * HEADS_PER_GROUP + 2 * p, is_first), bias(gi * HEADS_PER_GROUP + 2 * p + 1, is_first))
                      for p, _, _, _, is_first in units]
            res = _attn_units(qs, [u[2] for u in units], [u[3] for u in units], biases, lo)
            for (p, start_q, _, _, _), (o, lse) in zip(units, res):
                o_scr[gi, p, rows(start_q, QB, d), :] = o
                l_scr[gi, p, rows(start_q, QB, d), :] = lse

        n_first = max(j for j in range(1, ATT_UNROLL + 1) if d % j == 0)

        def first_blocks(it, carry):
            units = []
            for j in range(n_first):
                r = it * n_first + j
                prev_start = r + QB * d * (n_qb - 1)
                for p in range(2):
                    k = jnp.concatenate([kp_ref[p, rows(prev_start, QB, d), :], kc_ref[p, rows(r, QB, d), :]], axis=0)
                    v = jnp.concatenate([vp_ref[p, rows(prev_start, QB, d), :], vc_ref[p, rows(r, QB, d), :]], axis=0)
                    units.append((p, r, k, v, True))
            run(units)
            return carry

        lax.fori_loop(0, d // n_first, first_blocks, 0)

        n_later = d * (n_qb - 1)
        if n_later:
            n_par = max(j for j in range(1, ATT_UNROLL + 1) if n_later % j == 0)

            def later_blocks(it, carry):
                units = []
                for j in range(n_par):
                    u = it * n_par + j
                    start_q = u % d + QB * d * (1 + u // d)
                    for p in range(2):
                        k = kc_ref[p, rows(start_q - QB * d, 2 * QB, d), :]
                        v = vc_ref[p, rows(start_q - QB * d, 2 * QB, d), :]
                        units.append((p, start_q, k, v, False))
                run(units)
                return carry

            lax.fori_loop(0, n_later // n_par, later_blocks, 0)

    for gi in range(N_GROUPS):
        pl.when(g == gi)(functools.partial(group, gi))

    @pl.when(g == N_GROUPS - 1)
    def _():
        tile = 256

        def comb(i, carry):
            rs = pl.ds(pl.multiple_of(i * tile, tile), tile)
            for p in range(2):
                ls = [l_scr[gi, p, rs, :] for gi in range(N_GROUPS)]
                mx = jnp.maximum(jnp.maximum(ls[0], ls[1]), ls[2])
                ws = [jnp.exp(l - mx) for l in ls]
                num = ws[0] * o_scr[0, p, rs, :] + ws[1] * o_scr[1, p, rs, :] + ws[2] * o_scr[2, p, rs, :]
                o_ref[rs, p * LANES:(p + 1) * LANES] = _bf16(num / (ws[0] + ws[1] + ws[2]))
            return carry

        lax.fori_loop(0, ATT_CHUNK // tile, comb, 0)


def _attn(slabs, bias_tiles, batch, seq):
    m = slabs.shape[1]
    n_chunks = seq // ATT_CHUNK
    blk = (2, ATT_CHUNK, LANES)

    def cur(slab0):
        return pl.BlockSpec(blk, lambda b, c, g: (slab0 // 2 + g, b * n_chunks + c, 0))

    def prev(slab0):
        return pl.BlockSpec(blk, lambda b, c, g: (slab0 // 2 + g, b * n_chunks + jnp.maximum(c - 1, 0), 0))

    return pl.pallas_call(
        _attn_body,
        out_shape=jax.ShapeDtypeStruct((m, ATT_OUT_WIDTH), jnp.bfloat16),
        grid=(batch, n_chunks, N_GROUPS),
        in_specs=[cur(Q_SLAB0), cur(K_SLAB0), prev(K_SLAB0), cur(V_SLAB0), prev(V_SLAB0),
                  pl.BlockSpec(bias_tiles.shape, lambda b, c, g: (0, 0, 0))],
        out_specs=pl.BlockSpec((ATT_CHUNK, ATT_OUT_WIDTH), lambda b, c, g: (b * n_chunks + c, 0)),
        scratch_shapes=[pltpu.VMEM((N_GROUPS, 2, ATT_CHUNK, LANES), jnp.float32),
                        pltpu.VMEM((N_GROUPS, 2, ATT_CHUNK, LANES), jnp.float32)],
        compiler_params=pltpu.CompilerParams(
            dimension_semantics=("parallel", "parallel", "arbitrary"), vmem_limit_bytes=VMEM_LIMIT),
        name="attn",
    )(slabs, slabs, slabs, slabs, slabs, bias_tiles)


def _t5_bucket(dist):
    max_exact = N_BUCKETS // 2
    d_f = jnp.maximum(dist, 1).astype(jnp.float32)
    large = max_exact + (jnp.log(d_f / max_exact) / math.log(MAX_DISTANCE / max_exact)
                         * (N_BUCKETS - max_exact)).astype(jnp.int32)
    large = jnp.minimum(large, N_BUCKETS - 1)
    return jnp.where(dist < max_exact, dist, large)


def _bias_tiles(rel_bias):
    dil = jnp.array(DILATIONS, jnp.int32)
    dist = dil[:, None] * jnp.arange(KEYS_PER_QUERY + 1, dtype=jnp.int32)[None, :]
    bucket = _t5_bucket(dist)
    bias = rel_bias.reshape(N_BUCKETS, N_GROUPS, HEADS_PER_GROUP)[bucket, jnp.arange(N_GROUPS)[:, None]]
    bias = jnp.transpose(bias, (0, 2, 1)).astype(jnp.float32).reshape(ATT_HEADS, KEYS_PER_QUERY + 1)
    n = 3 * QB - 1
    neg = lambda w: jnp.full((ATT_HEADS, w), NEG, jnp.float32)
    e = jnp.concatenate([neg(QB - 1), bias[:, ::-1], neg(QB - 1), neg(1)], axis=1)
    e = jnp.roll(e, -(QB - 1), axis=1)
    return jnp.tile(e, (1, QB))[:, :QB * n].reshape(ATT_HEADS, QB, n)[:, :, :2 * QB]


def _interleave(*gens):
    gens = list(gens)
    while gens:
        for gen in list(gens):
            try:
                next(gen)
            except StopIteration:
                gens.remove(gen)


def _rwkv_body(r_ref, k_ref, v_ref, lora_ref, fg_ref, pra_ref, prs_ref, sh_ref, wl_ref, o_ref,
               carry, h_st, tok_s, post_s, up_s, zq_s, kcbt_s, g_s, vb_s, y_s, *, n_t):
    t_blk = o_ref.shape[0]
    cc = RWKV_CHUNK
    cs = range(t_blk // cc)
    i = pl.program_id(0)
    first_tok = (i % n_t) == 0
    first_seq = ((i + n_t - 2) % n_t) == 0

    @pl.when(i == 0)
    def _():
        for ref in (carry, h_st, tok_s, post_s, up_s, zq_s, kcbt_s, g_s, vb_s):
            ref[...] = jnp.zeros(ref.shape, ref.dtype)

    lane_head = lax.broadcasted_iota(jnp.int32, (QW, QW), 1) // HEAD_DIM
    row_head = lax.broadcasted_iota(jnp.int32, (QW, QW), 0) // HEAD_DIM
    same_head = lane_head == row_head
    head_ones = _bf16(jnp.where(same_head, 1.0, 0.0))
    sw = RWKV_QUAD * cc
    chunk_lane_head = lax.broadcasted_iota(jnp.int32, (cc, QW), 1) // HEAD_DIM
    keep = [chunk_lane_head == h for h in range(RWKV_QUAD)]
    chunk_col_head = lax.broadcasted_iota(jnp.int32, (cc, sw), 1) // cc
    keep_sbs = [chunk_col_head == h for h in range(RWKV_QUAD)]

    def head_sum(x):
        return _dot(_bf16(x), head_ones)

    def stack4(x):
        masks = keep_sbs if x.shape[1] == sw else keep
        z = jnp.zeros_like(x)
        return _bf16(jnp.concatenate([jnp.where(masks[h], x, z) for h in range(RWKV_QUAD)], axis=0))

    def each(fn, *lists):
        return [fn(*args) for args in zip(*lists)]

    def split3(x):
        hi = _bf16(x)
        r1 = x - hi.astype(jnp.float32)
        mid = _bf16(r1)
        return hi, mid, _bf16(r1 - mid.astype(jnp.float32))

    def per_token(tok_w):
        pr = pra_ref[0]
        sh = sh_ref[...]

        def shifted(x, idx, mu):
            width = x.shape[1]
            row = lax.broadcasted_iota(jnp.int32, x.shape, 0)
            last = jnp.where(first_tok, jnp.zeros((1, width), jnp.float32), carry[idx, 7:8, :width])
            prev = jnp.where(row == 0, last, pltpu.roll(x, 1, 0))
            carry[idx, :, :width] = x[t_blk - 8:, :]
            return x + (prev - x) * mu

        wide = lambda ref: jnp.concatenate([ref[0], ref[1]], axis=1).astype(jnp.float32)
        r = shifted(wide(r_ref), 0, pr[0:1])
        k = shifted(wide(k_ref), 1, pr[1:2])
        v = shifted(wide(v_ref), 2, pr[2:3])
        tok_s[tok_w, 0] = r
        tok_s[tok_w, 4] = v
        yield
        f_lora = shifted(lora_ref[0], 3, sh[0:1, :LANES])
        f_g = shifted(fg_ref[0], 4, sh[1:2, :LANES])
        w0, a0, k_k, k_a, r_k = (pr[j:j + 1] for j in range(3, 8))
        wd = w0 + _dot(_bf16(jnp.tanh(f_lora)), wl_ref[0, 0])
        tok_s[tok_w, 5] = -math.exp(-0.5) * jax.nn.sigmoid(wd)
        a = jax.nn.sigmoid(a0 + _dot(_bf16(f_lora), wl_ref[0, 1]))
        tok_s[tok_w, 7] = _dot(_bf16(jax.nn.sigmoid(f_g)), wl_ref[0, 2])
        yield
        kk = k * k_k
        kk = kk * lax.rsqrt(jnp.maximum(head_sum(kk * kk), L2_EPS * L2_EPS))
        tok_s[tok_w, 1] = kk
        tok_s[tok_w, 3] = kk * a
        yield
        k2 = k * (1.0 + (a - 1.0) * k_a)
        tok_s[tok_w, 2] = k2
        tok_s[tok_w, 6] = head_sum(r * k2 * r_k) * v
        yield

    def per_chunk(tok_r, chk_w, cs):
        t_i = lax.broadcasted_iota(jnp.int32, (cc, sw), 0)
        s_i = lax.broadcasted_iota(jnp.int32, (cc, sw), 1) % cc
        strict_lower = t_i > s_i
        lower = t_i >= s_i
        eye_f = jnp.where(t_i == s_i, 1.0, 0.0)
        zero_m = jnp.zeros((cc, sw), jnp.float32)
        ci = lax.broadcasted_iota(jnp.int32, (cc, cc), 0)
        cj = lax.broadcasted_iota(jnp.int32, (cc, cc), 1)
        tri = _bf16(jnp.where(ci >= cj, 1.0, 0.0))
        rows_of = lambda j: [tok_s[tok_r, j, c * cc:(c + 1) * cc, :] for c in cs]
        r, kk, k2, bb, v, lw = (rows_of(j) for j in range(6))

        parts = each(split3, lw)
        big_l = each(lambda p: _dot(tri, p[0]) + _dot(tri, p[1]) + _dot(tri, p[2]), parts)
        yield
        l_end = each(lambda l: l[cc - 1:cc], big_l)
        e_l = each(jnp.exp, big_l)
        e_lm = each(lambda l, w: jnp.exp(l - w), big_l, lw)
        e_nl = each(lambda l: jnp.exp(-l), big_l)
        e_c = each(lambda le, l: jnp.exp(le - l), l_end, big_l)
        a_kk_f = each(lambda x, e: x * e, kk, e_lm)
        a_kk = each(_bf16, a_kk_f)
        a_r_f = each(lambda x, e: x * e, r, e_l)
        a_r = each(_bf16, a_r_f)
        kb = each(lambda xk, xb, e: jnp.concatenate([stack4(xk * e), stack4(xb * e)], axis=0),
                  k2, bb, e_nl)
        vb = each(_bf16, v)
        vst = each(stack4, v)
        yield
        s12 = each(lambda ak, ar, kb_: _dot_nt(jnp.concatenate([ak, ar], axis=0), kb_), a_kk, a_r, kb)
        s1 = each(lambda s: s[:cc], s12)
        s2 = each(lambda s: s[cc:], s12)
        yield
        mk = each(lambda s: _bf16(jnp.where(strict_lower, s[:, :sw], zero_m)), s1)
        mb = each(lambda s: jnp.where(strict_lower, s[:, sw:], zero_m), s1)
        nkb = each(lambda s: _bf16(jnp.concatenate([jnp.where(lower, s[:, :sw], zero_m),
                                                     jnp.where(lower, -s[:, sw:], zero_m)], axis=1)), s2)

        x = each(lambda n: eye_f - jnp.where((t_i // 2) == (s_i // 2), n, zero_m), mb)
        blk = 4
        while blk <= cc:
            half = blk // 2
            sel = ((t_i // blk) == (s_i // blk)) & ((t_i % blk) >= half) & ((s_i % blk) < half)
            xb = each(_bf16, x)
            t1 = each(lambda xc, n: _bf16(_dot(xc, stack4(jnp.where(sel, n, zero_m)))), xb, mb)
            yield
            x = each(lambda xf, t: xf - _dot(t, stack4(xf)), x, t1)
            yield
            blk *= 2
        tinv = each(_bf16, x)

        pm = each(lambda t, ak: _dot(t, stack4(ak)), tinv, a_kk_f)
        mkv = each(lambda m_, v_: _dot(m_, v_), mk, vst)
        yield
        q = each(lambda t, m_: _dot(t, stack4(m_)), tinv, mkv)
        yield
        pmb = each(_bf16, pm)
        u = each(lambda af, n, p: af + _dot(n[:, sw:], stack4(p)), a_r_f, nkb, pm)
        z = each(lambda n, v_, q_: _dot(n, jnp.concatenate([v_, stack4(q_)], axis=0)), nkb, vst, q)
        yield
        for j, c in enumerate(cs):
            up_s[chk_w, c] = jnp.concatenate([pmb[j], _bf16(u[j])], axis=0)
            zq_s[chk_w, c, 0] = z[j]
            zq_s[chk_w, c, 1] = q[j]
            vb_s[chk_w, c] = vb[j]
        yield
        pad = [jnp.zeros((LANES - 2 * cc, QW), jnp.float32)] if 2 * cc < LANES else []
        kcg = each(lambda xk, xb, e, le: jnp.concatenate(
            [xk * e, -(xb * e)] + pad + [jnp.broadcast_to(jnp.exp(le), (LANES, QW))], axis=0).T,
            k2, bb, e_c, l_end)
        for j, c in enumerate(cs):
            kcbt_s[chk_w, c] = _bf16(kcg[j][:, :LANES])
            g_s[chk_w, c] = kcg[j][:, LANES:]
        yield

    def per_chunk_groups(tok_r, chk_w):
        for first in range(0, len(cs), RWKV_GROUP):
            yield from per_chunk(tok_r, chk_w, list(cs[first:first + RWKV_GROUP]))
        post_s[chk_w, 0] = tok_s[tok_r, 6]
        post_s[chk_w, 1] = tok_s[tok_r, 7]
        yield

    def sequential(chk_r):
        h = jnp.where(first_seq, jnp.zeros((QW, QW), jnp.float32), h_st[...])
        pad_b = [jnp.zeros((LANES - 2 * cc, QW), jnp.bfloat16)] if 2 * cc < LANES else []
        for c in cs:
            ys = _dot(up_s[chk_r, c], _bf16(h))
            sig = ys[:cc] + zq_s[chk_r, c, 1]
            y_s[c * cc:(c + 1) * cc, :] = ys[cc:] + zq_s[chk_r, c, 0]
            yield
            hu = _dot(kcbt_s[chk_r, c], jnp.concatenate([vb_s[chk_r, c], _bf16(sig)] + pad_b, axis=0))
            g = g_s[chk_r, c]
            h = h * jnp.concatenate([g, g], axis=1) + jnp.where(same_head, hu, jnp.zeros_like(hu))
            yield
        h_st[...] = h
        prs = prs_ref[0]
        y = y_s[...]
        mu = head_sum(y) * (1.0 / HEAD_DIM)
        yc = y - mu
        var = head_sum(yc * yc) * (1.0 / HEAD_DIM)
        yn = yc * lax.rsqrt(var + LN_X_EPS) * prs[8:9] + prs[9:10]
        o_ref[...] = _bf16((yn + post_s[chk_r, 0]) * post_s[chk_r, 1])
        yield

    for parity in (0, 1):
        @pl.when(i % 2 == parity)
        def _(parity=parity):
            _interleave(per_chunk_groups(1 - parity, 1 - parity), sequential(parity), per_token(parity))


def _rwkv(slabs32, slabs16, pr, sh, wl, batch, seq):
    m = slabs32.shape[1]
    t_blk = RWKV_T
    n_t = seq // t_blk
    n_c = t_blk // RWKV_CHUNK
    n_blocks = batch * N_QUADS * n_t
    slabs_per_quad = QW // LANES

    def where(blk):
        blk = jnp.clip(blk, 0, n_blocks - 1)
        seq_id, t = blk // n_t, blk % n_t
        return seq_id % N_QUADS, (seq_id // N_QUADS) * n_t + t

    def tok(slab0, per_quad):
        if per_quad:
            return pl.BlockSpec((slabs_per_quad, t_blk, LANES),
                                lambda i: (slab0 // slabs_per_quad + where(i)[0], where(i)[1], 0))
        return pl.BlockSpec((1, t_blk, LANES), lambda i: (slab0, where(i)[1], 0))

    return pl.pallas_call(
        functools.partial(_rwkv_body, n_t=n_t),
        out_shape=jax.ShapeDtypeStruct((m, RWKV_WIDTH), jnp.bfloat16),
        grid=(n_blocks + 2,),
        in_specs=[tok(R_SLAB0, True), tok(RK_SLAB0, True), tok(RV_SLAB0, True),
                  tok(LORA_SLAB, False), tok(FG_SLAB, False),
                  pl.BlockSpec((1,) + pr.shape[1:], lambda i: (where(i)[0], 0, 0)),
                  pl.BlockSpec((1,) + pr.shape[1:], lambda i: (where(i - 2)[0], 0, 0)),
                  pl.BlockSpec(sh.shape, lambda i: (0, 0)),
                  pl.BlockSpec((1,) + wl.shape[1:], lambda i: (where(i)[0], 0, 0, 0))],
        out_specs=pl.BlockSpec((t_blk, QW), lambda i: (where(i - 2)[1], where(i - 2)[0])),
        scratch_shapes=[
            pltpu.VMEM((5, 8, QW), jnp.float32),
            pltpu.VMEM((QW, QW), jnp.float32),
            pltpu.VMEM((2, 8, t_blk, QW), jnp.float32),
            pltpu.VMEM((2, 2, t_blk, QW), jnp.float32),
            pltpu.VMEM((2, n_c, 2 * RWKV_CHUNK, QW), jnp.bfloat16),
            pltpu.VMEM((2, n_c, 2, RWKV_CHUNK, QW), jnp.float32),
            pltpu.VMEM((2, n_c, QW, LANES), jnp.bfloat16),
            pltpu.VMEM((2, n_c, QW, LANES), jnp.float32),
            pltpu.VMEM((2, n_c, RWKV_CHUNK, QW), jnp.bfloat16),
            pltpu.VMEM((t_blk, QW), jnp.float32),
        ],
        compiler_params=pltpu.CompilerParams(
            dimension_semantics=("arbitrary",), vmem_limit_bytes=VMEM_LIMIT),
        name="rwkv",
    )(slabs16, slabs16, slabs16, slabs32, slabs32, pr, pr, sh, wl)


def _rwkv_params(shift_mu, w0, a0, k_k, k_a, r_k, ln_w, ln_b, w_w2, w_a2, w_g2):
    def quads(vec):
        return vec.reshape(N_QUADS, QW)

    mu_r, mu_k, mu_v = (quads(shift_mu[j * RWKV_WIDTH:(j + 1) * RWKV_WIDTH]) for j in range(3))
    rows = [mu_r, mu_k, mu_v, quads(w0), quads(a0), quads(k_k), quads(k_a), quads(r_k.reshape(-1)),
            quads(ln_w), quads(ln_b)]
    pr = jnp.stack(rows, axis=1)
    pr = jnp.pad(pr, ((0, 0), (0, 16 - pr.shape[1]), (0, 0)))
    mu_rest = shift_mu[3 * RWKV_WIDTH:]
    sh = jnp.pad(mu_rest.reshape(2, LANES), ((0, 6), (0, QW - LANES)))
    zeros = jnp.zeros((DECAY_LORA, RWKV_WIDTH), jnp.float32)
    wd = jnp.concatenate([w_w2, zeros], axis=0)
    wa = jnp.concatenate([zeros, w_a2], axis=0)
    wl = jnp.stack([wd, wa, w_g2], axis=0)
    wl = wl.reshape(3, LANES, N_QUADS, QW).transpose(2, 0, 1, 3)
    return pr, sh, _bf16(wl)


def _rms(x, g):
    ms = jnp.mean(x * x, axis=-1, keepdims=True)
    return x * lax.rsqrt(ms + NORM_EPS) * g


def _tail_body(x_ref, att_ref, rw_ref, gate_ref, bg_ref, wa_ref, wr_ref, wo_ref, gmix_ref,
               gpre_ref, w1_ref, w2_ref, gpost_ref, o_ref, *, tf):
    n_gs = D_MODEL // LANES
    f0 = jnp.concatenate([gate_ref[s] for s in range(n_gs)], axis=1).astype(jnp.float32)
    f1 = jnp.concatenate([gate_ref[n_gs + s] for s in range(n_gs)], axis=1).astype(jnp.float32)
    g0 = jax.nn.sigmoid(f0 + bg_ref[:, :D_MODEL])
    g1 = jax.nn.sigmoid(f1 + bg_ref[:, D_MODEL:])
    merged = g0 * _dot(att_ref[...], wa_ref[...]) + g1 * _dot(rw_ref[...], wr_ref[...])
    z = _dot(_bf16(merged), wo_ref[...])
    x = x_ref[...] + _rms(z, gmix_ref[...])
    h = _bf16(_rms(x, gpre_ref[...]))
    acc = jnp.zeros(x.shape, jnp.float32)
    for c in range(D_FF // tf):
        u = jnp.maximum(_dot(h, w1_ref[:, c * tf:(c + 1) * tf]), 0.0)
        acc = acc + _dot(_bf16(u * u), w2_ref[c * tf:(c + 1) * tf, :])
    o_ref[...] = x + _rms(acc, gpost_ref[...])


def _tail(x2, o_att, o_rwkv, slabs, b_gate, wa, wr, wo, g_mix, g_pre, w1, w2, g_post, tm, tf=1024):
    m = x2.shape[0]
    n_gate_slabs = GATE_COLS // LANES
    const = lambda shape: pl.BlockSpec(shape, lambda i: (0, 0), pipeline_mode=pl.Buffered(1))
    return pl.pallas_call(
        functools.partial(_tail_body, tf=tf),
        out_shape=jax.ShapeDtypeStruct((m, D_MODEL), jnp.float32),
        grid=(m // tm,),
        in_specs=[
            pl.BlockSpec((tm, D_MODEL), lambda i: (i, 0)),
            pl.BlockSpec((tm, ATT_OUT_WIDTH), lambda i: (i, 0)),
            pl.BlockSpec((tm, RWKV_WIDTH), lambda i: (i, 0)),
            pl.BlockSpec((n_gate_slabs, tm, LANES), lambda i: (GATE_SLAB0 // n_gate_slabs, i, 0)),
            const((1, GATE_COLS)), const(wa.shape), const(wr.shape), const(wo.shape), const((1, D_MODEL)),
            const((1, D_MODEL)), const(w1.shape), const(w2.shape), const((1, D_MODEL)),
        ],
        out_specs=pl.BlockSpec((tm, D_MODEL), lambda i: (i, 0)),
        compiler_params=pltpu.CompilerParams(
            dimension_semantics=("parallel",), vmem_limit_bytes=VMEM_LIMIT),
        name="tail",
    )(x2, o_att, o_rwkv, slabs, b_gate, wa, wr, wo, g_mix, g_pre, w1, w2, g_post)


def kernel(x, rel_bias, norm_mix_pre, norm_mix_post, norm_ffn_pre, norm_ffn_post, w_in, b_gate, shift_mu, w0, w_w2, a0, w_a2, w_g2, k_k, k_a, r_k, ln_x_w, ln_x_b, w_att_branch, w_rwkv_branch, w_out, w_ffn1, w_ffn2):
    batch, seq, d_model = x.shape
    assert d_model == D_MODEL and seq % ATT_CHUNK == 0 and seq % RWKV_T == 0
    m = batch * seq
    tm = ROW_TILE
    assert m % PROJ_TM == 0 and m % tm == 0
    bias_tiles = _bias_tiles(rel_bias)
    row = lambda vec: vec.reshape(1, -1)
    x2 = x.reshape(m, D_MODEL)
    for l in range(w_in.shape[0]):
        slabs32, slabs16 = _proj(x2, row(norm_mix_pre[l]), w_in[l], PROJ_TM)
        o_att = _attn(slabs32, bias_tiles, batch, seq)
        pr, sh, wl = _rwkv_params(shift_mu[l], w0[l], a0[l], k_k[l], k_a[l], r_k[l], ln_x_w[l], ln_x_b[l],
                                  w_w2[l], w_a2[l], w_g2[l])
        o_rwkv = _rwkv(slabs32, slabs16, pr, sh, wl, batch, seq)
        x2 = _tail(x2, o_att, o_rwkv, slabs16, row(b_gate[l]), _bf16(w_att_branch[l]), _bf16(w_rwkv_branch[l]),
                   _bf16(w_out[l]), row(norm_mix_post[l]), row(norm_ffn_pre[l]), _bf16(w_ffn1[l]),
                   _bf16(w_ffn2[l]), row(norm_ffn_post[l]), tm)
    return x2.reshape(batch, seq, D_MODEL)
```

```python
import functools
import math

import jax
import jax.numpy as jnp
from jax import lax
from jax.experimental import pallas as pl
from jax.experimental.pallas import tpu as pltpu

D_MODEL = 1024
HEAD_DIM = 64
DILATIONS = (1, 4, 16)
KEYS_PER_QUERY = 128
N_GROUPS = len(DILATIONS)
HEADS_PER_GROUP = 4
ATT_HEADS = N_GROUPS * HEADS_PER_GROUP
ATT_WIDTH = ATT_HEADS * HEAD_DIM
ATT_OUT_WIDTH = HEADS_PER_GROUP * HEAD_DIM
N_BUCKETS = 32
MAX_DISTANCE = KEYS_PER_QUERY * DILATIONS[-1]
RWKV_WIDTH = D_MODEL
DECAY_LORA = 64
ICLR_LORA = 64
GATE_LORA = 128
RWKV_COLS = 3 * RWKV_WIDTH + DECAY_LORA + ICLR_LORA + GATE_LORA
N_BRANCHES = 2
IN_COLS = 3 * ATT_WIDTH + RWKV_COLS + N_BRANCHES * D_MODEL
D_FF = 4 * D_MODEL
NORM_EPS = 1e-6
LN_X_EPS = 64e-5
L2_EPS = 1e-12

LANES = 128
VMEM_LIMIT = 56 * 1024 * 1024
NEG = -1e30

GATE_COLS = N_BRANCHES * D_MODEL
Q_SLAB0 = 0
K_SLAB0 = Q_SLAB0 + ATT_WIDTH // LANES
V_SLAB0 = K_SLAB0 + ATT_WIDTH // LANES
LORA_SLAB = V_SLAB0 + ATT_WIDTH // LANES
FG_SLAB = LORA_SLAB + 1
N_SLABS_F32 = FG_SLAB + 1
GATE_SLAB0 = 0
R_SLAB0 = GATE_COLS // LANES
RK_SLAB0 = R_SLAB0 + RWKV_WIDTH // LANES
RV_SLAB0 = RK_SLAB0 + RWKV_WIDTH // LANES
N_SLABS_BF16 = RV_SLAB0 + RWKV_WIDTH // LANES
RWKV_QUAD = 4
QW = RWKV_QUAD * HEAD_DIM
N_QUADS = RWKV_WIDTH // QW

PROJ_TN = 1280
PROJ_TM = 1024
ROW_TILE = 512
ATT_CHUNK = KEYS_PER_QUERY * DILATIONS[-1]
QB = KEYS_PER_QUERY
ATT_UNROLL = 3
RWKV_CHUNK = 64
RWKV_T = 512
RWKV_GROUP = 8


def _bf16(x):
    return x.astype(jnp.bfloat16)


def _dot(a, b):
    return jnp.dot(a, b, preferred_element_type=jnp.float32)


def _dot_nt(a, b):
    return lax.dot_general(a, b, (((1,), (1,)), ((), ())), preferred_element_type=jnp.float32)


def _proj_body(x_ref, g_ref, w_ref, o32_ref, o16_ref, h_scr):
    @pl.when(pl.program_id(1) == 0)
    def _():
        x = x_ref[...]
        ms = jnp.mean(x * x, axis=-1, keepdims=True)
        h_scr[...] = _bf16(x * lax.rsqrt(ms + NORM_EPS) * g_ref[...])

    acc = _dot(h_scr[...], w_ref[...])
    for s in range(PROJ_TN // LANES):
        o32_ref[s] = acc[:, s * LANES:(s + 1) * LANES]
        o16_ref[s] = _bf16(acc[:, s * LANES:(s + 1) * LANES])


def _proj(x2, g, w_in, tm):
    m = x2.shape[0]
    n_col_blocks = IN_COLS // PROJ_TN
    slabs_per_block = PROJ_TN // LANES
    n_f32_blocks = N_SLABS_F32 // slabs_per_block
    att_end = 3 * ATT_WIDTH
    rkv_end = att_end + 3 * RWKV_WIDTH
    lora_end = att_end + RWKV_COLS
    w_cols = _bf16(jnp.concatenate([w_in[:, :att_end], w_in[:, rkv_end:lora_end], w_in[:, lora_end:],
                                    w_in[:, att_end:rkv_end]], axis=1))
    blk = (slabs_per_block, tm, LANES)
    return pl.pallas_call(
        _proj_body,
        out_shape=(jax.ShapeDtypeStruct((N_SLABS_F32 + slabs_per_block, m, LANES), jnp.float32),
                   jax.ShapeDtypeStruct((N_SLABS_BF16, m, LANES), jnp.bfloat16)),
        grid=(m // tm, n_col_blocks),
        in_specs=[
            pl.BlockSpec((tm, D_MODEL), lambda i, j: (i, 0)),
            pl.BlockSpec((1, D_MODEL), lambda i, j: (0, 0)),
            pl.BlockSpec((D_MODEL, PROJ_TN), lambda i, j: (0, j)),
        ],
        out_specs=(pl.BlockSpec(blk, lambda i, j: (jnp.minimum(j, n_f32_blocks), i, 0)),
                   pl.BlockSpec(blk, lambda i, j: (jnp.maximum(j - n_f32_blocks, 0), i, 0))),
        scratch_shapes=[pltpu.VMEM((tm, D_MODEL), jnp.bfloat16)],
        compiler_params=pltpu.CompilerParams(
            dimension_semantics=("parallel", "arbitrary"), vmem_limit_bytes=VMEM_LIMIT),
        name="proj",
    )(x2, g, w_cols)


def _attn_units(qs, ks, vs, biases, lo):
    scale = HEAD_DIM ** -0.5

    def logits(q, k):
        qs_ = q * scale
        zero = jnp.zeros_like(qs_)
        lhs = _bf16(jnp.concatenate([jnp.where(lo, qs_, zero), jnp.where(lo, zero, qs_)], axis=0))
        return _dot_nt(lhs, _bf16(k))

    s = [logits(q, k) for q, k in zip(qs, ks)]
    s0 = [x[:QB] + b[0] for x, b in zip(s, biases)]
    s1 = [x[QB:] + b[1] for x, b in zip(s, biases)]
    m0 = [jnp.max(x, axis=-1, keepdims=True) for x in s0]
    m1 = [jnp.max(x, axis=-1, keepdims=True) for x in s1]
    p0 = [_bf16(jnp.exp(x - m)) for x, m in zip(s0, m0)]
    p1 = [_bf16(jnp.exp(x - m)) for x, m in zip(s1, m1)]

    def weighted(p0_, p1_, v):
        zv = jnp.zeros_like(v)
        ones = jnp.ones_like(v)
        rhs0 = _bf16(jnp.concatenate([jnp.where(lo, v, zv), jnp.where(lo, ones, zv)], axis=1))
        rhs1 = _bf16(jnp.concatenate([jnp.where(lo, zv, v), jnp.where(lo, zv, ones)], axis=1))
        return _dot(p0_, rhs0) + _dot(p1_, rhs1)

    ol = [weighted(a, b, v) for a, b, v in zip(p0, p1, vs)]
    out = []
    for x, a, b in zip(ol, m0, m1):
        l = x[:, LANES:]
        out.append((x[:, :LANES] / l, jnp.where(lo, a, b) + jnp.log(l)))
    return out


def _attn_body(q_ref, kc_ref, kp_ref, vc_ref, vp_ref, bias_ref, o_ref, o_scr, l_scr):
    c = pl.program_id(1)
    g = pl.program_id(2)
    lo = lax.broadcasted_iota(jnp.int32, (1, LANES), 1) < HEAD_DIM
    before_start = (c == 0) & (lax.broadcasted_iota(jnp.int32, (QB, 2 * QB), 1) < QB)

    def rows(start, n, d):
        return pl.ds(start, n) if d == 1 else pl.ds(start, n, stride=d)

    def group(gi):
        d = DILATIONS[gi]
        n_qb = ATT_CHUNK // (QB * d)

        def run(units):
            qs = [q_ref[p, rows(start_q, QB, d), :] for p, start_q, _, _, _ in units]

            def bias(head, is_first):
                tile = bias_ref[head]
                return jnp.where(before_start, NEG, tile) if is_first else tile

            biases = [(bias(gi * HEADS_PER_GROUP + 2 * p, is_first), bias(gi * HEADS_PER_GROUP + 2 * p + 1, is_first))
                      for p, _, _, _, is_first in units]
            res = _attn_units(qs, [u[2] for u in units], [u[3] for u in units], biases, lo)
            for (p, start_q, _, _, _), (o, lse) in zip(units, res):
                o_scr[gi, p, rows(start_q, QB, d), :] = o
                l_scr[gi, p, rows(start_q, QB, d), :] = lse

        n_first = max(j for j in range(1, ATT_UNROLL + 1) if d % j == 0)

        def first_blocks(it, carry):
            units = []
            for j in range(n_first):
                r = it * n_first + j
                prev_start = r + QB * d * (n_qb - 1)
                for p in range(2):
                    k = jnp.concatenate([kp_ref[p, rows(prev_start, QB, d), :], kc_ref[p, rows(r, QB, d), :]], axis=0)
                    v = jnp.concatenate([vp_ref[p, rows(prev_start, QB, d), :], vc_ref[p, rows(r, QB, d), :]], axis=0)
                    units.append((p, r, k, v, True))
            run(units)
            return carry

        lax.fori_loop(0, d // n_first, first_blocks, 0)

        n_later = d * (n_qb - 1)
        if n_later:
            n_par = max(j for j in range(1, ATT_UNROLL + 1) if n_later % j == 0)

            def later_blocks(it, carry):
                units = []
                for j in range(n_par):
                    u = it * n_par + j
                    start_q = u % d + QB * d * (1 + u // d)
                    for p in range(2):
                        k = kc_ref[p, rows(start_q - QB * d, 2 * QB, d), :]
                        v = vc_ref[p, rows(start_q - QB * d, 2 * QB, d), :]
                        units.append((p, start_q, k, v, False))
                run(units)
                return carry

            lax.fori_loop(0, n_later // n_par, later_blocks, 0)

    for gi in range(N_GROUPS):
        pl.when(g == gi)(functools.partial(group, gi))

    @pl.when(g == N_GROUPS - 1)
    def _():
        tile = 256

        def comb(i, carry):
            rs = pl.ds(pl.multiple_of(i * tile, tile), tile)
            for p in range(2):
                ls = [l_scr[gi, p, rs, :] for gi in range(N_GROUPS)]
                mx = jnp.maximum(jnp.maximum(ls[0], ls[1]), ls[2])
                ws = [jnp.exp(l - mx) for l in ls]
                num = ws[0] * o_scr[0, p, rs, :] + ws[1] * o_scr[1, p, rs, :] + ws[2] * o_scr[2, p, rs, :]
                o_ref[rs, p * LANES:(p + 1) * LANES] = _bf16(num / (ws[0] + ws[1] + ws[2]))
            return carry

        lax.fori_loop(0, ATT_CHUNK // tile, comb, 0)


def _attn(slabs, bias_tiles, batch, seq):
    m = slabs.shape[1]
    n_chunks = seq // ATT_CHUNK
    blk = (2, ATT_CHUNK, LANES)

    def cur(slab0):
        return pl.BlockSpec(blk, lambda b, c, g: (slab0 // 2 + g, b * n_chunks + c, 0))

    def prev(slab0):
        return pl.BlockSpec(blk, lambda b, c, g: (slab0 // 2 + g, b * n_chunks + jnp.maximum(c - 1, 0), 0))

    return pl.pallas_call(
        _attn_body,
        out_shape=jax.ShapeDtypeStruct((m, ATT_OUT_WIDTH), jnp.bfloat16),
        grid=(batch, n_chunks, N_GROUPS),
        in_specs=[cur(Q_SLAB0), cur(K_SLAB0), prev(K_SLAB0), cur(V_SLAB0), prev(V_SLAB0),
                  pl.BlockSpec(bias_tiles.shape, lambda b, c, g: (0, 0, 0))],
        out_specs=pl.BlockSpec((ATT_CHUNK, ATT_OUT_WIDTH), lambda b, c, g: (b * n_chunks + c, 0)),
        scratch_shapes=[pltpu.VMEM((N_GROUPS, 2, ATT_CHUNK, LANES), jnp.float32),
                        pltpu.VMEM((N_GROUPS, 2, ATT_CHUNK, LANES), jnp.float32)],
        compiler_params=pltpu.CompilerParams(
            dimension_semantics=("parallel", "parallel", "arbitrary"), vmem_limit_bytes=VMEM_LIMIT),
        name="attn",
    )(slabs, slabs, slabs, slabs, slabs, bias_tiles)


def _t5_bucket(dist):
    max_exact = N_BUCKETS // 2
    d_f = jnp.maximum(dist, 1).astype(jnp.float32)
    large = max_exact + (jnp.log(d_f / max_exact) / math.log(MAX_DISTANCE / max_exact)
                         * (N_BUCKETS - max_exact)).astype(jnp.int32)
    large = jnp.minimum(large, N_BUCKETS - 1)
    return jnp.where(dist < max_exact, dist, large)


def _bias_tiles(rel_bias):
    dil = jnp.array(DILATIONS, jnp.int32)
    dist = dil[:, None] * jnp.arange(KEYS_PER_QUERY + 1, dtype=jnp.int32)[None, :]
    bucket = _t5_bucket(dist)
    bias = rel_bias.reshape(N_BUCKETS, N_GROUPS, HEADS_PER_GROUP)[bucket, jnp.arange(N_GROUPS)[:, None]]
    bias = jnp.transpose(bias, (0, 2, 1)).astype(jnp.float32).reshape(ATT_HEADS, KEYS_PER_QUERY + 1)
    n = 3 * QB - 1
    neg = lambda w: jnp.full((ATT_HEADS, w), NEG, jnp.float32)
    e = jnp.concatenate([neg(QB - 1), bias[:, ::-1], neg(QB - 1), neg(1)], axis=1)
    e = jnp.roll(e, -(QB - 1), axis=1)
    return jnp.tile(e, (1, QB))[:, :QB * n].reshape(ATT_HEADS, QB, n)[:, :, :2 * QB]


def _interleave(*gens):
    gens = list(gens)
    while gens:
        for gen in list(gens):
            try:
                next(gen)
            except StopIteration:
                gens.remove(gen)


def _rwkv_body(r_ref, k_ref, v_ref, lora_ref, fg_ref, pra_ref, prs_ref, sh_ref, wl_ref, o_ref,
               carry, h_st, tok_s, post_s, up_s, zq_s, kcbt_s, g_s, vb_s, y_s, *, n_t):
    t_blk = o_ref.shape[0]
    cc = RWKV_CHUNK
    cs = range(t_blk // cc)
    i = pl.program_id(0)
    first_tok = (i % n_t) == 0
    first_seq = ((i + n_t - 2) % n_t) == 0

    @pl.when(i == 0)
    def _():
        for ref in (carry, h_st, tok_s, post_s, up_s, zq_s, kcbt_s, g_s, vb_s):
            ref[...] = jnp.zeros(ref.shape, ref.dtype)

    lane_head = lax.broadcasted_iota(jnp.int32, (QW, QW), 1) // HEAD_DIM
    row_head = lax.broadcasted_iota(jnp.int32, (QW, QW), 0) // HEAD_DIM
    same_head = lane_head == row_head
    head_ones = _bf16(jnp.where(same_head, 1.0, 0.0))
    sw = RWKV_QUAD * cc
    chunk_lane_head = lax.broadcasted_iota(jnp.int32, (cc, QW), 1) // HEAD_DIM
    keep = [chunk_lane_head == h for h in range(RWKV_QUAD)]
    chunk_col_head = lax.broadcasted_iota(jnp.int32, (cc, sw), 1) // cc
    keep_sbs = [chunk_col_head == h for h in range(RWKV_QUAD)]

    def head_sum(x):
        return _dot(_bf16(x), head_ones)

    def stack4(x):
        masks = keep_sbs if x.shape[1] == sw else keep
        z = jnp.zeros_like(x)
        return _bf16(jnp.concatenate([jnp.where(masks[h], x, z) for h in range(RWKV_QUAD)], axis=0))

    def each(fn, *lists):
        return [fn(*args) for args in zip(*lists)]

    def split3(x):
        hi = _bf16(x)
        r1 = x - hi.astype(jnp.float32)
        mid = _bf16(r1)
        return hi, mid, _bf16(r1 - mid.astype(jnp.float32))

    def per_token(tok_w):
        pr = pra_ref[0]
        sh = sh_ref[...]

        def shifted(x, idx, mu):
            width = x.shape[1]
            row = lax.broadcasted_iota(jnp.int32, x.shape, 0)
            last = jnp.where(first_tok, jnp.zeros((1, width), jnp.float32), carry[idx, 7:8, :width])
            prev = jnp.where(row == 0, last, pltpu.roll(x, 1, 0))
            carry[idx, :, :width] = x[t_blk - 8:, :]
            return x + (prev - x) * mu

        wide = lambda ref: jnp.concatenate([ref[0], ref[1]], axis=1).astype(jnp.float32)
        r = shifted(wide(r_ref), 0, pr[0:1])
        k = shifted(wide(k_ref), 1, pr[1:2])
        v = shifted(wide(v_ref), 2, pr[2:3])
        tok_s[tok_w, 0] = r
        tok_s[tok_w, 4] = v
        yield
        f_lora = shifted(lora_ref[0], 3, sh[0:1, :LANES])
        f_g = shifted(fg_ref[0], 4, sh[1:2, :LANES])
        w0, a0, k_k, k_a, r_k = (pr[j:j + 1] for j in range(3, 8))
        wd = w0 + _dot(_bf16(jnp.tanh(f_lora)), wl_ref[0, 0])
        tok_s[tok_w, 5] = -math.exp(-0.5) * jax.nn.sigmoid(wd)
        a = jax.nn.sigmoid(a0 + _dot(_bf16(f_lora), wl_ref[0, 1]))
        tok_s[tok_w, 7] = _dot(_bf16(jax.nn.sigmoid(f_g)), wl_ref[0, 2])
        yield
        kk = k * k_k
        kk = kk * lax.rsqrt(jnp.maximum(head_sum(kk * kk), L2_EPS * L2_EPS))
        tok_s[tok_w, 1] = kk
        tok_s[tok_w, 3] = kk * a
        yield
        k2 = k * (1.0 + (a - 1.0) * k_a)
        tok_s[tok_w, 2] = k2
        tok_s[tok_w, 6] = head_sum(r * k2 * r_k) * v
        yield

    def per_chunk(tok_r, chk_w, cs):
        t_i = lax.broadcasted_iota(jnp.int32, (cc, sw), 0)
        s_i = lax.broadcasted_iota(jnp.int32, (cc, sw), 1) % cc
        strict_lower = t_i > s_i
        lower = t_i >= s_i
        eye_f = jnp.where(t_i == s_i, 1.0, 0.0)
        zero_m = jnp.zeros((cc, sw), jnp.float32)
        ci = lax.broadcasted_iota(jnp.int32, (cc, cc), 0)
        cj = lax.broadcasted_iota(jnp.int32, (cc, cc), 1)
        tri = _bf16(jnp.where(ci >= cj, 1.0, 0.0))
        rows_of = lambda j: [tok_s[tok_r, j, c * cc:(c + 1) * cc, :] for c in cs]
        r, kk, k2, bb, v, lw = (rows_of(j) for j in range(6))

        parts = each(split3, lw)
        big_l = each(lambda p: _dot(tri, p[0]) + _dot(tri, p[1]) + _dot(tri, p[2]), parts)
        yield
        l_end = each(lambda l: l[cc - 1:cc], big_l)
        e_l = each(jnp.exp, big_l)
        e_lm = each(lambda l, w: jnp.exp(l - w), big_l, lw)
        e_nl = each(lambda l: jnp.exp(-l), big_l)
        e_c = each(lambda le, l: jnp.exp(le - l), l_end, big_l)
        a_kk_f = each(lambda x, e: x * e, kk, e_lm)
        a_kk = each(_bf16, a_kk_f)
        a_r_f = each(lambda x, e: x * e, r, e_l)
        a_r = each(_bf16, a_r_f)
        kb = each(lambda xk, xb, e: jnp.concatenate([stack4(xk * e), stack4(xb * e)], axis=0),
                  k2, bb, e_nl)
        vb = each(_bf16, v)
        vst = each(stack4, v)
        yield
        s12 = each(lambda ak, ar, kb_: _dot_nt(jnp.concatenate([ak, ar], axis=0), kb_), a_kk, a_r, kb)
        s1 = each(lambda s: s[:cc], s12)
        s2 = each(lambda s: s[cc:], s12)
        yield
        mk = each(lambda s: _bf16(jnp.where(strict_lower, s[:, :sw], zero_m)), s1)
        mb = each(lambda s: jnp.where(strict_lower, s[:, sw:], zero_m), s1)
        nkb = each(lambda s: _bf16(jnp.concatenate([jnp.where(lower, s[:, :sw], zero_m),
                                                     jnp.where(lower, -s[:, sw:], zero_m)], axis=1)), s2)

        x = each(lambda n: eye_f - jnp.where((t_i // 2) == (s_i // 2), n, zero_m), mb)
        blk = 4
        while blk <= cc:
            half = blk // 2
            sel = ((t_i // blk) == (s_i // blk)) & ((t_i % blk) >= half) & ((s_i % blk) < half)
            xb = each(_bf16, x)
            t1 = each(lambda xc, n: _bf16(_dot(xc, stack4(jnp.where(sel, n, zero_m)))), xb, mb)
            yield
            x = each(lambda xf, t: xf - _dot(t, stack4(xf)), x, t1)
            yield
            blk *= 2
        tinv = each(_bf16, x)

        pm = each(lambda t, ak: _dot(t, stack4(ak)), tinv, a_kk_f)
        mkv = each(lambda m_, v_: _dot(m_, v_), mk, vst)
        yield
        q = each(lambda t, m_: _dot(t, stack4(m_)), tinv, mkv)
        yield
        pmb = each(_bf16, pm)
        u = each(lambda af, n, p: af + _dot(n[:, sw:], stack4(p)), a_r_f, nkb, pm)
        z = each(lambda n, v_, q_: _dot(n, jnp.concatenate([v_, stack4(q_)], axis=0)), nkb, vst, q)
        yield
        for j, c in enumerate(cs):
            up_s[chk_w, c] = jnp.concatenate([pmb[j], _bf16(u[j])], axis=0)
            zq_s[chk_w, c, 0] = z[j]
            zq_s[chk_w, c, 1] = q[j]
            vb_s[chk_w, c] = vb[j]
        yield
        pad = [jnp.zeros((LANES - 2 * cc, QW), jnp.float32)] if 2 * cc < LANES else []
        kcg = each(lambda xk, xb, e, le: jnp.concatenate(
            [xk * e, -(xb * e)] + pad + [jnp.broadcast_to(jnp.exp(le), (LANES, QW))], axis=0).T,
            k2, bb, e_c, l_end)
        for j, c in enumerate(cs):
            kcbt_s[chk_w, c] = _bf16(kcg[j][:, :LANES])
            g_s[chk_w, c] = kcg[j][:, LANES:]
        yield

    def per_chunk_groups(tok_r, chk_w):
        for first in range(0, len(cs), RWKV_GROUP):
            yield from per_chunk(tok_r, chk_w, list(cs[first:first + RWKV_GROUP]))
        post_s[chk_w, 0] = tok_s[tok_r, 6]
        post_s[chk_w, 1] = tok_s[tok_r, 7]
        yield

    def sequential(chk_r):
        h = jnp.where(first_seq, jnp.zeros((QW, QW), jnp.float32), h_st[...])
        pad_b = [jnp.zeros((LANES - 2 * cc, QW), jnp.bfloat16)] if 2 * cc < LANES else []
        for c in cs:
            ys = _dot(up_s[chk_r, c], _bf16(h))
            sig = ys[:cc] + zq_s[chk_r, c, 1]
            y_s[c * cc:(c + 1) * cc, :] = ys[cc:] + zq_s[chk_r, c, 0]
            yield
            hu = _dot(kcbt_s[chk_r, c], jnp.concatenate([vb_s[chk_r, c], _bf16(sig)] + pad_b, axis=0))
            g = g_s[chk_r, c]
            h = h * jnp.concatenate([g, g], axis=1) + jnp.where(same_head, hu, jnp.zeros_like(hu))
            yield
        h_st[...] = h
        prs = prs_ref[0]
        y = y_s[...]
        mu = head_sum(y) * (1.0 / HEAD_DIM)
        yc = y - mu
        var = head_sum(yc * yc) * (1.0 / HEAD_DIM)
        yn = yc * lax.rsqrt(var + LN_X_EPS) * prs[8:9] + prs[9:10]
        o_ref[...] = _bf16((yn + post_s[chk_r, 0]) * post_s[chk_r, 1])
        yield

    for parity in (0, 1):
        @pl.when(i % 2 == parity)
        def _(parity=parity):
            _interleave(per_chunk_groups(1 - parity, 1 - parity), sequential(parity), per_token(parity))


def _rwkv(slabs32, slabs16, pr, sh, wl, batch, seq):
    m = slabs32.shape[1]
    t_blk = RWKV_T
    n_t = seq // t_blk
    n_c = t_blk // RWKV_CHUNK
    n_blocks = batch * N_QUADS * n_t
    slabs_per_quad = QW // LANES

    def where(blk):
        blk = jnp.clip(blk, 0, n_blocks - 1)
        seq_id, t = blk // n_t, blk % n_t
        return seq_id % N_QUADS, (seq_id // N_QUADS) * n_t + t

    def tok(slab0, per_quad):
        if per_quad:
            return pl.BlockSpec((slabs_per_quad, t_blk, LANES),
                                lambda i: (slab0 // slabs_per_quad + where(i)[0], where(i)[1], 0))
        return pl.BlockSpec((1, t_blk, LANES), lambda i: (slab0, where(i)[1], 0))

    return pl.pallas_call(
        functools.partial(_rwkv_body, n_t=n_t),
        out_shape=jax.ShapeDtypeStruct((m, RWKV_WIDTH), jnp.bfloat16),
        grid=(n_blocks + 2,),
        in_specs=[tok(R_SLAB0, True), tok(RK_SLAB0, True), tok(RV_SLAB0, True),
                  tok(LORA_SLAB, False), tok(FG_SLAB, False),
                  pl.BlockSpec((1,) + pr.shape[1:], lambda i: (where(i)[0], 0, 0)),
                  pl.BlockSpec((1,) + pr.shape[1:], lambda i: (where(i - 2)[0], 0, 0)),
                  pl.BlockSpec(sh.shape, lambda i: (0, 0)),
                  pl.BlockSpec((1,) + wl.shape[1:], lambda i: (where(i)[0], 0, 0, 0))],
        out_specs=pl.BlockSpec((t_blk, QW), lambda i: (where(i - 2)[1], where(i - 2)[0])),
        scratch_shapes=[
            pltpu.VMEM((5, 8, QW), jnp.float32),
            pltpu.VMEM((QW, QW), jnp.float32),
            pltpu.VMEM((2, 8, t_blk, QW), jnp.float32),
            pltpu.VMEM((2, 2, t_blk, QW), jnp.float32),
            pltpu.VMEM((2, n_c, 2 * RWKV_CHUNK, QW), jnp.bfloat16),
            pltpu.VMEM((2, n_c, 2, RWKV_CHUNK, QW), jnp.float32),
            pltpu.VMEM((2, n_c, QW, LANES), jnp.bfloat16),
            pltpu.VMEM((2, n_c, QW, LANES), jnp.float32),
            pltpu.VMEM((2, n_c, RWKV_CHUNK, QW), jnp.bfloat16),
            pltpu.VMEM((t_blk, QW), jnp.float32),
        ],
        compiler_params=pltpu.CompilerParams(
            dimension_semantics=("arbitrary",), vmem_limit_bytes=VMEM_LIMIT),
        name="rwkv",
    )(slabs16, slabs16, slabs16, slabs32, slabs32, pr, pr, sh, wl)


def _rwkv_params(shift_mu, w0, a0, k_k, k_a, r_k, ln_w, ln_b, w_w2, w_a2, w_g2):
    def quads(vec):
        return vec.reshape(N_QUADS, QW)

    mu_r, mu_k, mu_v = (quads(shift_mu[j * RWKV_WIDTH:(j + 1) * RWKV_WIDTH]) for j in range(3))
    rows = [mu_r, mu_k, mu_v, quads(w0), quads(a0), quads(k_k), quads(k_a), quads(r_k.reshape(-1)),
            quads(ln_w), quads(ln_b)]
    pr = jnp.stack(rows, axis=1)
    pr = jnp.pad(pr, ((0, 0), (0, 16 - pr.shape[1]), (0, 0)))
    mu_rest = shift_mu[3 * RWKV_WIDTH:]
    sh = jnp.pad(mu_rest.reshape(2, LANES), ((0, 6), (0, QW - LANES)))
    zeros = jnp.zeros((DECAY_LORA, RWKV_WIDTH), jnp.float32)
    wd = jnp.concatenate([w_w2, zeros], axis=0)
    wa = jnp.concatenate([zeros, w_a2], axis=0)
    wl = jnp.stack([wd, wa, w_g2], axis=0)
    wl = wl.reshape(3, LANES, N_QUADS, QW).transpose(2, 0, 1, 3)
    return pr, sh, _bf16(wl)


def _rms(x, g):
    ms = jnp.mean(x * x, axis=-1, keepdims=True)
    return x * lax.rsqrt(ms + NORM_EPS) * g


def _tail_body(x_ref, att_ref, rw_ref, gate_ref, bg_ref, wa_ref, wr_ref, wo_ref, gmix_ref,
               gpre_ref, w1_ref, w2_ref, gpost_ref, o_ref, *, tf):
    n_gs = D_MODEL // LANES
    f0 = jnp.concatenate([gate_ref[s] for s in range(n_gs)], axis=1).astype(jnp.float32)
    f1 = jnp.concatenate([gate_ref[n_gs + s] for s in range(n_gs)], axis=1).astype(jnp.float32)
    g0 = jax.nn.sigmoid(f0 + bg_ref[:, :D_MODEL])
    g1 = jax.nn.sigmoid(f1 + bg_ref[:, D_MODEL:])
    merged = g0 * _dot(att_ref[...], wa_ref[...]) + g1 * _dot(rw_ref[...], wr_ref[...])
    z = _dot(_bf16(merged), wo_ref[...])
    x = x_ref[...] + _rms(z, gmix_ref[...])
    h = _bf16(_rms(x, gpre_ref[...]))
    acc = jnp.zeros(x.shape, jnp.float32)
    for c in range(D_FF // tf):
        u = jnp.maximum(_dot(h, w1_ref[:, c * tf:(c + 1) * tf]), 0.0)
        acc = acc + _dot(_bf16(u * u), w2_ref[c * tf:(c + 1) * tf, :])
    o_ref[...] = x + _rms(acc, gpost_ref[...])


def _tail(x2, o_att, o_rwkv, slabs, b_gate, wa, wr, wo, g_mix, g_pre, w1, w2, g_post, tm, tf=1024):
    m = x2.shape[0]
    n_gate_slabs = GATE_COLS // LANES
    const = lambda shape: pl.BlockSpec(shape, lambda i: (0, 0), pipeline_mode=pl.Buffered(1))
    return pl.pallas_call(
        functools.partial(_tail_body, tf=tf),
        out_shape=jax.ShapeDtypeStruct((m, D_MODEL), jnp.float32),
        grid=(m // tm,),
        in_specs=[
            pl.BlockSpec((tm, D_MODEL), lambda i: (i, 0)),
            pl.BlockSpec((tm, ATT_OUT_WIDTH), lambda i: (i, 0)),
            pl.BlockSpec((tm, RWKV_WIDTH), lambda i: (i, 0)),
            pl.BlockSpec((n_gate_slabs, tm, LANES), lambda i: (GATE_SLAB0 // n_gate_slabs, i, 0)),
            const((1, GATE_COLS)), const(wa.shape), const(wr.shape), const(wo.shape), const((1, D_MODEL)),
            const((1, D_MODEL)), const(w1.shape), const(w2.shape), const((1, D_MODEL)),
        ],
        out_specs=pl.BlockSpec((tm, D_MODEL), lambda i: (i, 0)),
        compiler_params=pltpu.CompilerParams(
            dimension_semantics=("parallel",), vmem_limit_bytes=VMEM_LIMIT),
        name="tail",
    )(x2, o_att, o_rwkv, slabs, b_gate, wa, wr, wo, g_mix, g_pre, w1, w2, g_post)


def kernel(x, rel_bias, norm_mix_pre, norm_mix_post, norm_ffn_pre, norm_ffn_post, w_in, b_gate, shift_mu, w0, w_w2, a0, w_a2, w_g2, k_k, k_a, r_k, ln_x_w, ln_x_b, w_att_branch, w_rwkv_branch, w_out, w_ffn1, w_ffn2):
    batch, seq, d_model = x.shape
    assert d_model == D_MODEL and seq % ATT_CHUNK == 0 and seq % RWKV_T == 0
    m = batch * seq
    tm = ROW_TILE
    assert m % PROJ_TM == 0 and m % tm == 0
    bias_tiles = _bias_tiles(rel_bias)
    row = lambda vec: vec.reshape(1, -1)
    x2 = x.reshape(m, D_MODEL)
    for l in range(w_in.shape[0]):
        slabs32, slabs16 = _proj(x2, row(norm_mix_pre[l]), w_in[l], PROJ_TM)
        o_att = _attn(slabs32, bias_tiles, batch, seq)
        pr, sh, wl = _rwkv_params(shift_mu[l], w0[l], a0[l], k_k[l], k_a[l], r_k[l], ln_x_w[l], ln_x_b[l],
                                  w_w2[l], w_a2[l], w_g2[l])
        o_rwkv = _rwkv(slabs32, slabs16, pr, sh, wl, batch, seq)
        x2 = _tail(x2, o_att, o_rwkv, slabs16, row(b_gate[l]), _bf16(w_att_branch[l]), _bf16(w_rwkv_branch[l]),
                   _bf16(w_out[l]), row(norm_mix_post[l]), row(norm_ffn_pre[l]), _bf16(w_ffn1[l]),
                   _bf16(w_ffn2[l]), row(norm_ffn_post[l]), tm)
    return x2.reshape(batch, seq, D_MODEL)
```

```python
import functools
import math

import jax
import jax.numpy as jnp
from jax import lax
from jax.experimental import pallas as pl
from jax.experimental.pallas import tpu as pltpu

D_MODEL = 1024
HEAD_DIM = 64
DILATIONS = (1, 4, 16)
KEYS_PER_QUERY = 128
N_GROUPS = len(DILATIONS)
HEADS_PER_GROUP = 4
ATT_HEADS = N_GROUPS * HEADS_PER_GROUP
ATT_WIDTH = ATT_HEADS * HEAD_DIM
ATT_OUT_WIDTH = HEADS_PER_GROUP * HEAD_DIM
N_BUCKETS = 32
MAX_DISTANCE = KEYS_PER_QUERY * DILATIONS[-1]
RWKV_WIDTH = D_MODEL
DECAY_LORA = 64
ICLR_LORA = 64
GATE_LORA = 128
RWKV_COLS = 3 * RWKV_WIDTH + DECAY_LORA + ICLR_LORA + GATE_LORA
N_BRANCHES = 2
IN_COLS = 3 * ATT_WIDTH + RWKV_COLS + N_BRANCHES * D_MODEL
D_FF = 4 * D_MODEL
NORM_EPS = 1e-6
LN_X_EPS = 64e-5
L2_EPS = 1e-12

LANES = 128
VMEM_LIMIT = 56 * 1024 * 1024
NEG = -1e30

GATE_COLS = N_BRANCHES * D_MODEL
Q_SLAB0 = 0
K_SLAB0 = Q_SLAB0 + ATT_WIDTH // LANES
V_SLAB0 = K_SLAB0 + ATT_WIDTH // LANES
LORA_SLAB = V_SLAB0 + ATT_WIDTH // LANES
FG_SLAB = LORA_SLAB + 1
N_SLABS_F32 = FG_SLAB + 1
GATE_SLAB0 = 0
R_SLAB0 = GATE_COLS // LANES
RK_SLAB0 = R_SLAB0 + RWKV_WIDTH // LANES
RV_SLAB0 = RK_SLAB0 + RWKV_WIDTH // LANES
N_SLABS_BF16 = RV_SLAB0 + RWKV_WIDTH // LANES
RWKV_QUAD = 4
QW = RWKV_QUAD * HEAD_DIM
N_QUADS = RWKV_WIDTH // QW

PROJ_TN = 1280
PROJ_TM = 1024
ROW_TILE = 512
ATT_CHUNK = KEYS_PER_QUERY * DILATIONS[-1]
QB = KEYS_PER_QUERY
ATT_UNROLL = 3
RWKV_CHUNK = 64
RWKV_T = 512


def _bf16(x):
    return x.astype(jnp.bfloat16)


def _dot(a, b):
    return jnp.dot(a, b, preferred_element_type=jnp.float32)


def _dot_nt(a, b):
    return lax.dot_general(a, b, (((1,), (1,)), ((), ())), preferred_element_type=jnp.float32)


def _proj_body(x_ref, g_ref, w_ref, o32_ref, o16_ref, h_scr):
    @pl.when(pl.program_id(1) == 0)
    def _():
        x = x_ref[...]
        ms = jnp.mean(x * x, axis=-1, keepdims=True)
        h_scr[...] = _bf16(x * lax.rsqrt(ms + NORM_EPS) * g_ref[...])

    acc = _dot(h_scr[...], w_ref[...])
    for s in range(PROJ_TN // LANES):
        o32_ref[s] = acc[:, s * LANES:(s + 1) * LANES]
        o16_ref[s] = _bf16(acc[:, s * LANES:(s + 1) * LANES])


def _proj(x2, g, w_in, tm):
    m = x2.shape[0]
    n_col_blocks = IN_COLS // PROJ_TN
    slabs_per_block = PROJ_TN // LANES
    n_f32_blocks = N_SLABS_F32 // slabs_per_block
    att_end = 3 * ATT_WIDTH
    rkv_end = att_end + 3 * RWKV_WIDTH
    lora_end = att_end + RWKV_COLS
    w_cols = _bf16(jnp.concatenate([w_in[:, :att_end], w_in[:, rkv_end:lora_end], w_in[:, lora_end:],
                                    w_in[:, att_end:rkv_end]], axis=1))
    blk = (slabs_per_block, tm, LANES)
    return pl.pallas_call(
        _proj_body,
        out_shape=(jax.ShapeDtypeStruct((N_SLABS_F32 + slabs_per_block, m, LANES), jnp.float32),
                   jax.ShapeDtypeStruct((N_SLABS_BF16, m, LANES), jnp.bfloat16)),
        grid=(m // tm, n_col_blocks),
        in_specs=[
            pl.BlockSpec((tm, D_MODEL), lambda i, j: (i, 0)),
            pl.BlockSpec((1, D_MODEL), lambda i, j: (0, 0)),
            pl.BlockSpec((D_MODEL, PROJ_TN), lambda i, j: (0, j)),
        ],
        out_specs=(pl.BlockSpec(blk, lambda i, j: (jnp.minimum(j, n_f32_blocks), i, 0)),
                   pl.BlockSpec(blk, lambda i, j: (jnp.maximum(j - n_f32_blocks, 0), i, 0))),
        scratch_shapes=[pltpu.VMEM((tm, D_MODEL), jnp.bfloat16)],
        compiler_params=pltpu.CompilerParams(
            dimension_semantics=("parallel", "arbitrary"), vmem_limit_bytes=VMEM_LIMIT),
        name="proj",
    )(x2, g, w_cols)


def _attn_units(qs, ks, vs, biases, lo):
    scale = HEAD_DIM ** -0.5

    def logits(q, k):
        qs_ = q * scale
        zero = jnp.zeros_like(qs_)
        lhs = _bf16(jnp.concatenate([jnp.where(lo, qs_, zero), jnp.where(lo, zero, qs_)], axis=0))
        return _dot_nt(lhs, _bf16(k))

    s = [logits(q, k) for q, k in zip(qs, ks)]
    s0 = [x[:QB] + b[0] for x, b in zip(s, biases)]
    s1 = [x[QB:] + b[1] for x, b in zip(s, biases)]
    m0 = [jnp.max(x, axis=-1, keepdims=True) for x in s0]
    m1 = [jnp.max(x, axis=-1, keepdims=True) for x in s1]
    p0 = [_bf16(jnp.exp(x - m)) for x, m in zip(s0, m0)]
    p1 = [_bf16(jnp.exp(x - m)) for x, m in zip(s1, m1)]

    def weighted(p0_, p1_, v):
        zv = jnp.zeros_like(v)
        ones = jnp.ones_like(v)
        rhs0 = _bf16(jnp.concatenate([jnp.where(lo, v, zv), jnp.where(lo, ones, zv)], axis=1))
        rhs1 = _bf16(jnp.concatenate([jnp.where(lo, zv, v), jnp.where(lo, zv, ones)], axis=1))
        return _dot(p0_, rhs0) + _dot(p1_, rhs1)

    ol = [weighted(a, b, v) for a, b, v in zip(p0, p1, vs)]
    out = []
    for x, a, b in zip(ol, m0, m1):
        l = x[:, LANES:]
        out.append((x[:, :LANES] / l, jnp.where(lo, a, b) + jnp.log(l)))
    return out


def _attn_body(q_ref, kc_ref, kp_ref, vc_ref, vp_ref, bias_ref, o_ref, o_scr, l_scr):
    c = pl.program_id(1)
    g = pl.program_id(2)
    lo = lax.broadcasted_iota(jnp.int32, (1, LANES), 1) < HEAD_DIM
    before_start = (c == 0) & (lax.broadcasted_iota(jnp.int32, (QB, 2 * QB), 1) < QB)

    def rows(start, n, d):
        return pl.ds(start, n) if d == 1 else pl.ds(start, n, stride=d)

    def group(gi):
        d = DILATIONS[gi]
        n_qb = ATT_CHUNK // (QB * d)

        def run(units):
            qs = [q_ref[p, rows(start_q, QB, d), :] for p, start_q, _, _, _ in units]

            def bias(head, is_first):
                tile = bias_ref[head]
                return jnp.where(before_start, NEG, tile) if is_first else tile

            biases = [(bias(gi * HEADS_PER_GROUP + 2 * p, is_first), bias(gi * HEADS_PER_GROUP + 2 * p + 1, is_first))
                      for p, _, _, _, is_first in units]
            res = _attn_units(qs, [u[2] for u in units], [u[3] for u in units], biases, lo)
            for (p, start_q, _, _, _), (o, lse) in zip(units, res):
                o_scr[gi, p, rows(start_q, QB, d), :] = o
                l_scr[gi, p, rows(start_q, QB, d), :] = lse

        n_first = max(j for j in range(1, ATT_UNROLL + 1) if d % j == 0)

        def first_blocks(it, carry):
            units = []
            for j in range(n_first):
                r = it * n_first + j
                prev_start = r + QB * d * (n_qb - 1)
                for p in range(2):
                    k = jnp.concatenate([kp_ref[p, rows(prev_start, QB, d), :], kc_ref[p, rows(r, QB, d), :]], axis=0)
                    v = jnp.concatenate([vp_ref[p, rows(prev_start, QB, d), :], vc_ref[p, rows(r, QB, d), :]], axis=0)
                    units.append((p, r, k, v, True))
            run(units)
            return carry

        lax.fori_loop(0, d // n_first, first_blocks, 0)

        n_later = d * (n_qb - 1)
        if n_later:
            n_par = max(j for j in range(1, ATT_UNROLL + 1) if n_later % j == 0)

            def later_blocks(it, carry):
                units = []
                for j in range(n_par):
                    u = it * n_par + j
                    start_q = u % d + QB * d * (1 + u // d)
                    for p in range(2):
                        k = kc_ref[p, rows(start_q - QB * d, 2 * QB, d), :]
                        v = vc_ref[p, rows(start_q - QB * d, 2 * QB, d), :]
                        units.append((p, start_q, k, v, False))
                run(units)
                return carry

            lax.fori_loop(0, n_later // n_par, later_blocks, 0)

    for gi in range(N_GROUPS):
        pl.when(g == gi)(functools.partial(group, gi))

    @pl.when(g == N_GROUPS - 1)
    def _():
        tile = 256

        def comb(i, carry):
            rs = pl.ds(pl.multiple_of(i * tile, tile), tile)
            for p in range(2):
                ls = [l_scr[gi, p, rs, :] for gi in range(N_GROUPS)]
                mx = jnp.maximum(jnp.maximum(ls[0], ls[1]), ls[2])
                ws = [jnp.exp(l - mx) for l in ls]
                num = ws[0] * o_scr[0, p, rs, :] + ws[1] * o_scr[1, p, rs, :] + ws[2] * o_scr[2, p, rs, :]
                o_ref[rs, p * LANES:(p + 1) * LANES] = _bf16(num / (ws[0] + ws[1] + ws[2]))
            return carry

        lax.fori_loop(0, ATT_CHUNK // tile, comb, 0)


def _attn(slabs, bias_tiles, batch, seq):
    m = slabs.shape[1]
    n_chunks = seq // ATT_CHUNK
    blk = (2, ATT_CHUNK, LANES)

    def cur(slab0):
        return pl.BlockSpec(blk, lambda b, c, g: (slab0 // 2 + g, b * n_chunks + c, 0))

    def prev(slab0):
        return pl.BlockSpec(blk, lambda b, c, g: (slab0 // 2 + g, b * n_chunks + jnp.maximum(c - 1, 0), 0))

    return pl.pallas_call(
        _attn_body,
        out_shape=jax.ShapeDtypeStruct((m, ATT_OUT_WIDTH), jnp.bfloat16),
        grid=(batch, n_chunks, N_GROUPS),
        in_specs=[cur(Q_SLAB0), cur(K_SLAB0), prev(K_SLAB0), cur(V_SLAB0), prev(V_SLAB0),
                  pl.BlockSpec(bias_tiles.shape, lambda b, c, g: (0, 0, 0))],
        out_specs=pl.BlockSpec((ATT_CHUNK, ATT_OUT_WIDTH), lambda b, c, g: (b * n_chunks + c, 0)),
        scratch_shapes=[pltpu.VMEM((N_GROUPS, 2, ATT_CHUNK, LANES), jnp.float32),
                        pltpu.VMEM((N_GROUPS, 2, ATT_CHUNK, LANES), jnp.float32)],
        compiler_params=pltpu.CompilerParams(
            dimension_semantics=("parallel", "parallel", "arbitrary"), vmem_limit_bytes=VMEM_LIMIT),
        name="attn",
    )(slabs, slabs, slabs, slabs, slabs, bias_tiles)


def _t5_bucket(dist):
    max_exact = N_BUCKETS // 2
    d_f = jnp.maximum(dist, 1).astype(jnp.float32)
    large = max_exact + (jnp.log(d_f / max_exact) / math.log(MAX_DISTANCE / max_exact)
                         * (N_BUCKETS - max_exact)).astype(jnp.int32)
    large = jnp.minimum(large, N_BUCKETS - 1)
    return jnp.where(dist < max_exact, dist, large)


def _bias_tiles(rel_bias):
    dil = jnp.array(DILATIONS, jnp.int32)
    dist = dil[:, None] * jnp.arange(KEYS_PER_QUERY + 1, dtype=jnp.int32)[None, :]
    bucket = _t5_bucket(dist)
    bias = rel_bias.reshape(N_BUCKETS, N_GROUPS, HEADS_PER_GROUP)[bucket, jnp.arange(N_GROUPS)[:, None]]
    bias = jnp.transpose(bias, (0, 2, 1)).astype(jnp.float32).reshape(ATT_HEADS, KEYS_PER_QUERY + 1)
    n = 3 * QB - 1
    neg = lambda w: jnp.full((ATT_HEADS, w), NEG, jnp.float32)
    e = jnp.concatenate([neg(QB - 1), bias[:, ::-1], neg(QB - 1), neg(1)], axis=1)
    e = jnp.roll(e, -(QB - 1), axis=1)
    return jnp.tile(e, (1, QB))[:, :QB * n].reshape(ATT_HEADS, QB, n)[:, :, :2 * QB]


def _interleave(*gens):
    gens = list(gens)
    while gens:
        for gen in list(gens):
            try:
                next(gen)
            except StopIteration:
                gens.remove(gen)


def _rwkv_body(r_ref, k_ref, v_ref, lora_ref, fg_ref, pra_ref, prs_ref, sh_ref, wl_ref, o_ref,
               carry, h_st, tok_s, post_s, up_s, zq_s, kcbt_s, g_s, vb_s, y_s, *, n_t):
    t_blk = o_ref.shape[0]
    cc = RWKV_CHUNK
    cs = range(t_blk // cc)
    i = pl.program_id(0)
    first_tok = (i % n_t) == 0
    first_seq = ((i + n_t - 2) % n_t) == 0

    @pl.when(i == 0)
    def _():
        for ref in (carry, h_st, tok_s, post_s, up_s, zq_s, kcbt_s, g_s, vb_s):
            ref[...] = jnp.zeros(ref.shape, ref.dtype)

    lane_head = lax.broadcasted_iota(jnp.int32, (QW, QW), 1) // HEAD_DIM
    row_head = lax.broadcasted_iota(jnp.int32, (QW, QW), 0) // HEAD_DIM
    same_head = lane_head == row_head
    head_ones = _bf16(jnp.where(same_head, 1.0, 0.0))
    assert RWKV_QUAD * cc == QW
    chunk_lane_head = lax.broadcasted_iota(jnp.int32, (cc, QW), 1) // HEAD_DIM
    keep = [chunk_lane_head == h for h in range(RWKV_QUAD)]

    def head_sum(x):
        return _dot(_bf16(x), head_ones)

    def stack4(x):
        z = jnp.zeros_like(x)
        return _bf16(jnp.concatenate([jnp.where(keep[h], x, z) for h in range(RWKV_QUAD)], axis=0))

    def each(fn, *lists):
        return [fn(*args) for args in zip(*lists)]

    def split3(x):
        hi = _bf16(x)
        r1 = x - hi.astype(jnp.float32)
        mid = _bf16(r1)
        return hi, mid, _bf16(r1 - mid.astype(jnp.float32))

    def per_token(tok_w):
        pr = pra_ref[0]
        sh = sh_ref[...]

        def shifted(x, idx, mu):
            width = x.shape[1]
            row = lax.broadcasted_iota(jnp.int32, x.shape, 0)
            last = jnp.where(first_tok, jnp.zeros((1, width), jnp.float32), carry[idx, 7:8, :width])
            prev = jnp.where(row == 0, last, pltpu.roll(x, 1, 0))
            carry[idx, :, :width] = x[t_blk - 8:, :]
            return x + (prev - x) * mu

        wide = lambda ref: jnp.concatenate([ref[0], ref[1]], axis=1).astype(jnp.float32)
        r = shifted(wide(r_ref), 0, pr[0:1])
        k = shifted(wide(k_ref), 1, pr[1:2])
        v = shifted(wide(v_ref), 2, pr[2:3])
        tok_s[tok_w, 0] = r
        tok_s[tok_w, 4] = v
        yield
        f_lora = shifted(lora_ref[0], 3, sh[0:1, :LANES])
        f_g = shifted(fg_ref[0], 4, sh[1:2, :LANES])
        w0, a0, k_k, k_a, r_k = (pr[j:j + 1] for j in range(3, 8))
        wd = w0 + _dot(_bf16(jnp.tanh(f_lora)), wl_ref[0, 0])
        tok_s[tok_w, 5] = -math.exp(-0.5) * jax.nn.sigmoid(wd)
        a = jax.nn.sigmoid(a0 + _dot(_bf16(f_lora), wl_ref[0, 1]))
        tok_s[tok_w, 7] = _dot(_bf16(jax.nn.sigmoid(f_g)), wl_ref[0, 2])
        yield
        kk = k * k_k
        kk = kk * lax.rsqrt(jnp.maximum(head_sum(kk * kk), L2_EPS * L2_EPS))
        tok_s[tok_w, 1] = kk
        tok_s[tok_w, 3] = kk * a
        yield
        k2 = k * (1.0 + (a - 1.0) * k_a)
        tok_s[tok_w, 2] = k2
        tok_s[tok_w, 6] = head_sum(r * k2 * r_k) * v
        yield

    def per_chunk(tok_r, chk_w):
        t_i = lax.broadcasted_iota(jnp.int32, (cc, QW), 0)
        s_i = lax.broadcasted_iota(jnp.int32, (cc, QW), 1) % cc
        strict_lower = t_i > s_i
        lower = t_i >= s_i
        eye_f = jnp.where(t_i == s_i, 1.0, 0.0)
        zero_m = jnp.zeros((cc, QW), jnp.float32)
        ci = lax.broadcasted_iota(jnp.int32, (cc, cc), 0)
        cj = lax.broadcasted_iota(jnp.int32, (cc, cc), 1)
        tri = _bf16(jnp.where(ci >= cj, 1.0, 0.0))
        rows_of = lambda j: [tok_s[tok_r, j, c * cc:(c + 1) * cc, :] for c in cs]
        r, kk, k2, bb, v, lw = (rows_of(j) for j in range(6))

        parts = each(split3, lw)
        big_l = each(lambda p: _dot(tri, p[0]) + _dot(tri, p[1]) + _dot(tri, p[2]), parts)
        yield
        l_end = each(lambda l: l[cc - 1:cc], big_l)
        e_l = each(jnp.exp, big_l)
        e_lm = each(lambda l, w: jnp.exp(l - w), big_l, lw)
        e_nl = each(lambda l: jnp.exp(-l), big_l)
        e_c = each(lambda le, l: jnp.exp(le - l), l_end, big_l)
        a_kk_f = each(lambda x, e: x * e, kk, e_lm)
        a_kk = each(_bf16, a_kk_f)
        a_r_f = each(lambda x, e: x * e, r, e_l)
        a_r = each(_bf16, a_r_f)
        kb = each(lambda xk, xb, e: jnp.concatenate([stack4(xk * e), stack4(xb * e)], axis=0),
                  k2, bb, e_nl)
        vb = each(_bf16, v)
        vst = each(stack4, v)
        yield
        s12 = each(lambda ak, ar, kb_: _dot_nt(jnp.concatenate([ak, ar], axis=0), kb_), a_kk, a_r, kb)
        s1 = each(lambda s: s[:cc], s12)
        s2 = each(lambda s: s[cc:], s12)
        yield
        mk = each(lambda s: _bf16(jnp.where(strict_lower, s[:, :QW], zero_m)), s1)
        mb = each(lambda s: jnp.where(strict_lower, s[:, QW:], zero_m), s1)
        nkb = each(lambda s: _bf16(jnp.concatenate([jnp.where(lower, s[:, :QW], zero_m),
                                                     jnp.where(lower, -s[:, QW:], zero_m)], axis=1)), s2)

        x = each(lambda n: eye_f - jnp.where((t_i // 2) == (s_i // 2), n, zero_m), mb)
        blk = 4
        while blk <= cc:
            half = blk // 2
            sel = ((t_i // blk) == (s_i // blk)) & ((t_i % blk) >= half) & ((s_i % blk) < half)
            xb = each(_bf16, x)
            t1 = each(lambda xc, n: _bf16(_dot(xc, stack4(jnp.where(sel, n, zero_m)))), xb, mb)
            yield
            x = each(lambda xf, t: xf - _dot(t, stack4(xf)), x, t1)
            yield
            blk *= 2
        tinv = each(_bf16, x)

        pm = each(lambda t, ak: _dot(t, stack4(ak)), tinv, a_kk_f)
        mkv = each(lambda m_, v_: _dot(m_, v_), mk, vst)
        yield
        q = each(lambda t, m_: _dot(t, stack4(m_)), tinv, mkv)
        yield
        pmb = each(_bf16, pm)
        u = each(lambda af, n, p: af + _dot(n[:, QW:], stack4(p)), a_r_f, nkb, pm)
        z = each(lambda n, v_, q_: _dot(n, jnp.concatenate([v_, stack4(q_)], axis=0)), nkb, vst, q)
        yield
        for c in cs:
            up_s[chk_w, c] = jnp.concatenate([pmb[c], _bf16(u[c])], axis=0)
            zq_s[chk_w, c, 0] = z[c]
            zq_s[chk_w, c, 1] = q[c]
            vb_s[chk_w, c] = vb[c]
        yield
        kcg = each(lambda xk, xb, e, le: jnp.concatenate(
            [xk * e, -(xb * e), jnp.broadcast_to(jnp.exp(le), (2 * cc, QW))], axis=0).T,
            k2, bb, e_c, l_end)
        for c in cs:
            kcbt_s[chk_w, c] = _bf16(kcg[c][:, :2 * cc])
            g_s[chk_w, c] = kcg[c][:, 2 * cc:]
        yield
        post_s[chk_w, 0] = tok_s[tok_r, 6]
        post_s[chk_w, 1] = tok_s[tok_r, 7]
        yield

    def sequential(chk_r):
        h = jnp.where(first_seq, jnp.zeros((QW, QW), jnp.float32), h_st[...])
        for c in cs:
            ys = _dot(up_s[chk_r, c], _bf16(h))
            sig = ys[:cc] + zq_s[chk_r, c, 1]
            y_s[c * cc:(c + 1) * cc, :] = ys[cc:] + zq_s[chk_r, c, 0]
            yield
            hu = _dot(kcbt_s[chk_r, c], jnp.concatenate([vb_s[chk_r, c], _bf16(sig)], axis=0))
            g = g_s[chk_r, c]
            h = h * jnp.concatenate([g, g], axis=1) + jnp.where(same_head, hu, jnp.zeros_like(hu))
            yield
        h_st[...] = h
        prs = prs_ref[0]
        y = y_s[...]
        mu = head_sum(y) * (1.0 / HEAD_DIM)
        yc = y - mu
        var = head_sum(yc * yc) * (1.0 / HEAD_DIM)
        yn = yc * lax.rsqrt(var + LN_X_EPS) * prs[8:9] + prs[9:10]
        o_ref[...] = _bf16((yn + post_s[chk_r, 0]) * post_s[chk_r, 1])
        yield

    for parity in (0, 1):
        @pl.when(i % 2 == parity)
        def _(parity=parity):
            _interleave(per_chunk(1 - parity, 1 - parity), sequential(parity), per_token(parity))


def _rwkv(slabs32, slabs16, pr, sh, wl, batch, seq):
    m = slabs32.shape[1]
    t_blk = RWKV_T
    n_t = seq // t_blk
    n_c = t_blk // RWKV_CHUNK
    n_blocks = batch * N_QUADS * n_t
    slabs_per_quad = QW // LANES

    def where(blk):
        blk = jnp.clip(blk, 0, n_blocks - 1)
        seq_id, t = blk // n_t, blk % n_t
        return seq_id % N_QUADS, (seq_id // N_QUADS) * n_t + t

    def tok(slab0, per_quad):
        if per_quad:
            return pl.BlockSpec((slabs_per_quad, t_blk, LANES),
                                lambda i: (slab0 // slabs_per_quad + where(i)[0], where(i)[1], 0))
        return pl.BlockSpec((1, t_blk, LANES), lambda i: (slab0, where(i)[1], 0))

    return pl.pallas_call(
        functools.partial(_rwkv_body, n_t=n_t),
        out_shape=jax.ShapeDtypeStruct((m, RWKV_WIDTH), jnp.bfloat16),
        grid=(n_blocks + 2,),
        in_specs=[tok(R_SLAB0, True), tok(RK_SLAB0, True), tok(RV_SLAB0, True),
                  tok(LORA_SLAB, False), tok(FG_SLAB, False),
                  pl.BlockSpec((1,) + pr.shape[1:], lambda i: (where(i)[0], 0, 0)),
                  pl.BlockSpec((1,) + pr.shape[1:], lambda i: (where(i - 2)[0], 0, 0)),
                  pl.BlockSpec(sh.shape, lambda i: (0, 0)),
                  pl.BlockSpec((1,) + wl.shape[1:], lambda i: (where(i)[0], 0, 0, 0))],
        out_specs=pl.BlockSpec((t_blk, QW), lambda i: (where(i - 2)[1], where(i - 2)[0])),
        scratch_shapes=[
            pltpu.VMEM((5, 8, QW), jnp.float32),
            pltpu.VMEM((QW, QW), jnp.float32),
            pltpu.VMEM((2, 8, t_blk, QW), jnp.float32),
            pltpu.VMEM((2, 2, t_blk, QW), jnp.float32),
            pltpu.VMEM((2, n_c, 2 * RWKV_CHUNK, QW), jnp.bfloat16),
            pltpu.VMEM((2, n_c, 2, RWKV_CHUNK, QW), jnp.float32),
            pltpu.VMEM((2, n_c, QW, LANES), jnp.bfloat16),
            pltpu.VMEM((2, n_c, QW, LANES), jnp.float32),
            pltpu.VMEM((2, n_c, RWKV_CHUNK, QW), jnp.bfloat16),
            pltpu.VMEM((t_blk, QW), jnp.float32),
        ],
        compiler_params=pltpu.CompilerParams(
            dimension_semantics=("arbitrary",), vmem_limit_bytes=VMEM_LIMIT),
        name="rwkv",
    )(slabs16, slabs16, slabs16, slabs32, slabs32, pr, pr, sh, wl)


def _rwkv_params(shift_mu, w0, a0, k_k, k_a, r_k, ln_w, ln_b, w_w2, w_a2, w_g2):
    def quads(vec):
        return vec.reshape(N_QUADS, QW)

    mu_r, mu_k, mu_v = (quads(shift_mu[j * RWKV_WIDTH:(j + 1) * RWKV_WIDTH]) for j in range(3))
    rows = [mu_r, mu_k, mu_v, quads(w0), quads(a0), quads(k_k), quads(k_a), quads(r_k.reshape(-1)),
            quads(ln_w), quads(ln_b)]
    pr = jnp.stack(rows, axis=1)
    pr = jnp.pad(pr, ((0, 0), (0, 16 - pr.shape[1]), (0, 0)))
    mu_rest = shift_mu[3 * RWKV_WIDTH:]
    sh = jnp.pad(mu_rest.reshape(2, LANES), ((0, 6), (0, QW - LANES)))
    zeros = jnp.zeros((DECAY_LORA, RWKV_WIDTH), jnp.float32)
    wd = jnp.concatenate([w_w2, zeros], axis=0)
    wa = jnp.concatenate([zeros, w_a2], axis=0)
    wl = jnp.stack([wd, wa, w_g2], axis=0)
    wl = wl.reshape(3, LANES, N_QUADS, QW).transpose(2, 0, 1, 3)
    return pr, sh, _bf16(wl)


def _rms(x, g):
    ms = jnp.mean(x * x, axis=-1, keepdims=True)
    return x * lax.rsqrt(ms + NORM_EPS) * g


def _tail_body(x_ref, att_ref, rw_ref, gate_ref, bg_ref, wa_ref, wr_ref, wo_ref, gmix_ref,
               gpre_ref, w1_ref, w2_ref, gpost_ref, o_ref, *, tf):
    n_gs = D_MODEL // LANES
    f0 = jnp.concatenate([gate_ref[s] for s in range(n_gs)], axis=1).astype(jnp.float32)
    f1 = jnp.concatenate([gate_ref[n_gs + s] for s in range(n_gs)], axis=1).astype(jnp.float32)
    g0 = jax.nn.sigmoid(f0 + bg_ref[:, :D_MODEL])
    g1 = jax.nn.sigmoid(f1 + bg_ref[:, D_MODEL:])
    merged = g0 * _dot(att_ref[...], wa_ref[...]) + g1 * _dot(rw_ref[...], wr_ref[...])
    z = _dot(_bf16(merged), wo_ref[...])
    x = x_ref[...] + _rms(z, gmix_ref[...])
    h = _bf16(_rms(x, gpre_ref[...]))
    acc = jnp.zeros(x.shape, jnp.float32)
    for c in range(D_FF // tf):
        u = jnp.maximum(_dot(h, w1_ref[:, c * tf:(c + 1) * tf]), 0.0)
        acc = acc + _dot(_bf16(u * u), w2_ref[c * tf:(c + 1) * tf, :])
    o_ref[...] = x + _rms(acc, gpost_ref[...])


def _tail(x2, o_att, o_rwkv, slabs, b_gate, wa, wr, wo, g_mix, g_pre, w1, w2, g_post, tm, tf=1024):
    m = x2.shape[0]
    n_gate_slabs = GATE_COLS // LANES
    const = lambda shape: pl.BlockSpec(shape, lambda i: (0, 0), pipeline_mode=pl.Buffered(1))
    return pl.pallas_call(
        functools.partial(_tail_body, tf=tf),
        out_shape=jax.ShapeDtypeStruct((m, D_MODEL), jnp.float32),
        grid=(m // tm,),
        in_specs=[
            pl.BlockSpec((tm, D_MODEL), lambda i: (i, 0)),
            pl.BlockSpec((tm, ATT_OUT_WIDTH), lambda i: (i, 0)),
            pl.BlockSpec((tm, RWKV_WIDTH), lambda i: (i, 0)),
            pl.BlockSpec((n_gate_slabs, tm, LANES), lambda i: (GATE_SLAB0 // n_gate_slabs, i, 0)),
            const((1, GATE_COLS)), const(wa.shape), const(wr.shape), const(wo.shape), const((1, D_MODEL)),
            const((1, D_MODEL)), const(w1.shape), const(w2.shape), const((1, D_MODEL)),
        ],
        out_specs=pl.BlockSpec((tm, D_MODEL), lambda i: (i, 0)),
        compiler_params=pltpu.CompilerParams(
            dimension_semantics=("parallel",), vmem_limit_bytes=VMEM_LIMIT),
        name="tail",
    )(x2, o_att, o_rwkv, slabs, b_gate, wa, wr, wo, g_mix, g_pre, w1, w2, g_post)


def kernel(x, rel_bias, norm_mix_pre, norm_mix_post, norm_ffn_pre, norm_ffn_post, w_in, b_gate, shift_mu, w0, w_w2, a0, w_a2, w_g2, k_k, k_a, r_k, ln_x_w, ln_x_b, w_att_branch, w_rwkv_branch, w_out, w_ffn1, w_ffn2):
    batch, seq, d_model = x.shape
    assert d_model == D_MODEL and seq % ATT_CHUNK == 0 and seq % RWKV_T == 0
    m = batch * seq
    tm = ROW_TILE
    assert m % PROJ_TM == 0 and m % tm == 0
    bias_tiles = _bias_tiles(rel_bias)
    row = lambda vec: vec.reshape(1, -1)
    x2 = x.reshape(m, D_MODEL)
    for l in range(w_in.shape[0]):
        slabs32, slabs16 = _proj(x2, row(norm_mix_pre[l]), w_in[l], PROJ_TM)
        o_att = _attn(slabs32, bias_tiles, batch, seq)
        pr, sh, wl = _rwkv_params(shift_mu[l], w0[l], a0[l], k_k[l], k_a[l], r_k[l], ln_x_w[l], ln_x_b[l],
                                  w_w2[l], w_a2[l], w_g2[l])
        o_rwkv = _rwkv(slabs32, slabs16, pr, sh, wl, batch, seq)
        x2 = _tail(x2, o_att, o_rwkv, slabs16, row(b_gate[l]), _bf16(w_att_branch[l]), _bf16(w_rwkv_branch[l]),
                   _bf16(w_out[l]), row(norm_mix_post[l]), row(norm_ffn_pre[l]), _bf16(w_ffn1[l]),
                   _bf16(w_ffn2[l]), row(norm_ffn_post[l]), tm)
    return x2.reshape(batch, seq, D_MODEL)
```

```python
import functools
import math

import jax
import jax.numpy as jnp
from jax import lax
from jax.experimental import pallas as pl
from jax.experimental.pallas import tpu as pltpu

D_MODEL = 1024
HEAD_DIM = 64
DILATIONS = (1, 4, 16)
KEYS_PER_QUERY = 128
N_GROUPS = len(DILATIONS)
HEADS_PER_GROUP = 4
ATT_HEADS = N_GROUPS * HEADS_PER_GROUP
ATT_WIDTH = ATT_HEADS * HEAD_DIM
ATT_OUT_WIDTH = HEADS_PER_GROUP * HEAD_DIM
N_BUCKETS = 32
MAX_DISTANCE = KEYS_PER_QUERY * DILATIONS[-1]
RWKV_WIDTH = D_MODEL
DECAY_LORA = 64
ICLR_LORA = 64
GATE_LORA = 128
RWKV_COLS = 3 * RWKV_WIDTH + DECAY_LORA + ICLR_LORA + GATE_LORA
N_BRANCHES = 2
IN_COLS = 3 * ATT_WIDTH + RWKV_COLS + N_BRANCHES * D_MODEL
D_FF = 4 * D_MODEL
NORM_EPS = 1e-6
LN_X_EPS = 64e-5
L2_EPS = 1e-12

LANES = 128
VMEM_LIMIT = 56 * 1024 * 1024
NEG = -1e30

GATE_COLS = N_BRANCHES * D_MODEL
RWKV_QUAD = 4
QW = RWKV_QUAD * HEAD_DIM
N_QUADS = RWKV_WIDTH // QW
Q_SLAB0 = 0
K_SLAB0 = Q_SLAB0 + ATT_WIDTH // LANES
V_SLAB0 = K_SLAB0 + ATT_WIDTH // LANES
LORA_SLAB = V_SLAB0 + ATT_WIDTH // LANES
N_SLABS_F32 = LORA_SLAB + 2
RKV_SLABS = 3 * QW // LANES
GATE_SLAB0 = N_QUADS * RKV_SLABS
N_SLABS_BF16 = GATE_SLAB0 + GATE_COLS // LANES

PROJ_TN = 1280
PROJ_TM = 1024
ROW_TILE = 512
ATT_CHUNK = KEYS_PER_QUERY * DILATIONS[-1]
QB = KEYS_PER_QUERY
ATT_UNROLL = 3
RWKV_CHUNK = 64
RWKV_T = 512


def _bf16(x):
    return x.astype(jnp.bfloat16)


def _dot(a, b):
    return jnp.dot(a, b, preferred_element_type=jnp.float32)


def _dot_nt(a, b):
    return lax.dot_general(a, b, (((1,), (1,)), ((), ())), preferred_element_type=jnp.float32)


def _proj_body(x_ref, g_ref, w_ref, o32_ref, o16_ref, h_scr):
    @pl.when(pl.program_id(1) == 0)
    def _():
        x = x_ref[...]
        ms = jnp.mean(x * x, axis=-1, keepdims=True)
        h_scr[...] = _bf16(x * lax.rsqrt(ms + NORM_EPS) * g_ref[...])

    acc = _dot(h_scr[...], w_ref[...])
    for s in range(PROJ_TN // LANES):
        o32_ref[s] = acc[:, s * LANES:(s + 1) * LANES]
        o16_ref[s] = _bf16(acc[:, s * LANES:(s + 1) * LANES])


def _proj(x2, g, w_in, tm):
    m = x2.shape[0]
    n_col_blocks = IN_COLS // PROJ_TN
    slabs_per_block = PROJ_TN // LANES
    n_f32_blocks = N_SLABS_F32 // slabs_per_block
    att_end = 3 * ATT_WIDTH
    rkv_end = att_end + 3 * RWKV_WIDTH
    lora_end = att_end + RWKV_COLS
    rkv = [w_in[:, att_end + j * RWKV_WIDTH + qd * QW:att_end + j * RWKV_WIDTH + (qd + 1) * QW]
           for qd in range(N_QUADS) for j in range(3)]
    w_cols = _bf16(jnp.concatenate([w_in[:, :att_end], w_in[:, rkv_end:lora_end]] + rkv + [w_in[:, lora_end:]],
                                   axis=1))
    blk = (slabs_per_block, tm, LANES)
    return pl.pallas_call(
        _proj_body,
        out_shape=(jax.ShapeDtypeStruct((N_SLABS_F32 + slabs_per_block, m, LANES), jnp.float32),
                   jax.ShapeDtypeStruct((N_SLABS_BF16, m, LANES), jnp.bfloat16)),
        grid=(m // tm, n_col_blocks),
        in_specs=[
            pl.BlockSpec((tm, D_MODEL), lambda i, j: (i, 0)),
            pl.BlockSpec((1, D_MODEL), lambda i, j: (0, 0)),
            pl.BlockSpec((D_MODEL, PROJ_TN), lambda i, j: (0, j)),
        ],
        out_specs=(pl.BlockSpec(blk, lambda i, j: (jnp.minimum(j, n_f32_blocks), i, 0)),
                   pl.BlockSpec(blk, lambda i, j: (jnp.maximum(j - n_f32_blocks, 0), i, 0))),
        scratch_shapes=[pltpu.VMEM((tm, D_MODEL), jnp.bfloat16)],
        compiler_params=pltpu.CompilerParams(
            dimension_semantics=("parallel", "arbitrary"), vmem_limit_bytes=VMEM_LIMIT),
        name="proj",
    )(x2, g, w_cols)


def _attn_units(qs, ks, vs, biases, lo):
    scale = HEAD_DIM ** -0.5

    def logits(q, k):
        qs_ = q * scale
        zero = jnp.zeros_like(qs_)
        lhs = _bf16(jnp.concatenate([jnp.where(lo, qs_, zero), jnp.where(lo, zero, qs_)], axis=0))
        return _dot_nt(lhs, _bf16(k))

    s = [logits(q, k) for q, k in zip(qs, ks)]
    s0 = [x[:QB] + b[0] for x, b in zip(s, biases)]
    s1 = [x[QB:] + b[1] for x, b in zip(s, biases)]
    m0 = [jnp.max(x, axis=-1, keepdims=True) for x in s0]
    m1 = [jnp.max(x, axis=-1, keepdims=True) for x in s1]
    p0 = [_bf16(jnp.exp(x - m)) for x, m in zip(s0, m0)]
    p1 = [_bf16(jnp.exp(x - m)) for x, m in zip(s1, m1)]

    def weighted(p0_, p1_, v):
        zv = jnp.zeros_like(v)
        ones = jnp.ones_like(v)
        rhs0 = _bf16(jnp.concatenate([jnp.where(lo, v, zv), jnp.where(lo, ones, zv)], axis=1))
        rhs1 = _bf16(jnp.concatenate([jnp.where(lo, zv, v), jnp.where(lo, zv, ones)], axis=1))
        return _dot(p0_, rhs0) + _dot(p1_, rhs1)

    ol = [weighted(a, b, v) for a, b, v in zip(p0, p1, vs)]
    out = []
    for x, a, b in zip(ol, m0, m1):
        l = x[:, LANES:]
        out.append((x[:, :LANES] / l, jnp.where(lo, a, b) + jnp.log(l)))
    return out


def _attn_body(q_ref, kc_ref, kp_ref, vc_ref, vp_ref, bias_ref, o_ref, o_scr, l_scr):
    c = pl.program_id(1)
    g = pl.program_id(2)
    lo = lax.broadcasted_iota(jnp.int32, (1, LANES), 1) < HEAD_DIM
    before_start = (c == 0) & (lax.broadcasted_iota(jnp.int32, (QB, 2 * QB), 1) < QB)

    def rows(start, n, d):
        return pl.ds(start, n) if d == 1 else pl.ds(start, n, stride=d)

    def group(gi):
        d = DILATIONS[gi]
        n_qb = ATT_CHUNK // (QB * d)

        def run(units):
            qs = [q_ref[p, rows(start_q, QB, d), :] for p, start_q, _, _, _ in units]

            def bias(head, is_first):
                tile = bias_ref[head]
                return jnp.where(before_start, NEG, tile) if is_first else tile

            biases = [(bias(gi * HEADS_PER_GROUP + 2 * p, is_first), bias(gi * HEADS_PER_GROUP + 2 * p + 1, is_first))
                      for p, _, _, _, is_first in units]
            res = _attn_units(qs, [u[2] for u in units], [u[3] for u in units], biases, lo)
            for (p, start_q, _, _, _), (o, lse) in zip(units, res):
                o_scr[gi, p, rows(start_q, QB, d), :] = o
                l_scr[gi, p, rows(start_q, QB, d), :] = lse

        n_first = max(j for j in range(1, ATT_UNROLL + 1) if d % j == 0)

        def first_blocks(it, carry):
            units = []
            for j in range(n_first):
                r = it * n_first + j
                prev_start = r + QB * d * (n_qb - 1)
                for p in range(2):
                    k = jnp.concatenate([kp_ref[p, rows(prev_start, QB, d), :], kc_ref[p, rows(r, QB, d), :]], axis=0)
                    v = jnp.concatenate([vp_ref[p, rows(prev_start, QB, d), :], vc_ref[p, rows(r, QB, d), :]], axis=0)
                    units.append((p, r, k, v, True))
            run(units)
            return carry

        lax.fori_loop(0, d // n_first, first_blocks, 0)

        n_later = d * (n_qb - 1)
        if n_later:
            n_par = max(j for j in range(1, ATT_UNROLL + 1) if n_later % j == 0)

            def later_blocks(it, carry):
                units = []
                for j in range(n_par):
                    u = it * n_par + j
                    start_q = u % d + QB * d * (1 + u // d)
                    for p in range(2):
                        k = kc_ref[p, rows(start_q - QB * d, 2 * QB, d), :]
                        v = vc_ref[p, rows(start_q - QB * d, 2 * QB, d), :]
                        units.append((p, start_q, k, v, False))
                run(units)
                return carry

            lax.fori_loop(0, n_later // n_par, later_blocks, 0)

    for gi in range(N_GROUPS):
        pl.when(g == gi)(functools.partial(group, gi))

    @pl.when(g == N_GROUPS - 1)
    def _():
        tile = 256

        def comb(i, carry):
            rs = pl.ds(pl.multiple_of(i * tile, tile), tile)
            for p in range(2):
                ls = [l_scr[gi, p, rs, :] for gi in range(N_GROUPS)]
                mx = jnp.maximum(jnp.maximum(ls[0], ls[1]), ls[2])
                ws = [jnp.exp(l - mx) for l in ls]
                num = ws[0] * o_scr[0, p, rs, :] + ws[1] * o_scr[1, p, rs, :] + ws[2] * o_scr[2, p, rs, :]
                o_ref[rs, p * LANES:(p + 1) * LANES] = _bf16(num / (ws[0] + ws[1] + ws[2]))
            return carry

        lax.fori_loop(0, ATT_CHUNK // tile, comb, 0)


def _attn(slabs, bias_tiles, batch, seq):
    m = slabs.shape[1]
    n_chunks = seq // ATT_CHUNK
    blk = (2, ATT_CHUNK, LANES)

    def cur(slab0):
        return pl.BlockSpec(blk, lambda b, c, g: (slab0 // 2 + g, b * n_chunks + c, 0))

    def prev(slab0):
        return pl.BlockSpec(blk, lambda b, c, g: (slab0 // 2 + g, b * n_chunks + jnp.maximum(c - 1, 0), 0))

    return pl.pallas_call(
        _attn_body,
        out_shape=jax.ShapeDtypeStruct((m, ATT_OUT_WIDTH), jnp.bfloat16),
        grid=(batch, n_chunks, N_GROUPS),
        in_specs=[cur(Q_SLAB0), cur(K_SLAB0), prev(K_SLAB0), cur(V_SLAB0), prev(V_SLAB0),
                  pl.BlockSpec(bias_tiles.shape, lambda b, c, g: (0, 0, 0))],
        out_specs=pl.BlockSpec((ATT_CHUNK, ATT_OUT_WIDTH), lambda b, c, g: (b * n_chunks + c, 0)),
        scratch_shapes=[pltpu.VMEM((N_GROUPS, 2, ATT_CHUNK, LANES), jnp.float32),
                        pltpu.VMEM((N_GROUPS, 2, ATT_CHUNK, LANES), jnp.float32)],
        compiler_params=pltpu.CompilerParams(
            dimension_semantics=("parallel", "parallel", "arbitrary"), vmem_limit_bytes=VMEM_LIMIT),
        name="attn",
    )(slabs, slabs, slabs, slabs, slabs, bias_tiles)


def _t5_bucket(dist):
    max_exact = N_BUCKETS // 2
    d_f = jnp.maximum(dist, 1).astype(jnp.float32)
    large = max_exact + (jnp.log(d_f / max_exact) / math.log(MAX_DISTANCE / max_exact)
                         * (N_BUCKETS - max_exact)).astype(jnp.int32)
    large = jnp.minimum(large, N_BUCKETS - 1)
    return jnp.where(dist < max_exact, dist, large)


def _bias_tiles(rel_bias):
    dil = jnp.array(DILATIONS, jnp.int32)
    dist = dil[:, None] * jnp.arange(KEYS_PER_QUERY + 1, dtype=jnp.int32)[None, :]
    bucket = _t5_bucket(dist)
    bias = rel_bias.reshape(N_BUCKETS, N_GROUPS, HEADS_PER_GROUP)[bucket, jnp.arange(N_GROUPS)[:, None]]
    bias = jnp.transpose(bias, (0, 2, 1)).astype(jnp.float32).reshape(ATT_HEADS, KEYS_PER_QUERY + 1)
    n = 3 * QB - 1
    neg = lambda w: jnp.full((ATT_HEADS, w), NEG, jnp.float32)
    e = jnp.concatenate([neg(QB - 1), bias[:, ::-1], neg(QB - 1), neg(1)], axis=1)
    e = jnp.roll(e, -(QB - 1), axis=1)
    return jnp.tile(e, (1, QB))[:, :QB * n].reshape(ATT_HEADS, QB, n)[:, :, :2 * QB]


def _interleave(*gens):
    gens = list(gens)
    while gens:
        for gen in list(gens):
            try:
                next(gen)
            except StopIteration:
                gens.remove(gen)


def _rwkv_body(rkv_ref, lora_ref, pra_ref, prs_ref, sh_ref, wl_ref, o_ref,
               carry, h_st, tok_s, post_s, up_s, zq_s, kcbt_s, g_s, vb_s, y_s, *, n_t):
    t_blk = o_ref.shape[0]
    cc = RWKV_CHUNK
    cs = range(t_blk // cc)
    i = pl.program_id(0)
    first_tok = (i % n_t) == 0
    first_seq = ((i + n_t - 2) % n_t) == 0

    @pl.when(i == 0)
    def _():
        for ref in (carry, h_st, tok_s, post_s, up_s, zq_s, kcbt_s, g_s, vb_s):
            ref[...] = jnp.zeros(ref.shape, ref.dtype)

    lane_head = lax.broadcasted_iota(jnp.int32, (QW, QW), 1) // HEAD_DIM
    row_head = lax.broadcasted_iota(jnp.int32, (QW, QW), 0) // HEAD_DIM
    same_head = lane_head == row_head
    head_ones = _bf16(jnp.where(same_head, 1.0, 0.0))
    assert RWKV_QUAD * cc == QW
    chunk_lane_head = lax.broadcasted_iota(jnp.int32, (cc, QW), 1) // HEAD_DIM
    keep = [chunk_lane_head == h for h in range(RWKV_QUAD)]

    def head_sum(x):
        return _dot(_bf16(x), head_ones)

    def stack4(x):
        z = jnp.zeros_like(x)
        return _bf16(jnp.concatenate([jnp.where(keep[h], x, z) for h in range(RWKV_QUAD)], axis=0))

    def each(fn, *lists):
        return [fn(*args) for args in zip(*lists)]

    def split3(x):
        hi = _bf16(x)
        r1 = x - hi.astype(jnp.float32)
        mid = _bf16(r1)
        return hi, mid, _bf16(r1 - mid.astype(jnp.float32))

    def per_token(tok_w):
        pr = pra_ref[0]
        sh = sh_ref[...]

        def shifted(x, idx, mu):
            width = x.shape[1]
            row = lax.broadcasted_iota(jnp.int32, x.shape, 0)
            last = jnp.where(first_tok, jnp.zeros((1, width), jnp.float32), carry[idx, 7:8, :width])
            prev = jnp.where(row == 0, last, pltpu.roll(x, 1, 0))
            carry[idx, :, :width] = x[t_blk - 8:, :]
            return x + (prev - x) * mu

        wide = lambda j: jnp.concatenate([rkv_ref[2 * j], rkv_ref[2 * j + 1]], axis=1).astype(jnp.float32)
        r = shifted(wide(0), 0, pr[0:1])
        k = shifted(wide(1), 1, pr[1:2])
        v = shifted(wide(2), 2, pr[2:3])
        tok_s[tok_w, 0] = r
        tok_s[tok_w, 4] = v
        yield
        f_lora = shifted(lora_ref[0], 3, sh[0:1, :LANES])
        f_g = shifted(lora_ref[1], 4, sh[1:2, :LANES])
        w0, a0, k_k, k_a, r_k = (pr[j:j + 1] for j in range(3, 8))
        wd = w0 + _dot(_bf16(jnp.tanh(f_lora)), wl_ref[0, 0])
        tok_s[tok_w, 5] = -math.exp(-0.5) * jax.nn.sigmoid(wd)
        a = jax.nn.sigmoid(a0 + _dot(_bf16(f_lora), wl_ref[0, 1]))
        tok_s[tok_w, 7] = _dot(_bf16(jax.nn.sigmoid(f_g)), wl_ref[0, 2])
        yield
        kk = k * k_k
        kk = kk * lax.rsqrt(jnp.maximum(head_sum(kk * kk), L2_EPS * L2_EPS))
        tok_s[tok_w, 1] = kk
        tok_s[tok_w, 3] = kk * a
        yield
        k2 = k * (1.0 + (a - 1.0) * k_a)
        tok_s[tok_w, 2] = k2
        tok_s[tok_w, 6] = head_sum(r * k2 * r_k) * v
        yield

    def per_chunk(tok_r, chk_w):
        t_i = lax.broadcasted_iota(jnp.int32, (cc, QW), 0)
        s_i = lax.broadcasted_iota(jnp.int32, (cc, QW), 1) % cc
        strict_lower = t_i > s_i
        lower = t_i >= s_i
        eye_f = jnp.where(t_i == s_i, 1.0, 0.0)
        zero_m = jnp.zeros((cc, QW), jnp.float32)
        ci = lax.broadcasted_iota(jnp.int32, (cc, cc), 0)
        cj = lax.broadcasted_iota(jnp.int32, (cc, cc), 1)
        tri = _bf16(jnp.where(ci >= cj, 1.0, 0.0))
        rows_of = lambda j: [tok_s[tok_r, j, c * cc:(c + 1) * cc, :] for c in cs]
        r, kk, k2, bb, v, lw = (rows_of(j) for j in range(6))

        parts = each(split3, lw)
        big_l = each(lambda p: _dot(tri, p[0]) + _dot(tri, p[1]) + _dot(tri, p[2]), parts)
        yield
        l_end = each(lambda l: l[cc - 1:cc], big_l)
        e_l = each(jnp.exp, big_l)
        e_lm = each(lambda l, w: jnp.exp(l - w), big_l, lw)
        e_nl = each(lambda l: jnp.exp(-l), big_l)
        e_c = each(lambda le, l: jnp.exp(le - l), l_end, big_l)
        a_kk_f = each(lambda x, e: x * e, kk, e_lm)
        a_kk = each(_bf16, a_kk_f)
        a_r_f = each(lambda x, e: x * e, r, e_l)
        a_r = each(_bf16, a_r_f)
        kb = each(lambda xk, xb, e: jnp.concatenate([stack4(xk * e), stack4(xb * e)], axis=0),
                  k2, bb, e_nl)
        vb = each(_bf16, v)
        vst = each(stack4, v)
        yield
        s12 = each(lambda ak, ar, kb_: _dot_nt(jnp.concatenate([ak, ar], axis=0), kb_), a_kk, a_r, kb)
        s1 = each(lambda s: s[:cc], s12)
        s2 = each(lambda s: s[cc:], s12)
        yield
        mk = each(lambda s: _bf16(jnp.where(strict_lower, s[:, :QW], zero_m)), s1)
        mb = each(lambda s: jnp.where(strict_lower, s[:, QW:], zero_m), s1)
        nkb = each(lambda s: _bf16(jnp.concatenate([jnp.where(lower, s[:, :QW], zero_m),
                                                     jnp.where(lower, -s[:, QW:], zero_m)], axis=1)), s2)

        x = each(lambda n: eye_f - jnp.where((t_i // 2) == (s_i // 2), n, zero_m), mb)
        blk = 4
        while blk <= cc:
            half = blk // 2
            sel = ((t_i // blk) == (s_i // blk)) & ((t_i % blk) >= half) & ((s_i % blk) < half)
            xb = each(_bf16, x)
            t1 = each(lambda xc, n: _bf16(_dot(xc, stack4(jnp.where(sel, n, zero_m)))), xb, mb)
            yield
            x = each(lambda xf, t: xf - _dot(t, stack4(xf)), x, t1)
            yield
            blk *= 2
        tinv = each(_bf16, x)

        pm = each(lambda t, ak: _dot(t, stack4(ak)), tinv, a_kk_f)
        mkv = each(lambda m_, v_: _dot(m_, v_), mk, vst)
        yield
        q = each(lambda t, m_: _dot(t, stack4(m_)), tinv, mkv)
        yield
        pmb = each(_bf16, pm)
        u = each(lambda af, n, p: af + _dot(n[:, QW:], stack4(p)), a_r_f, nkb, pm)
        z = each(lambda n, v_, q_: _dot(n, jnp.concatenate([v_, stack4(q_)], axis=0)), nkb, vst, q)
        yield
        for c in cs:
            up_s[chk_w, c] = jnp.concatenate([pmb[c], _bf16(u[c])], axis=0)
            zq_s[chk_w, c, 0] = z[c]
            zq_s[chk_w, c, 1] = q[c]
            vb_s[chk_w, c] = vb[c]
        yield
        kcg = each(lambda xk, xb, e, le: jnp.concatenate(
            [xk * e, -(xb * e), jnp.broadcast_to(jnp.exp(le), (2 * cc, QW))], axis=0).T,
            k2, bb, e_c, l_end)
        for c in cs:
            kcbt_s[chk_w, c] = _bf16(kcg[c][:, :2 * cc])
            g_s[chk_w, c] = kcg[c][:, 2 * cc:]
        yield
        post_s[chk_w, 0] = tok_s[tok_r, 6]
        post_s[chk_w, 1] = tok_s[tok_r, 7]
        yield

    def sequential(chk_r):
        h = jnp.where(first_seq, jnp.zeros((QW, QW), jnp.float32), h_st[...])
        for c in cs:
            ys = _dot(up_s[chk_r, c], _bf16(h))
            sig = ys[:cc] + zq_s[chk_r, c, 1]
            y_s[c * cc:(c + 1) * cc, :] = ys[cc:] + zq_s[chk_r, c, 0]
            yield
            hu = _dot(kcbt_s[chk_r, c], jnp.concatenate([vb_s[chk_r, c], _bf16(sig)], axis=0))
            g = g_s[chk_r, c]
            h = h * jnp.concatenate([g, g], axis=1) + jnp.where(same_head, hu, jnp.zeros_like(hu))
            yield
        h_st[...] = h
        prs = prs_ref[0]
        y = y_s[...]
        mu = head_sum(y) * (1.0 / HEAD_DIM)
        yc = y - mu
        var = head_sum(yc * yc) * (1.0 / HEAD_DIM)
        yn = yc * lax.rsqrt(var + LN_X_EPS) * prs[8:9] + prs[9:10]
        o_ref[...] = _bf16((yn + post_s[chk_r, 0]) * post_s[chk_r, 1])
        yield

    for parity in (0, 1):
        @pl.when(i % 2 == parity)
        def _(parity=parity):
            _interleave(per_chunk(1 - parity, 1 - parity), sequential(parity), per_token(parity))


def _rwkv(slabs32, slabs16, pr, sh, wl, batch, seq):
    m = slabs32.shape[1]
    t_blk = RWKV_T
    n_t = seq // t_blk
    n_c = t_blk // RWKV_CHUNK
    n_blocks = batch * N_QUADS * n_t

    def where(blk):
        blk = jnp.clip(blk, 0, n_blocks - 1)
        seq_id, t = blk // n_t, blk % n_t
        return seq_id % N_QUADS, (seq_id // N_QUADS) * n_t + t

    return pl.pallas_call(
        functools.partial(_rwkv_body, n_t=n_t),
        out_shape=jax.ShapeDtypeStruct((m, RWKV_WIDTH), jnp.bfloat16),
        grid=(n_blocks + 2,),
        in_specs=[pl.BlockSpec((RKV_SLABS, t_blk, LANES), lambda i: (where(i)[0], where(i)[1], 0)),
                  pl.BlockSpec((2, t_blk, LANES), lambda i: (LORA_SLAB // 2, where(i)[1], 0)),
                  pl.BlockSpec((1,) + pr.shape[1:], lambda i: (where(i)[0], 0, 0)),
                  pl.BlockSpec((1,) + pr.shape[1:], lambda i: (where(i - 2)[0], 0, 0)),
                  pl.BlockSpec(sh.shape, lambda i: (0, 0)),
                  pl.BlockSpec((1,) + wl.shape[1:], lambda i: (where(i)[0], 0, 0, 0))],
        out_specs=pl.BlockSpec((t_blk, QW), lambda i: (where(i - 2)[1], where(i - 2)[0])),
        scratch_shapes=[
            pltpu.VMEM((5, 8, QW), jnp.float32),
            pltpu.VMEM((QW, QW), jnp.float32),
            pltpu.VMEM((2, 8, t_blk, QW), jnp.float32),
            pltpu.VMEM((2, 2, t_blk, QW), jnp.float32),
            pltpu.VMEM((2, n_c, 2 * RWKV_CHUNK, QW), jnp.bfloat16),
            pltpu.VMEM((2, n_c, 2, RWKV_CHUNK, QW), jnp.float32),
            pltpu.VMEM((2, n_c, QW, LANES), jnp.bfloat16),
            pltpu.VMEM((2, n_c, QW, LANES), jnp.float32),
            pltpu.VMEM((2, n_c, RWKV_CHUNK, QW), jnp.bfloat16),
            pltpu.VMEM((t_blk, QW), jnp.float32),
        ],
        compiler_params=pltpu.CompilerParams(
            dimension_semantics=("arbitrary",), vmem_limit_bytes=VMEM_LIMIT),
        name="rwkv",
    )(slabs16, slabs32, pr, pr, sh, wl)


def _rwkv_params(shift_mu, w0, a0, k_k, k_a, r_k, ln_w, ln_b, w_w2, w_a2, w_g2):
    def quads(vec):
        return vec.reshape(N_QUADS, QW)

    mu_r, mu_k, mu_v = (quads(shift_mu[j * RWKV_WIDTH:(j + 1) * RWKV_WIDTH]) for j in range(3))
    rows = [mu_r, mu_k, mu_v, quads(w0), quads(a0), quads(k_k), quads(k_a), quads(r_k.reshape(-1)),
            quads(ln_w), quads(ln_b)]
    pr = jnp.stack(rows, axis=1)
    pr = jnp.pad(pr, ((0, 0), (0, 16 - pr.shape[1]), (0, 0)))
    mu_rest = shift_mu[3 * RWKV_WIDTH:]
    sh = jnp.pad(mu_rest.reshape(2, LANES), ((0, 6), (0, QW - LANES)))
    zeros = jnp.zeros((DECAY_LORA, RWKV_WIDTH), jnp.float32)
    wd = jnp.concatenate([w_w2, zeros], axis=0)
    wa = jnp.concatenate([zeros, w_a2], axis=0)
    wl = jnp.stack([wd, wa, w_g2], axis=0)
    wl = wl.reshape(3, LANES, N_QUADS, QW).transpose(2, 0, 1, 3)
    return pr, sh, _bf16(wl)


def _rms(x, g):
    ms = jnp.mean(x * x, axis=-1, keepdims=True)
    return x * lax.rsqrt(ms + NORM_EPS) * g


def _tail_body(x_ref, att_ref, rw_ref, gate0_ref, gate1_ref, bg_ref, wa_ref, wr_ref, wo_ref, gmix_ref,
               gpre_ref, w1_ref, w2_ref, gpost_ref, o_ref, *, tf):
    n_gs = D_MODEL // LANES
    f0 = jnp.concatenate([gate0_ref[s] for s in range(n_gs)], axis=1).astype(jnp.float32)
    f1 = jnp.concatenate([gate1_ref[s] for s in range(n_gs)], axis=1).astype(jnp.float32)
    g0 = jax.nn.sigmoid(f0 + bg_ref[:, :D_MODEL])
    g1 = jax.nn.sigmoid(f1 + bg_ref[:, D_MODEL:])
    merged = g0 * _dot(att_ref[...], wa_ref[...]) + g1 * _dot(rw_ref[...], wr_ref[...])
    z = _dot(_bf16(merged), wo_ref[...])
    x = x_ref[...] + _rms(z, gmix_ref[...])
    h = _bf16(_rms(x, gpre_ref[...]))
    acc = jnp.zeros(x.shape, jnp.float32)
    for c in range(D_FF // tf):
        u = jnp.maximum(_dot(h, w1_ref[:, c * tf:(c + 1) * tf]), 0.0)
        acc = acc + _dot(_bf16(u * u), w2_ref[c * tf:(c + 1) * tf, :])
    o_ref[...] = x + _rms(acc, gpost_ref[...])


def _tail(x2, o_att, o_rwkv, slabs, b_gate, wa, wr, wo, g_mix, g_pre, w1, w2, g_post, tm, tf=1024):
    m = x2.shape[0]
    n_gs = D_MODEL // LANES
    const = lambda shape: pl.BlockSpec(shape, lambda i: (0, 0), pipeline_mode=pl.Buffered(1))
    return pl.pallas_call(
        functools.partial(_tail_body, tf=tf),
        out_shape=jax.ShapeDtypeStruct((m, D_MODEL), jnp.float32),
        grid=(m // tm,),
        in_specs=[
            pl.BlockSpec((tm, D_MODEL), lambda i: (i, 0)),
            pl.BlockSpec((tm, ATT_OUT_WIDTH), lambda i: (i, 0)),
            pl.BlockSpec((tm, RWKV_WIDTH), lambda i: (i, 0)),
            pl.BlockSpec((n_gs, tm, LANES), lambda i: (GATE_SLAB0 // n_gs, i, 0)),
            pl.BlockSpec((n_gs, tm, LANES), lambda i: (GATE_SLAB0 // n_gs + 1, i, 0)),
            const((1, GATE_COLS)), const(wa.shape), const(wr.shape), const(wo.shape), const((1, D_MODEL)),
            const((1, D_MODEL)), const(w1.shape), const(w2.shape), const((1, D_MODEL)),
        ],
        out_specs=pl.BlockSpec((tm, D_MODEL), lambda i: (i, 0)),
        compiler_params=pltpu.CompilerParams(
            dimension_semantics=("parallel",), vmem_limit_bytes=VMEM_LIMIT),
        name="tail",
    )(x2, o_att, o_rwkv, slabs, slabs, b_gate, wa, wr, wo, g_mix, g_pre, w1, w2, g_post)


def kernel(x, rel_bias, norm_mix_pre, norm_mix_post, norm_ffn_pre, norm_ffn_post, w_in, b_gate, shift_mu, w0, w_w2, a0, w_a2, w_g2, k_k, k_a, r_k, ln_x_w, ln_x_b, w_att_branch, w_rwkv_branch, w_out, w_ffn1, w_ffn2):
    batch, seq, d_model = x.shape
    assert d_model == D_MODEL and seq % ATT_CHUNK == 0 and seq % RWKV_T == 0
    m = batch * seq
    tm = ROW_TILE
    assert m % PROJ_TM == 0 and m % tm == 0
    bias_tiles = _bias_tiles(rel_bias)
    row = lambda vec: vec.reshape(1, -1)
    x2 = x.reshape(m, D_MODEL)
    for l in range(w_in.shape[0]):
        slabs32, slabs16 = _proj(x2, row(norm_mix_pre[l]), w_in[l], PROJ_TM)
        o_att = _attn(slabs32, bias_tiles, batch, seq)
        pr, sh, wl = _rwkv_params(shift_mu[l], w0[l], a0[l], k_k[l], k_a[l], r_k[l], ln_x_w[l], ln_x_b[l],
                                  w_w2[l], w_a2[l], w_g2[l])
        o_rwkv = _rwkv(slabs32, slabs16, pr, sh, wl, batch, seq)
        x2 = _tail(x2, o_att, o_rwkv, slabs16, row(b_gate[l]), _bf16(w_att_branch[l]), _bf16(w_rwkv_branch[l]),
                   _bf16(w_out[l]), row(norm_mix_post[l]), row(norm_ffn_pre[l]), _bf16(w_ffn1[l]),
                   _bf16(w_ffn2[l]), row(norm_ffn_post[l]), tm)
    return x2.reshape(batch, seq, D_MODEL)
```

```python
import functools
import math

import jax
import jax.numpy as jnp
from jax import lax
from jax.experimental import pallas as pl
from jax.experimental.pallas import tpu as pltpu

D_MODEL = 1024
HEAD_DIM = 64
DILATIONS = (1, 4, 16)
KEYS_PER_QUERY = 128
N_GROUPS = len(DILATIONS)
HEADS_PER_GROUP = 4
ATT_HEADS = N_GROUPS * HEADS_PER_GROUP
ATT_WIDTH = ATT_HEADS * HEAD_DIM
ATT_OUT_WIDTH = HEADS_PER_GROUP * HEAD_DIM
N_BUCKETS = 32
MAX_DISTANCE = KEYS_PER_QUERY * DILATIONS[-1]
RWKV_WIDTH = D_MODEL
DECAY_LORA = 64
ICLR_LORA = 64
GATE_LORA = 128
RWKV_COLS = 3 * RWKV_WIDTH + DECAY_LORA + ICLR_LORA + GATE_LORA
N_BRANCHES = 2
IN_COLS = 3 * ATT_WIDTH + RWKV_COLS + N_BRANCHES * D_MODEL
D_FF = 4 * D_MODEL
NORM_EPS = 1e-6
LN_X_EPS = 64e-5
L2_EPS = 1e-12

LANES = 128
VMEM_LIMIT = 56 * 1024 * 1024
NEG = -1e30

GATE_COLS = N_BRANCHES * D_MODEL
RWKV_QUAD = 4
QW = RWKV_QUAD * HEAD_DIM
N_QUADS = RWKV_WIDTH // QW
Q_SLAB0 = 0
K_SLAB0 = Q_SLAB0 + ATT_WIDTH // LANES
V_SLAB0 = K_SLAB0 + ATT_WIDTH // LANES
LORA_SLAB = V_SLAB0 + ATT_WIDTH // LANES
N_SLABS_F32 = LORA_SLAB + 2
RKV_SLABS = 3 * QW // LANES
GATE_SLAB0 = N_QUADS * RKV_SLABS
N_SLABS_BF16 = GATE_SLAB0 + GATE_COLS // LANES

PROJ_TN = 2560
PROJ_TM = 1024
ROW_TILE = 512
ATT_CHUNK = KEYS_PER_QUERY * DILATIONS[-1]
QB = KEYS_PER_QUERY
ATT_UNROLL = 3
RWKV_CHUNK = 64
RWKV_T = 512


def _bf16(x):
    return x.astype(jnp.bfloat16)


def _dot(a, b):
    return jnp.dot(a, b, preferred_element_type=jnp.float32)


def _dot_nt(a, b):
    return lax.dot_general(a, b, (((1,), (1,)), ((), ())), preferred_element_type=jnp.float32)


def _proj_body(x_ref, g_ref, w_ref, o32_ref, o16_ref, h_scr):
    @pl.when(pl.program_id(1) == 0)
    def _():
        x = x_ref[...]
        ms = jnp.mean(x * x, axis=-1, keepdims=True)
        h_scr[...] = _bf16(x * lax.rsqrt(ms + NORM_EPS) * g_ref[...])

    acc = _dot(h_scr[...], w_ref[...])
    for s in range(PROJ_TN // LANES):
        o32_ref[s] = acc[:, s * LANES:(s + 1) * LANES]
        o16_ref[s] = _bf16(acc[:, s * LANES:(s + 1) * LANES])


def _proj(x2, g, w_in, tm):
    m = x2.shape[0]
    n_col_blocks = IN_COLS // PROJ_TN
    slabs_per_block = PROJ_TN // LANES
    n_bf16_blocks = N_SLABS_BF16 // slabs_per_block
    att_end = 3 * ATT_WIDTH
    rkv_end = att_end + 3 * RWKV_WIDTH
    lora_end = att_end + RWKV_COLS
    rkv = [w_in[:, att_end + j * RWKV_WIDTH + qd * QW:att_end + j * RWKV_WIDTH + (qd + 1) * QW]
           for qd in range(N_QUADS) for j in range(3)]
    w_cols = _bf16(jnp.concatenate(rkv + [w_in[:, lora_end:], w_in[:, :att_end], w_in[:, rkv_end:lora_end]],
                                   axis=1))
    blk = (slabs_per_block, tm, LANES)
    return pl.pallas_call(
        _proj_body,
        out_shape=(jax.ShapeDtypeStruct((N_SLABS_F32, m, LANES), jnp.float32),
                   jax.ShapeDtypeStruct((N_SLABS_BF16 + slabs_per_block, m, LANES), jnp.bfloat16)),
        grid=(m // tm, n_col_blocks),
        in_specs=[
            pl.BlockSpec((tm, D_MODEL), lambda i, j: (i, 0)),
            pl.BlockSpec((1, D_MODEL), lambda i, j: (0, 0)),
            pl.BlockSpec((D_MODEL, PROJ_TN), lambda i, j: (0, j)),
        ],
        out_specs=(pl.BlockSpec(blk, lambda i, j: (jnp.maximum(j - n_bf16_blocks, 0), i, 0)),
                   pl.BlockSpec(blk, lambda i, j: (jnp.minimum(j, n_bf16_blocks), i, 0))),
        scratch_shapes=[pltpu.VMEM((tm, D_MODEL), jnp.bfloat16)],
        compiler_params=pltpu.CompilerParams(
            dimension_semantics=("parallel", "arbitrary"), vmem_limit_bytes=VMEM_LIMIT),
        name="proj",
    )(x2, g, w_cols)


def _attn_units(qs, ks, vs, biases, lo):
    scale = HEAD_DIM ** -0.5

    def logits(q, k):
        qs_ = q * scale
        zero = jnp.zeros_like(qs_)
        lhs = _bf16(jnp.concatenate([jnp.where(lo, qs_, zero), jnp.where(lo, zero, qs_)], axis=0))
        return _dot_nt(lhs, _bf16(k))

    s = [logits(q, k) for q, k in zip(qs, ks)]
    s0 = [x[:QB] + b[0] for x, b in zip(s, biases)]
    s1 = [x[QB:] + b[1] for x, b in zip(s, biases)]
    m0 = [jnp.max(x, axis=-1, keepdims=True) for x in s0]
    m1 = [jnp.max(x, axis=-1, keepdims=True) for x in s1]
    p0 = [_bf16(jnp.exp(x - m)) for x, m in zip(s0, m0)]
    p1 = [_bf16(jnp.exp(x - m)) for x, m in zip(s1, m1)]

    def weighted(p0_, p1_, v):
        zv = jnp.zeros_like(v)
        ones = jnp.ones_like(v)
        rhs0 = _bf16(jnp.concatenate([jnp.where(lo, v, zv), jnp.where(lo, ones, zv)], axis=1))
        rhs1 = _bf16(jnp.concatenate([jnp.where(lo, zv, v), jnp.where(lo, zv, ones)], axis=1))
        return _dot(p0_, rhs0) + _dot(p1_, rhs1)

    ol = [weighted(a, b, v) for a, b, v in zip(p0, p1, vs)]
    out = []
    for x, a, b in zip(ol, m0, m1):
        l = x[:, LANES:]
        out.append((x[:, :LANES] / l, jnp.where(lo, a, b) + jnp.log(l)))
    return out


def _attn_body(q_ref, kc_ref, kp_ref, vc_ref, vp_ref, bias_ref, o_ref, o_scr, l_scr):
    c = pl.program_id(1)
    g = pl.program_id(2)
    lo = lax.broadcasted_iota(jnp.int32, (1, LANES), 1) < HEAD_DIM
    before_start = (c == 0) & (lax.broadcasted_iota(jnp.int32, (QB, 2 * QB), 1) < QB)

    def rows(start, n, d):
        return pl.ds(start, n) if d == 1 else pl.ds(start, n, stride=d)

    def group(gi):
        d = DILATIONS[gi]
        n_qb = ATT_CHUNK // (QB * d)

        def run(units):
            qs = [q_ref[p, rows(start_q, QB, d), :] for p, start_q, _, _, _ in units]

            def bias(head, is_first):
                tile = bias_ref[head]
                return jnp.where(before_start, NEG, tile) if is_first else tile

            biases = [(bias(gi * HEADS_PER_GROUP + 2 * p, is_first), bias(gi * HEADS_PER_GROUP + 2 * p + 1, is_first))
                      for p, _, _, _, is_first in units]
            res = _attn_units(qs, [u[2] for u in units], [u[3] for u in units], biases, lo)
            for (p, start_q, _, _, _), (o, lse) in zip(units, res):
                o_scr[gi, p, rows(start_q, QB, d), :] = o
                l_scr[gi, p, rows(start_q, QB, d), :] = lse

        n_first = max(j for j in range(1, ATT_UNROLL + 1) if d % j == 0)

        def first_blocks(it, carry):
            units = []
            for j in range(n_first):
                r = it * n_first + j
                prev_start = r + QB * d * (n_qb - 1)
                for p in range(2):
                    k = jnp.concatenate([kp_ref[p, rows(prev_start, QB, d), :], kc_ref[p, rows(r, QB, d), :]], axis=0)
                    v = jnp.concatenate([vp_ref[p, rows(prev_start, QB, d), :], vc_ref[p, rows(r, QB, d), :]], axis=0)
                    units.append((p, r, k, v, True))
            run(units)
            return carry

        lax.fori_loop(0, d // n_first, first_blocks, 0)

        n_later = d * (n_qb - 1)
        if n_later:
            n_par = max(j for j in range(1, ATT_UNROLL + 1) if n_later % j == 0)

            def later_blocks(it, carry):
                units = []
                for j in range(n_par):
                    u = it * n_par + j
                    start_q = u % d + QB * d * (1 + u // d)
                    for p in range(2):
                        k = kc_ref[p, rows(start_q - QB * d, 2 * QB, d), :]
                        v = vc_ref[p, rows(start_q - QB * d, 2 * QB, d), :]
                        units.append((p, start_q, k, v, False))
                run(units)
                return carry

            lax.fori_loop(0, n_later // n_par, later_blocks, 0)

    for gi in range(N_GROUPS):
        pl.when(g == gi)(functools.partial(group, gi))

    @pl.when(g == N_GROUPS - 1)
    def _():
        tile = 256

        def comb(i, carry):
            rs = pl.ds(pl.multiple_of(i * tile, tile), tile)
            for p in range(2):
                ls = [l_scr[gi, p, rs, :] for gi in range(N_GROUPS)]
                mx = jnp.maximum(jnp.maximum(ls[0], ls[1]), ls[2])
                ws = [jnp.exp(l - mx) for l in ls]
                num = ws[0] * o_scr[0, p, rs, :] + ws[1] * o_scr[1, p, rs, :] + ws[2] * o_scr[2, p, rs, :]
                o_ref[rs, p * LANES:(p + 1) * LANES] = _bf16(num / (ws[0] + ws[1] + ws[2]))
            return carry

        lax.fori_loop(0, ATT_CHUNK // tile, comb, 0)


def _attn(slabs, bias_tiles, batch, seq):
    m = slabs.shape[1]
    n_chunks = seq // ATT_CHUNK
    blk = (2, ATT_CHUNK, LANES)

    def cur(slab0):
        return pl.BlockSpec(blk, lambda b, c, g: (slab0 // 2 + g, b * n_chunks + c, 0))

    def prev(slab0):
        return pl.BlockSpec(blk, lambda b, c, g: (slab0 // 2 + g, b * n_chunks + jnp.maximum(c - 1, 0), 0))

    return pl.pallas_call(
        _attn_body,
        out_shape=jax.ShapeDtypeStruct((m, ATT_OUT_WIDTH), jnp.bfloat16),
        grid=(batch, n_chunks, N_GROUPS),
        in_specs=[cur(Q_SLAB0), cur(K_SLAB0), prev(K_SLAB0), cur(V_SLAB0), prev(V_SLAB0),
                  pl.BlockSpec(bias_tiles.shape, lambda b, c, g: (0, 0, 0))],
        out_specs=pl.BlockSpec((ATT_CHUNK, ATT_OUT_WIDTH), lambda b, c, g: (b * n_chunks + c, 0)),
        scratch_shapes=[pltpu.VMEM((N_GROUPS, 2, ATT_CHUNK, LANES), jnp.float32),
                        pltpu.VMEM((N_GROUPS, 2, ATT_CHUNK, LANES), jnp.float32)],
        compiler_params=pltpu.CompilerParams(
            dimension_semantics=("parallel", "parallel", "arbitrary"), vmem_limit_bytes=VMEM_LIMIT),
        name="attn",
    )(slabs, slabs, slabs, slabs, slabs, bias_tiles)


def _t5_bucket(dist):
    max_exact = N_BUCKETS // 2
    d_f = jnp.maximum(dist, 1).astype(jnp.float32)
    large = max_exact + (jnp.log(d_f / max_exact) / math.log(MAX_DISTANCE / max_exact)
                         * (N_BUCKETS - max_exact)).astype(jnp.int32)
    large = jnp.minimum(large, N_BUCKETS - 1)
    return jnp.where(dist < max_exact, dist, large)


def _bias_tiles(rel_bias):
    dil = jnp.array(DILATIONS, jnp.int32)
    dist = dil[:, None] * jnp.arange(KEYS_PER_QUERY + 1, dtype=jnp.int32)[None, :]
    bucket = _t5_bucket(dist)
    bias = rel_bias.reshape(N_BUCKETS, N_GROUPS, HEADS_PER_GROUP)[bucket, jnp.arange(N_GROUPS)[:, None]]
    bias = jnp.transpose(bias, (0, 2, 1)).astype(jnp.float32).reshape(ATT_HEADS, KEYS_PER_QUERY + 1)
    n = 3 * QB - 1
    neg = lambda w: jnp.full((ATT_HEADS, w), NEG, jnp.float32)
    e = jnp.concatenate([neg(QB - 1), bias[:, ::-1], neg(QB - 1), neg(1)], axis=1)
    e = jnp.roll(e, -(QB - 1), axis=1)
    return jnp.tile(e, (1, QB))[:, :QB * n].reshape(ATT_HEADS, QB, n)[:, :, :2 * QB]


def _interleave(*gens):
    gens = list(gens)
    while gens:
        for gen in list(gens):
            try:
                next(gen)
            except StopIteration:
                gens.remove(gen)


def _rwkv_body(rkv_ref, lora_ref, pra_ref, prs_ref, sh_ref, wl_ref, o_ref,
               carry, h_st, tok_s, post_s, up_s, zq_s, kcbt_s, g_s, vb_s, y_s, *, n_t):
    t_blk = o_ref.shape[0]
    cc = RWKV_CHUNK
    cs = range(t_blk // cc)
    i = pl.program_id(0)
    first_tok = (i % n_t) == 0
    first_seq = ((i + n_t - 2) % n_t) == 0

    @pl.when(i == 0)
    def _():
        for ref in (carry, h_st, tok_s, post_s, up_s, zq_s, kcbt_s, g_s, vb_s):
            ref[...] = jnp.zeros(ref.shape, ref.dtype)

    lane_head = lax.broadcasted_iota(jnp.int32, (QW, QW), 1) // HEAD_DIM
    row_head = lax.broadcasted_iota(jnp.int32, (QW, QW), 0) // HEAD_DIM
    same_head = lane_head == row_head
    head_ones = _bf16(jnp.where(same_head, 1.0, 0.0))
    assert RWKV_QUAD * cc == QW
    chunk_lane_head = lax.broadcasted_iota(jnp.int32, (cc, QW), 1) // HEAD_DIM
    keep = [chunk_lane_head == h for h in range(RWKV_QUAD)]

    def head_sum(x):
        return _dot(_bf16(x), head_ones)

    def stack4(x):
        z = jnp.zeros_like(x)
        return _bf16(jnp.concatenate([jnp.where(keep[h], x, z) for h in range(RWKV_QUAD)], axis=0))

    def each(fn, *lists):
        return [fn(*args) for args in zip(*lists)]

    def split3(x):
        hi = _bf16(x)
        r1 = x - hi.astype(jnp.float32)
        mid = _bf16(r1)
        return hi, mid, _bf16(r1 - mid.astype(jnp.float32))

    def per_token(tok_w):
        pr = pra_ref[0]
        sh = sh_ref[...]

        def shifted(x, idx, mu):
            width = x.shape[1]
            row = lax.broadcasted_iota(jnp.int32, x.shape, 0)
            last = jnp.where(first_tok, jnp.zeros((1, width), jnp.float32), carry[idx, 7:8, :width])
            prev = jnp.where(row == 0, last, pltpu.roll(x, 1, 0))
            carry[idx, :, :width] = x[t_blk - 8:, :]
            return x + (prev - x) * mu

        wide = lambda j: jnp.concatenate([rkv_ref[2 * j], rkv_ref[2 * j + 1]], axis=1).astype(jnp.float32)
        r = shifted(wide(0), 0, pr[0:1])
        k = shifted(wide(1), 1, pr[1:2])
        v = shifted(wide(2), 2, pr[2:3])
        tok_s[tok_w, 0] = r
        tok_s[tok_w, 4] = v
        yield
        f_lora = shifted(lora_ref[0], 3, sh[0:1, :LANES])
        f_g = shifted(lora_ref[1], 4, sh[1:2, :LANES])
        w0, a0, k_k, k_a, r_k = (pr[j:j + 1] for j in range(3, 8))
        wd = w0 + _dot(_bf16(jnp.tanh(f_lora)), wl_ref[0, 0])
        tok_s[tok_w, 5] = -math.exp(-0.5) * jax.nn.sigmoid(wd)
        a = jax.nn.sigmoid(a0 + _dot(_bf16(f_lora), wl_ref[0, 1]))
        tok_s[tok_w, 7] = _dot(_bf16(jax.nn.sigmoid(f_g)), wl_ref[0, 2])
        yield
        kk = k * k_k
        kk = kk * lax.rsqrt(jnp.maximum(head_sum(kk * kk), L2_EPS * L2_EPS))
        tok_s[tok_w, 1] = kk
        tok_s[tok_w, 3] = kk * a
        yield
        k2 = k * (1.0 + (a - 1.0) * k_a)
        tok_s[tok_w, 2] = k2
        tok_s[tok_w, 6] = head_sum(r * k2 * r_k) * v
        yield

    def per_chunk(tok_r, chk_w):
        t_i = lax.broadcasted_iota(jnp.int32, (cc, QW), 0)
        s_i = lax.broadcasted_iota(jnp.int32, (cc, QW), 1) % cc
        strict_lower = t_i > s_i
        lower = t_i >= s_i
        eye_f = jnp.where(t_i == s_i, 1.0, 0.0)
        zero_m = jnp.zeros((cc, QW), jnp.float32)
        ci = lax.broadcasted_iota(jnp.int32, (cc, cc), 0)
        cj = lax.broadcasted_iota(jnp.int32, (cc, cc), 1)
        tri = _bf16(jnp.where(ci >= cj, 1.0, 0.0))
        rows_of = lambda j: [tok_s[tok_r, j, c * cc:(c + 1) * cc, :] for c in cs]
        r, kk, k2, bb, v, lw = (rows_of(j) for j in range(6))

        parts = each(split3, lw)
        big_l = each(lambda p: _dot(tri, p[0]) + _dot(tri, p[1]) + _dot(tri, p[2]), parts)
        yield
        l_end = each(lambda l: l[cc - 1:cc], big_l)
        e_l = each(jnp.exp, big_l)
        e_lm = each(lambda l, w: jnp.exp(l - w), big_l, lw)
        e_nl = each(lambda l: jnp.exp(-l), big_l)
        e_c = each(lambda le, l: jnp.exp(le - l), l_end, big_l)
        a_kk_f = each(lambda x, e: x * e, kk, e_lm)
        a_kk = each(_bf16, a_kk_f)
        a_r_f = each(lambda x, e: x * e, r, e_l)
        a_r = each(_bf16, a_r_f)
        kb = each(lambda xk, xb, e: jnp.concatenate([stack4(xk * e), stack4(xb * e)], axis=0),
                  k2, bb, e_nl)
        vb = each(_bf16, v)
        vst = each(stack4, v)
        yield
        s12 = each(lambda ak, ar, kb_: _dot_nt(jnp.concatenate([ak, ar], axis=0), kb_), a_kk, a_r, kb)
        s1 = each(lambda s: s[:cc], s12)
        s2 = each(lambda s: s[cc:], s12)
        yield
        mk = each(lambda s: _bf16(jnp.where(strict_lower, s[:, :QW], zero_m)), s1)
        mb = each(lambda s: jnp.where(strict_lower, s[:, QW:], zero_m), s1)
        nkb = each(lambda s: _bf16(jnp.concatenate([jnp.where(lower, s[:, :QW], zero_m),
                                                     jnp.where(lower, -s[:, QW:], zero_m)], axis=1)), s2)

        x = each(lambda n: eye_f - jnp.where((t_i // 2) == (s_i // 2), n, zero_m), mb)
        blk = 4
        while blk <= cc:
            half = blk // 2
            sel = ((t_i // blk) == (s_i // blk)) & ((t_i % blk) >= half) & ((s_i % blk) < half)
            xb = each(_bf16, x)
            t1 = each(lambda xc, n: _bf16(_dot(xc, stack4(jnp.where(sel, n, zero_m)))), xb, mb)
            yield
            x = each(lambda xf, t: xf - _dot(t, stack4(xf)), x, t1)
            yield
            blk *= 2
        tinv = each(_bf16, x)

        pm = each(lambda t, ak: _dot(t, stack4(ak)), tinv, a_kk_f)
        mkv = each(lambda m_, v_: _dot(m_, v_), mk, vst)
        yield
        q = each(lambda t, m_: _dot(t, stack4(m_)), tinv, mkv)
        yield
        pmb = each(_bf16, pm)
        u = each(lambda af, n, p: af + _dot(n[:, QW:], stack4(p)), a_r_f, nkb, pm)
        z = each(lambda n, v_, q_: _dot(n, jnp.concatenate([v_, stack4(q_)], axis=0)), nkb, vst, q)
        yield
        for c in cs:
            up_s[chk_w, c] = jnp.concatenate([pmb[c], _bf16(u[c])], axis=0)
            zq_s[chk_w, c, 0] = z[c]
            zq_s[chk_w, c, 1] = q[c]
            vb_s[chk_w, c] = vb[c]
        yield
        kcg = each(lambda xk, xb, e, le: jnp.concatenate(
            [xk * e, -(xb * e), jnp.broadcast_to(jnp.exp(le), (2 * cc, QW))], axis=0).T,
            k2, bb, e_c, l_end)
        for c in cs:
            kcbt_s[chk_w, c] = _bf16(kcg[c][:, :2 * cc])
            g_s[chk_w, c] = kcg[c][:, 2 * cc:]
        yield
        post_s[chk_w, 0] = tok_s[tok_r, 6]
        post_s[chk_w, 1] = tok_s[tok_r, 7]
        yield

    def sequential(chk_r):
        h = jnp.where(first_seq, jnp.zeros((QW, QW), jnp.float32), h_st[...])
        for c in cs:
            ys = _dot(up_s[chk_r, c], _bf16(h))
            sig = ys[:cc] + zq_s[chk_r, c, 1]
            y_s[c * cc:(c + 1) * cc, :] = ys[cc:] + zq_s[chk_r, c, 0]
            yield
            hu = _dot(kcbt_s[chk_r, c], jnp.concatenate([vb_s[chk_r, c], _bf16(sig)], axis=0))
            g = g_s[chk_r, c]
            h = h * jnp.concatenate([g, g], axis=1) + jnp.where(same_head, hu, jnp.zeros_like(hu))
            yield
        h_st[...] = h
        prs = prs_ref[0]
        y = y_s[...]
        mu = head_sum(y) * (1.0 / HEAD_DIM)
        yc = y - mu
        var = head_sum(yc * yc) * (1.0 / HEAD_DIM)
        yn = yc * lax.rsqrt(var + LN_X_EPS) * prs[8:9] + prs[9:10]
        o_ref[...] = _bf16((yn + post_s[chk_r, 0]) * post_s[chk_r, 1])
        yield

    for parity in (0, 1):
        @pl.when(i % 2 == parity)
        def _(parity=parity):
            _interleave(per_chunk(1 - parity, 1 - parity), sequential(parity), per_token(parity))


def _rwkv(slabs32, slabs16, pr, sh, wl, batch, seq):
    m = slabs32.shape[1]
    t_blk = RWKV_T
    n_t = seq // t_blk
    n_c = t_blk // RWKV_CHUNK
    n_blocks = batch * N_QUADS * n_t

    def where(blk):
        blk = jnp.clip(blk, 0, n_blocks - 1)
        seq_id, t = blk // n_t, blk % n_t
        return seq_id % N_QUADS, (seq_id // N_QUADS) * n_t + t

    return pl.pallas_call(
        functools.partial(_rwkv_body, n_t=n_t),
        out_shape=jax.ShapeDtypeStruct((m, RWKV_WIDTH), jnp.bfloat16),
        grid=(n_blocks + 2,),
        in_specs=[pl.BlockSpec((RKV_SLABS, t_blk, LANES), lambda i: (where(i)[0], where(i)[1], 0)),
                  pl.BlockSpec((2, t_blk, LANES), lambda i: (LORA_SLAB // 2, where(i)[1], 0)),
                  pl.BlockSpec((1,) + pr.shape[1:], lambda i: (where(i)[0], 0, 0)),
                  pl.BlockSpec((1,) + pr.shape[1:], lambda i: (where(i - 2)[0], 0, 0)),
                  pl.BlockSpec(sh.shape, lambda i: (0, 0)),
                  pl.BlockSpec((1,) + wl.shape[1:], lambda i: (where(i)[0], 0, 0, 0))],
        out_specs=pl.BlockSpec((t_blk, QW), lambda i: (where(i - 2)[1], where(i - 2)[0])),
        scratch_shapes=[
            pltpu.VMEM((5, 8, QW), jnp.float32),
            pltpu.VMEM((QW, QW), jnp.float32),
            pltpu.VMEM((2, 8, t_blk, QW), jnp.float32),
            pltpu.VMEM((2, 2, t_blk, QW), jnp.float32),
            pltpu.VMEM((2, n_c, 2 * RWKV_CHUNK, QW), jnp.bfloat16),
            pltpu.VMEM((2, n_c, 2, RWKV_CHUNK, QW), jnp.float32),
            pltpu.VMEM((2, n_c, QW, LANES), jnp.bfloat16),
            pltpu.VMEM((2, n_c, QW, LANES), jnp.float32),
            pltpu.VMEM((2, n_c, RWKV_CHUNK, QW), jnp.bfloat16),
            pltpu.VMEM((t_blk, QW), jnp.float32),
        ],
        compiler_params=pltpu.CompilerParams(
            dimension_semantics=("arbitrary",), vmem_limit_bytes=VMEM_LIMIT),
        name="rwkv",
    )(slabs16, slabs32, pr, pr, sh, wl)


def _rwkv_params(shift_mu, w0, a0, k_k, k_a, r_k, ln_w, ln_b, w_w2, w_a2, w_g2):
    def quads(vec):
        return vec.reshape(N_QUADS, QW)

    mu_r, mu_k, mu_v = (quads(shift_mu[j * RWKV_WIDTH:(j + 1) * RWKV_WIDTH]) for j in range(3))
    rows = [mu_r, mu_k, mu_v, quads(w0), quads(a0), quads(k_k), quads(k_a), quads(r_k.reshape(-1)),
            quads(ln_w), quads(ln_b)]
    pr = jnp.stack(rows, axis=1)
    pr = jnp.pad(pr, ((0, 0), (0, 16 - pr.shape[1]), (0, 0)))
    mu_rest = shift_mu[3 * RWKV_WIDTH:]
    sh = jnp.pad(mu_rest.reshape(2, LANES), ((0, 6), (0, QW - LANES)))
    zeros = jnp.zeros((DECAY_LORA, RWKV_WIDTH), jnp.float32)
    wd = jnp.concatenate([w_w2, zeros], axis=0)
    wa = jnp.concatenate([zeros, w_a2], axis=0)
    wl = jnp.stack([wd, wa, w_g2], axis=0)
    wl = wl.reshape(3, LANES, N_QUADS, QW).transpose(2, 0, 1, 3)
    return pr, sh, _bf16(wl)


def _rms(x, g):
    ms = jnp.mean(x * x, axis=-1, keepdims=True)
    return x * lax.rsqrt(ms + NORM_EPS) * g


def _tail_body(x_ref, att_ref, rw_ref, gate0_ref, gate1_ref, bg_ref, wa_ref, wr_ref, wo_ref, gmix_ref,
               gpre_ref, w1_ref, w2_ref, gpost_ref, o_ref, *, tf):
    n_gs = D_MODEL // LANES
    f0 = jnp.concatenate([gate0_ref[s] for s in range(n_gs)], axis=1).astype(jnp.float32)
    f1 = jnp.concatenate([gate1_ref[s] for s in range(n_gs)], axis=1).astype(jnp.float32)
    g0 = jax.nn.sigmoid(f0 + bg_ref[:, :D_MODEL])
    g1 = jax.nn.sigmoid(f1 + bg_ref[:, D_MODEL:])
    merged = g0 * _dot(att_ref[...], wa_ref[...]) + g1 * _dot(rw_ref[...], wr_ref[...])
    z = _dot(_bf16(merged), wo_ref[...])
    x = x_ref[...] + _rms(z, gmix_ref[...])
    h = _bf16(_rms(x, gpre_ref[...]))
    acc = jnp.zeros(x.shape, jnp.float32)
    for c in range(D_FF // tf):
        u = jnp.maximum(_dot(h, w1_ref[:, c * tf:(c + 1) * tf]), 0.0)
        acc = acc + _dot(_bf16(u * u), w2_ref[c * tf:(c + 1) * tf, :])
    o_ref[...] = x + _rms(acc, gpost_ref[...])


def _tail(x2, o_att, o_rwkv, slabs, b_gate, wa, wr, wo, g_mix, g_pre, w1, w2, g_post, tm, tf=1024):
    m = x2.shape[0]
    n_gs = D_MODEL // LANES
    const = lambda shape: pl.BlockSpec(shape, lambda i: (0, 0), pipeline_mode=pl.Buffered(1))
    return pl.pallas_call(
        functools.partial(_tail_body, tf=tf),
        out_shape=jax.ShapeDtypeStruct((m, D_MODEL), jnp.float32),
        grid=(m // tm,),
        in_specs=[
            pl.BlockSpec((tm, D_MODEL), lambda i: (i, 0)),
            pl.BlockSpec((tm, ATT_OUT_WIDTH), lambda i: (i, 0)),
            pl.BlockSpec((tm, RWKV_WIDTH), lambda i: (i, 0)),
            pl.BlockSpec((n_gs, tm, LANES), lambda i: (GATE_SLAB0 // n_gs, i, 0)),
            pl.BlockSpec((n_gs, tm, LANES), lambda i: (GATE_SLAB0 // n_gs + 1, i, 0)),
            const((1, GATE_COLS)), const(wa.shape), const(wr.shape), const(wo.shape), const((1, D_MODEL)),
            const((1, D_MODEL)), const(w1.shape), const(w2.shape), const((1, D_MODEL)),
        ],
        out_specs=pl.BlockSpec((tm, D_MODEL), lambda i: (i, 0)),
        compiler_params=pltpu.CompilerParams(
            dimension_semantics=("parallel",), vmem_limit_bytes=VMEM_LIMIT),
        name="tail",
    )(x2, o_att, o_rwkv, slabs, slabs, b_gate, wa, wr, wo, g_mix, g_pre, w1, w2, g_post)


def kernel(x, rel_bias, norm_mix_pre, norm_mix_post, norm_ffn_pre, norm_ffn_post, w_in, b_gate, shift_mu, w0, w_w2, a0, w_a2, w_g2, k_k, k_a, r_k, ln_x_w, ln_x_b, w_att_branch, w_rwkv_branch, w_out, w_ffn1, w_ffn2):
    batch, seq, d_model = x.shape
    assert d_model == D_MODEL and seq % ATT_CHUNK == 0 and seq % RWKV_T == 0
    m = batch * seq
    tm = ROW_TILE
    assert m % PROJ_TM == 0 and m % tm == 0
    bias_tiles = _bias_tiles(rel_bias)
    row = lambda vec: vec.reshape(1, -1)
    x2 = x.reshape(m, D_MODEL)
    for l in range(w_in.shape[0]):
        slabs32, slabs16 = _proj(x2, row(norm_mix_pre[l]), w_in[l], PROJ_TM)
        o_att = _attn(slabs32, bias_tiles, batch, seq)
        pr, sh, wl = _rwkv_params(shift_mu[l], w0[l], a0[l], k_k[l], k_a[l], r_k[l], ln_x_w[l], ln_x_b[l],
                                  w_w2[l], w_a2[l], w_g2[l])
        o_rwkv = _rwkv(slabs32, slabs16, pr, sh, wl, batch, seq)
        x2 = _tail(x2, o_att, o_rwkv, slabs16, row(b_gate[l]), _bf16(w_att_branch[l]), _bf16(w_rwkv_branch[l]),
                   _bf16(w_out[l]), row(norm_mix_post[l]), row(norm_ffn_pre[l]), _bf16(w_ffn1[l]),
                   _bf16(w_ffn2[l]), row(norm_ffn_post[l]), tm)
    return x2.reshape(batch, seq, D_MODEL)
```

```python
import functools
import math

import jax
import jax.numpy as jnp
from jax import lax
from jax.experimental import pallas as pl
from jax.experimental.pallas import tpu as pltpu

D_MODEL = 1024
HEAD_DIM = 64
DILATIONS = (1, 4, 16)
KEYS_PER_QUERY = 128
N_GROUPS = len(DILATIONS)
HEADS_PER_GROUP = 4
ATT_HEADS = N_GROUPS * HEADS_PER_GROUP
ATT_WIDTH = ATT_HEADS * HEAD_DIM
ATT_OUT_WIDTH = HEADS_PER_GROUP * HEAD_DIM
N_BUCKETS = 32
MAX_DISTANCE = KEYS_PER_QUERY * DILATIONS[-1]
RWKV_WIDTH = D_MODEL
DECAY_LORA = 64
ICLR_LORA = 64
GATE_LORA = 128
RWKV_COLS = 3 * RWKV_WIDTH + DECAY_LORA + ICLR_LORA + GATE_LORA
N_BRANCHES = 2
IN_COLS = 3 * ATT_WIDTH + RWKV_COLS + N_BRANCHES * D_MODEL
D_FF = 4 * D_MODEL
NORM_EPS = 1e-6
LN_X_EPS = 64e-5
L2_EPS = 1e-12

LANES = 128
VMEM_LIMIT = 56 * 1024 * 1024
NEG = -1e30

GATE_COLS = N_BRANCHES * D_MODEL
RWKV_QUAD = 4
QW = RWKV_QUAD * HEAD_DIM
N_QUADS = RWKV_WIDTH // QW
Q_SLAB0 = 0
K_SLAB0 = Q_SLAB0 + ATT_WIDTH // LANES
V_SLAB0 = K_SLAB0 + ATT_WIDTH // LANES
LORA_SLAB = V_SLAB0 + ATT_WIDTH // LANES
N_SLABS_F32 = LORA_SLAB + 2
RKV_SLABS = 3 * QW // LANES
GATE_SLAB0 = N_QUADS * RKV_SLABS
N_SLABS_BF16 = GATE_SLAB0 + GATE_COLS // LANES

PROJ_TN = 2560
PROJ_TM = 1024
ROW_TILE = 512
ATT_CHUNK = KEYS_PER_QUERY * DILATIONS[-1]
QB = KEYS_PER_QUERY
ATT_UNROLL = 3
RWKV_CHUNK = 64
RWKV_T = 512


def _bf16(x):
    return x.astype(jnp.bfloat16)


def _dot(a, b):
    return jnp.dot(a, b, preferred_element_type=jnp.float32)


def _dot_nt(a, b):
    return lax.dot_general(a, b, (((1,), (1,)), ((), ())), preferred_element_type=jnp.float32)


def _proj_body(x_ref, g_ref, w_ref, o32_ref, o16_ref, h_scr):
    @pl.when(pl.program_id(1) == 0)
    def _():
        x = x_ref[...]
        ms = jnp.mean(x * x, axis=-1, keepdims=True)
        h_scr[...] = _bf16(x * lax.rsqrt(ms + NORM_EPS) * g_ref[...])

    acc = _dot(h_scr[...], w_ref[...])
    for s in range(PROJ_TN // LANES):
        o32_ref[s] = acc[:, s * LANES:(s + 1) * LANES]
        o16_ref[s] = _bf16(acc[:, s * LANES:(s + 1) * LANES])


def _proj(x2, g, w_in, tm):
    m = x2.shape[0]
    n_col_blocks = IN_COLS // PROJ_TN
    slabs_per_block = PROJ_TN // LANES
    n_bf16_blocks = N_SLABS_BF16 // slabs_per_block
    att_end = 3 * ATT_WIDTH
    rkv_end = att_end + 3 * RWKV_WIDTH
    lora_end = att_end + RWKV_COLS
    rkv = [w_in[:, att_end + j * RWKV_WIDTH + qd * QW:att_end + j * RWKV_WIDTH + (qd + 1) * QW]
           for qd in range(N_QUADS) for j in range(3)]
    w_cols = _bf16(jnp.concatenate(rkv + [w_in[:, lora_end:], w_in[:, :att_end], w_in[:, rkv_end:lora_end]],
                                   axis=1))
    blk = (slabs_per_block, tm, LANES)
    return pl.pallas_call(
        _proj_body,
        out_shape=(jax.ShapeDtypeStruct((N_SLABS_F32, m, LANES), jnp.float32),
                   jax.ShapeDtypeStruct((N_SLABS_BF16 + slabs_per_block, m, LANES), jnp.bfloat16)),
        grid=(m // tm, n_col_blocks),
        in_specs=[
            pl.BlockSpec((tm, D_MODEL), lambda i, j: (i, 0)),
            pl.BlockSpec((1, D_MODEL), lambda i, j: (0, 0)),
            pl.BlockSpec((D_MODEL, PROJ_TN), lambda i, j: (0, j)),
        ],
        out_specs=(pl.BlockSpec(blk, lambda i, j: (jnp.maximum(j - n_bf16_blocks, 0), i, 0)),
                   pl.BlockSpec(blk, lambda i, j: (jnp.minimum(j, n_bf16_blocks), i, 0))),
        scratch_shapes=[pltpu.VMEM((tm, D_MODEL), jnp.bfloat16)],
        compiler_params=pltpu.CompilerParams(
            dimension_semantics=("parallel", "arbitrary"), vmem_limit_bytes=VMEM_LIMIT),
        name="proj",
    )(x2, g, w_cols)


def _attn_units(qs, ks, vs, biases, lo):
    scale = HEAD_DIM ** -0.5

    def logits(q, k):
        qs_ = q * scale
        zero = jnp.zeros_like(qs_)
        lhs = _bf16(jnp.concatenate([jnp.where(lo, qs_, zero), jnp.where(lo, zero, qs_)], axis=0))
        return _dot_nt(lhs, _bf16(k))

    s = [logits(q, k) for q, k in zip(qs, ks)]
    s0 = [x[:QB] + b[0] for x, b in zip(s, biases)]
    s1 = [x[QB:] + b[1] for x, b in zip(s, biases)]
    m0 = [jnp.max(x, axis=-1, keepdims=True) for x in s0]
    m1 = [jnp.max(x, axis=-1, keepdims=True) for x in s1]
    p0 = [_bf16(jnp.exp(x - m)) for x, m in zip(s0, m0)]
    p1 = [_bf16(jnp.exp(x - m)) for x, m in zip(s1, m1)]

    def weighted(p0_, p1_, v):
        zv = jnp.zeros_like(v)
        ones = jnp.ones_like(v)
        rhs0 = _bf16(jnp.concatenate([jnp.where(lo, v, zv), jnp.where(lo, ones, zv)], axis=1))
        rhs1 = _bf16(jnp.concatenate([jnp.where(lo, zv, v), jnp.where(lo, zv, ones)], axis=1))
        return _dot(p0_, rhs0) + _dot(p1_, rhs1)

    ol = [weighted(a, b, v) for a, b, v in zip(p0, p1, vs)]
    out = []
    for x, a, b in zip(ol, m0, m1):
        l = x[:, LANES:]
        out.append((x[:, :LANES] / l, jnp.where(lo, a, b) + jnp.log(l)))
    return out


def _attn_body(q_ref, kc_ref, kp_ref, vc_ref, vp_ref, bias_ref, o_ref, o_scr, l_scr, k_cache, v_cache):
    c = pl.program_id(1)
    g = pl.program_id(2)
    lo = lax.broadcasted_iota(jnp.int32, (1, LANES), 1) < HEAD_DIM
    before_start = (c == 0) & (lax.broadcasted_iota(jnp.int32, (QB, 2 * QB), 1) < QB)

    def rows(start, n, d):
        return pl.ds(start, n) if d == 1 else pl.ds(start, n, stride=d)

    def group(gi):
        d = DILATIONS[gi]
        n_qb = ATT_CHUNK // (QB * d)

        def run(units):
            qs = [q_ref[p, rows(start_q, QB, d), :] for p, start_q, _, _, _ in units]

            def bias(head, is_first):
                tile = bias_ref[head]
                return jnp.where(before_start, NEG, tile) if is_first else tile

            biases = [(bias(gi * HEADS_PER_GROUP + 2 * p, is_first), bias(gi * HEADS_PER_GROUP + 2 * p + 1, is_first))
                      for p, _, _, _, is_first in units]
            res = _attn_units(qs, [u[2] for u in units], [u[3] for u in units], biases, lo)
            for (p, start_q, _, _, _), (o, lse) in zip(units, res):
                o_scr[gi, p, rows(start_q, QB, d), :] = o
                l_scr[gi, p, rows(start_q, QB, d), :] = lse

        n_first = max(j for j in range(1, ATT_UNROLL + 1) if d % j == 0)
        cached = n_qb == 1
        if cached:
            @pl.when(c == 0)
            def _():
                k_cache[...] = jnp.zeros(k_cache.shape, k_cache.dtype)
                v_cache[...] = jnp.zeros(v_cache.shape, v_cache.dtype)

        def first_blocks(it, carry):
            units = []
            for j in range(n_first):
                r = it * n_first + j
                prev_start = r + QB * d * (n_qb - 1)
                for p in range(2):
                    k_cur = kc_ref[p, rows(r, QB, d), :]
                    v_cur = vc_ref[p, rows(r, QB, d), :]
                    if cached:
                        slot = pl.ds(pl.multiple_of(r * QB, QB), QB)
                        k = jnp.concatenate([k_cache[p, slot, :], _bf16(k_cur)], axis=0)
                        v = jnp.concatenate([v_cache[p, slot, :], v_cur], axis=0)
                        k_cache[p, slot, :] = _bf16(k_cur)
                        v_cache[p, slot, :] = v_cur
                    else:
                        k = jnp.concatenate([kp_ref[p, rows(prev_start, QB, d), :], k_cur], axis=0)
                        v = jnp.concatenate([vp_ref[p, rows(prev_start, QB, d), :], v_cur], axis=0)
                    units.append((p, r, k, v, True))
            run(units)
            return carry

        lax.fori_loop(0, d // n_first, first_blocks, 0)

        n_later = d * (n_qb - 1)
        if n_later:
            n_par = max(j for j in range(1, ATT_UNROLL + 1) if n_later % j == 0)

            def later_blocks(it, carry):
                units = []
                for j in range(n_par):
                    u = it * n_par + j
                    start_q = u % d + QB * d * (1 + u // d)
                    for p in range(2):
                        k = kc_ref[p, rows(start_q - QB * d, 2 * QB, d), :]
                        v = vc_ref[p, rows(start_q - QB * d, 2 * QB, d), :]
                        units.append((p, start_q, k, v, False))
                run(units)
                return carry

            lax.fori_loop(0, n_later // n_par, later_blocks, 0)

    for gi in range(N_GROUPS):
        pl.when(g == gi)(functools.partial(group, gi))

    @pl.when(g == N_GROUPS - 1)
    def _():
        tile = 256

        def comb(i, carry):
            rs = pl.ds(pl.multiple_of(i * tile, tile), tile)
            for p in range(2):
                ls = [l_scr[gi, p, rs, :] for gi in range(N_GROUPS)]
                mx = jnp.maximum(jnp.maximum(ls[0], ls[1]), ls[2])
                ws = [jnp.exp(l - mx) for l in ls]
                num = ws[0] * o_scr[0, p, rs, :] + ws[1] * o_scr[1, p, rs, :] + ws[2] * o_scr[2, p, rs, :]
                o_ref[rs, p * LANES:(p + 1) * LANES] = _bf16(num / (ws[0] + ws[1] + ws[2]))
            return carry

        lax.fori_loop(0, ATT_CHUNK // tile, comb, 0)


def _attn(slabs, bias_tiles, batch, seq):
    m = slabs.shape[1]
    n_chunks = seq // ATT_CHUNK
    blk = (2, ATT_CHUNK, LANES)

    def cur(slab0):
        return pl.BlockSpec(blk, lambda b, c, g: (slab0 // 2 + g, b * n_chunks + c, 0))

    def prev(slab0):
        return pl.BlockSpec(blk, lambda b, c, g: (slab0 // 2 + g, b * n_chunks + jnp.maximum(c - 1, 0), 0))

    return pl.pallas_call(
        _attn_body,
        out_shape=jax.ShapeDtypeStruct((m, ATT_OUT_WIDTH), jnp.bfloat16),
        grid=(batch, n_chunks, N_GROUPS),
        in_specs=[cur(Q_SLAB0), cur(K_SLAB0), prev(K_SLAB0), cur(V_SLAB0), prev(V_SLAB0),
                  pl.BlockSpec(bias_tiles.shape, lambda b, c, g: (0, 0, 0))],
        out_specs=pl.BlockSpec((ATT_CHUNK, ATT_OUT_WIDTH), lambda b, c, g: (b * n_chunks + c, 0)),
        scratch_shapes=[pltpu.VMEM((N_GROUPS, 2, ATT_CHUNK, LANES), jnp.float32),
                        pltpu.VMEM((N_GROUPS, 2, ATT_CHUNK, LANES), jnp.float32),
                        pltpu.VMEM((2, ATT_CHUNK, LANES), jnp.bfloat16),
                        pltpu.VMEM((2, ATT_CHUNK, LANES), jnp.float32)],
        compiler_params=pltpu.CompilerParams(
            dimension_semantics=("parallel", "arbitrary", "arbitrary"), vmem_limit_bytes=VMEM_LIMIT),
        name="attn",
    )(slabs, slabs, slabs, slabs, slabs, bias_tiles)


def _t5_bucket(dist):
    max_exact = N_BUCKETS // 2
    d_f = jnp.maximum(dist, 1).astype(jnp.float32)
    large = max_exact + (jnp.log(d_f / max_exact) / math.log(MAX_DISTANCE / max_exact)
                         * (N_BUCKETS - max_exact)).astype(jnp.int32)
    large = jnp.minimum(large, N_BUCKETS - 1)
    return jnp.where(dist < max_exact, dist, large)


def _bias_tiles(rel_bias):
    dil = jnp.array(DILATIONS, jnp.int32)
    dist = dil[:, None] * jnp.arange(KEYS_PER_QUERY + 1, dtype=jnp.int32)[None, :]
    bucket = _t5_bucket(dist)
    bias = rel_bias.reshape(N_BUCKETS, N_GROUPS, HEADS_PER_GROUP)[bucket, jnp.arange(N_GROUPS)[:, None]]
    bias = jnp.transpose(bias, (0, 2, 1)).astype(jnp.float32).reshape(ATT_HEADS, KEYS_PER_QUERY + 1)
    n = 3 * QB - 1
    neg = lambda w: jnp.full((ATT_HEADS, w), NEG, jnp.float32)
    e = jnp.concatenate([neg(QB - 1), bias[:, ::-1], neg(QB - 1), neg(1)], axis=1)
    e = jnp.roll(e, -(QB - 1), axis=1)
    return jnp.tile(e, (1, QB))[:, :QB * n].reshape(ATT_HEADS, QB, n)[:, :, :2 * QB]


def _interleave(*gens):
    gens = list(gens)
    while gens:
        for gen in list(gens):
            try:
                next(gen)
            except StopIteration:
                gens.remove(gen)


def _rwkv_body(rkv_ref, lora_ref, pra_ref, prs_ref, sh_ref, wl_ref, o_ref,
               carry, h_st, tok_s, post_s, up_s, zq_s, kcbt_s, g_s, vb_s, y_s, *, n_t):
    t_blk = o_ref.shape[0]
    cc = RWKV_CHUNK
    cs = range(t_blk // cc)
    i = pl.program_id(0)
    first_tok = (i % n_t) == 0
    first_seq = ((i + n_t - 2) % n_t) == 0

    @pl.when(i == 0)
    def _():
        for ref in (carry, h_st, tok_s, post_s, up_s, zq_s, kcbt_s, g_s, vb_s):
            ref[...] = jnp.zeros(ref.shape, ref.dtype)

    lane_head = lax.broadcasted_iota(jnp.int32, (QW, QW), 1) // HEAD_DIM
    row_head = lax.broadcasted_iota(jnp.int32, (QW, QW), 0) // HEAD_DIM
    same_head = lane_head == row_head
    head_ones = _bf16(jnp.where(same_head, 1.0, 0.0))
    assert RWKV_QUAD * cc == QW
    chunk_lane_head = lax.broadcasted_iota(jnp.int32, (cc, QW), 1) // HEAD_DIM
    keep = [chunk_lane_head == h for h in range(RWKV_QUAD)]

    def head_sum(x):
        return _dot(_bf16(x), head_ones)

    def stack4(x):
        z = jnp.zeros_like(x)
        return _bf16(jnp.concatenate([jnp.where(keep[h], x, z) for h in range(RWKV_QUAD)], axis=0))

    def each(fn, *lists):
        return [fn(*args) for args in zip(*lists)]

    def split3(x):
        hi = _bf16(x)
        r1 = x - hi.astype(jnp.float32)
        mid = _bf16(r1)
        return hi, mid, _bf16(r1 - mid.astype(jnp.float32))

    def per_token(tok_w):
        pr = pra_ref[0]
        sh = sh_ref[...]

        def shifted(x, idx, mu):
            width = x.shape[1]
            row = lax.broadcasted_iota(jnp.int32, x.shape, 0)
            last = jnp.where(first_tok, jnp.zeros((1, width), jnp.float32), carry[idx, 7:8, :width])
            prev = jnp.where(row == 0, last, pltpu.roll(x, 1, 0))
            carry[idx, :, :width] = x[t_blk - 8:, :]
            return x + (prev - x) * mu

        wide = lambda j: jnp.concatenate([rkv_ref[2 * j], rkv_ref[2 * j + 1]], axis=1).astype(jnp.float32)
        r = shifted(wide(0), 0, pr[0:1])
        k = shifted(wide(1), 1, pr[1:2])
        v = shifted(wide(2), 2, pr[2:3])
        tok_s[tok_w, 0] = r
        tok_s[tok_w, 4] = v
        yield
        f_lora = shifted(lora_ref[0], 3, sh[0:1, :LANES])
        f_g = shifted(lora_ref[1], 4, sh[1:2, :LANES])
        w0, a0, k_k, k_a, r_k = (pr[j:j + 1] for j in range(3, 8))
        wd = w0 + _dot(_bf16(jnp.tanh(f_lora)), wl_ref[0, 0])
        tok_s[tok_w, 5] = -math.exp(-0.5) * jax.nn.sigmoid(wd)
        a = jax.nn.sigmoid(a0 + _dot(_bf16(f_lora), wl_ref[0, 1]))
        tok_s[tok_w, 7] = _dot(_bf16(jax.nn.sigmoid(f_g)), wl_ref[0, 2])
        yield
        kk = k * k_k
        kk = kk * lax.rsqrt(jnp.maximum(head_sum(kk * kk), L2_EPS * L2_EPS))
        tok_s[tok_w, 1] = kk
        tok_s[tok_w, 3] = kk * a
        yield
        k2 = k * (1.0 + (a - 1.0) * k_a)
        tok_s[tok_w, 2] = k2
        tok_s[tok_w, 6] = head_sum(r * k2 * r_k) * v
        yield

    def per_chunk(tok_r, chk_w):
        t_i = lax.broadcasted_iota(jnp.int32, (cc, QW), 0)
        s_i = lax.broadcasted_iota(jnp.int32, (cc, QW), 1) % cc
        strict_lower = t_i > s_i
        lower = t_i >= s_i
        eye_f = jnp.where(t_i == s_i, 1.0, 0.0)
        zero_m = jnp.zeros((cc, QW), jnp.float32)
        ci = lax.broadcasted_iota(jnp.int32, (cc, cc), 0)
        cj = lax.broadcasted_iota(jnp.int32, (cc, cc), 1)
        tri = _bf16(jnp.where(ci >= cj, 1.0, 0.0))
        rows_of = lambda j: [tok_s[tok_r, j, c * cc:(c + 1) * cc, :] for c in cs]
        r, kk, k2, bb, v, lw = (rows_of(j) for j in range(6))

        parts = each(split3, lw)
        big_l = each(lambda p: _dot(tri, p[0]) + _dot(tri, p[1]) + _dot(tri, p[2]), parts)
        yield
        l_end = each(lambda l: l[cc - 1:cc], big_l)
        e_l = each(jnp.exp, big_l)
        e_lm = each(lambda l, w: jnp.exp(l - w), big_l, lw)
        e_nl = each(lambda l: jnp.exp(-l), big_l)
        e_c = each(lambda le, l: jnp.exp(le - l), l_end, big_l)
        a_kk_f = each(lambda x, e: x * e, kk, e_lm)
        a_kk = each(_bf16, a_kk_f)
        a_r_f = each(lambda x, e: x * e, r, e_l)
        a_r = each(_bf16, a_r_f)
        kb = each(lambda xk, xb, e: jnp.concatenate([stack4(xk * e), stack4(xb * e)], axis=0),
                  k2, bb, e_nl)
        vb = each(_bf16, v)
        vst = each(stack4, v)
        yield
        s12 = each(lambda ak, ar, kb_: _dot_nt(jnp.concatenate([ak, ar], axis=0), kb_), a_kk, a_r, kb)
        s1 = each(lambda s: s[:cc], s12)
        s2 = each(lambda s: s[cc:], s12)
        yield
        mk = each(lambda s: _bf16(jnp.where(strict_lower, s[:, :QW], zero_m)), s1)
        mb = each(lambda s: jnp.where(strict_lower, s[:, QW:], zero_m), s1)
        nkb = each(lambda s: _bf16(jnp.concatenate([jnp.where(lower, s[:, :QW], zero_m),
                                                     jnp.where(lower, -s[:, QW:], zero_m)], axis=1)), s2)

        x = each(lambda n: eye_f - jnp.where((t_i // 2) == (s_i // 2), n, zero_m), mb)
        blk = 4
        while blk <= cc:
            half = blk // 2
            sel = ((t_i // blk) == (s_i // blk)) & ((t_i % blk) >= half) & ((s_i % blk) < half)
            xb = each(_bf16, x)
            t1 = each(lambda xc, n: _bf16(_dot(xc, stack4(jnp.where(sel, n, zero_m)))), xb, mb)
            yield
            x = each(lambda xf, t: xf - _dot(t, stack4(xf)), x, t1)
            yield
            blk *= 2
        tinv = each(_bf16, x)

        pm = each(lambda t, ak: _dot(t, stack4(ak)), tinv, a_kk_f)
        mkv = each(lambda m_, v_: _dot(m_, v_), mk, vst)
        yield
        q = each(lambda t, m_: _dot(t, stack4(m_)), tinv, mkv)
        yield
        pmb = each(_bf16, pm)
        u = each(lambda af, n, p: af + _dot(n[:, QW:], stack4(p)), a_r_f, nkb, pm)
        z = each(lambda n, v_, q_: _dot(n, jnp.concatenate([v_, stack4(q_)], axis=0)), nkb, vst, q)
        yield
        for c in cs:
            up_s[chk_w, c] = jnp.concatenate([pmb[c], _bf16(u[c])], axis=0)
            zq_s[chk_w, c, 0] = z[c]
            zq_s[chk_w, c, 1] = q[c]
            vb_s[chk_w, c] = vb[c]
        yield
        kcg = each(lambda xk, xb, e, le: jnp.concatenate(
            [xk * e, -(xb * e), jnp.broadcast_to(jnp.exp(le), (2 * cc, QW))], axis=0).T,
            k2, bb, e_c, l_end)
        for c in cs:
            kcbt_s[chk_w, c] = _bf16(kcg[c][:, :2 * cc])
            g_s[chk_w, c] = kcg[c][:, 2 * cc:]
        yield
        post_s[chk_w, 0] = tok_s[tok_r, 6]
        post_s[chk_w, 1] = tok_s[tok_r, 7]
        yield

    def sequential(chk_r):
        h = jnp.where(first_seq, jnp.zeros((QW, QW), jnp.float32), h_st[...])
        for c in cs:
            ys = _dot(up_s[chk_r, c], _bf16(h))
            sig = ys[:cc] + zq_s[chk_r, c, 1]
            y_s[c * cc:(c + 1) * cc, :] = ys[cc:] + zq_s[chk_r, c, 0]
            yield
            hu = _dot(kcbt_s[chk_r, c], jnp.concatenate([vb_s[chk_r, c], _bf16(sig)], axis=0))
            g = g_s[chk_r, c]
            h = h * jnp.concatenate([g, g], axis=1) + jnp.where(same_head, hu, jnp.zeros_like(hu))
            yield
        h_st[...] = h
        prs = prs_ref[0]
        y = y_s[...]
        mu = head_sum(y) * (1.0 / HEAD_DIM)
        yc = y - mu
        var = head_sum(yc * yc) * (1.0 / HEAD_DIM)
        yn = yc * lax.rsqrt(var + LN_X_EPS) * prs[8:9] + prs[9:10]
        o_ref[...] = _bf16((yn + post_s[chk_r, 0]) * post_s[chk_r, 1])
        yield

    for parity in (0, 1):
        @pl.when(i % 2 == parity)
        def _(parity=parity):
            _interleave(per_chunk(1 - parity, 1 - parity), sequential(parity), per_token(parity))


def _rwkv(slabs32, slabs16, pr, sh, wl, batch, seq):
    m = slabs32.shape[1]
    t_blk = RWKV_T
    n_t = seq // t_blk
    n_c = t_blk // RWKV_CHUNK
    n_blocks = batch * N_QUADS * n_t

    def where(blk):
        blk = jnp.clip(blk, 0, n_blocks - 1)
        seq_id, t = blk // n_t, blk % n_t
        return seq_id % N_QUADS, (seq_id // N_QUADS) * n_t + t

    return pl.pallas_call(
        functools.partial(_rwkv_body, n_t=n_t),
        out_shape=jax.ShapeDtypeStruct((m, RWKV_WIDTH), jnp.bfloat16),
        grid=(n_blocks + 2,),
        in_specs=[pl.BlockSpec((RKV_SLABS, t_blk, LANES), lambda i: (where(i)[0], where(i)[1], 0)),
                  pl.BlockSpec((2, t_blk, LANES), lambda i: (LORA_SLAB // 2, where(i)[1], 0)),
                  pl.BlockSpec((1,) + pr.shape[1:], lambda i: (where(i)[0], 0, 0)),
                  pl.BlockSpec((1,) + pr.shape[1:], lambda i: (where(i - 2)[0], 0, 0)),
                  pl.BlockSpec(sh.shape, lambda i: (0, 0)),
                  pl.BlockSpec((1,) + wl.shape[1:], lambda i: (where(i)[0], 0, 0, 0))],
        out_specs=pl.BlockSpec((t_blk, QW), lambda i: (where(i - 2)[1], where(i - 2)[0])),
        scratch_shapes=[
            pltpu.VMEM((5, 8, QW), jnp.float32),
            pltpu.VMEM((QW, QW), jnp.float32),
            pltpu.VMEM((2, 8, t_blk, QW), jnp.float32),
            pltpu.VMEM((2, 2, t_blk, QW), jnp.float32),
            pltpu.VMEM((2, n_c, 2 * RWKV_CHUNK, QW), jnp.bfloat16),
            pltpu.VMEM((2, n_c, 2, RWKV_CHUNK, QW), jnp.float32),
            pltpu.VMEM((2, n_c, QW, LANES), jnp.bfloat16),
            pltpu.VMEM((2, n_c, QW, LANES), jnp.float32),
            pltpu.VMEM((2, n_c, RWKV_CHUNK, QW), jnp.bfloat16),
            pltpu.VMEM((t_blk, QW), jnp.float32),
        ],
        compiler_params=pltpu.CompilerParams(
            dimension_semantics=("arbitrary",), vmem_limit_bytes=VMEM_LIMIT),
        name="rwkv",
    )(slabs16, slabs32, pr, pr, sh, wl)


def _rwkv_params(shift_mu, w0, a0, k_k, k_a, r_k, ln_w, ln_b, w_w2, w_a2, w_g2):
    def quads(vec):
        return vec.reshape(N_QUADS, QW)

    mu_r, mu_k, mu_v = (quads(shift_mu[j * RWKV_WIDTH:(j + 1) * RWKV_WIDTH]) for j in range(3))
    rows = [mu_r, mu_k, mu_v, quads(w0), quads(a0), quads(k_k), quads(k_a), quads(r_k.reshape(-1)),
            quads(ln_w), quads(ln_b)]
    pr = jnp.stack(rows, axis=1)
    pr = jnp.pad(pr, ((0, 0), (0, 16 - pr.shape[1]), (0, 0)))
    mu_rest = shift_mu[3 * RWKV_WIDTH:]
    sh = jnp.pad(mu_rest.reshape(2, LANES), ((0, 6), (0, QW - LANES)))
    zeros = jnp.zeros((DECAY_LORA, RWKV_WIDTH), jnp.float32)
    wd = jnp.concatenate([w_w2, zeros], axis=0)
    wa = jnp.concatenate([zeros, w_a2], axis=0)
    wl = jnp.stack([wd, wa, w_g2], axis=0)
    wl = wl.reshape(3, LANES, N_QUADS, QW).transpose(2, 0, 1, 3)
    return pr, sh, _bf16(wl)


def _rms(x, g):
    ms = jnp.mean(x * x, axis=-1, keepdims=True)
    return x * lax.rsqrt(ms + NORM_EPS) * g


def _tail_body(x_ref, att_ref, rw_ref, gate0_ref, gate1_ref, bg_ref, wa_ref, wr_ref, wo_ref, gmix_ref,
               gpre_ref, w1_ref, w2_ref, gpost_ref, o_ref, *, tf):
    n_gs = D_MODEL // LANES
    f0 = jnp.concatenate([gate0_ref[s] for s in range(n_gs)], axis=1).astype(jnp.float32)
    f1 = jnp.concatenate([gate1_ref[s] for s in range(n_gs)], axis=1).astype(jnp.float32)
    g0 = jax.nn.sigmoid(f0 + bg_ref[:, :D_MODEL])
    g1 = jax.nn.sigmoid(f1 + bg_ref[:, D_MODEL:])
    merged = g0 * _dot(att_ref[...], wa_ref[...]) + g1 * _dot(rw_ref[...], wr_ref[...])
    z = _dot(_bf16(merged), wo_ref[...])
    x = x_ref[...] + _rms(z, gmix_ref[...])
    h = _bf16(_rms(x, gpre_ref[...]))
    acc = jnp.zeros(x.shape, jnp.float32)
    for c in range(D_FF // tf):
        u = jnp.maximum(_dot(h, w1_ref[:, c * tf:(c + 1) * tf]), 0.0)
        acc = acc + _dot(_bf16(u * u), w2_ref[c * tf:(c + 1) * tf, :])
    o_ref[...] = x + _rms(acc, gpost_ref[...])


def _tail(x2, o_att, o_rwkv, slabs, b_gate, wa, wr, wo, g_mix, g_pre, w1, w2, g_post, tm, tf=1024):
    m = x2.shape[0]
    n_gs = D_MODEL // LANES
    const = lambda shape: pl.BlockSpec(shape, lambda i: (0, 0), pipeline_mode=pl.Buffered(1))
    return pl.pallas_call(
        functools.partial(_tail_body, tf=tf),
        out_shape=jax.ShapeDtypeStruct((m, D_MODEL), jnp.float32),
        grid=(m // tm,),
        in_specs=[
            pl.BlockSpec((tm, D_MODEL), lambda i: (i, 0)),
            pl.BlockSpec((tm, ATT_OUT_WIDTH), lambda i: (i, 0)),
            pl.BlockSpec((tm, RWKV_WIDTH), lambda i: (i, 0)),
            pl.BlockSpec((n_gs, tm, LANES), lambda i: (GATE_SLAB0 // n_gs, i, 0)),
            pl.BlockSpec((n_gs, tm, LANES), lambda i: (GATE_SLAB0 // n_gs + 1, i, 0)),
            const((1, GATE_COLS)), const(wa.shape), const(wr.shape), const(wo.shape), const((1, D_MODEL)),
            const((1, D_MODEL)), const(w1.shape), const(w2.shape), const((1, D_MODEL)),
        ],
        out_specs=pl.BlockSpec((tm, D_MODEL), lambda i: (i, 0)),
        compiler_params=pltpu.CompilerParams(
            dimension_semantics=("parallel",), vmem_limit_bytes=VMEM_LIMIT),
        name="tail",
    )(x2, o_att, o_rwkv, slabs, slabs, b_gate, wa, wr, wo, g_mix, g_pre, w1, w2, g_post)


def kernel(x, rel_bias, norm_mix_pre, norm_mix_post, norm_ffn_pre, norm_ffn_post, w_in, b_gate, shift_mu, w0, w_w2, a0, w_a2, w_g2, k_k, k_a, r_k, ln_x_w, ln_x_b, w_att_branch, w_rwkv_branch, w_out, w_ffn1, w_ffn2):
    batch, seq, d_model = x.shape
    assert d_model == D_MODEL and seq % ATT_CHUNK == 0 and seq % RWKV_T == 0
    m = batch * seq
    tm = ROW_TILE
    assert m % PROJ_TM == 0 and m % tm == 0
    bias_tiles = _bias_tiles(rel_bias)
    row = lambda vec: vec.reshape(1, -1)
    x2 = x.reshape(m, D_MODEL)
    for l in range(w_in.shape[0]):
        slabs32, slabs16 = _proj(x2, row(norm_mix_pre[l]), w_in[l], PROJ_TM)
        o_att = _attn(slabs32, bias_tiles, batch, seq)
        pr, sh, wl = _rwkv_params(shift_mu[l], w0[l], a0[l], k_k[l], k_a[l], r_k[l], ln_x_w[l], ln_x_b[l],
                                  w_w2[l], w_a2[l], w_g2[l])
        o_rwkv = _rwkv(slabs32, slabs16, pr, sh, wl, batch, seq)
        x2 = _tail(x2, o_att, o_rwkv, slabs16, row(b_gate[l]), _bf16(w_att_branch[l]), _bf16(w_rwkv_branch[l]),
                   _bf16(w_out[l]), row(norm_mix_post[l]), row(norm_ffn_pre[l]), _bf16(w_ffn1[l]),
                   _bf16(w_ffn2[l]), row(norm_ffn_post[l]), tm)
    return x2.reshape(batch, seq, D_MODEL)
```

```python
import functools
import math

import jax
import jax.numpy as jnp
from jax import lax
from jax.experimental import pallas as pl
from jax.experimental.pallas import tpu as pltpu

D_MODEL = 1024
HEAD_DIM = 64
DILATIONS = (1, 4, 16)
KEYS_PER_QUERY = 128
N_GROUPS = len(DILATIONS)
HEADS_PER_GROUP = 4
ATT_HEADS = N_GROUPS * HEADS_PER_GROUP
ATT_WIDTH = ATT_HEADS * HEAD_DIM
ATT_OUT_WIDTH = HEADS_PER_GROUP * HEAD_DIM
N_BUCKETS = 32
MAX_DISTANCE = KEYS_PER_QUERY * DILATIONS[-1]
RWKV_WIDTH = D_MODEL
DECAY_LORA = 64
ICLR_LORA = 64
GATE_LORA = 128
RWKV_COLS = 3 * RWKV_WIDTH + DECAY_LORA + ICLR_LORA + GATE_LORA
N_BRANCHES = 2
IN_COLS = 3 * ATT_WIDTH + RWKV_COLS + N_BRANCHES * D_MODEL
D_FF = 4 * D_MODEL
NORM_EPS = 1e-6
LN_X_EPS = 64e-5
L2_EPS = 1e-12

LANES = 128
VMEM_LIMIT = 56 * 1024 * 1024
NEG = -1e30

GATE_COLS = N_BRANCHES * D_MODEL
RWKV_QUAD = 4
QW = RWKV_QUAD * HEAD_DIM
N_QUADS = RWKV_WIDTH // QW
Q_SLAB0 = 0
K_SLAB0 = Q_SLAB0 + ATT_WIDTH // LANES
V_SLAB0 = K_SLAB0 + ATT_WIDTH // LANES
LORA_SLAB = V_SLAB0 + ATT_WIDTH // LANES
N_SLABS_F32 = LORA_SLAB + 2
RKV_SLABS = 3 * QW // LANES
GATE_SLAB0 = N_QUADS * RKV_SLABS
N_SLABS_BF16 = GATE_SLAB0 + GATE_COLS // LANES

PROJ_TN = 2560
PROJ_TM = 1024
ROW_TILE = 512
ATT_CHUNK = KEYS_PER_QUERY * DILATIONS[-1]
QB = KEYS_PER_QUERY
ATT_UNROLL = 3
RWKV_CHUNK = 64
RWKV_T = 512


def _bf16(x):
    return x.astype(jnp.bfloat16)


def _dot(a, b):
    return jnp.dot(a, b, preferred_element_type=jnp.float32)


def _dot_nt(a, b):
    return lax.dot_general(a, b, (((1,), (1,)), ((), ())), preferred_element_type=jnp.float32)


def _proj_body(x_ref, g_ref, w_ref, o32_ref, o16_ref, h_scr):
    @pl.when(pl.program_id(1) == 0)
    def _():
        x = x_ref[...]
        ms = jnp.mean(x * x, axis=-1, keepdims=True)
        h_scr[...] = _bf16(x * lax.rsqrt(ms + NORM_EPS) * g_ref[...])

    acc = _dot(h_scr[...], w_ref[...])
    for s in range(PROJ_TN // LANES):
        o32_ref[s] = acc[:, s * LANES:(s + 1) * LANES]
        o16_ref[s] = _bf16(acc[:, s * LANES:(s + 1) * LANES])


def _proj(x2, g, w_in, tm):
    m = x2.shape[0]
    n_col_blocks = IN_COLS // PROJ_TN
    slabs_per_block = PROJ_TN // LANES
    n_bf16_blocks = N_SLABS_BF16 // slabs_per_block
    att_end = 3 * ATT_WIDTH
    rkv_end = att_end + 3 * RWKV_WIDTH
    lora_end = att_end + RWKV_COLS
    rkv = [w_in[:, att_end + j * RWKV_WIDTH + qd * QW:att_end + j * RWKV_WIDTH + (qd + 1) * QW]
           for qd in range(N_QUADS) for j in range(3)]
    w_cols = _bf16(jnp.concatenate(rkv + [w_in[:, lora_end:], w_in[:, :att_end], w_in[:, rkv_end:lora_end]],
                                   axis=1))
    blk = (slabs_per_block, tm, LANES)
    return pl.pallas_call(
        _proj_body,
        out_shape=(jax.ShapeDtypeStruct((N_SLABS_F32, m, LANES), jnp.float32),
                   jax.ShapeDtypeStruct((N_SLABS_BF16 + slabs_per_block, m, LANES), jnp.bfloat16)),
        grid=(m // tm, n_col_blocks),
        in_specs=[
            pl.BlockSpec((tm, D_MODEL), lambda i, j: (i, 0)),
            pl.BlockSpec((1, D_MODEL), lambda i, j: (0, 0)),
            pl.BlockSpec((D_MODEL, PROJ_TN), lambda i, j: (0, j)),
        ],
        out_specs=(pl.BlockSpec(blk, lambda i, j: (jnp.maximum(j - n_bf16_blocks, 0), i, 0)),
                   pl.BlockSpec(blk, lambda i, j: (jnp.minimum(j, n_bf16_blocks), i, 0))),
        scratch_shapes=[pltpu.VMEM((tm, D_MODEL), jnp.bfloat16)],
        compiler_params=pltpu.CompilerParams(
            dimension_semantics=("parallel", "arbitrary"), vmem_limit_bytes=VMEM_LIMIT),
        name="proj",
    )(x2, g, w_cols)


def _attn_units(qs, ks, vs, biases, lo):
    scale = HEAD_DIM ** -0.5

    def logits(q, k):
        qs_ = q * scale
        zero = jnp.zeros_like(qs_)
        lhs = _bf16(jnp.concatenate([jnp.where(lo, qs_, zero), jnp.where(lo, zero, qs_)], axis=0))
        return _dot_nt(lhs, _bf16(k))

    s = [logits(q, k) for q, k in zip(qs, ks)]
    s0 = [x[:QB] + b[0] for x, b in zip(s, biases)]
    s1 = [x[QB:] + b[1] for x, b in zip(s, biases)]
    m0 = [jnp.max(x, axis=-1, keepdims=True) for x in s0]
    m1 = [jnp.max(x, axis=-1, keepdims=True) for x in s1]
    p0 = [_bf16(jnp.exp(x - m)) for x, m in zip(s0, m0)]
    p1 = [_bf16(jnp.exp(x - m)) for x, m in zip(s1, m1)]

    def weighted(p0_, p1_, v):
        rhs = _bf16(jnp.concatenate([v, jnp.ones_like(v)], axis=1))
        x = _dot(jnp.concatenate([p0_, p1_], axis=0), rhs)
        return jnp.where(lo, x[:QB, :LANES], x[QB:, :LANES]), jnp.where(lo, x[:QB, LANES:], x[QB:, LANES:])

    ol = [weighted(a, b, v) for a, b, v in zip(p0, p1, vs)]
    out = []
    for (o, l), a, b in zip(ol, m0, m1):
        out.append((o / l, jnp.where(lo, a, b) + jnp.log(l)))
    return out


def _attn_body(q_ref, kc_ref, kp_ref, vc_ref, vp_ref, bias_ref, o_ref, o_scr, l_scr, k_cache, v_cache):
    c = pl.program_id(1)
    g = pl.program_id(2)
    lo = lax.broadcasted_iota(jnp.int32, (1, LANES), 1) < HEAD_DIM
    before_start = (c == 0) & (lax.broadcasted_iota(jnp.int32, (QB, 2 * QB), 1) < QB)

    def rows(start, n, d):
        return pl.ds(start, n) if d == 1 else pl.ds(start, n, stride=d)

    def group(gi):
        d = DILATIONS[gi]
        n_qb = ATT_CHUNK // (QB * d)

        def run(units):
            qs = [q_ref[p, rows(start_q, QB, d), :] for p, start_q, _, _, _ in units]

            def bias(head, is_first):
                tile = bias_ref[head]
                return jnp.where(before_start, NEG, tile) if is_first else tile

            biases = [(bias(gi * HEADS_PER_GROUP + 2 * p, is_first), bias(gi * HEADS_PER_GROUP + 2 * p + 1, is_first))
                      for p, _, _, _, is_first in units]
            res = _attn_units(qs, [u[2] for u in units], [u[3] for u in units], biases, lo)
            for (p, start_q, _, _, _), (o, lse) in zip(units, res):
                o_scr[gi, p, rows(start_q, QB, d), :] = o
                l_scr[gi, p, rows(start_q, QB, d), :] = lse

        n_first = max(j for j in range(1, ATT_UNROLL + 1) if d % j == 0)
        cached = n_qb == 1
        if cached:
            @pl.when(c == 0)
            def _():
                k_cache[...] = jnp.zeros(k_cache.shape, k_cache.dtype)
                v_cache[...] = jnp.zeros(v_cache.shape, v_cache.dtype)

        def first_blocks(it, carry):
            units = []
            for j in range(n_first):
                r = it * n_first + j
                prev_start = r + QB * d * (n_qb - 1)
                for p in range(2):
                    k_cur = kc_ref[p, rows(r, QB, d), :]
                    v_cur = vc_ref[p, rows(r, QB, d), :]
                    if cached:
                        slot = pl.ds(pl.multiple_of(r * QB, QB), QB)
                        k = jnp.concatenate([k_cache[p, slot, :], _bf16(k_cur)], axis=0)
                        v = jnp.concatenate([v_cache[p, slot, :], v_cur], axis=0)
                        k_cache[p, slot, :] = _bf16(k_cur)
                        v_cache[p, slot, :] = v_cur
                    else:
                        k = jnp.concatenate([kp_ref[p, rows(prev_start, QB, d), :], k_cur], axis=0)
                        v = jnp.concatenate([vp_ref[p, rows(prev_start, QB, d), :], v_cur], axis=0)
                    units.append((p, r, k, v, True))
            run(units)
            return carry

        lax.fori_loop(0, d // n_first, first_blocks, 0)

        n_later = d * (n_qb - 1)
        if n_later:
            n_par = max(j for j in range(1, ATT_UNROLL + 1) if n_later % j == 0)

            def later_blocks(it, carry):
                units = []
                for j in range(n_par):
                    u = it * n_par + j
                    start_q = u % d + QB * d * (1 + u // d)
                    for p in range(2):
                        k = kc_ref[p, rows(start_q - QB * d, 2 * QB, d), :]
                        v = vc_ref[p, rows(start_q - QB * d, 2 * QB, d), :]
                        units.append((p, start_q, k, v, False))
                run(units)
                return carry

            lax.fori_loop(0, n_later // n_par, later_blocks, 0)

    for gi in range(N_GROUPS):
        pl.when(g == gi)(functools.partial(group, gi))

    @pl.when(g == N_GROUPS - 1)
    def _():
        tile = 256

        def comb(i, carry):
            rs = pl.ds(pl.multiple_of(i * tile, tile), tile)
            for p in range(2):
                ls = [l_scr[gi, p, rs, :] for gi in range(N_GROUPS)]
                mx = jnp.maximum(jnp.maximum(ls[0], ls[1]), ls[2])
                ws = [jnp.exp(l - mx) for l in ls]
                num = ws[0] * o_scr[0, p, rs, :] + ws[1] * o_scr[1, p, rs, :] + ws[2] * o_scr[2, p, rs, :]
                o_ref[rs, p * LANES:(p + 1) * LANES] = _bf16(num / (ws[0] + ws[1] + ws[2]))
            return carry

        lax.fori_loop(0, ATT_CHUNK // tile, comb, 0)


def _attn(slabs, bias_tiles, batch, seq):
    m = slabs.shape[1]
    n_chunks = seq // ATT_CHUNK
    blk = (2, ATT_CHUNK, LANES)

    def cur(slab0):
        return pl.BlockSpec(blk, lambda b, c, g: (slab0 // 2 + g, b * n_chunks + c, 0))

    def prev(slab0):
        return pl.BlockSpec(blk, lambda b, c, g: (slab0 // 2 + g, b * n_chunks + jnp.maximum(c - 1, 0), 0))

    return pl.pallas_call(
        _attn_body,
        out_shape=jax.ShapeDtypeStruct((m, ATT_OUT_WIDTH), jnp.bfloat16),
        grid=(batch, n_chunks, N_GROUPS),
        in_specs=[cur(Q_SLAB0), cur(K_SLAB0), prev(K_SLAB0), cur(V_SLAB0), prev(V_SLAB0),
                  pl.BlockSpec(bias_tiles.shape, lambda b, c, g: (0, 0, 0))],
        out_specs=pl.BlockSpec((ATT_CHUNK, ATT_OUT_WIDTH), lambda b, c, g: (b * n_chunks + c, 0)),
        scratch_shapes=[pltpu.VMEM((N_GROUPS, 2, ATT_CHUNK, LANES), jnp.float32),
                        pltpu.VMEM((N_GROUPS, 2, ATT_CHUNK, LANES), jnp.float32),
                        pltpu.VMEM((2, ATT_CHUNK, LANES), jnp.bfloat16),
                        pltpu.VMEM((2, ATT_CHUNK, LANES), jnp.float32)],
        compiler_params=pltpu.CompilerParams(
            dimension_semantics=("parallel", "arbitrary", "arbitrary"), vmem_limit_bytes=VMEM_LIMIT),
        name="attn",
    )(slabs, slabs, slabs, slabs, slabs, bias_tiles)


def _t5_bucket(dist):
    max_exact = N_BUCKETS // 2
    d_f = jnp.maximum(dist, 1).astype(jnp.float32)
    large = max_exact + (jnp.log(d_f / max_exact) / math.log(MAX_DISTANCE / max_exact)
                         * (N_BUCKETS - max_exact)).astype(jnp.int32)
    large = jnp.minimum(large, N_BUCKETS - 1)
    return jnp.where(dist < max_exact, dist, large)


def _bias_tiles(rel_bias):
    dil = jnp.array(DILATIONS, jnp.int32)
    dist = dil[:, None] * jnp.arange(KEYS_PER_QUERY + 1, dtype=jnp.int32)[None, :]
    bucket = _t5_bucket(dist)
    bias = rel_bias.reshape(N_BUCKETS, N_GROUPS, HEADS_PER_GROUP)[bucket, jnp.arange(N_GROUPS)[:, None]]
    bias = jnp.transpose(bias, (0, 2, 1)).astype(jnp.float32).reshape(ATT_HEADS, KEYS_PER_QUERY + 1)
    n = 3 * QB - 1
    neg = lambda w: jnp.full((ATT_HEADS, w), NEG, jnp.float32)
    e = jnp.concatenate([neg(QB - 1), bias[:, ::-1], neg(QB - 1), neg(1)], axis=1)
    e = jnp.roll(e, -(QB - 1), axis=1)
    return jnp.tile(e, (1, QB))[:, :QB * n].reshape(ATT_HEADS, QB, n)[:, :, :2 * QB]


def _interleave(*gens):
    gens = list(gens)
    while gens:
        for gen in list(gens):
            try:
                next(gen)
            except StopIteration:
                gens.remove(gen)


def _rwkv_body(rkv_ref, lora_ref, pra_ref, prs_ref, sh_ref, wl_ref, o_ref,
               carry, h_st, tok_s, post_s, up_s, zq_s, kcbt_s, g_s, vb_s, y_s, *, n_t):
    t_blk = o_ref.shape[0]
    cc = RWKV_CHUNK
    cs = range(t_blk // cc)
    i = pl.program_id(0)
    first_tok = (i % n_t) == 0
    first_seq = ((i + n_t - 2) % n_t) == 0

    @pl.when(i == 0)
    def _():
        for ref in (carry, h_st, tok_s, post_s, up_s, zq_s, kcbt_s, g_s, vb_s):
            ref[...] = jnp.zeros(ref.shape, ref.dtype)

    lane_head = lax.broadcasted_iota(jnp.int32, (QW, QW), 1) // HEAD_DIM
    row_head = lax.broadcasted_iota(jnp.int32, (QW, QW), 0) // HEAD_DIM
    same_head = lane_head == row_head
    head_ones = _bf16(jnp.where(same_head, 1.0, 0.0))
    assert RWKV_QUAD * cc == QW
    chunk_lane_head = lax.broadcasted_iota(jnp.int32, (cc, QW), 1) // HEAD_DIM
    keep = [chunk_lane_head == h for h in range(RWKV_QUAD)]

    def head_sum(x):
        return _dot(_bf16(x), head_ones)

    def stack4(x):
        z = jnp.zeros_like(x)
        return _bf16(jnp.concatenate([jnp.where(keep[h], x, z) for h in range(RWKV_QUAD)], axis=0))

    def each(fn, *lists):
        return [fn(*args) for args in zip(*lists)]

    def split3(x):
        hi = _bf16(x)
        r1 = x - hi.astype(jnp.float32)
        mid = _bf16(r1)
        return hi, mid, _bf16(r1 - mid.astype(jnp.float32))

    def per_token(tok_w):
        pr = pra_ref[0]
        sh = sh_ref[...]

        def shifted(x, idx, mu):
            width = x.shape[1]
            row = lax.broadcasted_iota(jnp.int32, x.shape, 0)
            last = jnp.where(first_tok, jnp.zeros((1, width), jnp.float32), carry[idx, 7:8, :width])
            prev = jnp.where(row == 0, last, pltpu.roll(x, 1, 0))
            carry[idx, :, :width] = x[t_blk - 8:, :]
            return x + (prev - x) * mu

        wide = lambda j: jnp.concatenate([rkv_ref[2 * j], rkv_ref[2 * j + 1]], axis=1).astype(jnp.float32)
        r = shifted(wide(0), 0, pr[0:1])
        k = shifted(wide(1), 1, pr[1:2])
        v = shifted(wide(2), 2, pr[2:3])
        tok_s[tok_w, 0] = r
        tok_s[tok_w, 4] = v
        yield
        f_lora = shifted(lora_ref[0], 3, sh[0:1, :LANES])
        f_g = shifted(lora_ref[1], 4, sh[1:2, :LANES])
        w0, a0, k_k, k_a, r_k = (pr[j:j + 1] for j in range(3, 8))
        wd = w0 + _dot(_bf16(jnp.tanh(f_lora)), wl_ref[0, 0])
        tok_s[tok_w, 5] = -math.exp(-0.5) * jax.nn.sigmoid(wd)
        a = jax.nn.sigmoid(a0 + _dot(_bf16(f_lora), wl_ref[0, 1]))
        tok_s[tok_w, 7] = _dot(_bf16(jax.nn.sigmoid(f_g)), wl_ref[0, 2])
        yield
        kk = k * k_k
        kk = kk * lax.rsqrt(jnp.maximum(head_sum(kk * kk), L2_EPS * L2_EPS))
        tok_s[tok_w, 1] = kk
        tok_s[tok_w, 3] = kk * a
        yield
        k2 = k * (1.0 + (a - 1.0) * k_a)
        tok_s[tok_w, 2] = k2
        tok_s[tok_w, 6] = head_sum(r * k2 * r_k) * v
        yield

    def per_chunk(tok_r, chk_w):
        t_i = lax.broadcasted_iota(jnp.int32, (cc, QW), 0)
        s_i = lax.broadcasted_iota(jnp.int32, (cc, QW), 1) % cc
        strict_lower = t_i > s_i
        lower = t_i >= s_i
        eye_f = jnp.where(t_i == s_i, 1.0, 0.0)
        zero_m = jnp.zeros((cc, QW), jnp.float32)
        ci = lax.broadcasted_iota(jnp.int32, (cc, cc), 0)
        cj = lax.broadcasted_iota(jnp.int32, (cc, cc), 1)
        tri = _bf16(jnp.where(ci >= cj, 1.0, 0.0))
        rows_of = lambda j: [tok_s[tok_r, j, c * cc:(c + 1) * cc, :] for c in cs]
        r, kk, k2, bb, v, lw = (rows_of(j) for j in range(6))

        parts = each(split3, lw)
        big_l = each(lambda p: _dot(tri, p[0]) + _dot(tri, p[1]) + _dot(tri, p[2]), parts)
        yield
        l_end = each(lambda l: l[cc - 1:cc], big_l)
        e_l = each(jnp.exp, big_l)
        e_lm = each(lambda l, w: jnp.exp(l - w), big_l, lw)
        e_nl = each(lambda l: jnp.exp(-l), big_l)
        e_c = each(lambda le, l: jnp.exp(le - l), l_end, big_l)
        a_kk_f = each(lambda x, e: x * e, kk, e_lm)
        a_kk = each(_bf16, a_kk_f)
        a_r_f = each(lambda x, e: x * e, r, e_l)
        a_r = each(_bf16, a_r_f)
        kb = each(lambda xk, xb, e: jnp.concatenate([stack4(xk * e), stack4(xb * e)], axis=0),
                  k2, bb, e_nl)
        vb = each(_bf16, v)
        vst = each(stack4, v)
        yield
        s12 = each(lambda ak, ar, kb_: _dot_nt(jnp.concatenate([ak, ar], axis=0), kb_), a_kk, a_r, kb)
        s1 = each(lambda s: s[:cc], s12)
        s2 = each(lambda s: s[cc:], s12)
        yield
        mk = each(lambda s: _bf16(jnp.where(strict_lower, s[:, :QW], zero_m)), s1)
        mb = each(lambda s: jnp.where(strict_lower, s[:, QW:], zero_m), s1)
        nkb = each(lambda s: _bf16(jnp.concatenate([jnp.where(lower, s[:, :QW], zero_m),
                                                     jnp.where(lower, -s[:, QW:], zero_m)], axis=1)), s2)

        x = each(lambda n: eye_f - jnp.where((t_i // 2) == (s_i // 2), n, zero_m), mb)
        blk = 4
        while blk <= cc:
            half = blk // 2
            sel = ((t_i // blk) == (s_i // blk)) & ((t_i % blk) >= half) & ((s_i % blk) < half)
            xb = each(_bf16, x)
            t1 = each(lambda xc, n: _bf16(_dot(xc, stack4(jnp.where(sel, n, zero_m)))), xb, mb)
            yield
            x = each(lambda xf, t: xf - _dot(t, stack4(xf)), x, t1)
            yield
            blk *= 2
        tinv = each(_bf16, x)

        pm = each(lambda t, ak: _dot(t, stack4(ak)), tinv, a_kk_f)
        mkv = each(lambda m_, v_: _dot(m_, v_), mk, vst)
        yield
        q = each(lambda t, m_: _dot(t, stack4(m_)), tinv, mkv)
        yield
        pmb = each(_bf16, pm)
        u = each(lambda af, n, p: af + _dot(n[:, QW:], stack4(p)), a_r_f, nkb, pm)
        z = each(lambda n, v_, q_: _dot(n, jnp.concatenate([v_, stack4(q_)], axis=0)), nkb, vst, q)
        yield
        for c in cs:
            up_s[chk_w, c] = jnp.concatenate([pmb[c], _bf16(u[c])], axis=0)
            zq_s[chk_w, c, 0] = z[c]
            zq_s[chk_w, c, 1] = q[c]
            vb_s[chk_w, c] = vb[c]
        yield
        kcg = each(lambda xk, xb, e, le: jnp.concatenate(
            [xk * e, -(xb * e), jnp.broadcast_to(jnp.exp(le), (2 * cc, QW))], axis=0).T,
            k2, bb, e_c, l_end)
        for c in cs:
            kcbt_s[chk_w, c] = _bf16(kcg[c][:, :2 * cc])
            g_s[chk_w, c] = kcg[c][:, 2 * cc:]
        yield
        post_s[chk_w, 0] = tok_s[tok_r, 6]
        post_s[chk_w, 1] = tok_s[tok_r, 7]
        yield

    def sequential(chk_r):
        h = jnp.where(first_seq, jnp.zeros((QW, QW), jnp.float32), h_st[...])
        for c in cs:
            ys = _dot(up_s[chk_r, c], _bf16(h))
            sig = ys[:cc] + zq_s[chk_r, c, 1]
            y_s[c * cc:(c + 1) * cc, :] = ys[cc:] + zq_s[chk_r, c, 0]
            yield
            hu = _dot(kcbt_s[chk_r, c], jnp.concatenate([vb_s[chk_r, c], _bf16(sig)], axis=0))
            g = g_s[chk_r, c]
            h = h * jnp.concatenate([g, g], axis=1) + jnp.where(same_head, hu, jnp.zeros_like(hu))
            yield
        h_st[...] = h
        prs = prs_ref[0]
        y = y_s[...]
        mu = head_sum(y) * (1.0 / HEAD_DIM)
        yc = y - mu
        var = head_sum(yc * yc) * (1.0 / HEAD_DIM)
        yn = yc * lax.rsqrt(var + LN_X_EPS) * prs[8:9] + prs[9:10]
        o_ref[...] = _bf16((yn + post_s[chk_r, 0]) * post_s[chk_r, 1])
        yield

    for parity in (0, 1):
        @pl.when(i % 2 == parity)
        def _(parity=parity):
            _interleave(per_chunk(1 - parity, 1 - parity), sequential(parity), per_token(parity))


def _rwkv(slabs32, slabs16, pr, sh, wl, batch, seq):
    m = slabs32.shape[1]
    t_blk = RWKV_T
    n_t = seq // t_blk
    n_c = t_blk // RWKV_CHUNK
    n_blocks = batch * N_QUADS * n_t

    def where(blk):
        blk = jnp.clip(blk, 0, n_blocks - 1)
        seq_id, t = blk // n_t, blk % n_t
        return seq_id % N_QUADS, (seq_id // N_QUADS) * n_t + t

    return pl.pallas_call(
        functools.partial(_rwkv_body, n_t=n_t),
        out_shape=jax.ShapeDtypeStruct((m, RWKV_WIDTH), jnp.bfloat16),
        grid=(n_blocks + 2,),
        in_specs=[pl.BlockSpec((RKV_SLABS, t_blk, LANES), lambda i: (where(i)[0], where(i)[1], 0)),
                  pl.BlockSpec((2, t_blk, LANES), lambda i: (LORA_SLAB // 2, where(i)[1], 0)),
                  pl.BlockSpec((1,) + pr.shape[1:], lambda i: (where(i)[0], 0, 0)),
                  pl.BlockSpec((1,) + pr.shape[1:], lambda i: (where(i - 2)[0], 0, 0)),
                  pl.BlockSpec(sh.shape, lambda i: (0, 0)),
                  pl.BlockSpec((1,) + wl.shape[1:], lambda i: (where(i)[0], 0, 0, 0))],
        out_specs=pl.BlockSpec((t_blk, QW), lambda i: (where(i - 2)[1], where(i - 2)[0])),
        scratch_shapes=[
            pltpu.VMEM((5, 8, QW), jnp.float32),
            pltpu.VMEM((QW, QW), jnp.float32),
            pltpu.VMEM((2, 8, t_blk, QW), jnp.float32),
            pltpu.VMEM((2, 2, t_blk, QW), jnp.float32),
            pltpu.VMEM((2, n_c, 2 * RWKV_CHUNK, QW), jnp.bfloat16),
            pltpu.VMEM((2, n_c, 2, RWKV_CHUNK, QW), jnp.float32),
            pltpu.VMEM((2, n_c, QW, LANES), jnp.bfloat16),
            pltpu.VMEM((2, n_c, QW, LANES), jnp.float32),
            pltpu.VMEM((2, n_c, RWKV_CHUNK, QW), jnp.bfloat16),
            pltpu.VMEM((t_blk, QW), jnp.float32),
        ],
        compiler_params=pltpu.CompilerParams(
            dimension_semantics=("arbitrary",), vmem_limit_bytes=VMEM_LIMIT),
        name="rwkv",
    )(slabs16, slabs32, pr, pr, sh, wl)


def _rwkv_params(shift_mu, w0, a0, k_k, k_a, r_k, ln_w, ln_b, w_w2, w_a2, w_g2):
    def quads(vec):
        return vec.reshape(N_QUADS, QW)

    mu_r, mu_k, mu_v = (quads(shift_mu[j * RWKV_WIDTH:(j + 1) * RWKV_WIDTH]) for j in range(3))
    rows = [mu_r, mu_k, mu_v, quads(w0), quads(a0), quads(k_k), quads(k_a), quads(r_k.reshape(-1)),
            quads(ln_w), quads(ln_b)]
    pr = jnp.stack(rows, axis=1)
    pr = jnp.pad(pr, ((0, 0), (0, 16 - pr.shape[1]), (0, 0)))
    mu_rest = shift_mu[3 * RWKV_WIDTH:]
    sh = jnp.pad(mu_rest.reshape(2, LANES), ((0, 6), (0, QW - LANES)))
    zeros = jnp.zeros((DECAY_LORA, RWKV_WIDTH), jnp.float32)
    wd = jnp.concatenate([w_w2, zeros], axis=0)
    wa = jnp.concatenate([zeros, w_a2], axis=0)
    wl = jnp.stack([wd, wa, w_g2], axis=0)
    wl = wl.reshape(3, LANES, N_QUADS, QW).transpose(2, 0, 1, 3)
    return pr, sh, _bf16(wl)


def _rms(x, g):
    ms = jnp.mean(x * x, axis=-1, keepdims=True)
    return x * lax.rsqrt(ms + NORM_EPS) * g


def _tail_body(x_ref, att_ref, rw_ref, gate0_ref, gate1_ref, bg_ref, wa_ref, wr_ref, wo_ref, gmix_ref,
               gpre_ref, w1_ref, w2_ref, gpost_ref, o_ref, *, tf):
    n_gs = D_MODEL // LANES
    f0 = jnp.concatenate([gate0_ref[s] for s in range(n_gs)], axis=1).astype(jnp.float32)
    f1 = jnp.concatenate([gate1_ref[s] for s in range(n_gs)], axis=1).astype(jnp.float32)
    g0 = jax.nn.sigmoid(f0 + bg_ref[:, :D_MODEL])
    g1 = jax.nn.sigmoid(f1 + bg_ref[:, D_MODEL:])
    merged = g0 * _dot(att_ref[...], wa_ref[...]) + g1 * _dot(rw_ref[...], wr_ref[...])
    z = _dot(_bf16(merged), wo_ref[...])
    x = x_ref[...] + _rms(z, gmix_ref[...])
    h = _bf16(_rms(x, gpre_ref[...]))
    acc = jnp.zeros(x.shape, jnp.float32)
    for c in range(D_FF // tf):
        u = jnp.maximum(_dot(h, w1_ref[:, c * tf:(c + 1) * tf]), 0.0)
        acc = acc + _dot(_bf16(u * u), w2_ref[c * tf:(c + 1) * tf, :])
    o_ref[...] = x + _rms(acc, gpost_ref[...])


def _tail(x2, o_att, o_rwkv, slabs, b_gate, wa, wr, wo, g_mix, g_pre, w1, w2, g_post, tm, tf=1024):
    m = x2.shape[0]
    n_gs = D_MODEL // LANES
    const = lambda shape: pl.BlockSpec(shape, lambda i: (0, 0), pipeline_mode=pl.Buffered(1))
    return pl.pallas_call(
        functools.partial(_tail_body, tf=tf),
        out_shape=jax.ShapeDtypeStruct((m, D_MODEL), jnp.float32),
        grid=(m // tm,),
        in_specs=[
            pl.BlockSpec((tm, D_MODEL), lambda i: (i, 0)),
            pl.BlockSpec((tm, ATT_OUT_WIDTH), lambda i: (i, 0)),
            pl.BlockSpec((tm, RWKV_WIDTH), lambda i: (i, 0)),
            pl.BlockSpec((n_gs, tm, LANES), lambda i: (GATE_SLAB0 // n_gs, i, 0)),
            pl.BlockSpec((n_gs, tm, LANES), lambda i: (GATE_SLAB0 // n_gs + 1, i, 0)),
            const((1, GATE_COLS)), const(wa.shape), const(wr.shape), const(wo.shape), const((1, D_MODEL)),
            const((1, D_MODEL)), const(w1.shape), const(w2.shape), const((1, D_MODEL)),
        ],
        out_specs=pl.BlockSpec((tm, D_MODEL), lambda i: (i, 0)),
        compiler_params=pltpu.CompilerParams(
            dimension_semantics=("parallel",), vmem_limit_bytes=VMEM_LIMIT),
        name="tail",
    )(x2, o_att, o_rwkv, slabs, slabs, b_gate, wa, wr, wo, g_mix, g_pre, w1, w2, g_post)


def kernel(x, rel_bias, norm_mix_pre, norm_mix_post, norm_ffn_pre, norm_ffn_post, w_in, b_gate, shift_mu, w0, w_w2, a0, w_a2, w_g2, k_k, k_a, r_k, ln_x_w, ln_x_b, w_att_branch, w_rwkv_branch, w_out, w_ffn1, w_ffn2):
    batch, seq, d_model = x.shape
    assert d_model == D_MODEL and seq % ATT_CHUNK == 0 and seq % RWKV_T == 0
    m = batch * seq
    tm = ROW_TILE
    assert m % PROJ_TM == 0 and m % tm == 0
    bias_tiles = _bias_tiles(rel_bias)
    row = lambda vec: vec.reshape(1, -1)
    x2 = x.reshape(m, D_MODEL)
    for l in range(w_in.shape[0]):
        slabs32, slabs16 = _proj(x2, row(norm_mix_pre[l]), w_in[l], PROJ_TM)
        o_att = _attn(slabs32, bias_tiles, batch, seq)
        pr, sh, wl = _rwkv_params(shift_mu[l], w0[l], a0[l], k_k[l], k_a[l], r_k[l], ln_x_w[l], ln_x_b[l],
                                  w_w2[l], w_a2[l], w_g2[l])
        o_rwkv = _rwkv(slabs32, slabs16, pr, sh, wl, batch, seq)
        x2 = _tail(x2, o_att, o_rwkv, slabs16, row(b_gate[l]), _bf16(w_att_branch[l]), _bf16(w_rwkv_branch[l]),
                   _bf16(w_out[l]), row(norm_mix_post[l]), row(norm_ffn_pre[l]), _bf16(w_ffn1[l]),
                   _bf16(w_ffn2[l]), row(norm_ffn_post[l]), tm)
    return x2.reshape(batch, seq, D_MODEL)
```

```python
import functools
import math

import jax
import jax.numpy as jnp
from jax import lax
from jax.experimental import pallas as pl
from jax.experimental.pallas import tpu as pltpu

D_MODEL = 1024
HEAD_DIM = 64
DILATIONS = (1, 4, 16)
KEYS_PER_QUERY = 128
N_GROUPS = len(DILATIONS)
HEADS_PER_GROUP = 4
ATT_HEADS = N_GROUPS * HEADS_PER_GROUP
ATT_WIDTH = ATT_HEADS * HEAD_DIM
ATT_OUT_WIDTH = HEADS_PER_GROUP * HEAD_DIM
N_BUCKETS = 32
MAX_DISTANCE = KEYS_PER_QUERY * DILATIONS[-1]
RWKV_WIDTH = D_MODEL
DECAY_LORA = 64
ICLR_LORA = 64
GATE_LORA = 128
RWKV_COLS = 3 * RWKV_WIDTH + DECAY_LORA + ICLR_LORA + GATE_LORA
N_BRANCHES = 2
IN_COLS = 3 * ATT_WIDTH + RWKV_COLS + N_BRANCHES * D_MODEL
D_FF = 4 * D_MODEL
NORM_EPS = 1e-6
LN_X_EPS = 64e-5
L2_EPS = 1e-12

LANES = 128
VMEM_LIMIT = 56 * 1024 * 1024
NEG = -1e30

GATE_COLS = N_BRANCHES * D_MODEL
RWKV_QUAD = 4
QW = RWKV_QUAD * HEAD_DIM
N_QUADS = RWKV_WIDTH // QW
Q_SLAB0 = 0
K_SLAB0 = Q_SLAB0 + ATT_WIDTH // LANES
V_SLAB0 = K_SLAB0 + ATT_WIDTH // LANES
LORA_SLAB = V_SLAB0 + ATT_WIDTH // LANES
N_SLABS_F32 = LORA_SLAB + 2
RKV_SLABS = 3 * QW // LANES
GATE_SLAB0 = N_QUADS * RKV_SLABS
N_SLABS_BF16 = GATE_SLAB0 + GATE_COLS // LANES

PROJ_TN = 2560
PROJ_TM = 1024
ROW_TILE = 512
ATT_CHUNK = KEYS_PER_QUERY * DILATIONS[-1]
QB = KEYS_PER_QUERY
ATT_UNROLL = 3
RWKV_CHUNK = 64
RWKV_T = 512


def _bf16(x):
    return x.astype(jnp.bfloat16)


def _dot(a, b):
    return jnp.dot(a, b, preferred_element_type=jnp.float32)


def _dot_nt(a, b):
    return lax.dot_general(a, b, (((1,), (1,)), ((), ())), preferred_element_type=jnp.float32)


def _proj_body(x_ref, g_ref, w_ref, o32_ref, o16_ref, h_scr):
    @pl.when(pl.program_id(1) == 0)
    def _():
        x = x_ref[...]
        ms = jnp.mean(x * x, axis=-1, keepdims=True)
        h_scr[...] = _bf16(x * lax.rsqrt(ms + NORM_EPS) * g_ref[...])

    acc = _dot(h_scr[...], w_ref[...])
    for s in range(PROJ_TN // LANES):
        o32_ref[s] = acc[:, s * LANES:(s + 1) * LANES]
        o16_ref[s] = _bf16(acc[:, s * LANES:(s + 1) * LANES])


def _proj(x2, g, w_in, tm):
    m = x2.shape[0]
    n_col_blocks = IN_COLS // PROJ_TN
    slabs_per_block = PROJ_TN // LANES
    n_bf16_blocks = N_SLABS_BF16 // slabs_per_block
    att_end = 3 * ATT_WIDTH
    rkv_end = att_end + 3 * RWKV_WIDTH
    lora_end = att_end + RWKV_COLS
    rkv = [w_in[:, att_end + j * RWKV_WIDTH + qd * QW:att_end + j * RWKV_WIDTH + (qd + 1) * QW]
           for qd in range(N_QUADS) for j in range(3)]
    w_cols = _bf16(jnp.concatenate(rkv + [w_in[:, lora_end:], w_in[:, :att_end], w_in[:, rkv_end:lora_end]],
                                   axis=1))
    blk = (slabs_per_block, tm, LANES)
    return pl.pallas_call(
        _proj_body,
        out_shape=(jax.ShapeDtypeStruct((N_SLABS_F32, m, LANES), jnp.float32),
                   jax.ShapeDtypeStruct((N_SLABS_BF16 + slabs_per_block, m, LANES), jnp.bfloat16)),
        grid=(m // tm, n_col_blocks),
        in_specs=[
            pl.BlockSpec((tm, D_MODEL), lambda i, j: (i, 0)),
            pl.BlockSpec((1, D_MODEL), lambda i, j: (0, 0)),
            pl.BlockSpec((D_MODEL, PROJ_TN), lambda i, j: (0, j)),
        ],
        out_specs=(pl.BlockSpec(blk, lambda i, j: (jnp.maximum(j - n_bf16_blocks, 0), i, 0)),
                   pl.BlockSpec(blk, lambda i, j: (jnp.minimum(j, n_bf16_blocks), i, 0))),
        scratch_shapes=[pltpu.VMEM((tm, D_MODEL), jnp.bfloat16)],
        compiler_params=pltpu.CompilerParams(
            dimension_semantics=("parallel", "arbitrary"), vmem_limit_bytes=VMEM_LIMIT),
        name="proj",
    )(x2, g, w_cols)


def _attn_units(qs, ks, vs, biases, lo):
    scale = HEAD_DIM ** -0.5

    def logits(q, k):
        qs_ = q * scale
        zero = jnp.zeros_like(qs_)
        lhs = _bf16(jnp.concatenate([jnp.where(lo, qs_, zero), jnp.where(lo, zero, qs_)], axis=0))
        return _dot_nt(lhs, _bf16(k))

    s = [logits(q, k) for q, k in zip(qs, ks)]
    s0 = [x[:QB] + b[0] for x, b in zip(s, biases)]
    s1 = [x[QB:] + b[1] for x, b in zip(s, biases)]
    m0 = [jnp.max(x, axis=-1, keepdims=True) for x in s0]
    m1 = [jnp.max(x, axis=-1, keepdims=True) for x in s1]
    p0 = [_bf16(jnp.exp(x - m)) for x, m in zip(s0, m0)]
    p1 = [_bf16(jnp.exp(x - m)) for x, m in zip(s1, m1)]

    def weighted(p0_, p1_, v):
        rhs = _bf16(jnp.concatenate([v, jnp.ones_like(v)], axis=1))
        x = _dot(jnp.concatenate([p0_, p1_], axis=0), rhs)
        return jnp.where(lo, x[:QB, :LANES], x[QB:, :LANES]), jnp.where(lo, x[:QB, LANES:], x[QB:, LANES:])

    ol = [weighted(a, b, v) for a, b, v in zip(p0, p1, vs)]
    out = []
    for (o, l), a, b in zip(ol, m0, m1):
        out.append((o / l, jnp.where(lo, a, b) + jnp.log(l)))
    return out


def _attn_body(q_ref, kc_ref, kp_ref, vc_ref, vp_ref, bias_ref, o_ref, o_scr, l_scr, k_cache, v_cache):
    c = pl.program_id(1)
    g = pl.program_id(2)
    lo = lax.broadcasted_iota(jnp.int32, (1, LANES), 1) < HEAD_DIM
    before_start = (c == 0) & (lax.broadcasted_iota(jnp.int32, (QB, 2 * QB), 1) < QB)

    def rows(start, n, d):
        return pl.ds(start, n) if d == 1 else pl.ds(start, n, stride=d)

    def group(gi):
        d = DILATIONS[gi]
        n_qb = ATT_CHUNK // (QB * d)

        def run(units):
            qs = [q_ref[p, rows(start_q, QB, d), :] for p, start_q, _, _, _ in units]

            def bias(head, is_first):
                tile = bias_ref[head]
                return jnp.where(before_start, NEG, tile) if is_first else tile

            biases = [(bias(gi * HEADS_PER_GROUP + 2 * p, is_first), bias(gi * HEADS_PER_GROUP + 2 * p + 1, is_first))
                      for p, _, _, _, is_first in units]
            res = _attn_units(qs, [u[2] for u in units], [u[3] for u in units], biases, lo)
            for (p, start_q, _, _, _), (o, lse) in zip(units, res):
                o_scr[gi, p, rows(start_q, QB, d), :] = o
                l_scr[gi, p, rows(start_q, QB, d), :] = lse

        n_first = max(j for j in range(1, ATT_UNROLL + 1) if d % j == 0)
        cached = n_qb == 1
        if cached:
            @pl.when(c == 0)
            def _():
                k_cache[...] = jnp.zeros(k_cache.shape, k_cache.dtype)
                v_cache[...] = jnp.zeros(v_cache.shape, v_cache.dtype)

        def first_blocks(it, carry):
            units = []
            for j in range(n_first):
                r = it * n_first + j
                prev_start = r + QB * d * (n_qb - 1)
                for p in range(2):
                    k_cur = kc_ref[p, rows(r, QB, d), :]
                    v_cur = vc_ref[p, rows(r, QB, d), :]
                    if cached:
                        slot = pl.ds(pl.multiple_of(r * QB, QB), QB)
                        k = jnp.concatenate([k_cache[p, slot, :], _bf16(k_cur)], axis=0)
                        v = jnp.concatenate([v_cache[p, slot, :], v_cur], axis=0)
                        k_cache[p, slot, :] = _bf16(k_cur)
                        v_cache[p, slot, :] = v_cur
                    else:
                        k = jnp.concatenate([kp_ref[p, rows(prev_start, QB, d), :], k_cur], axis=0)
                        v = jnp.concatenate([vp_ref[p, rows(prev_start, QB, d), :], v_cur], axis=0)
                    units.append((p, r, k, v, True))
            run(units)
            return carry

        lax.fori_loop(0, d // n_first, first_blocks, 0)

        n_later = d * (n_qb - 1)
        if n_later:
            n_par = max(j for j in range(1, ATT_UNROLL + 1) if n_later % j == 0)

            def later_blocks(it, carry):
                units = []
                for j in range(n_par):
                    u = it * n_par + j
                    start_q = u % d + QB * d * (1 + u // d)
                    for p in range(2):
                        k = kc_ref[p, rows(start_q - QB * d, 2 * QB, d), :]
                        v = vc_ref[p, rows(start_q - QB * d, 2 * QB, d), :]
                        units.append((p, start_q, k, v, False))
                run(units)
                return carry

            lax.fori_loop(0, n_later // n_par, later_blocks, 0)

    for gi in range(N_GROUPS):
        pl.when(g == gi)(functools.partial(group, gi))

    @pl.when(g == N_GROUPS - 1)
    def _():
        tile = 256

        def comb(i, carry):
            rs = pl.ds(pl.multiple_of(i * tile, tile), tile)
            for p in range(2):
                ls = [l_scr[gi, p, rs, :] for gi in range(N_GROUPS)]
                mx = jnp.maximum(jnp.maximum(ls[0], ls[1]), ls[2])
                ws = [jnp.exp(l - mx) for l in ls]
                num = ws[0] * o_scr[0, p, rs, :] + ws[1] * o_scr[1, p, rs, :] + ws[2] * o_scr[2, p, rs, :]
                o_ref[rs, p * LANES:(p + 1) * LANES] = _bf16(num / (ws[0] + ws[1] + ws[2]))
            return carry

        lax.fori_loop(0, ATT_CHUNK // tile, comb, 0)


def _attn(slabs, bias_tiles, batch, seq):
    m = slabs.shape[1]
    n_chunks = seq // ATT_CHUNK
    blk = (2, ATT_CHUNK, LANES)

    def cur(slab0):
        return pl.BlockSpec(blk, lambda b, c, g: (slab0 // 2 + g, b * n_chunks + c, 0))

    def prev(slab0):
        return pl.BlockSpec(blk, lambda b, c, g: (slab0 // 2 + g, b * n_chunks + jnp.maximum(c - 1, 0), 0))

    return pl.pallas_call(
        _attn_body,
        out_shape=jax.ShapeDtypeStruct((m, ATT_OUT_WIDTH), jnp.bfloat16),
        grid=(batch, n_chunks, N_GROUPS),
        in_specs=[cur(Q_SLAB0), cur(K_SLAB0), prev(K_SLAB0), cur(V_SLAB0), prev(V_SLAB0),
                  pl.BlockSpec(bias_tiles.shape, lambda b, c, g: (0, 0, 0))],
        out_specs=pl.BlockSpec((ATT_CHUNK, ATT_OUT_WIDTH), lambda b, c, g: (b * n_chunks + c, 0)),
        scratch_shapes=[pltpu.VMEM((N_GROUPS, 2, ATT_CHUNK, LANES), jnp.float32),
                        pltpu.VMEM((N_GROUPS, 2, ATT_CHUNK, LANES), jnp.float32),
                        pltpu.VMEM((2, ATT_CHUNK, LANES), jnp.bfloat16),
                        pltpu.VMEM((2, ATT_CHUNK, LANES), jnp.float32)],
        compiler_params=pltpu.CompilerParams(
            dimension_semantics=("parallel", "arbitrary", "arbitrary"), vmem_limit_bytes=VMEM_LIMIT),
        name="attn",
    )(slabs, slabs, slabs, slabs, slabs, bias_tiles)


def _t5_bucket(dist):
    max_exact = N_BUCKETS // 2
    d_f = jnp.maximum(dist, 1).astype(jnp.float32)
    large = max_exact + (jnp.log(d_f / max_exact) / math.log(MAX_DISTANCE / max_exact)
                         * (N_BUCKETS - max_exact)).astype(jnp.int32)
    large = jnp.minimum(large, N_BUCKETS - 1)
    return jnp.where(dist < max_exact, dist, large)


def _bias_tiles(rel_bias):
    dil = jnp.array(DILATIONS, jnp.int32)
    dist = dil[:, None] * jnp.arange(KEYS_PER_QUERY + 1, dtype=jnp.int32)[None, :]
    bucket = _t5_bucket(dist)
    bias = rel_bias.reshape(N_BUCKETS, N_GROUPS, HEADS_PER_GROUP)[bucket, jnp.arange(N_GROUPS)[:, None]]
    bias = jnp.transpose(bias, (0, 2, 1)).astype(jnp.float32).reshape(ATT_HEADS, KEYS_PER_QUERY + 1)
    n = 3 * QB - 1
    neg = lambda w: jnp.full((ATT_HEADS, w), NEG, jnp.float32)
    e = jnp.concatenate([neg(QB - 1), bias[:, ::-1], neg(QB - 1), neg(1)], axis=1)
    e = jnp.roll(e, -(QB - 1), axis=1)
    return jnp.tile(e, (1, QB))[:, :QB * n].reshape(ATT_HEADS, QB, n)[:, :, :2 * QB]


def _interleave(*gens):
    gens = list(gens)
    while gens:
        for gen in list(gens):
            try:
                next(gen)
            except StopIteration:
                gens.remove(gen)


def _rwkv_body(rkv_ref, lora_ref, pra_ref, prs_ref, sh_ref, wl_ref, o_ref,
               carry, h_st, tok_s, post_s, up_s, zq_s, kcbt_s, g_s, vb_s, y_s, *, n_t):
    t_blk = o_ref.shape[0]
    cc = RWKV_CHUNK
    cs = range(t_blk // cc)
    i = pl.program_id(0)
    first_tok = (i % n_t) == 0
    first_seq = ((i + n_t - 2) % n_t) == 0

    @pl.when(i == 0)
    def _():
        for ref in (carry, h_st, tok_s, post_s, up_s, zq_s, kcbt_s, g_s, vb_s):
            ref[...] = jnp.zeros(ref.shape, ref.dtype)

    lane_head = lax.broadcasted_iota(jnp.int32, (QW, QW), 1) // HEAD_DIM
    row_head = lax.broadcasted_iota(jnp.int32, (QW, QW), 0) // HEAD_DIM
    same_head = lane_head == row_head
    head_ones = _bf16(jnp.where(same_head, 1.0, 0.0))
    assert RWKV_QUAD * cc == QW
    chunk_lane_head = lax.broadcasted_iota(jnp.int32, (cc, QW), 1) // HEAD_DIM
    keep = [chunk_lane_head == h for h in range(RWKV_QUAD)]

    def head_sum(x):
        return _dot(_bf16(x), head_ones)

    def stack4(x):
        z = jnp.zeros_like(x)
        return _bf16(jnp.concatenate([jnp.where(keep[h], x, z) for h in range(RWKV_QUAD)], axis=0))

    def each(fn, *lists):
        return [fn(*args) for args in zip(*lists)]

    def split3(x):
        hi = _bf16(x)
        r1 = x - hi.astype(jnp.float32)
        mid = _bf16(r1)
        return hi, mid, _bf16(r1 - mid.astype(jnp.float32))

    def per_token(tok_w):
        pr = pra_ref[0]
        sh = sh_ref[...]

        def shifted(x, idx, mu):
            width = x.shape[1]
            row = lax.broadcasted_iota(jnp.int32, x.shape, 0)
            last = jnp.where(first_tok, jnp.zeros((1, width), jnp.float32), carry[idx, 7:8, :width])
            prev = jnp.where(row == 0, last, pltpu.roll(x, 1, 0))
            carry[idx, :, :width] = x[t_blk - 8:, :]
            return x + (prev - x) * mu

        wide = lambda j: jnp.concatenate([rkv_ref[2 * j], rkv_ref[2 * j + 1]], axis=1).astype(jnp.float32)
        r = shifted(wide(0), 0, pr[0:1])
        k = shifted(wide(1), 1, pr[1:2])
        v = shifted(wide(2), 2, pr[2:3])
        tok_s[tok_w, 0] = r
        tok_s[tok_w, 4] = v
        yield
        f_lora = shifted(lora_ref[0], 3, sh[0:1, :LANES])
        f_g = shifted(lora_ref[1], 4, sh[1:2, :LANES])
        w0, a0, k_k, k_a, r_k = (pr[j:j + 1] for j in range(3, 8))
        wd = w0 + _dot(_bf16(jnp.tanh(f_lora)), wl_ref[0, 0])
        tok_s[tok_w, 5] = -math.exp(-0.5) * jax.nn.sigmoid(wd)
        a = jax.nn.sigmoid(a0 + _dot(_bf16(f_lora), wl_ref[0, 1]))
        tok_s[tok_w, 7] = _dot(_bf16(jax.nn.sigmoid(f_g)), wl_ref[0, 2])
        yield
        kk = k * k_k
        kk = kk * lax.rsqrt(jnp.maximum(head_sum(kk * kk), L2_EPS * L2_EPS))
        tok_s[tok_w, 1] = kk
        tok_s[tok_w, 3] = kk * a
        yield
        k2 = k * (1.0 + (a - 1.0) * k_a)
        tok_s[tok_w, 2] = k2
        tok_s[tok_w, 6] = head_sum(r * k2 * r_k) * v
        yield

    def per_chunk(tok_r, chk_w):
        t_i = lax.broadcasted_iota(jnp.int32, (cc, QW), 0)
        s_i = lax.broadcasted_iota(jnp.int32, (cc, QW), 1) % cc
        strict_lower = t_i > s_i
        lower = t_i >= s_i
        eye_f = jnp.where(t_i == s_i, 1.0, 0.0)
        zero_m = jnp.zeros((cc, QW), jnp.float32)
        ci = lax.broadcasted_iota(jnp.int32, (cc, cc), 0)
        cj = lax.broadcasted_iota(jnp.int32, (cc, cc), 1)
        tri = _bf16(jnp.where(ci >= cj, 1.0, 0.0))
        rows_of = lambda j: [tok_s[tok_r, j, c * cc:(c + 1) * cc, :] for c in cs]
        r, kk, k2, bb, v, lw = (rows_of(j) for j in range(6))

        parts = each(split3, lw)
        big_l = each(lambda p: _dot(tri, p[0]) + _dot(tri, p[1]) + _dot(tri, p[2]), parts)
        yield
        l_end = each(lambda l: l[cc - 1:cc], big_l)
        e_l = each(jnp.exp, big_l)
        e_lm = each(lambda l, w: jnp.exp(l - w), big_l, lw)
        e_nl = each(lambda l: jnp.exp(-l), big_l)
        e_c = each(lambda le, l: jnp.exp(le - l), l_end, big_l)
        a_kk_f = each(lambda x, e: x * e, kk, e_lm)
        a_kk = each(_bf16, a_kk_f)
        a_r_f = each(lambda x, e: x * e, r, e_l)
        a_r = each(_bf16, a_r_f)
        kb = each(lambda xk, xb, e: jnp.concatenate([stack4(xk * e), stack4(xb * e)], axis=0),
                  k2, bb, e_nl)
        vb = each(_bf16, v)
        vst = each(stack4, v)
        yield
        s12 = each(lambda ak, ar, kb_: _dot_nt(jnp.concatenate([ak, ar], axis=0), kb_), a_kk, a_r, kb)
        s1 = each(lambda s: s[:cc], s12)
        s2 = each(lambda s: s[cc:], s12)
        yield
        mk = each(lambda s: _bf16(jnp.where(strict_lower, s[:, :QW], zero_m)), s1)
        mb = each(lambda s: jnp.where(strict_lower, s[:, QW:], zero_m), s1)
        nk = each(lambda s: _bf16(jnp.where(lower, s[:, :QW], zero_m)), s2)
        nb_neg = each(lambda s: _bf16(jnp.where(lower, -s[:, QW:], zero_m)), s2)

        x = each(lambda n: eye_f - jnp.where((t_i // 2) == (s_i // 2), n, zero_m), mb)
        blk = 4
        while blk <= cc:
            half = blk // 2
            sel = ((t_i // blk) == (s_i // blk)) & ((t_i % blk) >= half) & ((s_i % blk) < half)
            xb = each(_bf16, x)
            t1 = each(lambda xc, n: _bf16(_dot(xc, stack4(jnp.where(sel, n, zero_m)))), xb, mb)
            yield
            x = each(lambda xf, t: xf - _dot(t, stack4(xf)), x, t1)
            yield
            blk *= 2
        tinv = each(_bf16, x)

        pm = each(lambda t, ak: _dot(t, stack4(ak)), tinv, a_kk_f)
        mnv = each(lambda m_, n_, v_: _dot(jnp.concatenate([m_, n_], axis=0), v_), mk, nk, vst)
        yield
        q = each(lambda t, m_: _dot(t, stack4(m_[:cc])), tinv, mnv)
        yield
        pmb = each(_bf16, pm)
        u = each(lambda af, n, p: af + _dot(n, stack4(p)), a_r_f, nb_neg, pm)
        z = each(lambda m_, n, q_: m_[cc:] + _dot(n, stack4(q_)), mnv, nb_neg, q)
        yield
        for c in cs:
            up_s[chk_w, c] = jnp.concatenate([pmb[c], _bf16(u[c])], axis=0)
            zq_s[chk_w, c, 0] = z[c]
            zq_s[chk_w, c, 1] = q[c]
            vb_s[chk_w, c] = vb[c]
        yield
        kcg = each(lambda xk, xb, e, le: jnp.concatenate(
            [xk * e, -(xb * e), jnp.broadcast_to(jnp.exp(le), (2 * cc, QW))], axis=0).T,
            k2, bb, e_c, l_end)
        for c in cs:
            kcbt_s[chk_w, c] = _bf16(kcg[c][:, :2 * cc])
            g_s[chk_w, c] = kcg[c][:, 2 * cc:]
        yield
        post_s[chk_w, 0] = tok_s[tok_r, 6]
        post_s[chk_w, 1] = tok_s[tok_r, 7]
        yield

    def sequential(chk_r):
        h = jnp.where(first_seq, jnp.zeros((QW, QW), jnp.float32), h_st[...])
        for c in cs:
            ys = _dot(up_s[chk_r, c], _bf16(h))
            sig = ys[:cc] + zq_s[chk_r, c, 1]
            y_s[c * cc:(c + 1) * cc, :] = ys[cc:] + zq_s[chk_r, c, 0]
            yield
            hu = _dot(kcbt_s[chk_r, c], jnp.concatenate([vb_s[chk_r, c], _bf16(sig)], axis=0))
            g = g_s[chk_r, c]
            h = h * jnp.concatenate([g, g], axis=1) + jnp.where(same_head, hu, jnp.zeros_like(hu))
            yield
        h_st[...] = h
        prs = prs_ref[0]
        y = y_s[...]
        mu = head_sum(y) * (1.0 / HEAD_DIM)
        yc = y - mu
        var = head_sum(yc * yc) * (1.0 / HEAD_DIM)
        yn = yc * lax.rsqrt(var + LN_X_EPS) * prs[8:9] + prs[9:10]
        o_ref[...] = _bf16((yn + post_s[chk_r, 0]) * post_s[chk_r, 1])
        yield

    for parity in (0, 1):
        @pl.when(i % 2 == parity)
        def _(parity=parity):
            _interleave(per_chunk(1 - parity, 1 - parity), sequential(parity), per_token(parity))


def _rwkv(slabs32, slabs16, pr, sh, wl, batch, seq):
    m = slabs32.shape[1]
    t_blk = RWKV_T
    n_t = seq // t_blk
    n_c = t_blk // RWKV_CHUNK
    n_blocks = batch * N_QUADS * n_t

    def where(blk):
        blk = jnp.clip(blk, 0, n_blocks - 1)
        seq_id, t = blk // n_t, blk % n_t
        return seq_id % N_QUADS, (seq_id // N_QUADS) * n_t + t

    return pl.pallas_call(
        functools.partial(_rwkv_body, n_t=n_t),
        out_shape=jax.ShapeDtypeStruct((m, RWKV_WIDTH), jnp.bfloat16),
        grid=(n_blocks + 2,),
        in_specs=[pl.BlockSpec((RKV_SLABS, t_blk, LANES), lambda i: (where(i)[0], where(i)[1], 0)),
                  pl.BlockSpec((2, t_blk, LANES), lambda i: (LORA_SLAB // 2, where(i)[1], 0)),
                  pl.BlockSpec((1,) + pr.shape[1:], lambda i: (where(i)[0], 0, 0)),
                  pl.BlockSpec((1,) + pr.shape[1:], lambda i: (where(i - 2)[0], 0, 0)),
                  pl.BlockSpec(sh.shape, lambda i: (0, 0)),
                  pl.BlockSpec((1,) + wl.shape[1:], lambda i: (where(i)[0], 0, 0, 0))],
        out_specs=pl.BlockSpec((t_blk, QW), lambda i: (where(i - 2)[1], where(i - 2)[0])),
        scratch_shapes=[
            pltpu.VMEM((5, 8, QW), jnp.float32),
            pltpu.VMEM((QW, QW), jnp.float32),
            pltpu.VMEM((2, 8, t_blk, QW), jnp.float32),
            pltpu.VMEM((2, 2, t_blk, QW), jnp.float32),
            pltpu.VMEM((2, n_c, 2 * RWKV_CHUNK, QW), jnp.bfloat16),
            pltpu.VMEM((2, n_c, 2, RWKV_CHUNK, QW), jnp.float32),
            pltpu.VMEM((2, n_c, QW, LANES), jnp.bfloat16),
            pltpu.VMEM((2, n_c, QW, LANES), jnp.float32),
            pltpu.VMEM((2, n_c, RWKV_CHUNK, QW), jnp.bfloat16),
            pltpu.VMEM((t_blk, QW), jnp.float32),
        ],
        compiler_params=pltpu.CompilerParams(
            dimension_semantics=("arbitrary",), vmem_limit_bytes=VMEM_LIMIT),
        name="rwkv",
    )(slabs16, slabs32, pr, pr, sh, wl)


def _rwkv_params(shift_mu, w0, a0, k_k, k_a, r_k, ln_w, ln_b, w_w2, w_a2, w_g2):
    def quads(vec):
        return vec.reshape(N_QUADS, QW)

    mu_r, mu_k, mu_v = (quads(shift_mu[j * RWKV_WIDTH:(j + 1) * RWKV_WIDTH]) for j in range(3))
    rows = [mu_r, mu_k, mu_v, quads(w0), quads(a0), quads(k_k), quads(k_a), quads(r_k.reshape(-1)),
            quads(ln_w), quads(ln_b)]
    pr = jnp.stack(rows, axis=1)
    pr = jnp.pad(pr, ((0, 0), (0, 16 - pr.shape[1]), (0, 0)))
    mu_rest = shift_mu[3 * RWKV_WIDTH:]
    sh = jnp.pad(mu_rest.reshape(2, LANES), ((0, 6), (0, QW - LANES)))
    zeros = jnp.zeros((DECAY_LORA, RWKV_WIDTH), jnp.float32)
    wd = jnp.concatenate([w_w2, zeros], axis=0)
    wa = jnp.concatenate([zeros, w_a2], axis=0)
    wl = jnp.stack([wd, wa, w_g2], axis=0)
    wl = wl.reshape(3, LANES, N_QUADS, QW).transpose(2, 0, 1, 3)
    return pr, sh, _bf16(wl)


def _rms(x, g):
    ms = jnp.mean(x * x, axis=-1, keepdims=True)
    return x * lax.rsqrt(ms + NORM_EPS) * g


def _tail_body(x_ref, att_ref, rw_ref, gate0_ref, gate1_ref, bg_ref, wa_ref, wr_ref, wo_ref, gmix_ref,
               gpre_ref, w1_ref, w2_ref, gpost_ref, o_ref, *, tf):
    n_gs = D_MODEL // LANES
    f0 = jnp.concatenate([gate0_ref[s] for s in range(n_gs)], axis=1).astype(jnp.float32)
    f1 = jnp.concatenate([gate1_ref[s] for s in range(n_gs)], axis=1).astype(jnp.float32)
    g0 = jax.nn.sigmoid(f0 + bg_ref[:, :D_MODEL])
    g1 = jax.nn.sigmoid(f1 + bg_ref[:, D_MODEL:])
    merged = g0 * _dot(att_ref[...], wa_ref[...]) + g1 * _dot(rw_ref[...], wr_ref[...])
    z = _dot(_bf16(merged), wo_ref[...])
    x = x_ref[...] + _rms(z, gmix_ref[...])
    h = _bf16(_rms(x, gpre_ref[...]))
    acc = jnp.zeros(x.shape, jnp.float32)
    for c in range(D_FF // tf):
        u = jnp.maximum(_dot(h, w1_ref[:, c * tf:(c + 1) * tf]), 0.0)
        acc = acc + _dot(_bf16(u * u), w2_ref[c * tf:(c + 1) * tf, :])
    o_ref[...] = x + _rms(acc, gpost_ref[...])


def _tail(x2, o_att, o_rwkv, slabs, b_gate, wa, wr, wo, g_mix, g_pre, w1, w2, g_post, tm, tf=1024):
    m = x2.shape[0]
    n_gs = D_MODEL // LANES
    const = lambda shape: pl.BlockSpec(shape, lambda i: (0, 0), pipeline_mode=pl.Buffered(1))
    return pl.pallas_call(
        functools.partial(_tail_body, tf=tf),
        out_shape=jax.ShapeDtypeStruct((m, D_MODEL), jnp.float32),
        grid=(m // tm,),
        in_specs=[
            pl.BlockSpec((tm, D_MODEL), lambda i: (i, 0)),
            pl.BlockSpec((tm, ATT_OUT_WIDTH), lambda i: (i, 0)),
            pl.BlockSpec((tm, RWKV_WIDTH), lambda i: (i, 0)),
            pl.BlockSpec((n_gs, tm, LANES), lambda i: (GATE_SLAB0 // n_gs, i, 0)),
            pl.BlockSpec((n_gs, tm, LANES), lambda i: (GATE_SLAB0 // n_gs + 1, i, 0)),
            const((1, GATE_COLS)), const(wa.shape), const(wr.shape), const(wo.shape), const((1, D_MODEL)),
            const((1, D_MODEL)), const(w1.shape), const(w2.shape), const((1, D_MODEL)),
        ],
        out_specs=pl.BlockSpec((tm, D_MODEL), lambda i: (i, 0)),
        compiler_params=pltpu.CompilerParams(
            dimension_semantics=("parallel",), vmem_limit_bytes=VMEM_LIMIT),
        name="tail",
    )(x2, o_att, o_rwkv, slabs, slabs, b_gate, wa, wr, wo, g_mix, g_pre, w1, w2, g_post)


def kernel(x, rel_bias, norm_mix_pre, norm_mix_post, norm_ffn_pre, norm_ffn_post, w_in, b_gate, shift_mu, w0, w_w2, a0, w_a2, w_g2, k_k, k_a, r_k, ln_x_w, ln_x_b, w_att_branch, w_rwkv_branch, w_out, w_ffn1, w_ffn2):
    batch, seq, d_model = x.shape
    assert d_model == D_MODEL and seq % ATT_CHUNK == 0 and seq % RWKV_T == 0
    m = batch * seq
    tm = ROW_TILE
    assert m % PROJ_TM == 0 and m % tm == 0
    bias_tiles = _bias_tiles(rel_bias)
    row = lambda vec: vec.reshape(1, -1)
    x2 = x.reshape(m, D_MODEL)
    for l in range(w_in.shape[0]):
        slabs32, slabs16 = _proj(x2, row(norm_mix_pre[l]), w_in[l], PROJ_TM)
        o_att = _attn(slabs32, bias_tiles, batch, seq)
        pr, sh, wl = _rwkv_params(shift_mu[l], w0[l], a0[l], k_k[l], k_a[l], r_k[l], ln_x_w[l], ln_x_b[l],
                                  w_w2[l], w_a2[l], w_g2[l])
        o_rwkv = _rwkv(slabs32, slabs16, pr, sh, wl, batch, seq)
        x2 = _tail(x2, o_att, o_rwkv, slabs16, row(b_gate[l]), _bf16(w_att_branch[l]), _bf16(w_rwkv_branch[l]),
                   _bf16(w_out[l]), row(norm_mix_post[l]), row(norm_ffn_pre[l]), _bf16(w_ffn1[l]),
                   _bf16(w_ffn2[l]), row(norm_ffn_post[l]), tm)
    return x2.reshape(batch, seq, D_MODEL)
```

```python
import functools
import math

import jax
import jax.numpy as jnp
from jax import lax
from jax.experimental import pallas as pl
from jax.experimental.pallas import tpu as pltpu

D_MODEL = 1024
HEAD_DIM = 64
DILATIONS = (1, 4, 16)
KEYS_PER_QUERY = 128
N_GROUPS = len(DILATIONS)
HEADS_PER_GROUP = 4
ATT_HEADS = N_GROUPS * HEADS_PER_GROUP
ATT_WIDTH = ATT_HEADS * HEAD_DIM
ATT_OUT_WIDTH = HEADS_PER_GROUP * HEAD_DIM
N_BUCKETS = 32
MAX_DISTANCE = KEYS_PER_QUERY * DILATIONS[-1]
RWKV_WIDTH = D_MODEL
DECAY_LORA = 64
ICLR_LORA = 64
GATE_LORA = 128
RWKV_COLS = 3 * RWKV_WIDTH + DECAY_LORA + ICLR_LORA + GATE_LORA
N_BRANCHES = 2
IN_COLS = 3 * ATT_WIDTH + RWKV_COLS + N_BRANCHES * D_MODEL
D_FF = 4 * D_MODEL
NORM_EPS = 1e-6
LN_X_EPS = 64e-5
L2_EPS = 1e-12

LANES = 128
VMEM_LIMIT = 56 * 1024 * 1024
NEG = -1e30

GATE_COLS = N_BRANCHES * D_MODEL
RWKV_QUAD = 4
QW = RWKV_QUAD * HEAD_DIM
N_QUADS = RWKV_WIDTH // QW
Q_SLAB0 = 0
K_SLAB0 = Q_SLAB0 + ATT_WIDTH // LANES
V_SLAB0 = K_SLAB0 + ATT_WIDTH // LANES
LORA_SLAB = V_SLAB0 + ATT_WIDTH // LANES
N_SLABS_F32 = LORA_SLAB + 2
RKV_SLABS = 3 * QW // LANES
GATE_SLAB0 = N_QUADS * RKV_SLABS
N_SLABS_BF16 = GATE_SLAB0 + GATE_COLS // LANES

PROJ_TN = 1280
PROJ_TM = 512
ROW_TILE = 512
ATT_CHUNK = KEYS_PER_QUERY * DILATIONS[-1]
QB = KEYS_PER_QUERY
ATT_UNROLL = 3
RWKV_CHUNK = 64
RWKV_T = 512


def _bf16(x):
    return x.astype(jnp.bfloat16)


def _dot(a, b):
    return jnp.dot(a, b, preferred_element_type=jnp.float32)


def _dot_nt(a, b):
    return lax.dot_general(a, b, (((1,), (1,)), ((), ())), preferred_element_type=jnp.float32)


def _proj_body(x_ref, g_ref, w_ref, o32_ref, o16_ref):
    x = x_ref[...]
    ms = jnp.mean(x * x, axis=-1, keepdims=True)
    h = _bf16(x * lax.rsqrt(ms + NORM_EPS) * g_ref[...])
    slabs_per_chunk = PROJ_TN // LANES
    for c in range(IN_COLS // PROJ_TN):
        acc = _dot(h, w_ref[:, c * PROJ_TN:(c + 1) * PROJ_TN])
        for s in range(slabs_per_chunk):
            slab = c * slabs_per_chunk + s
            part = acc[:, s * LANES:(s + 1) * LANES]
            if slab < N_SLABS_BF16:
                o16_ref[slab] = _bf16(part)
            else:
                o32_ref[slab - N_SLABS_BF16] = part


def _proj(x2, g, w_in, tm):
    m = x2.shape[0]
    att_end = 3 * ATT_WIDTH
    rkv_end = att_end + 3 * RWKV_WIDTH
    lora_end = att_end + RWKV_COLS
    rkv = [w_in[:, att_end + j * RWKV_WIDTH + qd * QW:att_end + j * RWKV_WIDTH + (qd + 1) * QW]
           for qd in range(N_QUADS) for j in range(3)]
    w_cols = _bf16(jnp.concatenate(rkv + [w_in[:, lora_end:], w_in[:, :att_end], w_in[:, rkv_end:lora_end]],
                                   axis=1))
    return pl.pallas_call(
        _proj_body,
        out_shape=(jax.ShapeDtypeStruct((N_SLABS_F32, m, LANES), jnp.float32),
                   jax.ShapeDtypeStruct((N_SLABS_BF16, m, LANES), jnp.bfloat16)),
        grid=(m // tm,),
        in_specs=[
            pl.BlockSpec((tm, D_MODEL), lambda i: (i, 0)),
            pl.BlockSpec((1, D_MODEL), lambda i: (0, 0), pipeline_mode=pl.Buffered(1)),
            pl.BlockSpec((D_MODEL, IN_COLS), lambda i: (0, 0), pipeline_mode=pl.Buffered(1)),
        ],
        out_specs=(pl.BlockSpec((N_SLABS_F32, tm, LANES), lambda i: (0, i, 0)),
                   pl.BlockSpec((N_SLABS_BF16, tm, LANES), lambda i: (0, i, 0))),
        compiler_params=pltpu.CompilerParams(
            dimension_semantics=("parallel",), vmem_limit_bytes=VMEM_LIMIT),
        name="proj",
    )(x2, g, w_cols)


def _attn_units(qs, ks, vs, biases, lo):
    scale = HEAD_DIM ** -0.5

    def logits(q, k):
        qs_ = q * scale
        zero = jnp.zeros_like(qs_)
        lhs = _bf16(jnp.concatenate([jnp.where(lo, qs_, zero), jnp.where(lo, zero, qs_)], axis=0))
        return _dot_nt(lhs, _bf16(k))

    s = [logits(q, k) for q, k in zip(qs, ks)]
    s0 = [x[:QB] + b[0] for x, b in zip(s, biases)]
    s1 = [x[QB:] + b[1] for x, b in zip(s, biases)]
    m0 = [jnp.max(x, axis=-1, keepdims=True) for x in s0]
    m1 = [jnp.max(x, axis=-1, keepdims=True) for x in s1]
    p0 = [_bf16(jnp.exp(x - m)) for x, m in zip(s0, m0)]
    p1 = [_bf16(jnp.exp(x - m)) for x, m in zip(s1, m1)]

    def weighted(p0_, p1_, v):
        rhs = _bf16(jnp.concatenate([v, jnp.ones_like(v)], axis=1))
        x = _dot(jnp.concatenate([p0_, p1_], axis=0), rhs)
        return jnp.where(lo, x[:QB, :LANES], x[QB:, :LANES]), jnp.where(lo, x[:QB, LANES:], x[QB:, LANES:])

    ol = [weighted(a, b, v) for a, b, v in zip(p0, p1, vs)]
    out = []
    for (o, l), a, b in zip(ol, m0, m1):
        out.append((o / l, jnp.where(lo, a, b) + jnp.log(l)))
    return out


def _attn_body(q_ref, kc_ref, kp_ref, vc_ref, vp_ref, bias_ref, o_ref, o_scr, l_scr, k_cache, v_cache):
    c = pl.program_id(1)
    g = pl.program_id(2)
    lo = lax.broadcasted_iota(jnp.int32, (1, LANES), 1) < HEAD_DIM
    before_start = (c == 0) & (lax.broadcasted_iota(jnp.int32, (QB, 2 * QB), 1) < QB)

    def rows(start, n, d):
        return pl.ds(start, n) if d == 1 else pl.ds(start, n, stride=d)

    def group(gi):
        d = DILATIONS[gi]
        n_qb = ATT_CHUNK // (QB * d)

        def run(units):
            qs = [q_ref[p, rows(start_q, QB, d), :] for p, start_q, _, _, _ in units]

            def bias(head, is_first):
                tile = bias_ref[head]
                return jnp.where(before_start, NEG, tile) if is_first else tile

            biases = [(bias(gi * HEADS_PER_GROUP + 2 * p, is_first), bias(gi * HEADS_PER_GROUP + 2 * p + 1, is_first))
                      for p, _, _, _, is_first in units]
            res = _attn_units(qs, [u[2] for u in units], [u[3] for u in units], biases, lo)
            for (p, start_q, _, _, _), (o, lse) in zip(units, res):
                o_scr[gi, p, rows(start_q, QB, d), :] = o
                l_scr[gi, p, rows(start_q, QB, d), :] = lse

        n_first = max(j for j in range(1, ATT_UNROLL + 1) if d % j == 0)
        cached = n_qb == 1
        if cached:
            @pl.when(c == 0)
            def _():
                k_cache[...] = jnp.zeros(k_cache.shape, k_cache.dtype)
                v_cache[...] = jnp.zeros(v_cache.shape, v_cache.dtype)

        def first_blocks(it, carry):
            units = []
            for j in range(n_first):
                r = it * n_first + j
                prev_start = r + QB * d * (n_qb - 1)
                for p in range(2):
                    k_cur = kc_ref[p, rows(r, QB, d), :]
                    v_cur = vc_ref[p, rows(r, QB, d), :]
                    if cached:
                        slot = pl.ds(pl.multiple_of(r * QB, QB), QB)
                        k = jnp.concatenate([k_cache[p, slot, :], _bf16(k_cur)], axis=0)
                        v = jnp.concatenate([v_cache[p, slot, :], v_cur], axis=0)
                        k_cache[p, slot, :] = _bf16(k_cur)
                        v_cache[p, slot, :] = v_cur
                    else:
                        k = jnp.concatenate([kp_ref[p, rows(prev_start, QB, d), :], k_cur], axis=0)
                        v = jnp.concatenate([vp_ref[p, rows(prev_start, QB, d), :], v_cur], axis=0)
                    units.append((p, r, k, v, True))
            run(units)
            return carry

        lax.fori_loop(0, d // n_first, first_blocks, 0)

        n_later = d * (n_qb - 1)
        if n_later:
            n_par = max(j for j in range(1, ATT_UNROLL + 1) if n_later % j == 0)

            def later_blocks(it, carry):
                units = []
                for j in range(n_par):
                    u = it * n_par + j
                    start_q = u % d + QB * d * (1 + u // d)
                    for p in range(2):
                        k = kc_ref[p, rows(start_q - QB * d, 2 * QB, d), :]
                        v = vc_ref[p, rows(start_q - QB * d, 2 * QB, d), :]
                        units.append((p, start_q, k, v, False))
                run(units)
                return carry

            lax.fori_loop(0, n_later // n_par, later_blocks, 0)

    for gi in range(N_GROUPS):
        pl.when(g == gi)(functools.partial(group, gi))

    @pl.when(g == N_GROUPS - 1)
    def _():
        tile = 256

        def comb(i, carry):
            rs = pl.ds(pl.multiple_of(i * tile, tile), tile)
            for p in range(2):
                ls = [l_scr[gi, p, rs, :] for gi in range(N_GROUPS)]
                mx = jnp.maximum(jnp.maximum(ls[0], ls[1]), ls[2])
                ws = [jnp.exp(l - mx) for l in ls]
                num = ws[0] * o_scr[0, p, rs, :] + ws[1] * o_scr[1, p, rs, :] + ws[2] * o_scr[2, p, rs, :]
                o_ref[rs, p * LANES:(p + 1) * LANES] = _bf16(num / (ws[0] + ws[1] + ws[2]))
            return carry

        lax.fori_loop(0, ATT_CHUNK // tile, comb, 0)


def _attn(slabs, bias_tiles, batch, seq):
    m = slabs.shape[1]
    n_chunks = seq // ATT_CHUNK
    blk = (2, ATT_CHUNK, LANES)

    def cur(slab0):
        return pl.BlockSpec(blk, lambda b, c, g: (slab0 // 2 + g, b * n_chunks + c, 0))

    def prev(slab0):
        return pl.BlockSpec(blk, lambda b, c, g: (slab0 // 2 + g, b * n_chunks + jnp.maximum(c - 1, 0), 0))

    return pl.pallas_call(
        _attn_body,
        out_shape=jax.ShapeDtypeStruct((m, ATT_OUT_WIDTH), jnp.bfloat16),
        grid=(batch, n_chunks, N_GROUPS),
        in_specs=[cur(Q_SLAB0), cur(K_SLAB0), prev(K_SLAB0), cur(V_SLAB0), prev(V_SLAB0),
                  pl.BlockSpec(bias_tiles.shape, lambda b, c, g: (0, 0, 0))],
        out_specs=pl.BlockSpec((ATT_CHUNK, ATT_OUT_WIDTH), lambda b, c, g: (b * n_chunks + c, 0)),
        scratch_shapes=[pltpu.VMEM((N_GROUPS, 2, ATT_CHUNK, LANES), jnp.float32),
                        pltpu.VMEM((N_GROUPS, 2, ATT_CHUNK, LANES), jnp.float32),
                        pltpu.VMEM((2, ATT_CHUNK, LANES), jnp.bfloat16),
                        pltpu.VMEM((2, ATT_CHUNK, LANES), jnp.float32)],
        compiler_params=pltpu.CompilerParams(
            dimension_semantics=("parallel", "arbitrary", "arbitrary"), vmem_limit_bytes=VMEM_LIMIT),
        name="attn",
    )(slabs, slabs, slabs, slabs, slabs, bias_tiles)


def _t5_bucket(dist):
    max_exact = N_BUCKETS // 2
    d_f = jnp.maximum(dist, 1).astype(jnp.float32)
    large = max_exact + (jnp.log(d_f / max_exact) / math.log(MAX_DISTANCE / max_exact)
                         * (N_BUCKETS - max_exact)).astype(jnp.int32)
    large = jnp.minimum(large, N_BUCKETS - 1)
    return jnp.where(dist < max_exact, dist, large)


def _bias_tiles(rel_bias):
    dil = jnp.array(DILATIONS, jnp.int32)
    dist = dil[:, None] * jnp.arange(KEYS_PER_QUERY + 1, dtype=jnp.int32)[None, :]
    bucket = _t5_bucket(dist)
    bias = rel_bias.reshape(N_BUCKETS, N_GROUPS, HEADS_PER_GROUP)[bucket, jnp.arange(N_GROUPS)[:, None]]
    bias = jnp.transpose(bias, (0, 2, 1)).astype(jnp.float32).reshape(ATT_HEADS, KEYS_PER_QUERY + 1)
    n = 3 * QB - 1
    neg = lambda w: jnp.full((ATT_HEADS, w), NEG, jnp.float32)
    e = jnp.concatenate([neg(QB - 1), bias[:, ::-1], neg(QB - 1), neg(1)], axis=1)
    e = jnp.roll(e, -(QB - 1), axis=1)
    return jnp.tile(e, (1, QB))[:, :QB * n].reshape(ATT_HEADS, QB, n)[:, :, :2 * QB]


def _interleave(*gens):
    gens = list(gens)
    while gens:
        for gen in list(gens):
            try:
                next(gen)
            except StopIteration:
                gens.remove(gen)


def _rwkv_body(rkv_ref, lora_ref, pra_ref, prs_ref, sh_ref, wl_ref, o_ref,
               carry, h_st, tok_s, post_s, up_s, zq_s, kcbt_s, g_s, vb_s, y_s, *, n_t):
    t_blk = o_ref.shape[0]
    cc = RWKV_CHUNK
    cs = range(t_blk // cc)
    i = pl.program_id(0)
    first_tok = (i % n_t) == 0
    first_seq = ((i + n_t - 2) % n_t) == 0

    @pl.when(i == 0)
    def _():
        for ref in (carry, h_st, tok_s, post_s, up_s, zq_s, kcbt_s, g_s, vb_s):
            ref[...] = jnp.zeros(ref.shape, ref.dtype)

    lane_head = lax.broadcasted_iota(jnp.int32, (QW, QW), 1) // HEAD_DIM
    row_head = lax.broadcasted_iota(jnp.int32, (QW, QW), 0) // HEAD_DIM
    same_head = lane_head == row_head
    head_ones = _bf16(jnp.where(same_head, 1.0, 0.0))
    assert RWKV_QUAD * cc == QW
    chunk_lane_head = lax.broadcasted_iota(jnp.int32, (cc, QW), 1) // HEAD_DIM
    keep = [chunk_lane_head == h for h in range(RWKV_QUAD)]

    def head_sum(x):
        return _dot(_bf16(x), head_ones)

    def stack4(x):
        z = jnp.zeros_like(x)
        return _bf16(jnp.concatenate([jnp.where(keep[h], x, z) for h in range(RWKV_QUAD)], axis=0))

    def each(fn, *lists):
        return [fn(*args) for args in zip(*lists)]

    def split3(x):
        hi = _bf16(x)
        r1 = x - hi.astype(jnp.float32)
        mid = _bf16(r1)
        return hi, mid, _bf16(r1 - mid.astype(jnp.float32))

    def per_token(tok_w):
        pr = pra_ref[0]
        sh = sh_ref[...]

        def shifted(x, idx, mu):
            width = x.shape[1]
            row = lax.broadcasted_iota(jnp.int32, x.shape, 0)
            last = jnp.where(first_tok, jnp.zeros((1, width), jnp.float32), carry[idx, 7:8, :width])
            prev = jnp.where(row == 0, last, pltpu.roll(x, 1, 0))
            carry[idx, :, :width] = x[t_blk - 8:, :]
            return x + (prev - x) * mu

        wide = lambda j: jnp.concatenate([rkv_ref[2 * j], rkv_ref[2 * j + 1]], axis=1).astype(jnp.float32)
        r = shifted(wide(0), 0, pr[0:1])
        k = shifted(wide(1), 1, pr[1:2])
        v = shifted(wide(2), 2, pr[2:3])
        tok_s[tok_w, 0] = r
        tok_s[tok_w, 4] = v
        yield
        f_lora = shifted(lora_ref[0], 3, sh[0:1, :LANES])
        f_g = shifted(lora_ref[1], 4, sh[1:2, :LANES])
        w0, a0, k_k, k_a, r_k = (pr[j:j + 1] for j in range(3, 8))
        wd = w0 + _dot(_bf16(jnp.tanh(f_lora)), wl_ref[0, 0])
        tok_s[tok_w, 5] = -math.exp(-0.5) * jax.nn.sigmoid(wd)
        a = jax.nn.sigmoid(a0 + _dot(_bf16(f_lora), wl_ref[0, 1]))
        tok_s[tok_w, 7] = _dot(_bf16(jax.nn.sigmoid(f_g)), wl_ref[0, 2])
        yield
        kk = k * k_k
        kk = kk * lax.rsqrt(jnp.maximum(head_sum(kk * kk), L2_EPS * L2_EPS))
        tok_s[tok_w, 1] = kk
        tok_s[tok_w, 3] = kk * a
        yield
        k2 = k * (1.0 + (a - 1.0) * k_a)
        tok_s[tok_w, 2] = k2
        tok_s[tok_w, 6] = head_sum(r * k2 * r_k) * v
        yield

    def per_chunk(tok_r, chk_w):
        t_i = lax.broadcasted_iota(jnp.int32, (cc, QW), 0)
        s_i = lax.broadcasted_iota(jnp.int32, (cc, QW), 1) % cc
        strict_lower = t_i > s_i
        lower = t_i >= s_i
        eye_f = jnp.where(t_i == s_i, 1.0, 0.0)
        zero_m = jnp.zeros((cc, QW), jnp.float32)
        ci = lax.broadcasted_iota(jnp.int32, (cc, cc), 0)
        cj = lax.broadcasted_iota(jnp.int32, (cc, cc), 1)
        tri = _bf16(jnp.where(ci >= cj, 1.0, 0.0))
        rows_of = lambda j: [tok_s[tok_r, j, c * cc:(c + 1) * cc, :] for c in cs]
        r, kk, k2, bb, v, lw = (rows_of(j) for j in range(6))

        parts = each(split3, lw)
        big_l = each(lambda p: _dot(tri, p[0]) + _dot(tri, p[1]) + _dot(tri, p[2]), parts)
        yield
        l_end = each(lambda l: l[cc - 1:cc], big_l)
        e_l = each(jnp.exp, big_l)
        e_lm = each(lambda l, w: jnp.exp(l - w), big_l, lw)
        e_nl = each(lambda l: jnp.exp(-l), big_l)
        e_c = each(lambda le, l: jnp.exp(le - l), l_end, big_l)
        a_kk_f = each(lambda x, e: x * e, kk, e_lm)
        a_kk = each(_bf16, a_kk_f)
        a_r_f = each(lambda x, e: x * e, r, e_l)
        a_r = each(_bf16, a_r_f)
        kb = each(lambda xk, xb, e: jnp.concatenate([stack4(xk * e), stack4(xb * e)], axis=0),
                  k2, bb, e_nl)
        vb = each(_bf16, v)
        vst = each(stack4, v)
        yield
        s12 = each(lambda ak, ar, kb_: _dot_nt(jnp.concatenate([ak, ar], axis=0), kb_), a_kk, a_r, kb)
        s1 = each(lambda s: s[:cc], s12)
        s2 = each(lambda s: s[cc:], s12)
        yield
        mk = each(lambda s: _bf16(jnp.where(strict_lower, s[:, :QW], zero_m)), s1)
        mb = each(lambda s: jnp.where(strict_lower, s[:, QW:], zero_m), s1)
        nk = each(lambda s: _bf16(jnp.where(lower, s[:, :QW], zero_m)), s2)
        nb_neg = each(lambda s: _bf16(jnp.where(lower, -s[:, QW:], zero_m)), s2)

        x = each(lambda n: eye_f - jnp.where((t_i // 2) == (s_i // 2), n, zero_m), mb)
        blk = 4
        while blk <= cc:
            half = blk // 2
            sel = ((t_i // blk) == (s_i // blk)) & ((t_i % blk) >= half) & ((s_i % blk) < half)
            xb = each(_bf16, x)
            t1 = each(lambda xc, n: _bf16(_dot(xc, stack4(jnp.where(sel, n, zero_m)))), xb, mb)
            yield
            x = each(lambda xf, t: xf - _dot(t, stack4(xf)), x, t1)
            yield
            blk *= 2
        tinv = each(_bf16, x)

        pm = each(lambda t, ak: _dot(t, stack4(ak)), tinv, a_kk_f)
        mnv = each(lambda m_, n_, v_: _dot(jnp.concatenate([m_, n_], axis=0), v_), mk, nk, vst)
        yield
        q = each(lambda t, m_: _dot(t, stack4(m_[:cc])), tinv, mnv)
        yield
        pmb = each(_bf16, pm)
        u = each(lambda af, n, p: af + _dot(n, stack4(p)), a_r_f, nb_neg, pm)
        z = each(lambda m_, n, q_: m_[cc:] + _dot(n, stack4(q_)), mnv, nb_neg, q)
        yield
        for c in cs:
            up_s[chk_w, c] = jnp.concatenate([pmb[c], _bf16(u[c])], axis=0)
            zq_s[chk_w, c, 0] = z[c]
            zq_s[chk_w, c, 1] = q[c]
            vb_s[chk_w, c] = vb[c]
        yield
        kcg = each(lambda xk, xb, e, le: jnp.concatenate(
            [xk * e, -(xb * e), jnp.broadcast_to(jnp.exp(le), (2 * cc, QW))], axis=0).T,
            k2, bb, e_c, l_end)
        for c in cs:
            kcbt_s[chk_w, c] = _bf16(kcg[c][:, :2 * cc])
            g_s[chk_w, c] = kcg[c][:, 2 * cc:]
        yield
        post_s[chk_w, 0] = tok_s[tok_r, 6]
        post_s[chk_w, 1] = tok_s[tok_r, 7]
        yield

    def sequential(chk_r):
        h = jnp.where(first_seq, jnp.zeros((QW, QW), jnp.float32), h_st[...])
        for c in cs:
            ys = _dot(up_s[chk_r, c], _bf16(h))
            sig = ys[:cc] + zq_s[chk_r, c, 1]
            y_s[c * cc:(c + 1) * cc, :] = ys[cc:] + zq_s[chk_r, c, 0]
            yield
            hu = _dot(kcbt_s[chk_r, c], jnp.concatenate([vb_s[chk_r, c], _bf16(sig)], axis=0))
            g = g_s[chk_r, c]
            h = h * jnp.concatenate([g, g], axis=1) + jnp.where(same_head, hu, jnp.zeros_like(hu))
            yield
        h_st[...] = h
        prs = prs_ref[0]
        y = y_s[...]
        mu = head_sum(y) * (1.0 / HEAD_DIM)
        yc = y - mu
        var = head_sum(yc * yc) * (1.0 / HEAD_DIM)
        yn = yc * lax.rsqrt(var + LN_X_EPS) * prs[8:9] + prs[9:10]
        o_ref[...] = _bf16((yn + post_s[chk_r, 0]) * post_s[chk_r, 1])
        yield

    for parity in (0, 1):
        @pl.when(i % 2 == parity)
        def _(parity=parity):
            _interleave(per_chunk(1 - parity, 1 - parity), sequential(parity), per_token(parity))


def _rwkv(slabs32, slabs16, pr, sh, wl, batch, seq):
    m = slabs32.shape[1]
    t_blk = RWKV_T
    n_t = seq // t_blk
    n_c = t_blk // RWKV_CHUNK
    n_blocks = batch * N_QUADS * n_t

    def where(blk):
        blk = jnp.clip(blk, 0, n_blocks - 1)
        seq_id, t = blk // n_t, blk % n_t
        return seq_id % N_QUADS, (seq_id // N_QUADS) * n_t + t

    return pl.pallas_call(
        functools.partial(_rwkv_body, n_t=n_t),
        out_shape=jax.ShapeDtypeStruct((m, RWKV_WIDTH), jnp.bfloat16),
        grid=(n_blocks + 2,),
        in_specs=[pl.BlockSpec((RKV_SLABS, t_blk, LANES), lambda i: (where(i)[0], where(i)[1], 0)),
                  pl.BlockSpec((2, t_blk, LANES), lambda i: (LORA_SLAB // 2, where(i)[1], 0)),
                  pl.BlockSpec((1,) + pr.shape[1:], lambda i: (where(i)[0], 0, 0)),
                  pl.BlockSpec((1,) + pr.shape[1:], lambda i: (where(i - 2)[0], 0, 0)),
                  pl.BlockSpec(sh.shape, lambda i: (0, 0)),
                  pl.BlockSpec((1,) + wl.shape[1:], lambda i: (where(i)[0], 0, 0, 0))],
        out_specs=pl.BlockSpec((t_blk, QW), lambda i: (where(i - 2)[1], where(i - 2)[0])),
        scratch_shapes=[
            pltpu.VMEM((5, 8, QW), jnp.float32),
            pltpu.VMEM((QW, QW), jnp.float32),
            pltpu.VMEM((2, 8, t_blk, QW), jnp.float32),
            pltpu.VMEM((2, 2, t_blk, QW), jnp.float32),
            pltpu.VMEM((2, n_c, 2 * RWKV_CHUNK, QW), jnp.bfloat16),
            pltpu.VMEM((2, n_c, 2, RWKV_CHUNK, QW), jnp.float32),
            pltpu.VMEM((2, n_c, QW, LANES), jnp.bfloat16),
            pltpu.VMEM((2, n_c, QW, LANES), jnp.float32),
            pltpu.VMEM((2, n_c, RWKV_CHUNK, QW), jnp.bfloat16),
            pltpu.VMEM((t_blk, QW), jnp.float32),
        ],
        compiler_params=pltpu.CompilerParams(
            dimension_semantics=("arbitrary",), vmem_limit_bytes=VMEM_LIMIT),
        name="rwkv",
    )(slabs16, slabs32, pr, pr, sh, wl)


def _rwkv_params(shift_mu, w0, a0, k_k, k_a, r_k, ln_w, ln_b, w_w2, w_a2, w_g2):
    def quads(vec):
        return vec.reshape(N_QUADS, QW)

    mu_r, mu_k, mu_v = (quads(shift_mu[j * RWKV_WIDTH:(j + 1) * RWKV_WIDTH]) for j in range(3))
    rows = [mu_r, mu_k, mu_v, quads(w0), quads(a0), quads(k_k), quads(k_a), quads(r_k.reshape(-1)),
            quads(ln_w), quads(ln_b)]
    pr = jnp.stack(rows, axis=1)
    pr = jnp.pad(pr, ((0, 0), (0, 16 - pr.shape[1]), (0, 0)))
    mu_rest = shift_mu[3 * RWKV_WIDTH:]
    sh = jnp.pad(mu_rest.reshape(2, LANES), ((0, 6), (0, QW - LANES)))
    zeros = jnp.zeros((DECAY_LORA, RWKV_WIDTH), jnp.float32)
    wd = jnp.concatenate([w_w2, zeros], axis=0)
    wa = jnp.concatenate([zeros, w_a2], axis=0)
    wl = jnp.stack([wd, wa, w_g2], axis=0)
    wl = wl.reshape(3, LANES, N_QUADS, QW).transpose(2, 0, 1, 3)
    return pr, sh, _bf16(wl)


def _rms(x, g):
    ms = jnp.mean(x * x, axis=-1, keepdims=True)
    return x * lax.rsqrt(ms + NORM_EPS) * g


def _tail_body(x_ref, att_ref, rw_ref, gate0_ref, gate1_ref, bg_ref, wa_ref, wr_ref, wo_ref, gmix_ref,
               gpre_ref, w1_ref, w2_ref, gpost_ref, o_ref, *, tf):
    n_gs = D_MODEL // LANES
    f0 = jnp.concatenate([gate0_ref[s] for s in range(n_gs)], axis=1).astype(jnp.float32)
    f1 = jnp.concatenate([gate1_ref[s] for s in range(n_gs)], axis=1).astype(jnp.float32)
    g0 = jax.nn.sigmoid(f0 + bg_ref[:, :D_MODEL])
    g1 = jax.nn.sigmoid(f1 + bg_ref[:, D_MODEL:])
    merged = g0 * _dot(att_ref[...], wa_ref[...]) + g1 * _dot(rw_ref[...], wr_ref[...])
    z = _dot(_bf16(merged), wo_ref[...])
    x = x_ref[...] + _rms(z, gmix_ref[...])
    h = _bf16(_rms(x, gpre_ref[...]))
    acc = jnp.zeros(x.shape, jnp.float32)
    for c in range(D_FF // tf):
        u = jnp.maximum(_dot(h, w1_ref[:, c * tf:(c + 1) * tf]), 0.0)
        acc = acc + _dot(_bf16(u * u), w2_ref[c * tf:(c + 1) * tf, :])
    o_ref[...] = x + _rms(acc, gpost_ref[...])


def _tail(x2, o_att, o_rwkv, slabs, b_gate, wa, wr, wo, g_mix, g_pre, w1, w2, g_post, tm, tf=1024):
    m = x2.shape[0]
    n_gs = D_MODEL // LANES
    const = lambda shape: pl.BlockSpec(shape, lambda i: (0, 0), pipeline_mode=pl.Buffered(1))
    return pl.pallas_call(
        functools.partial(_tail_body, tf=tf),
        out_shape=jax.ShapeDtypeStruct((m, D_MODEL), jnp.float32),
        grid=(m // tm,),
        in_specs=[
            pl.BlockSpec((tm, D_MODEL), lambda i: (i, 0)),
            pl.BlockSpec((tm, ATT_OUT_WIDTH), lambda i: (i, 0)),
            pl.BlockSpec((tm, RWKV_WIDTH), lambda i: (i, 0)),
            pl.BlockSpec((n_gs, tm, LANES), lambda i: (GATE_SLAB0 // n_gs, i, 0)),
            pl.BlockSpec((n_gs, tm, LANES), lambda i: (GATE_SLAB0 // n_gs + 1, i, 0)),
            const((1, GATE_COLS)), const(wa.shape), const(wr.shape), const(wo.shape), const((1, D_MODEL)),
            const((1, D_MODEL)), const(w1.shape), const(w2.shape), const((1, D_MODEL)),
        ],
        out_specs=pl.BlockSpec((tm, D_MODEL), lambda i: (i, 0)),
        compiler_params=pltpu.CompilerParams(
            dimension_semantics=("parallel",), vmem_limit_bytes=VMEM_LIMIT),
        name="tail",
    )(x2, o_att, o_rwkv, slabs, slabs, b_gate, wa, wr, wo, g_mix, g_pre, w1, w2, g_post)


def kernel(x, rel_bias, norm_mix_pre, norm_mix_post, norm_ffn_pre, norm_ffn_post, w_in, b_gate, shift_mu, w0, w_w2, a0, w_a2, w_g2, k_k, k_a, r_k, ln_x_w, ln_x_b, w_att_branch, w_rwkv_branch, w_out, w_ffn1, w_ffn2):
    batch, seq, d_model = x.shape
    assert d_model == D_MODEL and seq % ATT_CHUNK == 0 and seq % RWKV_T == 0
    m = batch * seq
    tm = ROW_TILE
    assert m % PROJ_TM == 0 and m % tm == 0
    bias_tiles = _bias_tiles(rel_bias)
    row = lambda vec: vec.reshape(1, -1)
    x2 = x.reshape(m, D_MODEL)
    for l in range(w_in.shape[0]):
        slabs32, slabs16 = _proj(x2, row(norm_mix_pre[l]), w_in[l], PROJ_TM)
        o_att = _attn(slabs32, bias_tiles, batch, seq)
        pr, sh, wl = _rwkv_params(shift_mu[l], w0[l], a0[l], k_k[l], k_a[l], r_k[l], ln_x_w[l], ln_x_b[l],
                                  w_w2[l], w_a2[l], w_g2[l])
        o_rwkv = _rwkv(slabs32, slabs16, pr, sh, wl, batch, seq)
        x2 = _tail(x2, o_att, o_rwkv, slabs16, row(b_gate[l]), _bf16(w_att_branch[l]), _bf16(w_rwkv_branch[l]),
                   _bf16(w_out[l]), row(norm_mix_post[l]), row(norm_ffn_pre[l]), _bf16(w_ffn1[l]),
                   _bf16(w_ffn2[l]), row(norm_ffn_post[l]), tm)
    return x2.reshape(batch, seq, D_MODEL)
```

```python
import functools
import math

import jax
import jax.numpy as jnp
from jax import lax
from jax.experimental import pallas as pl
from jax.experimental.pallas import tpu as pltpu

D_MODEL = 1024
HEAD_DIM = 64
DILATIONS = (1, 4, 16)
KEYS_PER_QUERY = 128
N_GROUPS = len(DILATIONS)
HEADS_PER_GROUP = 4
ATT_HEADS = N_GROUPS * HEADS_PER_GROUP
ATT_WIDTH = ATT_HEADS * HEAD_DIM
ATT_OUT_WIDTH = HEADS_PER_GROUP * HEAD_DIM
N_BUCKETS = 32
MAX_DISTANCE = KEYS_PER_QUERY * DILATIONS[-1]
RWKV_WIDTH = D_MODEL
DECAY_LORA = 64
ICLR_LORA = 64
GATE_LORA = 128
RWKV_COLS = 3 * RWKV_WIDTH + DECAY_LORA + ICLR_LORA + GATE_LORA
N_BRANCHES = 2
IN_COLS = 3 * ATT_WIDTH + RWKV_COLS + N_BRANCHES * D_MODEL
D_FF = 4 * D_MODEL
NORM_EPS = 1e-6
LN_X_EPS = 64e-5
L2_EPS = 1e-12

LANES = 128
VMEM_LIMIT = 56 * 1024 * 1024
NEG = -1e30

GATE_COLS = N_BRANCHES * D_MODEL
RWKV_QUAD = 4
QW = RWKV_QUAD * HEAD_DIM
N_QUADS = RWKV_WIDTH // QW
Q_SLAB0 = 0
K_SLAB0 = Q_SLAB0 + ATT_WIDTH // LANES
V_SLAB0 = K_SLAB0 + ATT_WIDTH // LANES
LORA_SLAB = V_SLAB0 + ATT_WIDTH // LANES
N_SLABS_F32 = LORA_SLAB + 2
RKV_SLABS = 3 * QW // LANES
GATE_SLAB0 = N_QUADS * RKV_SLABS
N_SLABS_BF16 = GATE_SLAB0 + GATE_COLS // LANES

PROJ_TN = 1280
PROJ_TM = 512
ROW_TILE = 512
ATT_CHUNK = KEYS_PER_QUERY * DILATIONS[-1]
QB = KEYS_PER_QUERY
ATT_PREV = QB * max([d for d in DILATIONS if ATT_CHUNK // (QB * d) > 1] or [DILATIONS[0]])
ATT_UNROLL = 3
RWKV_CHUNK = 64
RWKV_T = 512


def _bf16(x):
    return x.astype(jnp.bfloat16)


def _dot(a, b):
    return jnp.dot(a, b, preferred_element_type=jnp.float32)


def _dot_nt(a, b):
    return lax.dot_general(a, b, (((1,), (1,)), ((), ())), preferred_element_type=jnp.float32)


def _proj_body(x_ref, g_ref, w_ref, o32_ref, o16_ref):
    x = x_ref[...]
    ms = jnp.mean(x * x, axis=-1, keepdims=True)
    h = _bf16(x * lax.rsqrt(ms + NORM_EPS) * g_ref[...])
    slabs_per_chunk = PROJ_TN // LANES
    for c in range(IN_COLS // PROJ_TN):
        acc = _dot(h, w_ref[:, c * PROJ_TN:(c + 1) * PROJ_TN])
        for s in range(slabs_per_chunk):
            slab = c * slabs_per_chunk + s
            part = acc[:, s * LANES:(s + 1) * LANES]
            if slab < N_SLABS_BF16:
                o16_ref[slab] = _bf16(part)
            else:
                o32_ref[slab - N_SLABS_BF16] = part


def _proj(x2, g, w_in, tm):
    m = x2.shape[0]
    att_end = 3 * ATT_WIDTH
    rkv_end = att_end + 3 * RWKV_WIDTH
    lora_end = att_end + RWKV_COLS
    rkv = [w_in[:, att_end + j * RWKV_WIDTH + qd * QW:att_end + j * RWKV_WIDTH + (qd + 1) * QW]
           for qd in range(N_QUADS) for j in range(3)]
    w_cols = _bf16(jnp.concatenate(rkv + [w_in[:, lora_end:], w_in[:, :att_end], w_in[:, rkv_end:lora_end]],
                                   axis=1))
    return pl.pallas_call(
        _proj_body,
        out_shape=(jax.ShapeDtypeStruct((N_SLABS_F32, m, LANES), jnp.float32),
                   jax.ShapeDtypeStruct((N_SLABS_BF16, m, LANES), jnp.bfloat16)),
        grid=(m // tm,),
        in_specs=[
            pl.BlockSpec((tm, D_MODEL), lambda i: (i, 0)),
            pl.BlockSpec((1, D_MODEL), lambda i: (0, 0), pipeline_mode=pl.Buffered(1)),
            pl.BlockSpec((D_MODEL, IN_COLS), lambda i: (0, 0), pipeline_mode=pl.Buffered(1)),
        ],
        out_specs=(pl.BlockSpec((N_SLABS_F32, tm, LANES), lambda i: (0, i, 0)),
                   pl.BlockSpec((N_SLABS_BF16, tm, LANES), lambda i: (0, i, 0))),
        compiler_params=pltpu.CompilerParams(
            dimension_semantics=("parallel",), vmem_limit_bytes=VMEM_LIMIT),
        name="proj",
    )(x2, g, w_cols)


def _attn_units(qs, ks, vs, biases, lo):
    scale = HEAD_DIM ** -0.5

    def logits(q, k):
        qs_ = q * scale
        zero = jnp.zeros_like(qs_)
        lhs = _bf16(jnp.concatenate([jnp.where(lo, qs_, zero), jnp.where(lo, zero, qs_)], axis=0))
        return _dot_nt(lhs, _bf16(k))

    s = [logits(q, k) for q, k in zip(qs, ks)]
    s0 = [x[:QB] + b[0] for x, b in zip(s, biases)]
    s1 = [x[QB:] + b[1] for x, b in zip(s, biases)]
    m0 = [jnp.max(x, axis=-1, keepdims=True) for x in s0]
    m1 = [jnp.max(x, axis=-1, keepdims=True) for x in s1]
    p0 = [_bf16(jnp.exp(x - m)) for x, m in zip(s0, m0)]
    p1 = [_bf16(jnp.exp(x - m)) for x, m in zip(s1, m1)]

    def weighted(p0_, p1_, v):
        rhs = _bf16(jnp.concatenate([v, jnp.ones_like(v)], axis=1))
        x = _dot(jnp.concatenate([p0_, p1_], axis=0), rhs)
        return jnp.where(lo, x[:QB, :LANES], x[QB:, :LANES]), jnp.where(lo, x[:QB, LANES:], x[QB:, LANES:])

    ol = [weighted(a, b, v) for a, b, v in zip(p0, p1, vs)]
    out = []
    for (o, l), a, b in zip(ol, m0, m1):
        out.append((o / l, jnp.where(lo, a, b) + jnp.log(l)))
    return out


def _attn_body(q_ref, kc_ref, kp_ref, vc_ref, vp_ref, bias_ref, o_ref, o_scr, l_scr, k_cache, v_cache):
    c = pl.program_id(1)
    g = pl.program_id(2)
    lo = lax.broadcasted_iota(jnp.int32, (1, LANES), 1) < HEAD_DIM
    before_start = (c == 0) & (lax.broadcasted_iota(jnp.int32, (QB, 2 * QB), 1) < QB)

    def rows(start, n, d):
        return pl.ds(start, n) if d == 1 else pl.ds(start, n, stride=d)

    def group(gi):
        d = DILATIONS[gi]
        n_qb = ATT_CHUNK // (QB * d)

        def run(units):
            qs = [q_ref[p, rows(start_q, QB, d), :] for p, start_q, _, _, _ in units]

            def bias(head, is_first):
                tile = bias_ref[head]
                return jnp.where(before_start, NEG, tile) if is_first else tile

            biases = [(bias(gi * HEADS_PER_GROUP + 2 * p, is_first), bias(gi * HEADS_PER_GROUP + 2 * p + 1, is_first))
                      for p, _, _, _, is_first in units]
            res = _attn_units(qs, [u[2] for u in units], [u[3] for u in units], biases, lo)
            for (p, start_q, _, _, _), (o, lse) in zip(units, res):
                o_scr[gi, p, rows(start_q, QB, d), :] = o
                l_scr[gi, p, rows(start_q, QB, d), :] = lse

        n_first = max(j for j in range(1, ATT_UNROLL + 1) if d % j == 0)
        cached = n_qb == 1
        if cached:
            @pl.when(c == 0)
            def _():
                k_cache[...] = jnp.zeros(k_cache.shape, k_cache.dtype)
                v_cache[...] = jnp.zeros(v_cache.shape, v_cache.dtype)

        def first_blocks(it, carry):
            units = []
            for j in range(n_first):
                r = it * n_first + j
                prev_start = r + QB * d * (n_qb - 1) - (ATT_CHUNK - ATT_PREV)
                for p in range(2):
                    k_cur = kc_ref[p, rows(r, QB, d), :]
                    v_cur = vc_ref[p, rows(r, QB, d), :]
                    if cached:
                        slot = pl.ds(pl.multiple_of(r * QB, QB), QB)
                        k = jnp.concatenate([k_cache[p, slot, :], _bf16(k_cur)], axis=0)
                        v = jnp.concatenate([v_cache[p, slot, :], v_cur], axis=0)
                        k_cache[p, slot, :] = _bf16(k_cur)
                        v_cache[p, slot, :] = v_cur
                    else:
                        k = jnp.concatenate([kp_ref[p, rows(prev_start, QB, d), :], k_cur], axis=0)
                        v = jnp.concatenate([vp_ref[p, rows(prev_start, QB, d), :], v_cur], axis=0)
                    units.append((p, r, k, v, True))
            run(units)
            return carry

        lax.fori_loop(0, d // n_first, first_blocks, 0)

        n_later = d * (n_qb - 1)
        if n_later:
            n_par = max(j for j in range(1, ATT_UNROLL + 1) if n_later % j == 0)

            def later_blocks(it, carry):
                units = []
                for j in range(n_par):
                    u = it * n_par + j
                    start_q = u % d + QB * d * (1 + u // d)
                    for p in range(2):
                        k = kc_ref[p, rows(start_q - QB * d, 2 * QB, d), :]
                        v = vc_ref[p, rows(start_q - QB * d, 2 * QB, d), :]
                        units.append((p, start_q, k, v, False))
                run(units)
                return carry

            lax.fori_loop(0, n_later // n_par, later_blocks, 0)

    for gi in range(N_GROUPS):
        pl.when(g == gi)(functools.partial(group, gi))

    @pl.when(g == N_GROUPS - 1)
    def _():
        tile = 256

        def comb(i, carry):
            rs = pl.ds(pl.multiple_of(i * tile, tile), tile)
            for p in range(2):
                ls = [l_scr[gi, p, rs, :] for gi in range(N_GROUPS)]
                mx = jnp.maximum(jnp.maximum(ls[0], ls[1]), ls[2])
                ws = [jnp.exp(l - mx) for l in ls]
                num = ws[0] * o_scr[0, p, rs, :] + ws[1] * o_scr[1, p, rs, :] + ws[2] * o_scr[2, p, rs, :]
                o_ref[rs, p * LANES:(p + 1) * LANES] = _bf16(num / (ws[0] + ws[1] + ws[2]))
            return carry

        lax.fori_loop(0, ATT_CHUNK // tile, comb, 0)


def _attn(slabs, bias_tiles, batch, seq):
    m = slabs.shape[1]
    n_chunks = seq // ATT_CHUNK
    blk = (2, ATT_CHUNK, LANES)

    def cur(slab0):
        return pl.BlockSpec(blk, lambda b, c, g: (slab0 // 2 + g, b * n_chunks + c, 0))

    def prev(slab0):
        per_chunk = ATT_CHUNK // ATT_PREV
        return pl.BlockSpec((2, ATT_PREV, LANES),
                            lambda b, c, g: (slab0 // 2 + g, jnp.maximum((b * n_chunks + c) * per_chunk - 1, 0), 0))

    return pl.pallas_call(
        _attn_body,
        out_shape=jax.ShapeDtypeStruct((m, ATT_OUT_WIDTH), jnp.bfloat16),
        grid=(batch, n_chunks, N_GROUPS),
        in_specs=[cur(Q_SLAB0), cur(K_SLAB0), prev(K_SLAB0), cur(V_SLAB0), prev(V_SLAB0),
                  pl.BlockSpec(bias_tiles.shape, lambda b, c, g: (0, 0, 0))],
        out_specs=pl.BlockSpec((ATT_CHUNK, ATT_OUT_WIDTH), lambda b, c, g: (b * n_chunks + c, 0)),
        scratch_shapes=[pltpu.VMEM((N_GROUPS, 2, ATT_CHUNK, LANES), jnp.float32),
                        pltpu.VMEM((N_GROUPS, 2, ATT_CHUNK, LANES), jnp.float32),
                        pltpu.VMEM((2, ATT_CHUNK, LANES), jnp.bfloat16),
                        pltpu.VMEM((2, ATT_CHUNK, LANES), jnp.float32)],
        compiler_params=pltpu.CompilerParams(
            dimension_semantics=("parallel", "arbitrary", "arbitrary"), vmem_limit_bytes=VMEM_LIMIT),
        name="attn",
    )(slabs, slabs, slabs, slabs, slabs, bias_tiles)


def _t5_bucket(dist):
    max_exact = N_BUCKETS // 2
    d_f = jnp.maximum(dist, 1).astype(jnp.float32)
    large = max_exact + (jnp.log(d_f / max_exact) / math.log(MAX_DISTANCE / max_exact)
                         * (N_BUCKETS - max_exact)).astype(jnp.int32)
    large = jnp.minimum(large, N_BUCKETS - 1)
    return jnp.where(dist < max_exact, dist, large)


def _bias_tiles(rel_bias):
    dil = jnp.array(DILATIONS, jnp.int32)
    dist = dil[:, None] * jnp.arange(KEYS_PER_QUERY + 1, dtype=jnp.int32)[None, :]
    bucket = _t5_bucket(dist)
    bias = rel_bias.reshape(N_BUCKETS, N_GROUPS, HEADS_PER_GROUP)[bucket, jnp.arange(N_GROUPS)[:, None]]
    bias = jnp.transpose(bias, (0, 2, 1)).astype(jnp.float32).reshape(ATT_HEADS, KEYS_PER_QUERY + 1)
    n = 3 * QB - 1
    neg = lambda w: jnp.full((ATT_HEADS, w), NEG, jnp.float32)
    e = jnp.concatenate([neg(QB - 1), bias[:, ::-1], neg(QB - 1), neg(1)], axis=1)
    e = jnp.roll(e, -(QB - 1), axis=1)
    return jnp.tile(e, (1, QB))[:, :QB * n].reshape(ATT_HEADS, QB, n)[:, :, :2 * QB]


def _interleave(*gens):
    gens = list(gens)
    while gens:
        for gen in list(gens):
            try:
                next(gen)
            except StopIteration:
                gens.remove(gen)


def _rwkv_body(rkv_ref, lora_ref, pra_ref, prs_ref, sh_ref, wl_ref, o_ref,
               carry, h_st, tok_s, post_s, up_s, zq_s, kcbt_s, g_s, vb_s, y_s, *, n_t):
    t_blk = o_ref.shape[0]
    cc = RWKV_CHUNK
    cs = range(t_blk // cc)
    i = pl.program_id(0)
    first_tok = (i % n_t) == 0
    first_seq = ((i + n_t - 2) % n_t) == 0

    @pl.when(i == 0)
    def _():
        for ref in (carry, h_st, tok_s, post_s, up_s, zq_s, kcbt_s, g_s, vb_s):
            ref[...] = jnp.zeros(ref.shape, ref.dtype)

    lane_head = lax.broadcasted_iota(jnp.int32, (QW, QW), 1) // HEAD_DIM
    row_head = lax.broadcasted_iota(jnp.int32, (QW, QW), 0) // HEAD_DIM
    same_head = lane_head == row_head
    head_ones = _bf16(jnp.where(same_head, 1.0, 0.0))
    assert RWKV_QUAD * cc == QW
    chunk_lane_head = lax.broadcasted_iota(jnp.int32, (cc, QW), 1) // HEAD_DIM
    keep = [chunk_lane_head == h for h in range(RWKV_QUAD)]

    def head_sum(x):
        return _dot(_bf16(x), head_ones)

    def stack4(x):
        z = jnp.zeros_like(x)
        return _bf16(jnp.concatenate([jnp.where(keep[h], x, z) for h in range(RWKV_QUAD)], axis=0))

    def each(fn, *lists):
        return [fn(*args) for args in zip(*lists)]

    def split3(x):
        hi = _bf16(x)
        r1 = x - hi.astype(jnp.float32)
        mid = _bf16(r1)
        return hi, mid, _bf16(r1 - mid.astype(jnp.float32))

    def per_token(tok_w):
        pr = pra_ref[0]
        sh = sh_ref[...]

        def shifted(x, idx, mu):
            width = x.shape[1]
            row = lax.broadcasted_iota(jnp.int32, x.shape, 0)
            last = jnp.where(first_tok, jnp.zeros((1, width), jnp.float32), carry[idx, 7:8, :width])
            prev = jnp.where(row == 0, last, pltpu.roll(x, 1, 0))
            carry[idx, :, :width] = x[t_blk - 8:, :]
            return x + (prev - x) * mu

        wide = lambda j: jnp.concatenate([rkv_ref[2 * j], rkv_ref[2 * j + 1]], axis=1).astype(jnp.float32)
        r = shifted(wide(0), 0, pr[0:1])
        k = shifted(wide(1), 1, pr[1:2])
        v = shifted(wide(2), 2, pr[2:3])
        tok_s[tok_w, 0] = r
        tok_s[tok_w, 4] = v
        yield
        f_lora = shifted(lora_ref[0], 3, sh[0:1, :LANES])
        f_g = shifted(lora_ref[1], 4, sh[1:2, :LANES])
        w0, a0, k_k, k_a, r_k = (pr[j:j + 1] for j in range(3, 8))
        wd = w0 + _dot(_bf16(jnp.tanh(f_lora)), wl_ref[0, 0])
        tok_s[tok_w, 5] = -math.exp(-0.5) * jax.nn.sigmoid(wd)
        a = jax.nn.sigmoid(a0 + _dot(_bf16(f_lora), wl_ref[0, 1]))
        tok_s[tok_w, 7] = _dot(_bf16(jax.nn.sigmoid(f_g)), wl_ref[0, 2])
        yield
        kk = k * k_k
        kk = kk * lax.rsqrt(jnp.maximum(head_sum(kk * kk), L2_EPS * L2_EPS))
        tok_s[tok_w, 1] = kk
        tok_s[tok_w, 3] = kk * a
        yield
        k2 = k * (1.0 + (a - 1.0) * k_a)
        tok_s[tok_w, 2] = k2
        tok_s[tok_w, 6] = head_sum(r * k2 * r_k) * v
        yield

    def per_chunk(tok_r, chk_w):
        t_i = lax.broadcasted_iota(jnp.int32, (cc, QW), 0)
        s_i = lax.broadcasted_iota(jnp.int32, (cc, QW), 1) % cc
        strict_lower = t_i > s_i
        lower = t_i >= s_i
        eye_f = jnp.where(t_i == s_i, 1.0, 0.0)
        zero_m = jnp.zeros((cc, QW), jnp.float32)
        ci = lax.broadcasted_iota(jnp.int32, (cc, cc), 0)
        cj = lax.broadcasted_iota(jnp.int32, (cc, cc), 1)
        tri = _bf16(jnp.where(ci >= cj, 1.0, 0.0))
        rows_of = lambda j: [tok_s[tok_r, j, c * cc:(c + 1) * cc, :] for c in cs]
        r, kk, k2, bb, v, lw = (rows_of(j) for j in range(6))

        parts = each(split3, lw)
        big_l = each(lambda p: _dot(tri, p[0]) + _dot(tri, p[1]) + _dot(tri, p[2]), parts)
        yield
        l_end = each(lambda l: l[cc - 1:cc], big_l)
        e_l = each(jnp.exp, big_l)
        e_lm = each(lambda l, w: jnp.exp(l - w), big_l, lw)
        e_nl = each(lambda l: jnp.exp(-l), big_l)
        e_c = each(lambda le, l: jnp.exp(le - l), l_end, big_l)
        a_kk_f = each(lambda x, e: x * e, kk, e_lm)
        a_kk = each(_bf16, a_kk_f)
        a_r_f = each(lambda x, e: x * e, r, e_l)
        a_r = each(_bf16, a_r_f)
        kb = each(lambda xk, xb, e: jnp.concatenate([stack4(xk * e), stack4(xb * e)], axis=0),
                  k2, bb, e_nl)
        vb = each(_bf16, v)
        vst = each(stack4, v)
        yield
        s12 = each(lambda ak, ar, kb_: _dot_nt(jnp.concatenate([ak, ar], axis=0), kb_), a_kk, a_r, kb)
        s1 = each(lambda s: s[:cc], s12)
        s2 = each(lambda s: s[cc:], s12)
        yield
        mk = each(lambda s: _bf16(jnp.where(strict_lower, s[:, :QW], zero_m)), s1)
        mb = each(lambda s: jnp.where(strict_lower, s[:, QW:], zero_m), s1)
        nk = each(lambda s: _bf16(jnp.where(lower, s[:, :QW], zero_m)), s2)
        nb_neg = each(lambda s: _bf16(jnp.where(lower, -s[:, QW:], zero_m)), s2)

        x = each(lambda n: eye_f - jnp.where((t_i // 2) == (s_i // 2), n, zero_m), mb)
        blk = 4
        while blk <= cc:
            half = blk // 2
            sel = ((t_i // blk) == (s_i // blk)) & ((t_i % blk) >= half) & ((s_i % blk) < half)
            xb = each(_bf16, x)
            t1 = each(lambda xc, n: _bf16(_dot(xc, stack4(jnp.where(sel, n, zero_m)))), xb, mb)
            yield
            x = each(lambda xf, t: xf - _dot(t, stack4(xf)), x, t1)
            yield
            blk *= 2
        tinv = each(_bf16, x)

        pm = each(lambda t, ak: _dot(t, stack4(ak)), tinv, a_kk_f)
        mnv = each(lambda m_, n_, v_: _dot(jnp.concatenate([m_, n_], axis=0), v_), mk, nk, vst)
        yield
        q = each(lambda t, m_: _dot(t, stack4(m_[:cc])), tinv, mnv)
        yield
        pmb = each(_bf16, pm)
        u = each(lambda af, n, p: af + _dot(n, stack4(p)), a_r_f, nb_neg, pm)
        z = each(lambda m_, n, q_: m_[cc:] + _dot(n, stack4(q_)), mnv, nb_neg, q)
        yield
        for c in cs:
            up_s[chk_w, c] = jnp.concatenate([pmb[c], _bf16(u[c])], axis=0)
            zq_s[chk_w, c, 0] = z[c]
            zq_s[chk_w, c, 1] = q[c]
            vb_s[chk_w, c] = vb[c]
        yield
        kcg = each(lambda xk, xb, e, le: jnp.concatenate(
            [xk * e, -(xb * e), jnp.broadcast_to(jnp.exp(le), (2 * cc, QW))], axis=0).T,
            k2, bb, e_c, l_end)
        for c in cs:
            kcbt_s[chk_w, c] = _bf16(kcg[c][:, :2 * cc])
            g_s[chk_w, c] = kcg[c][:, 2 * cc:]
        yield
        post_s[chk_w, 0] = tok_s[tok_r, 6]
        post_s[chk_w, 1] = tok_s[tok_r, 7]
        yield

    def sequential(chk_r):
        h = jnp.where(first_seq, jnp.zeros((QW, QW), jnp.float32), h_st[...])
        for c in cs:
            ys = _dot(up_s[chk_r, c], _bf16(h))
            sig = ys[:cc] + zq_s[chk_r, c, 1]
            y_s[c * cc:(c + 1) * cc, :] = ys[cc:] + zq_s[chk_r, c, 0]
            yield
            hu = _dot(kcbt_s[chk_r, c], jnp.concatenate([vb_s[chk_r, c], _bf16(sig)], axis=0))
            g = g_s[chk_r, c]
            h = h * jnp.concatenate([g, g], axis=1) + jnp.where(same_head, hu, jnp.zeros_like(hu))
            yield
        h_st[...] = h
        prs = prs_ref[0]
        y = y_s[...]
        mu = head_sum(y) * (1.0 / HEAD_DIM)
        yc = y - mu
        var = head_sum(yc * yc) * (1.0 / HEAD_DIM)
        yn = yc * lax.rsqrt(var + LN_X_EPS) * prs[8:9] + prs[9:10]
        o_ref[...] = _bf16((yn + post_s[chk_r, 0]) * post_s[chk_r, 1])
        yield

    for parity in (0, 1):
        @pl.when(i % 2 == parity)
        def _(parity=parity):
            _interleave(per_chunk(1 - parity, 1 - parity), sequential(parity), per_token(parity))


def _rwkv(slabs32, slabs16, pr, sh, wl, batch, seq):
    m = slabs32.shape[1]
    t_blk = RWKV_T
    n_t = seq // t_blk
    n_c = t_blk // RWKV_CHUNK
    n_blocks = batch * N_QUADS * n_t

    def where(blk):
        blk = jnp.clip(blk, 0, n_blocks - 1)
        seq_id, t = blk // n_t, blk % n_t
        return seq_id % N_QUADS, (seq_id // N_QUADS) * n_t + t

    return pl.pallas_call(
        functools.partial(_rwkv_body, n_t=n_t),
        out_shape=jax.ShapeDtypeStruct((m, RWKV_WIDTH), jnp.bfloat16),
        grid=(n_blocks + 2,),
        in_specs=[pl.BlockSpec((RKV_SLABS, t_blk, LANES), lambda i: (where(i)[0], where(i)[1], 0)),
                  pl.BlockSpec((2, t_blk, LANES), lambda i: (LORA_SLAB // 2, where(i)[1], 0)),
                  pl.BlockSpec((1,) + pr.shape[1:], lambda i: (where(i)[0], 0, 0)),
                  pl.BlockSpec((1,) + pr.shape[1:], lambda i: (where(i - 2)[0], 0, 0)),
                  pl.BlockSpec(sh.shape, lambda i: (0, 0)),
                  pl.BlockSpec((1,) + wl.shape[1:], lambda i: (where(i)[0], 0, 0, 0))],
        out_specs=pl.BlockSpec((t_blk, QW), lambda i: (where(i - 2)[1], where(i - 2)[0])),
        scratch_shapes=[
            pltpu.VMEM((5, 8, QW), jnp.float32),
            pltpu.VMEM((QW, QW), jnp.float32),
            pltpu.VMEM((2, 8, t_blk, QW), jnp.float32),
            pltpu.VMEM((2, 2, t_blk, QW), jnp.float32),
            pltpu.VMEM((2, n_c, 2 * RWKV_CHUNK, QW), jnp.bfloat16),
            pltpu.VMEM((2, n_c, 2, RWKV_CHUNK, QW), jnp.float32),
            pltpu.VMEM((2, n_c, QW, LANES), jnp.bfloat16),
            pltpu.VMEM((2, n_c, QW, LANES), jnp.float32),
            pltpu.VMEM((2, n_c, RWKV_CHUNK, QW), jnp.bfloat16),
            pltpu.VMEM((t_blk, QW), jnp.float32),
        ],
        compiler_params=pltpu.CompilerParams(
            dimension_semantics=("arbitrary",), vmem_limit_bytes=VMEM_LIMIT),
        name="rwkv",
    )(slabs16, slabs32, pr, pr, sh, wl)


def _rwkv_params(shift_mu, w0, a0, k_k, k_a, r_k, ln_w, ln_b, w_w2, w_a2, w_g2):
    def quads(vec):
        return vec.reshape(N_QUADS, QW)

    mu_r, mu_k, mu_v = (quads(shift_mu[j * RWKV_WIDTH:(j + 1) * RWKV_WIDTH]) for j in range(3))
    rows = [mu_r, mu_k, mu_v, quads(w0), quads(a0), quads(k_k), quads(k_a), quads(r_k.reshape(-1)),
            quads(ln_w), quads(ln_b)]
    pr = jnp.stack(rows, axis=1)
    pr = jnp.pad(pr, ((0, 0), (0, 16 - pr.shape[1]), (0, 0)))
    mu_rest = shift_mu[3 * RWKV_WIDTH:]
    sh = jnp.pad(mu_rest.reshape(2, LANES), ((0, 6), (0, QW - LANES)))
    zeros = jnp.zeros((DECAY_LORA, RWKV_WIDTH), jnp.float32)
    wd = jnp.concatenate([w_w2, zeros], axis=0)
    wa = jnp.concatenate([zeros, w_a2], axis=0)
    wl = jnp.stack([wd, wa, w_g2], axis=0)
    wl = wl.reshape(3, LANES, N_QUADS, QW).transpose(2, 0, 1, 3)
    return pr, sh, _bf16(wl)


def _rms(x, g):
    ms = jnp.mean(x * x, axis=-1, keepdims=True)
    return x * lax.rsqrt(ms + NORM_EPS) * g


def _tail_body(x_ref, att_ref, rw_ref, gate0_ref, gate1_ref, bg_ref, wa_ref, wr_ref, wo_ref, gmix_ref,
               gpre_ref, w1_ref, w2_ref, gpost_ref, o_ref, *, tf):
    n_gs = D_MODEL // LANES
    f0 = jnp.concatenate([gate0_ref[s] for s in range(n_gs)], axis=1).astype(jnp.float32)
    f1 = jnp.concatenate([gate1_ref[s] for s in range(n_gs)], axis=1).astype(jnp.float32)
    g0 = jax.nn.sigmoid(f0 + bg_ref[:, :D_MODEL])
    g1 = jax.nn.sigmoid(f1 + bg_ref[:, D_MODEL:])
    merged = g0 * _dot(att_ref[...], wa_ref[...]) + g1 * _dot(rw_ref[...], wr_ref[...])
    z = _dot(_bf16(merged), wo_ref[...])
    x = x_ref[...] + _rms(z, gmix_ref[...])
    h = _bf16(_rms(x, gpre_ref[...]))
    acc = jnp.zeros(x.shape, jnp.float32)
    for c in range(D_FF // tf):
        u = jnp.maximum(_dot(h, w1_ref[:, c * tf:(c + 1) * tf]), 0.0)
        acc = acc + _dot(_bf16(u * u), w2_ref[c * tf:(c + 1) * tf, :])
    o_ref[...] = x + _rms(acc, gpost_ref[...])


def _tail(x2, o_att, o_rwkv, slabs, b_gate, wa, wr, wo, g_mix, g_pre, w1, w2, g_post, tm, tf=1024):
    m = x2.shape[0]
    n_gs = D_MODEL // LANES
    const = lambda shape: pl.BlockSpec(shape, lambda i: (0, 0), pipeline_mode=pl.Buffered(1))
    return pl.pallas_call(
        functools.partial(_tail_body, tf=tf),
        out_shape=jax.ShapeDtypeStruct((m, D_MODEL), jnp.float32),
        grid=(m // tm,),
        in_specs=[
            pl.BlockSpec((tm, D_MODEL), lambda i: (i, 0)),
            pl.BlockSpec((tm, ATT_OUT_WIDTH), lambda i: (i, 0)),
            pl.BlockSpec((tm, RWKV_WIDTH), lambda i: (i, 0)),
            pl.BlockSpec((n_gs, tm, LANES), lambda i: (GATE_SLAB0 // n_gs, i, 0)),
            pl.BlockSpec((n_gs, tm, LANES), lambda i: (GATE_SLAB0 // n_gs + 1, i, 0)),
            const((1, GATE_COLS)), const(wa.shape), const(wr.shape), const(wo.shape), const((1, D_MODEL)),
            const((1, D_MODEL)), const(w1.shape), const(w2.shape), const((1, D_MODEL)),
        ],
        out_specs=pl.BlockSpec((tm, D_MODEL), lambda i: (i, 0)),
        compiler_params=pltpu.CompilerParams(
            dimension_semantics=("parallel",), vmem_limit_bytes=VMEM_LIMIT),
        name="tail",
    )(x2, o_att, o_rwkv, slabs, slabs, b_gate, wa, wr, wo, g_mix, g_pre, w1, w2, g_post)


def kernel(x, rel_bias, norm_mix_pre, norm_mix_post, norm_ffn_pre, norm_ffn_post, w_in, b_gate, shift_mu, w0, w_w2, a0, w_a2, w_g2, k_k, k_a, r_k, ln_x_w, ln_x_b, w_att_branch, w_rwkv_branch, w_out, w_ffn1, w_ffn2):
    batch, seq, d_model = x.shape
    assert d_model == D_MODEL and seq % ATT_CHUNK == 0 and seq % RWKV_T == 0
    m = batch * seq
    tm = ROW_TILE
    assert m % PROJ_TM == 0 and m % tm == 0
    bias_tiles = _bias_tiles(rel_bias)
    row = lambda vec: vec.reshape(1, -1)
    x2 = x.reshape(m, D_MODEL)
    for l in range(w_in.shape[0]):
        slabs32, slabs16 = _proj(x2, row(norm_mix_pre[l]), w_in[l], PROJ_TM)
        o_att = _attn(slabs32, bias_tiles, batch, seq)
        pr, sh, wl = _rwkv_params(shift_mu[l], w0[l], a0[l], k_k[l], k_a[l], r_k[l], ln_x_w[l], ln_x_b[l],
                                  w_w2[l], w_a2[l], w_g2[l])
        o_rwkv = _rwkv(slabs32, slabs16, pr, sh, wl, batch, seq)
        x2 = _tail(x2, o_att, o_rwkv, slabs16, row(b_gate[l]), _bf16(w_att_branch[l]), _bf16(w_rwkv_branch[l]),
                   _bf16(w_out[l]), row(norm_mix_post[l]), row(norm_ffn_pre[l]), _bf16(w_ffn1[l]),
                   _bf16(w_ffn2[l]), row(norm_ffn_post[l]), tm)
    return x2.reshape(batch, seq, D_MODEL)
```
